```python
import jax, jax.numpy as jnp
from jax import lax
import numpy as np

D_MODEL = 4096
BATCH = 8
SEQ = 2048
DEPTH = 1

PLE_DIM = 256
NORM_EPS = 1e-6
NEG_INF = -1e30
A_WIDTH = D_MODEL // 2
A_HEAD = 64
A_HEADS = A_WIDTH // A_HEAD
A_LORA_W = 64
A_LORA_A = 64
RWKV_GN_EPS = 64e-5
B_WIDTH = D_MODEL // 2
B_HEAD = 128
B_HEADS = B_WIDTH // B_HEAD
B_KV_HEADS = 4
B_REP = B_HEADS // B_KV_HEADS
KV_WIDTH = B_KV_HEADS * B_HEAD
L_CMP = 32
CMP_STRIDE = 16
CMP_HIDDEN = 256
L_SEL = 64
N_SEL = 16
WINDOW = 512
BLOCK_Q = 128
SEL_Q_CHUNK = 16
N_NSA_BRANCH = 3
FORCE_BONUS = 1e3
ATTN_SCALE = B_HEAD ** -0.5

RWKV_SIZES = (A_WIDTH, A_WIDTH, A_WIDTH, A_WIDTH, A_LORA_W, A_LORA_A)
NSA_SIZES = (B_WIDTH, KV_WIDTH, KV_WIDTH, KV_WIDTH, KV_WIDTH, KV_WIDTH, KV_WIDTH, B_HEADS * N_NSA_BRANCH, B_WIDTH)
A_COLS = sum(RWKV_SIZES)
B_COLS = sum(NSA_SIZES)
N_IN = A_COLS + B_COLS + 2 * D_MODEL
RWKV_OFFSETS = [int(c) for c in np.cumsum(RWKV_SIZES)[:-1]]
NSA_OFFSETS = [int(c) for c in np.cumsum(NSA_SIZES)[:-1]]

kernel_name = 'rwkv7_nsa_griffin_gated_hybrid'


def rms_norm(x, g, eps=NORM_EPS):
    xf = x.astype(jnp.float32)
    y = xf * lax.rsqrt(jnp.mean(xf * xf, axis=-1, keepdims=True) + eps)
    return (y * g.astype(jnp.float32)).astype(x.dtype)


def token_shift(f, mu):
    prev = jnp.pad(f, ((0, 0), (1, 0), (0, 0)))[:, :-1]
    return f + mu * (prev - f)


def rwkv7_scan(r, w, k, v, kk, a):
    B, S, H, N = r.shape

    def step(state, inp):
        r_t, w_t, k_t, v_t, kk_t, a_t = inp
        sa = jnp.einsum('bhij,bhj->bhi', state, -kk_t)
        state = (state * w_t[:, :, None, :]
                 + sa[..., :, None] * (kk_t * a_t)[..., None, :]
                 + v_t[..., :, None] * k_t[..., None, :])
        return state, jnp.einsum('bhij,bhj->bhi', state, r_t)

    xs = tuple(jnp.moveaxis(t, 1, 0) for t in (r, w, k, v, kk, a))
    _, out = lax.scan(step, jnp.zeros((B, H, N, N), jnp.float32), xs)
    return jnp.moveaxis(out, 0, 1)


def rwkv7_branch(f, w_lora_up, w0, a_lora_up, a0, k_k, k_a, r_k, lnx_w, lnx_b):
    B, S, _ = f.shape
    f32 = jnp.float32
    r, k, v, z, wd, ad = jnp.split(f, RWKV_OFFSETS, axis=-1)
    heads = lambda t: t.astype(f32).reshape(B, S, A_HEADS, A_HEAD)
    w_log = -jax.nn.softplus(-(w0 + jnp.tanh(wd) @ w_lora_up).astype(f32)) - 0.5
    decay = jnp.exp(-jnp.exp(w_log))
    a = jax.nn.sigmoid((a0 + ad @ a_lora_up).astype(f32))
    kk = heads(k * k_k)
    kk = kk / jnp.maximum(jnp.sqrt(jnp.sum(kk * kk, axis=-1, keepdims=True)), 1e-12)
    k_mod = k.astype(f32) * (1.0 + (a - 1.0) * k_a.astype(f32))
    rh, kh, vh = heads(r), heads(k_mod), heads(v)
    o = rwkv7_scan(rh, heads(decay), kh, vh, kk, heads(a))
    mu = jnp.mean(o, axis=-1, keepdims=True)
    var = jnp.mean(jnp.square(o - mu), axis=-1, keepdims=True)
    o = ((o - mu) * lax.rsqrt(var + RWKV_GN_EPS)).reshape(B, S, A_WIDTH)
    o = o * lnx_w.astype(f32) + lnx_b.astype(f32)
    bonus = jnp.sum(rh * kh * r_k.astype(f32).reshape(A_HEADS, A_HEAD), axis=-1, keepdims=True) * vh
    o = o + bonus.reshape(B, S, A_WIDTH)
    return o.astype(f.dtype) * jax.nn.silu(z)


def compress(kv, pe, w1, w2):
    B, S, G, d = kv.shape
    nc = (S - L_CMP) // CMP_STRIDE + 1
    idx = jnp.arange(nc)[:, None] * CMP_STRIDE + jnp.arange(L_CMP)[None, :]
    blk = kv[:, idx] + pe[None, None, :, None, :]
    blk = jnp.moveaxis(blk, 3, 2).reshape(B, nc, G, L_CMP * d)
    return jax.nn.gelu(blk @ w1) @ w2


def compressed_attention(q, kc, vc):
    B, S, G, R, d = q.shape
    nc = kc.shape[1]
    ns = S // L_SEL
    t = jnp.arange(S)
    c_start = jnp.arange(nc) * CMP_STRIDE
    mask = (c_start + L_CMP - 1)[None, :] <= t[:, None]
    s = jnp.einsum('bsgrd,bngd->bgrsn', q, kc).astype(jnp.float32) * ATTN_SCALE
    pr = jax.nn.softmax(jnp.where(mask, s, NEG_INF), axis=-1)
    pr = pr * jnp.any(mask, axis=-1).astype(jnp.float32)[:, None]
    o = jnp.einsum('bgrsn,bngd->bsgrd', pr.astype(vc.dtype), vc)
    s_start = jnp.arange(ns) * L_SEL
    overlap = ((c_start[:, None] < (s_start + L_SEL)[None, :]) &
               (s_start[None, :] < (c_start + L_CMP)[:, None])).astype(jnp.float32)
    imp = jnp.einsum('bgsn,nm->bgsm', jnp.sum(pr, axis=2), overlap)
    return o, imp


def selected_attention(q, k, v, imp):
    B, S, G, R, d = q.shape
    ns = S // L_SEL
    n_top = min(N_SEL, ns)
    t = jnp.arange(S)
    j = jnp.arange(ns)
    cur = t // L_SEL
    forced = ((j[None, :] == 0) | (j[None, :] == cur[:, None]) | (j[None, :] == cur[:, None] - 1)).astype(jnp.float32)
    causal = j[None, :] * L_SEL <= t[:, None]
    score = jnp.where(causal, imp + FORCE_BONUS * forced, NEG_INF)
    vals, idx = lax.top_k(score, n_top)
    valid = vals > 0.5 * NEG_INF
    kb = jnp.moveaxis(k.reshape(B, ns, L_SEL, G, d), 3, 1)
    vb = jnp.moveaxis(v.reshape(B, ns, L_SEL, G, d), 3, 1)
    nq = S // SEL_Q_CHUNK
    qch = jnp.moveaxis(q.reshape(B, nq, SEL_Q_CHUNK, G, R, d), 1, 0)
    ich = jnp.moveaxis(idx.reshape(B, G, nq, SEL_Q_CHUNK, n_top), 2, 0)
    vch = jnp.moveaxis(valid.reshape(B, G, nq, SEL_Q_CHUNK, n_top), 2, 0)
    tch = t.reshape(nq, SEL_Q_CHUNK)
    bi = jnp.arange(B)[:, None, None, None]
    gi = jnp.arange(G)[None, :, None, None]

    def chunk(args):
        qc, ic, okc, tc = args
        kg = kb[bi, gi, ic]
        vg = vb[bi, gi, ic]
        s = jnp.einsum('btgrd,bgtkld->bgrtkl', qc, kg).astype(jnp.float32) * ATTN_SCALE
        pos = ic[..., None] * L_SEL + jnp.arange(L_SEL)
        m = (pos <= tc[None, None, :, None, None]) & okc[..., None]
        s = jnp.where(m[:, :, None], s, NEG_INF)
        T = qc.shape[1]
        pr = jax.nn.softmax(s.reshape(B, G, R, T, n_top * L_SEL), axis=-1).reshape(s.shape)
        return jnp.einsum('bgrtkl,bgtkld->btgrd', pr.astype(v.dtype), vg)

    o = lax.map(chunk, (qch, ich, vch, tch))
    return jnp.moveaxis(o, 0, 1).reshape(B, S, G, R, d)


def window_attention(q, k, v):
    B, S, G, R, d = q.shape
    nb = S // BLOCK_Q
    span = WINDOW + BLOCK_Q
    kp = jnp.pad(k, ((0, 0), (WINDOW, 0), (0, 0), (0, 0)))
    vp = jnp.pad(v, ((0, 0), (WINDOW, 0), (0, 0), (0, 0)))
    qb = jnp.moveaxis(q.reshape(B, nb, BLOCK_Q, G, R, d), 1, 0)

    def block(args):
        qblk, bidx = args
        start = bidx * BLOCK_Q
        kblk = lax.dynamic_slice_in_dim(kp, start, span, axis=1)
        vblk = lax.dynamic_slice_in_dim(vp, start, span, axis=1)
        s = jnp.einsum('btgrd,bugd->bgrtu', qblk, kblk).astype(jnp.float32) * ATTN_SCALE
        tq = start + jnp.arange(BLOCK_Q)
        tk = start - WINDOW + jnp.arange(span)
        diff = tq[:, None] - tk[None, :]
        m = (diff >= 0) & (diff < WINDOW) & (tk[None, :] >= 0)
        pr = jax.nn.softmax(jnp.where(m, s, NEG_INF), axis=-1)
        return jnp.einsum('bgrtu,bugd->btgrd', pr.astype(v.dtype), vblk)

    o = lax.map(block, (qb, jnp.arange(nb)))
    return jnp.moveaxis(o, 0, 1).reshape(B, S, G, R, d)


def nsa_branch(f, q_norm_g, k_norm_g, pe_k, pe_v, ck1, ck2, cv1, cv2):
    B, S, _ = f.shape
    q, kc, vc, ks, vs, kw, vw, g, z = jnp.split(f, NSA_OFFSETS, axis=-1)
    kvh = lambda t: t.reshape(B, S, B_KV_HEADS, B_HEAD)
    q = rms_norm(q.reshape(B, S, B_KV_HEADS, B_REP, B_HEAD), q_norm_g)
    kc = rms_norm(compress(kvh(kc), pe_k, ck1, ck2), k_norm_g[0])
    vc = compress(kvh(vc), pe_v, cv1, cv2)
    ks = rms_norm(kvh(ks), k_norm_g[1])
    kw = rms_norm(kvh(kw), k_norm_g[2])
    o_c, imp = compressed_attention(q, kc, vc)
    o_s = selected_attention(q, ks, kvh(vs), imp)
    o_w = window_attention(q, kw, kvh(vw))
    g = jax.nn.sigmoid(g.reshape(B, S, B_KV_HEADS, B_REP, N_NSA_BRANCH))
    o = g[..., 0:1] * o_c + g[..., 1:2] * o_s + g[..., 2:3] * o_w
    return o.reshape(B, S, B_WIDTH) * jax.nn.silu(z)


def setup_inputs(seed: int = 0) -> dict:
    key = jax.random.key(seed)
    ks = jax.random.split(key, 32)
    f32 = jnp.float32
    nrm = lambda k, shape, scale: jax.random.normal(k, shape, f32) * scale
    L = DEPTH
    return {
        'x': nrm(ks[0], (BATCH, SEQ, D_MODEL), 1.0),
        'p': nrm(ks[1], (DEPTH, BATCH, SEQ, PLE_DIM), 1.0),
        'norm_g': 1.0 + nrm(ks[2], (L, D_MODEL), 0.02),
        'w_in': nrm(ks[3], (L, D_MODEL, N_IN), D_MODEL ** -0.5),
        'shift_mu': jax.random.uniform(ks[4], (L, A_COLS), f32),
        'w_lora_up': nrm(ks[5], (L, A_LORA_W, A_WIDTH), 0.5 * A_LORA_W ** -0.5),
        'w0': jax.random.uniform(ks[6], (L, A_WIDTH), f32, -5.0, 1.0),
        'a_lora_up': nrm(ks[7], (L, A_LORA_A, A_WIDTH), 0.5 * A_LORA_A ** -0.5),
        'a0': nrm(ks[8], (L, A_WIDTH), 0.5),
        'k_k': 0.85 + nrm(ks[9], (L, A_WIDTH), 0.05),
        'k_a': 1.0 + nrm(ks[10], (L, A_WIDTH), 0.05),
        'r_k': nrm(ks[11], (L, A_WIDTH), 0.1),
        'lnx_w': 1.0 + nrm(ks[12], (L, A_WIDTH), 0.02),
        'lnx_b': nrm(ks[13], (L, A_WIDTH), 0.02),
        'q_norm_g': 1.0 + nrm(ks[14], (L, B_HEAD), 0.02),
        'k_norm_g': 1.0 + nrm(ks[15], (L, N_NSA_BRANCH, B_HEAD), 0.02),
        'pe_cmp_k': nrm(ks[16], (L, L_CMP, B_HEAD), 0.1),
        'pe_cmp_v': nrm(ks[17], (L, L_CMP, B_HEAD), 0.1),
        'cmp_k_w1': nrm(ks[18], (L, L_CMP * B_HEAD, CMP_HIDDEN), (L_CMP * B_HEAD) ** -0.5),
        'cmp_k_w2': nrm(ks[19], (L, CMP_HIDDEN, B_HEAD), CMP_HIDDEN ** -0.5),
        'cmp_v_w1': nrm(ks[20], (L, L_CMP * B_HEAD, CMP_HIDDEN), (L_CMP * B_HEAD) ** -0.5),
        'cmp_v_w2': nrm(ks[21], (L, CMP_HIDDEN, B_HEAD), CMP_HIDDEN ** -0.5),
        'w_up_a': nrm(ks[22], (L, A_WIDTH, D_MODEL), A_WIDTH ** -0.5),
        'w_up_b': nrm(ks[23], (L, B_WIDTH, D_MODEL), B_WIDTH ** -0.5),
        'w_out': nrm(ks[24], (L, D_MODEL, D_MODEL), D_MODEL ** -0.5),
        'ple_pre_g': 1.0 + nrm(ks[25], (L, D_MODEL), 0.02),
        'w_ple_gate': nrm(ks[26], (L, D_MODEL, D_MODEL), D_MODEL ** -0.5),
        'w_ple': nrm(ks[27], (L, PLE_DIM, D_MODEL), PLE_DIM ** -0.5),
        'ple_post_g': 1.0 + nrm(ks[28], (L, D_MODEL), 0.02),
    }


def reference(x, p, norm_g, w_in, shift_mu, w_lora_up, w0, a_lora_up, a0, k_k, k_a, r_k, lnx_w, lnx_b,
              q_norm_g, k_norm_g, pe_cmp_k, pe_cmp_v, cmp_k_w1, cmp_k_w2, cmp_v_w1, cmp_v_w2,
              w_up_a, w_up_b, w_out, ple_pre_g, w_ple_gate, w_ple, ple_post_g):
    for i in range(DEPTH):
        h = rms_norm(x, norm_g[i])
        proj = h @ w_in[i]
        fa = token_shift(proj[..., :A_COLS], shift_mu[i])
        fb = proj[..., A_COLS:A_COLS + B_COLS]
        gate_a, gate_b = jnp.split(proj[..., A_COLS + B_COLS:], 2, axis=-1)
        ya = rwkv7_branch(fa, w_lora_up[i], w0[i], a_lora_up[i], a0[i], k_k[i], k_a[i], r_k[i], lnx_w[i], lnx_b[i])
        yb = nsa_branch(fb, q_norm_g[i], k_norm_g[i], pe_cmp_k[i], pe_cmp_v[i],
                        cmp_k_w1[i], cmp_k_w2[i], cmp_v_w1[i], cmp_v_w2[i])
        merged = jax.nn.sigmoid(gate_a) * (ya @ w_up_a[i]) + jax.nn.sigmoid(gate_b) * (yb @ w_up_b[i])
        x = x + merged @ w_out[i]
        e = rms_norm(p[i] @ w_ple[i], ple_post_g[i])
        x = x + jax.nn.sigmoid(rms_norm(x, ple_pre_g[i]) @ w_ple_gate[i]) * e
    return x
```

```python
import functools

import jax
import jax.numpy as jnp
from jax import lax
from jax.experimental import pallas as pl
from jax.experimental.pallas import tpu as pltpu

F32 = jnp.float32
BF16 = jnp.bfloat16

LANES = 128
SUBLANES = 8
VMEM_LIMIT = 56 * 1024 * 1024

NORM_EPS = 1e-6
NEG_INF = -1e30
HEAD_A = 64
GN_EPS = 64e-5
HEAD_B = 128
KV_GROUPS = 4
REP = 4
L_CMP = 32
CMP_STRIDE = 16
L_SEL = 64
N_SEL = 16
WINDOW = 512
TQ = 128
KC = 256
FORCE_BONUS = 1e3
ATTN_SCALE = HEAD_B ** -0.5
DECAY_SCALE = 0.6065306597126334


def _params(vmem=VMEM_LIMIT, ndim=1):
    return pltpu.CompilerParams(dimension_semantics=("arbitrary",) * ndim, vmem_limit_bytes=vmem)


def _rmsnorm_kernel(x_ref, g_ref, o_ref):
    x = x_ref[...]
    ms = jnp.mean(x * x, axis=-1, keepdims=True)
    o_ref[...] = (x * lax.rsqrt(ms + NORM_EPS) * g_ref[...]).astype(o_ref.dtype)


def rmsnorm_rows(x, g, out_dtype, tm=256):
    m, d = x.shape
    return pl.pallas_call(
        _rmsnorm_kernel,
        grid=(m // tm,),
        in_specs=[pl.BlockSpec((tm, d), lambda i: (i, 0)), pl.BlockSpec((1, d), lambda i: (0, 0))],
        out_specs=pl.BlockSpec((tm, d), lambda i: (i, 0)),
        out_shape=jax.ShapeDtypeStruct((m, d), out_dtype),
        compiler_params=_params(),
        name="rmsnorm",
    )(x, g.reshape(1, d))


def _mm_kernel(*refs, mode, tiles_per_seq):
    a_ref, b_ref = refs[0], refs[1]
    acc = jnp.dot(a_ref[...], b_ref[...], preferred_element_type=F32)
    if mode == "plain":
        o_ref = refs[2]
        o_ref[...] = acc.astype(o_ref.dtype)
    elif mode == "shift":
        mu_ref, o_ref, carry_ref = refs[2], refs[3], refs[4]
        i = pl.program_id(1)
        tm = acc.shape[0]
        first = (i % tiles_per_seq) == 0
        last_prev = jnp.where(first, 0.0, carry_ref[SUBLANES - 1:SUBLANES, :])
        rolled = pltpu.roll(acc, 1, axis=0)
        row = lax.broadcasted_iota(jnp.int32, acc.shape, 0)
        prev = jnp.where(row == 0, last_prev, rolled)
        carry_ref[...] = acc[tm - SUBLANES:tm, :]
        o_ref[...] = (acc + mu_ref[...] * (prev - acc)).astype(o_ref.dtype)
    elif mode == "resid":
        r_ref, o_ref = refs[2], refs[3]
        o_ref[...] = (r_ref[...] + acc).astype(o_ref.dtype)
    elif mode == "ple":
        x_ref, e_ref, o_ref = refs[2], refs[3], refs[4]
        o_ref[...] = (x_ref[...] + jax.nn.sigmoid(acc) * e_ref[...]).astype(o_ref.dtype)
    else:
        raise ValueError(mode)


def matmul(a, b, *, mode="plain", extras=(), out_dtype=F32, tm=512, tn=1024, tiles_per_seq=1, name="mm"):
    m, k = a.shape
    n = b.shape[1]
    assert m % tm == 0 and n % tn == 0, (m, n, tm, tn)
    in_specs = [pl.BlockSpec((tm, k), lambda j, i: (i, 0)), pl.BlockSpec((k, tn), lambda j, i: (0, j))]
    scratch = []
    if mode == "shift":
        in_specs.append(pl.BlockSpec((1, tn), lambda j, i: (0, j)))
        scratch.append(pltpu.VMEM((SUBLANES, tn), F32))
    elif mode == "resid":
        in_specs.append(pl.BlockSpec((tm, tn), lambda j, i: (i, j)))
    elif mode == "ple":
        in_specs += [pl.BlockSpec((tm, tn), lambda j, i: (i, j)), pl.BlockSpec((tm, tn), lambda j, i: (i, j))]
    return pl.pallas_call(
        functools.partial(_mm_kernel, mode=mode, tiles_per_seq=tiles_per_seq),
        grid=(n // tn, m // tm),
        in_specs=in_specs,
        out_specs=pl.BlockSpec((tm, tn), lambda j, i: (i, j)),
        out_shape=jax.ShapeDtypeStruct((m, n), out_dtype),
        scratch_shapes=scratch,
        compiler_params=_params(ndim=2),
        name=name,
    )(a, b, *extras)


def _lora_kernel(x_ref, w_ref, b_ref, o_ref):
    x = x_ref[...]
    lane = lax.broadcasted_iota(jnp.int32, x.shape, 1)
    x = jnp.where(lane < HEAD_A, jnp.tanh(x), x)
    acc = jnp.dot(x, w_ref[...], preferred_element_type=F32, precision=lax.Precision.HIGHEST)
    o_ref[...] = acc + b_ref[...]


def lora_project(proj_a, col_block, w_blockdiag, bias, tm=1024, tn=1024):
    m = proj_a.shape[0]
    n = w_blockdiag.shape[1]
    tm = min(tm, m)
    assert m % tm == 0 and n % tn == 0
    return pl.pallas_call(
        _lora_kernel,
        grid=(m // tm, n // tn),
        in_specs=[
            pl.BlockSpec((tm, LANES), lambda i, j: (i, col_block)),
            pl.BlockSpec((LANES, tn), lambda i, j: (0, j)),
            pl.BlockSpec((1, tn), lambda i, j: (0, j)),
        ],
        out_specs=pl.BlockSpec((tm, tn), lambda i, j: (i, j)),
        out_shape=jax.ShapeDtypeStruct((m, n), F32),
        compiler_params=_params(ndim=2),
        name="lora",
    )(proj_a, w_blockdiag, bias)


def _scan_kernel(r_ref, k_ref, v_ref, wp_ref, ap_ref, kkw_ref, kaw_ref, rkw_ref, lnw_ref, lnb_ref,
                 o_ref, state_ref, orow_ref, *, tc):
    @pl.when(pl.program_id(1) == 0)
    def _():
        state_ref[...] = jnp.zeros_like(state_ref)

    def step(t, carry):
        r = r_ref[t]
        k = k_ref[t]
        v = v_ref[t]
        w = jnp.exp(-DECAY_SCALE * jax.nn.sigmoid(wp_ref[t]))
        a = jax.nn.sigmoid(ap_ref[t])
        kkr = k * kkw_ref[...]
        nrm = jnp.sqrt(jnp.sum(kkr * kkr, axis=0, keepdims=True))
        kk = kkr / jnp.maximum(nrm, 1e-12)
        kmod = k * (1.0 + (a - 1.0) * kaw_ref[...])
        b = kk * a
        nkk = -kk

        def irow(i, c):
            srow = state_ref[i]
            sa = jnp.sum(srow * nkk, axis=0, keepdims=True)
            vi = v_ref[t, pl.ds(i, 1), :]
            snew = srow * w + sa * b + vi * kmod
            state_ref[i] = snew
            orow_ref[pl.ds(i, 1), :] = jnp.sum(snew * r, axis=0, keepdims=True)
            return c

        lax.fori_loop(0, HEAD_A, irow, 0, unroll=8)
        o = orow_ref[...]
        mu = jnp.mean(o, axis=0, keepdims=True)
        d = o - mu
        var = jnp.mean(d * d, axis=0, keepdims=True)
        on = d * lax.rsqrt(var + GN_EPS) * lnw_ref[...] + lnb_ref[...]
        bonus = jnp.sum(r * kmod * rkw_ref[...], axis=0, keepdims=True) * v
        o_ref[t] = on + bonus
        return carry

    lax.fori_loop(0, tc, step, 0)


def rwkv_scan(r, k, v, wp, ap, kkw, kaw, rkw, lnw, lnb, tc=32):
    s, n, l = r.shape
    seq = pl.BlockSpec((tc, n, LANES), lambda g, c: (c, 0, g))
    par = pl.BlockSpec((n, LANES), lambda g, c: (0, g))
    return pl.pallas_call(
        functools.partial(_scan_kernel, tc=tc),
        grid=(l // LANES, s // tc),
        in_specs=[seq] * 5 + [par] * 5,
        out_specs=seq,
        out_shape=jax.ShapeDtypeStruct((s, n, l), F32),
        scratch_shapes=[pltpu.VMEM((n, n, LANES), F32), pltpu.VMEM((n, LANES), F32)],
        compiler_params=_params(ndim=2),
        name="rwkv_scan",
    )(r, k, v, wp, ap, kkw, kaw, rkw, lnw, lnb)


def _compress_kernel(x_ref, pe_ref, w1_ref, w2_ref, kg_ref, o_ref):
    which = pl.program_id(0)
    half = L_CMP // 2
    nblk = x_ref.shape[0] // CMP_STRIDE
    h1 = jnp.zeros((nblk, w1_ref.shape[-1]), F32)
    h2 = jnp.zeros((nblk, w1_ref.shape[-1]), F32)
    for l in range(half):
        x = x_ref[pl.ds(l, nblk, stride=CMP_STRIDE), :]
        a1 = (x + pe_ref[l:l + 1, :]).astype(BF16)
        a2 = (x + pe_ref[half + l:half + l + 1, :]).astype(BF16)
        h1 = h1 + jnp.dot(a1, w1_ref[l], preferred_element_type=F32)
        h2 = h2 + jnp.dot(a2, w1_ref[half + l], preferred_element_type=F32)
    hid = h1 + pltpu.roll(h2, nblk - 1, axis=0)
    hid = jax.nn.gelu(hid)
    out = jnp.dot(hid.astype(BF16), w2_ref[...], preferred_element_type=F32)
    ms = jnp.mean(out * out, axis=-1, keepdims=True)
    normed = out * lax.rsqrt(ms + NORM_EPS) * kg_ref[...]
    out = jnp.where(which == 0, normed, out)
    row = lax.broadcasted_iota(jnp.int32, out.shape, 0)
    o_ref[...] = jnp.where(row < nblk - 1, out, 0.0)


def compress(proj_b, kc_block, vc_block, pe, w1, w2, kgain, batch, seq):
    nblk = seq // CMP_STRIDE
    hidden = w1.shape[-1]

    def xmap(w, b, g):
        return (b, kc_block + w * (vc_block - kc_block) + g)

    return pl.pallas_call(
        _compress_kernel,
        grid=(2, batch, KV_GROUPS),
        in_specs=[
            pl.BlockSpec((seq, HEAD_B), xmap),
            pl.BlockSpec((None, L_CMP, HEAD_B), lambda w, b, g: (w, 0, 0)),
            pl.BlockSpec((None, L_CMP, HEAD_B, hidden), lambda w, b, g: (w, 0, 0, 0)),
            pl.BlockSpec((None, hidden, HEAD_B), lambda w, b, g: (w, 0, 0)),
            pl.BlockSpec((1, HEAD_B), lambda w, b, g: (0, 0)),
        ],
        out_specs=pl.BlockSpec((None, None, None, nblk, HEAD_B), lambda w, b, g: (w, b, g, 0, 0)),
        out_shape=jax.ShapeDtypeStruct((2, batch, KV_GROUPS, nblk, HEAD_B), F32),
        compiler_params=_params(ndim=3),
        name="nsa_compress",
    )(proj_b, pe, w1, w2, kgain)


def _rms(x, gain):
    ms = jnp.mean(x * x, axis=-1, keepdims=True)
    return x * lax.rsqrt(ms + NORM_EPS) * gain


def _dot_nt(a, b):
    return lax.dot_general(a, b, (((1,), (1,)), ((), ())), preferred_element_type=F32)


def _attn_kernel(q_ref, ks_ref, vs_ref, kw_ref, vw_ref, kc_ref, vc_ref, gt_ref, z_ref, qg_ref, kg_ref,
                 ov_ref, ex_ref, o_ref, ksn_ref, vsb_ref, kwn_ref, vwb_ref, *, seq):
    qt = pl.program_id(2)
    rows = REP * TQ
    ncmp = kc_ref.shape[0]

    @pl.when(qt == 0)
    def _():
        ksn_ref[...] = _rms(ks_ref[...], kg_ref[1:2, :]).astype(BF16)
        vsb_ref[...] = vs_ref[...].astype(BF16)
        kwn_ref[0:WINDOW, :] = jnp.zeros((WINDOW, HEAD_B), BF16)
        vwb_ref[0:WINDOW, :] = jnp.zeros((WINDOW, HEAD_B), BF16)
        kwn_ref[WINDOW:WINDOW + seq, :] = _rms(kw_ref[...], kg_ref[2:3, :]).astype(BF16)
        vwb_ref[WINDOW:WINDOW + seq, :] = vw_ref[...].astype(BF16)

    t0 = pl.multiple_of(qt * TQ, TQ)
    q = jnp.concatenate(
        [_rms(q_ref[:, r * HEAD_B:(r + 1) * HEAD_B], qg_ref[...]) * ATTN_SCALE for r in range(REP)], axis=0
    ).astype(BF16)

    def t_of(shape):
        return t0 + lax.broadcasted_iota(jnp.int32, shape, 0)

    def softmax_pv(s3, ok, v):
        s3 = jnp.where(ok[None], s3, NEG_INF)
        m = jnp.max(s3, axis=-1, keepdims=True)
        p = jnp.where(ok[None], jnp.exp(s3 - m), 0.0)
        l = jnp.sum(p, axis=-1, keepdims=True)
        pr = p / jnp.maximum(l, 1e-30)
        return pr

    s = _dot_nt(q, kc_ref[...].astype(BF16)).reshape(REP, TQ, ncmp)
    n_idx = lax.broadcasted_iota(jnp.int32, (TQ, ncmp), 1)
    ok_c = (n_idx * CMP_STRIDE + (L_CMP - 1) <= t_of((TQ, ncmp))) & (n_idx < ncmp - 1)
    pr_c = softmax_pv(s, ok_c, None)
    o_c = jnp.dot(pr_c.reshape(rows, ncmp).astype(BF16), vc_ref[...].astype(BF16), preferred_element_type=F32)
    psum = pr_c[0] + pr_c[1] + pr_c[2] + pr_c[3]
    imp = jnp.dot(psum, ov_ref[...], preferred_element_type=F32, precision=lax.Precision.HIGHEST)

    nsel = seq // L_SEL
    j_idx = lax.broadcasted_iota(jnp.int32, (TQ, LANES), 1)
    t_sel = t_of((TQ, LANES))
    cur = t_sel // L_SEL
    forced = ((j_idx == 0) | (j_idx == cur) | (j_idx == cur - 1)).astype(F32)
    causal = (j_idx * L_SEL <= t_sel) & (j_idx < nsel)
    score = jnp.where(causal, imp + FORCE_BONUS * forced, NEG_INF)
    rank = jnp.zeros((TQ, LANES), F32)
    for i in range(nsel):
        si = score[:, i:i + 1]
        ahead = (si > score) | ((si == score) & (j_idx > i))
        rank = rank + jnp.where(ahead, 1.0, 0.0)
    sel = jnp.where((rank < min(N_SEL, nsel)) & (score > 0.5 * NEG_INF), 1.0, 0.0).astype(BF16)

    def sel_chunk(c, carry):
        m_i, l_i, acc = carry
        k0 = pl.multiple_of(c * KC, KC)
        kblk = ksn_ref[pl.ds(k0, KC), :]
        vblk = vsb_ref[pl.ds(k0, KC), :]
        s3 = _dot_nt(q, kblk).reshape(REP, TQ, KC)
        member = jnp.dot(sel, ex_ref[c], preferred_element_type=F32)
        u_idx = k0 + lax.broadcasted_iota(jnp.int32, (TQ, KC), 1)
        ok = (member > 0.5) & (u_idx <= t_of((TQ, KC)))
        s3 = jnp.where(ok[None], s3, NEG_INF)
        m_new = jnp.maximum(m_i, jnp.max(s3, axis=-1, keepdims=True))
        alpha = jnp.exp(m_i - m_new)
        p = jnp.where(ok[None], jnp.exp(s3 - m_new), 0.0)
        l_new = alpha * l_i + jnp.sum(p, axis=-1, keepdims=True)
        pv = jnp.dot(p.reshape(rows, KC).astype(BF16), vblk, preferred_element_type=F32).reshape(REP, TQ, HEAD_B)
        return m_new, l_new, alpha * acc + pv

    n_chunks = (t0 + TQ + KC - 1) // KC
    init = (jnp.full((REP, TQ, 1), NEG_INF, F32), jnp.zeros((REP, TQ, 1), F32), jnp.zeros((REP, TQ, HEAD_B), F32))
    _, l_s, acc_s = lax.fori_loop(0, n_chunks, sel_chunk, init)
    o_s = (acc_s / l_s).reshape(rows, HEAD_B)

    span = WINDOW + TQ
    kwin = kwn_ref[pl.ds(t0, span), :]
    vwin = vwb_ref[pl.ds(t0, span), :]
    s3 = _dot_nt(q, kwin).reshape(REP, TQ, span)
    tk = t0 - WINDOW + lax.broadcasted_iota(jnp.int32, (TQ, span), 1)
    diff = t_of((TQ, span)) - tk
    ok_w = (diff >= 0) & (diff < WINDOW) & (tk >= 0)
    pr_w = softmax_pv(s3, ok_w, None)
    o_w = jnp.dot(pr_w.reshape(rows, span).astype(BF16), vwin, preferred_element_type=F32)

    gts = jax.nn.sigmoid(gt_ref[...])
    for r in range(REP):
        sl = slice(r * TQ, (r + 1) * TQ)
        o = (gts[:, 3 * r:3 * r + 1] * o_c[sl] + gts[:, 3 * r + 1:3 * r + 2] * o_s[sl]
             + gts[:, 3 * r + 2:3 * r + 3] * o_w[sl])
        z = z_ref[:, r * HEAD_B:(r + 1) * HEAD_B]
        o_ref[:, r * HEAD_B:(r + 1) * HEAD_B] = (o * (z * jax.nn.sigmoid(z))).astype(o_ref.dtype)


def nsa_attention(proj_b, cmp_kv, gates, q_gain, k_gain, overlap, expand, blocks, batch, seq):
    nq = seq // TQ
    ncmp = seq // CMP_STRIDE
    gw = REP * HEAD_B // LANES
    qspec = lambda off: pl.BlockSpec((TQ, REP * HEAD_B), lambda b, g, t: (b * nq + t, off // gw + g))
    kvspec = lambda off: pl.BlockSpec((seq, HEAD_B), lambda b, g, t: (b, off + g))
    cspec = lambda w: pl.BlockSpec((None, None, None, ncmp, HEAD_B), lambda b, g, t: (w, b, g, 0, 0))
    full = lambda a: pl.BlockSpec(a.shape, lambda b, g, t: (0,) * a.ndim)
    return pl.pallas_call(
        functools.partial(_attn_kernel, seq=seq),
        grid=(batch, KV_GROUPS, nq),
        in_specs=[
            qspec(blocks["q"]), kvspec(blocks["ks"]), kvspec(blocks["vs"]), kvspec(blocks["kw"]), kvspec(blocks["vw"]),
            cspec(0), cspec(1),
            pl.BlockSpec((None, None, TQ, 16), lambda b, g, t: (b, g, t, 0)),
            qspec(blocks["z"]),
            full(q_gain), full(k_gain), full(overlap), full(expand),
        ],
        out_specs=pl.BlockSpec((TQ, REP * HEAD_B), lambda b, g, t: (b * nq + t, g)),
        out_shape=jax.ShapeDtypeStruct((batch * seq, KV_GROUPS * REP * HEAD_B), BF16),
        scratch_shapes=[
            pltpu.VMEM((seq, HEAD_B), BF16), pltpu.VMEM((seq, HEAD_B), BF16),
            pltpu.VMEM((seq + WINDOW, HEAD_B), BF16), pltpu.VMEM((seq + WINDOW, HEAD_B), BF16),
        ],
        compiler_params=_params(ndim=3),
        name="nsa_attention",
    )(proj_b, proj_b, proj_b, proj_b, proj_b, cmp_kv, cmp_kv, gates, proj_b, q_gain, k_gain, overlap, expand)


def _merge_kernel(o_ref, z_ref, yb_ref, wa_ref, wb_ref, ga_ref, gb_ref, out_ref, ya_ref):
    @pl.when(pl.program_id(1) == 0)
    def _():
        z = z_ref[...]
        ya_ref[...] = (o_ref[...] * (z * jax.nn.sigmoid(z))).astype(BF16)

    ua = jnp.dot(ya_ref[...], wa_ref[...], preferred_element_type=F32)
    ub = jnp.dot(yb_ref[...], wb_ref[...], preferred_element_type=F32)
    out_ref[...] = (jax.nn.sigmoid(ga_ref[...]) * ua + jax.nn.sigmoid(gb_ref[...]) * ub).astype(out_ref.dtype)


def merge(o_a, proj_a, z_block, yb, w_a, w_b, proj_b, ga_block, gb_block, tm=512, tn=512):
    m, ka = o_a.shape
    n = w_a.shape[1]
    return pl.pallas_call(
        _merge_kernel,
        grid=(m // tm, n // tn),
        in_specs=[
            pl.BlockSpec((tm, ka), lambda i, j: (i, 0)),
            pl.BlockSpec((tm, ka), lambda i, j: (i, z_block)),
            pl.BlockSpec((tm, ka), lambda i, j: (i, 0)),
            pl.BlockSpec((ka, tn), lambda i, j: (0, j)),
            pl.BlockSpec((ka, tn), lambda i, j: (0, j)),
            pl.BlockSpec((tm, tn), lambda i, j: (i, ga_block + j)),
            pl.BlockSpec((tm, tn), lambda i, j: (i, gb_block + j)),
        ],
        out_specs=pl.BlockSpec((tm, tn), lambda i, j: (i, j)),
        out_shape=jax.ShapeDtypeStruct((m, n), BF16),
        scratch_shapes=[pltpu.VMEM((tm, ka), BF16)],
        compiler_params=_params(ndim=2),
        name="merge",
    )(o_a, proj_a, yb, w_a, w_b, proj_b, proj_b)


def _ple_embed_kernel(p_ref, w_ref, g_ref, o_ref):
    acc = jnp.dot(p_ref[...].astype(BF16), w_ref[...], preferred_element_type=F32)
    o_ref[...] = _rms(acc, g_ref[...]).astype(o_ref.dtype)


def ple_embed(p, w, g, tm=256):
    m, k = p.shape
    n = w.shape[1]
    return pl.pallas_call(
        _ple_embed_kernel,
        grid=(m // tm,),
        in_specs=[pl.BlockSpec((tm, k), lambda i: (i, 0)), pl.BlockSpec((k, n), lambda i: (0, 0)),
                  pl.BlockSpec((1, n), lambda i: (0, 0))],
        out_specs=pl.BlockSpec((tm, n), lambda i: (i, 0)),
        out_shape=jax.ShapeDtypeStruct((m, n), F32),
        compiler_params=_params(),
        name="ple_embed",
    )(p, w, g.reshape(1, n))


def rwkv_branch(proj_a, batch, seq, a_width, w_lora_up, w0, a_lora_up, a0, k_k, k_a, r_k, lnx_w, lnx_b):
    tokens = batch * seq
    heads_a = a_width // HEAD_A
    lora = w_lora_up.shape[0]
    zeros = jnp.zeros((lora, a_width), F32)
    w_lora = jnp.concatenate([jnp.concatenate([w_lora_up, zeros], 1), jnp.concatenate([zeros, a_lora_up], 1)], 0)
    b_lora = jnp.concatenate([w0, a0]).reshape(1, -1)
    wa_pre = lora_project(proj_a, 4 * a_width // LANES, w_lora, b_lora)

    def to_scan(t):
        return t.reshape(batch, seq, heads_a, HEAD_A).transpose(1, 3, 0, 2).reshape(seq, HEAD_A, batch * heads_a)

    def par_scan(t):
        return jnp.tile(t.reshape(heads_a, HEAD_A).T, (1, batch))

    o_scan = rwkv_scan(
        to_scan(proj_a[:, :a_width]), to_scan(proj_a[:, a_width:2 * a_width]),
        to_scan(proj_a[:, 2 * a_width:3 * a_width]), to_scan(wa_pre[:, :a_width]), to_scan(wa_pre[:, a_width:]),
        par_scan(k_k), par_scan(k_a), par_scan(r_k), par_scan(lnx_w), par_scan(lnx_b))
    return o_scan.reshape(seq, HEAD_A, batch, heads_a).transpose(2, 0, 3, 1).reshape(tokens, a_width)


def nsa_branch(proj_b, graw, batch, seq, b_width, q_norm_g, k_norm_g, pe_cmp_k, pe_cmp_v,
               cmp_k_w1, cmp_k_w2, cmp_v_w1, cmp_v_w2):
    kv_width = KV_GROUPS * HEAD_B
    blk = lambda cols: cols // LANES
    blocks = {"q": 0, "kc": blk(b_width), "vc": blk(b_width + kv_width), "ks": blk(b_width + 2 * kv_width),
              "vs": blk(b_width + 3 * kv_width), "kw": blk(b_width + 4 * kv_width), "vw": blk(b_width + 5 * kv_width),
              "z": blk(b_width + 6 * kv_width)}
    pe = jnp.stack([pe_cmp_k, pe_cmp_v])
    hidden = cmp_k_w1.shape[1]
    w1 = jnp.stack([cmp_k_w1, cmp_v_w1]).reshape(2, L_CMP, HEAD_B, hidden).astype(BF16)
    w2 = jnp.stack([cmp_k_w2, cmp_v_w2]).astype(BF16)
    cmp_kv = compress(proj_b, blocks["kc"], blocks["vc"], pe, w1, w2, k_norm_g[0:1], batch, seq)

    ncmp = seq // CMP_STRIDE
    nsel = seq // L_SEL
    c_start = jnp.arange(ncmp) * CMP_STRIDE
    s_start = jnp.arange(LANES) * L_SEL
    overlap = ((c_start[:, None] < (s_start + L_SEL)[None, :]) & (s_start[None, :] < (c_start + L_CMP)[:, None])
               & (jnp.arange(LANES)[None, :] < nsel) & (jnp.arange(ncmp)[:, None] < ncmp - 1)).astype(F32)
    key_blk = jnp.arange(seq) // L_SEL
    expand = (jnp.arange(LANES)[:, None] == key_blk[None, :]).astype(BF16)
    expand = expand.reshape(LANES, seq // KC, KC).transpose(1, 0, 2)
    gates = graw.reshape(batch, seq, KV_GROUPS, REP * 3).transpose(0, 2, 1, 3)
    gates = jnp.pad(gates, ((0, 0), (0, 0), (0, 0), (0, 16 - REP * 3)))
    return nsa_attention(proj_b, cmp_kv, gates, q_norm_g.reshape(1, HEAD_B), k_norm_g, overlap, expand,
                         blocks, batch, seq)


def _layer(x, p, norm_g, w_in, shift_mu, w_lora_up, w0, a_lora_up, a0, k_k, k_a, r_k, lnx_w, lnx_b,
           q_norm_g, k_norm_g, pe_cmp_k, pe_cmp_v, cmp_k_w1, cmp_k_w2, cmp_v_w1, cmp_v_w2,
           w_up_a, w_up_b, w_out, ple_pre_g, w_ple_gate, w_ple, ple_post_g):
    batch, seq, d = x.shape
    tokens = batch * seq
    a_width = w_up_a.shape[0]
    b_width = w_up_b.shape[0]
    heads_a = a_width // HEAD_A
    kv_width = KV_GROUPS * HEAD_B
    n_gate = KV_GROUPS * REP * 3
    lora = w_lora_up.shape[0]
    assert lora == HEAD_A and a_lora_up.shape[0] == HEAD_A and 2 * lora == LANES
    a_cols = 4 * a_width + 2 * lora
    g_off = a_cols + b_width + 6 * kv_width
    zb_off = g_off + n_gate
    assert w_in.shape[1] == zb_off + b_width + 2 * d

    pad_a = LANES - n_gate
    w_a = jnp.concatenate([w_in[:, :a_cols], w_in[:, g_off:zb_off], jnp.zeros((d, pad_a), F32)], axis=1).astype(BF16)
    mu_a = jnp.concatenate([shift_mu, jnp.zeros((LANES,), F32)]).reshape(1, -1)
    w_b = jnp.concatenate([w_in[:, a_cols:g_off], w_in[:, zb_off:]], axis=1).astype(BF16)
    na = w_a.shape[1]
    tn_a = 768 if na % 768 == 0 else LANES

    x2 = x.reshape(tokens, d)
    h = rmsnorm_rows(x2, norm_g, BF16)
    tm = 512
    proj_a = matmul(h, w_a, mode="shift", extras=(mu_a,), tm=tm, tn=tn_a, tiles_per_seq=seq // tm, name="proj_a")
    proj_b = matmul(h, w_b, tm=tm, tn=1024, name="proj_b")

    o_a = rwkv_branch(proj_a, batch, seq, a_width, w_lora_up, w0, a_lora_up, a0, k_k, k_a, r_k, lnx_w, lnx_b)
    graw = proj_a[:, a_cols:a_cols + n_gate]
    yb = nsa_branch(proj_b, graw, batch, seq, b_width, q_norm_g, k_norm_g, pe_cmp_k, pe_cmp_v,
                    cmp_k_w1, cmp_k_w2, cmp_v_w1, cmp_v_w2)

    gate_blk = (b_width + 6 * kv_width + b_width) // 512
    merged = merge(o_a, proj_a, 3, yb, w_up_a.astype(BF16), w_up_b.astype(BF16), proj_b,
                   gate_blk, gate_blk + d // 512)
    x1 = matmul(merged, w_out.astype(BF16), mode="resid", extras=(x2,), tm=tm, tn=1024, name="out_proj")
    hn = rmsnorm_rows(x1, ple_pre_g, BF16)
    e = ple_embed(p.reshape(tokens, -1), w_ple.astype(BF16), ple_post_g)
    out = matmul(hn, w_ple_gate.astype(BF16), mode="ple", extras=(x1, e), tm=tm, tn=1024, name="ple_gate")
    return out.reshape(batch, seq, d)


def kernel(x, p, norm_g, w_in, shift_mu, w_lora_up, w0, a_lora_up, a0, k_k, k_a, r_k, lnx_w, lnx_b, q_norm_g, k_norm_g, pe_cmp_k, pe_cmp_v, cmp_k_w1, cmp_k_w2, cmp_v_w1, cmp_v_w2, w_up_a, w_up_b, w_out, ple_pre_g, w_ple_gate, w_ple, ple_post_g):
    depth = w_in.shape[0]
    for i in range(depth):
        x = _layer(x, p[i], norm_g[i], w_in[i], shift_mu[i], w_lora_up[i], w0[i], a_lora_up[i], a0[i], k_k[i],
                   k_a[i], r_k[i], lnx_w[i], lnx_b[i], q_norm_g[i], k_norm_g[i], pe_cmp_k[i], pe_cmp_v[i],
                   cmp_k_w1[i], cmp_k_w2[i], cmp_v_w1[i], cmp_v_w2[i], w_up_a[i], w_up_b[i], w_out[i],
                   ple_pre_g[i], w_ple_gate[i], w_ple[i], ple_post_g[i])
    return x
```

```python
import functools

import jax
import jax.numpy as jnp
from jax import lax
from jax.experimental import pallas as pl
from jax.experimental.pallas import tpu as pltpu

F32 = jnp.float32
BF16 = jnp.bfloat16

LANES = 128
SUBLANES = 8
VMEM_LIMIT = 56 * 1024 * 1024

NORM_EPS = 1e-6
NEG_INF = -1e30
HEAD_A = 64
GN_EPS = 64e-5
HEAD_B = 128
KV_GROUPS = 4
REP = 4
L_CMP = 32
CMP_STRIDE = 16
L_SEL = 64
N_SEL = 16
WINDOW = 512
TQ = 128
KC = 256
FORCE_BONUS = 1e3
ATTN_SCALE = HEAD_B ** -0.5
DECAY_SCALE = 0.6065306597126334


def _params(vmem=VMEM_LIMIT, ndim=1):
    return pltpu.CompilerParams(dimension_semantics=("arbitrary",) * ndim, vmem_limit_bytes=vmem)


def _rmsnorm_kernel(x_ref, g_ref, o_ref):
    x = x_ref[...]
    ms = jnp.mean(x * x, axis=-1, keepdims=True)
    o_ref[...] = (x * lax.rsqrt(ms + NORM_EPS) * g_ref[...]).astype(o_ref.dtype)


def rmsnorm_rows(x, g, out_dtype, tm=256):
    m, d = x.shape
    return pl.pallas_call(
        _rmsnorm_kernel,
        grid=(m // tm,),
        in_specs=[pl.BlockSpec((tm, d), lambda i: (i, 0)), pl.BlockSpec((1, d), lambda i: (0, 0))],
        out_specs=pl.BlockSpec((tm, d), lambda i: (i, 0)),
        out_shape=jax.ShapeDtypeStruct((m, d), out_dtype),
        compiler_params=_params(),
        name="rmsnorm",
    )(x, g.reshape(1, d))


def _mm_kernel(*refs, mode, tiles_per_seq):
    a_ref, b_ref = refs[0], refs[1]
    acc = jnp.dot(a_ref[...], b_ref[...], preferred_element_type=F32)
    if mode == "plain":
        o_ref = refs[2]
        o_ref[...] = acc.astype(o_ref.dtype)
    elif mode == "shift":
        mu_ref, o_ref, carry_ref = refs[2], refs[3], refs[4]
        i = pl.program_id(1)
        tm = acc.shape[0]
        first = (i % tiles_per_seq) == 0
        last_prev = jnp.where(first, 0.0, carry_ref[SUBLANES - 1:SUBLANES, :])
        rolled = pltpu.roll(acc, 1, axis=0)
        row = lax.broadcasted_iota(jnp.int32, acc.shape, 0)
        prev = jnp.where(row == 0, last_prev, rolled)
        carry_ref[...] = acc[tm - SUBLANES:tm, :]
        o_ref[...] = (acc + mu_ref[...] * (prev - acc)).astype(o_ref.dtype)
    elif mode == "resid":
        r_ref, o_ref = refs[2], refs[3]
        o_ref[...] = (r_ref[...] + acc).astype(o_ref.dtype)
    elif mode == "ple":
        x_ref, e_ref, o_ref = refs[2], refs[3], refs[4]
        o_ref[...] = (x_ref[...] + jax.nn.sigmoid(acc) * e_ref[...]).astype(o_ref.dtype)
    else:
        raise ValueError(mode)


def matmul(a, b, *, mode="plain", extras=(), out_dtype=F32, tm=512, tn=1024, tiles_per_seq=1, name="mm"):
    m, k = a.shape
    n = b.shape[1]
    assert m % tm == 0 and n % tn == 0, (m, n, tm, tn)
    in_specs = [pl.BlockSpec((tm, k), lambda j, i: (i, 0)), pl.BlockSpec((k, tn), lambda j, i: (0, j))]
    scratch = []
    if mode == "shift":
        in_specs.append(pl.BlockSpec((1, tn), lambda j, i: (0, j)))
        scratch.append(pltpu.VMEM((SUBLANES, tn), F32))
    elif mode == "resid":
        in_specs.append(pl.BlockSpec((tm, tn), lambda j, i: (i, j)))
    elif mode == "ple":
        in_specs += [pl.BlockSpec((tm, tn), lambda j, i: (i, j)), pl.BlockSpec((tm, tn), lambda j, i: (i, j))]
    return pl.pallas_call(
        functools.partial(_mm_kernel, mode=mode, tiles_per_seq=tiles_per_seq),
        grid=(n // tn, m // tm),
        in_specs=in_specs,
        out_specs=pl.BlockSpec((tm, tn), lambda j, i: (i, j)),
        out_shape=jax.ShapeDtypeStruct((m, n), out_dtype),
        scratch_shapes=scratch,
        compiler_params=_params(ndim=2),
        name=name,
    )(a, b, *extras)


def _lora_kernel(x_ref, w_ref, b_ref, o_ref):
    x = x_ref[...]
    lane = lax.broadcasted_iota(jnp.int32, x.shape, 1)
    x = jnp.where(lane < HEAD_A, jnp.tanh(x), x)
    acc = jnp.dot(x, w_ref[...], preferred_element_type=F32, precision=lax.Precision.HIGHEST)
    o_ref[...] = acc + b_ref[...]


def lora_project(proj_a, col_block, w_blockdiag, bias, tm=1024, tn=1024):
    m = proj_a.shape[0]
    n = w_blockdiag.shape[1]
    tm = min(tm, m)
    assert m % tm == 0 and n % tn == 0
    return pl.pallas_call(
        _lora_kernel,
        grid=(m // tm, n // tn),
        in_specs=[
            pl.BlockSpec((tm, LANES), lambda i, j: (i, col_block)),
            pl.BlockSpec((LANES, tn), lambda i, j: (0, j)),
            pl.BlockSpec((1, tn), lambda i, j: (0, j)),
        ],
        out_specs=pl.BlockSpec((tm, tn), lambda i, j: (i, j)),
        out_shape=jax.ShapeDtypeStruct((m, n), F32),
        compiler_params=_params(ndim=2),
        name="lora",
    )(proj_a, w_blockdiag, bias)


SCAN_BATCH = 4
SCAN_HEADS = LANES // SCAN_BATCH
NGRP = LANES // SCAN_HEADS


def _segment_transpose(x):
    seg = lax.broadcasted_iota(jnp.int32, x[0].shape, 1) // SCAN_HEADS
    y = []
    for i in range(NGRP):
        out = None
        for j in range(SCAN_BATCH):
            shift = ((j - i) % NGRP) * SCAN_HEADS
            piece = x[j] if shift == 0 else pltpu.roll(x[j], shift, axis=1)
            out = piece if out is None else jnp.where(seg == j, piece, out)
        y.append(out)
    return y


def _to_lanes(x_ref, dst_ref, tc):
    for g in range(HEAD_A // NGRP):
        y = _segment_transpose([x_ref[b, :, g * LANES:(g + 1) * LANES] for b in range(SCAN_BATCH)])
        for n_lo in range(NGRP):
            n = g * NGRP + n_lo
            dst_ref[n * tc:(n + 1) * tc, :] = y[n_lo]


def _from_lanes(src_ref, o_ref, tc):
    for g in range(HEAD_A // NGRP):
        y = _segment_transpose([src_ref[(g * NGRP + n_lo) * tc:(g * NGRP + n_lo + 1) * tc, :] for n_lo in range(NGRP)])
        for b in range(SCAN_BATCH):
            o_ref[b, :, g * LANES:(g + 1) * LANES] = y[b]


def _scan_kernel(r_ref, k_ref, v_ref, wp_ref, ap_ref, kkw_ref, kaw_ref, rkw_ref, lnw_ref, lnb_ref,
                 o_ref, state_ref, rs, ks, vs, ws, ans, bvs, os, bon, *, tc):
    n_ch = HEAD_A

    @pl.when(pl.program_id(1) == 0)
    def _():
        state_ref[...] = jnp.zeros_like(state_ref)

    _to_lanes(r_ref, rs, tc)
    _to_lanes(k_ref, ks, tc)
    _to_lanes(v_ref, vs, tc)
    _to_lanes(wp_ref, ws, tc)
    _to_lanes(ap_ref, bvs, tc)

    def rows(n):
        return pl.ds(pl.multiple_of(n * tc, tc), tc)

    def norm_acc(n, acc):
        kkr = ks[rows(n), :] * kkw_ref[n]
        return acc + kkr * kkr

    nsq = lax.fori_loop(0, n_ch, norm_acc, jnp.zeros((tc, LANES), F32), unroll=8)
    inv = 1.0 / jnp.maximum(jnp.sqrt(nsq), 1e-12)

    def prep(n, bacc):
        k = ks[rows(n), :]
        a = jax.nn.sigmoid(bvs[rows(n), :])
        kk = k * kkw_ref[n] * inv
        ws[rows(n), :] = jnp.exp(-DECAY_SCALE * jax.nn.sigmoid(ws[rows(n), :]))
        ans[rows(n), :] = -kk
        bvs[rows(n), :] = kk * a
        kmod = k * (1.0 + (a - 1.0) * kaw_ref[n])
        ks[rows(n), :] = kmod
        return bacc + rs[rows(n), :] * kmod * rkw_ref[n]

    bon[...] = lax.fori_loop(0, n_ch, prep, jnp.zeros((tc, LANES), F32), unroll=4)

    def bcast(ref, row):
        return ref[pl.ds(row, 1), :][None]

    def sa_first(j, acc):
        return acc + state_ref[j] * bcast(ans, j * tc)

    slab = (n_ch // SUBLANES, SUBLANES, LANES)
    sa0 = lax.fori_loop(0, n_ch, sa_first, jnp.zeros(slab, F32), unroll=4)

    def step(t, sa):
        v = vs[pl.ds(t, n_ch, stride=tc), :].reshape(slab)
        t_next = jnp.minimum(t + 1, tc - 1)

        def jbody(j, carry):
            out, sa_next = carry
            row = j * tc + t
            s_new = state_ref[j] * bcast(ws, row) + sa * bcast(bvs, row) + v * bcast(ks, row)
            state_ref[j] = s_new
            return out + s_new * bcast(rs, row), sa_next + s_new * bcast(ans, j * tc + t_next)

        out, sa_next = lax.fori_loop(0, n_ch, jbody, (jnp.zeros(slab, F32), jnp.zeros(slab, F32)), unroll=4)
        o = out.reshape(n_ch, LANES)
        mu = jnp.mean(o, axis=0, keepdims=True)
        d = o - mu
        var = jnp.mean(d * d, axis=0, keepdims=True)
        on = d * lax.rsqrt(var + GN_EPS) * lnw_ref[...] + lnb_ref[...]
        bonus = bon[pl.ds(t, 1), :] * v.reshape(n_ch, LANES)
        os[pl.ds(t, n_ch, stride=tc), :] = on + bonus
        return sa_next

    lax.fori_loop(0, tc, step, sa0)
    _from_lanes(os, o_ref, tc)


def rwkv_scan(proj_a, wa_pre, kkw, kaw, rkw, lnw, lnb, batch, seq, tc=32):
    width = HEAD_A * SCAN_HEADS
    seqb = lambda col: pl.BlockSpec((SCAN_BATCH, tc, width), lambda g, c: (g, c, col))
    par3 = pl.BlockSpec((HEAD_A, 1, LANES), lambda g, c: (0, 0, 0))
    par2 = pl.BlockSpec((HEAD_A, LANES), lambda g, c: (0, 0))
    buf = pltpu.VMEM((HEAD_A * tc, LANES), F32)
    return pl.pallas_call(
        functools.partial(_scan_kernel, tc=tc),
        grid=(batch // SCAN_BATCH, seq // tc),
        in_specs=[seqb(0), seqb(1), seqb(2), seqb(0), seqb(1), par3, par3, par3, par2, par2],
        out_specs=seqb(0),
        out_shape=jax.ShapeDtypeStruct((batch, seq, width), F32),
        scratch_shapes=[pltpu.VMEM((HEAD_A, HEAD_A // SUBLANES, SUBLANES, LANES), F32)] + [buf] * 7
        + [pltpu.VMEM((tc, LANES), F32)],
        compiler_params=_params(ndim=2),
        name="rwkv_scan",
    )(proj_a, proj_a, proj_a, wa_pre, wa_pre, kkw, kaw, rkw, lnw, lnb)


def _compress_kernel(x_ref, pe_ref, w1_ref, w2_ref, kg_ref, o_ref):
    which = pl.program_id(0)
    half = L_CMP // 2
    nblk = x_ref.shape[0] // CMP_STRIDE
    h1 = jnp.zeros((nblk, w1_ref.shape[-1]), F32)
    h2 = jnp.zeros((nblk, w1_ref.shape[-1]), F32)
    for l in range(half):
        x = x_ref[pl.ds(l, nblk, stride=CMP_STRIDE), :]
        a1 = (x + pe_ref[l:l + 1, :]).astype(BF16)
        a2 = (x + pe_ref[half + l:half + l + 1, :]).astype(BF16)
        h1 = h1 + jnp.dot(a1, w1_ref[l], preferred_element_type=F32)
        h2 = h2 + jnp.dot(a2, w1_ref[half + l], preferred_element_type=F32)
    hid = h1 + pltpu.roll(h2, nblk - 1, axis=0)
    hid = jax.nn.gelu(hid)
    out = jnp.dot(hid.astype(BF16), w2_ref[...], preferred_element_type=F32)
    ms = jnp.mean(out * out, axis=-1, keepdims=True)
    normed = out * lax.rsqrt(ms + NORM_EPS) * kg_ref[...]
    out = jnp.where(which == 0, normed, out)
    row = lax.broadcasted_iota(jnp.int32, out.shape, 0)
    o_ref[...] = jnp.where(row < nblk - 1, out, 0.0)


def compress(proj_b, kc_block, vc_block, pe, w1, w2, kgain, batch, seq):
    nblk = seq // CMP_STRIDE
    hidden = w1.shape[-1]

    def xmap(w, b, g):
        return (b, kc_block + w * (vc_block - kc_block) + g)

    return pl.pallas_call(
        _compress_kernel,
        grid=(2, batch, KV_GROUPS),
        in_specs=[
            pl.BlockSpec((seq, HEAD_B), xmap),
            pl.BlockSpec((None, L_CMP, HEAD_B), lambda w, b, g: (w, 0, 0)),
            pl.BlockSpec((None, L_CMP, HEAD_B, hidden), lambda w, b, g: (w, 0, 0, 0)),
            pl.BlockSpec((None, hidden, HEAD_B), lambda w, b, g: (w, 0, 0)),
            pl.BlockSpec((1, HEAD_B), lambda w, b, g: (0, 0)),
        ],
        out_specs=pl.BlockSpec((None, None, None, nblk, HEAD_B), lambda w, b, g: (w, b, g, 0, 0)),
        out_shape=jax.ShapeDtypeStruct((2, batch, KV_GROUPS, nblk, HEAD_B), F32),
        compiler_params=_params(ndim=3),
        name="nsa_compress",
    )(proj_b, pe, w1, w2, kgain)


def _rms(x, gain):
    ms = jnp.mean(x * x, axis=-1, keepdims=True)
    return x * lax.rsqrt(ms + NORM_EPS) * gain


def _dot_nt(a, b):
    return lax.dot_general(a, b, (((1,), (1,)), ((), ())), preferred_element_type=F32)


def _attn_kernel(q_ref, ks_ref, vs_ref, kw_ref, vw_ref, kc_ref, vc_ref, gt_ref, z_ref, qg_ref, kg_ref,
                 ov_ref, ex_ref, o_ref, ksn_ref, vsb_ref, kwn_ref, vwb_ref, *, seq):
    qt = pl.program_id(2)
    rows = REP * TQ
    ncmp = kc_ref.shape[0]

    @pl.when(qt == 0)
    def _():
        ksn_ref[...] = _rms(ks_ref[...], kg_ref[1:2, :]).astype(BF16)
        vsb_ref[...] = vs_ref[...].astype(BF16)
        kwn_ref[0:WINDOW, :] = jnp.zeros((WINDOW, HEAD_B), BF16)
        vwb_ref[0:WINDOW, :] = jnp.zeros((WINDOW, HEAD_B), BF16)
        kwn_ref[WINDOW:WINDOW + seq, :] = _rms(kw_ref[...], kg_ref[2:3, :]).astype(BF16)
        vwb_ref[WINDOW:WINDOW + seq, :] = vw_ref[...].astype(BF16)

    t0 = pl.multiple_of(qt * TQ, TQ)
    q = jnp.concatenate(
        [_rms(q_ref[:, r * HEAD_B:(r + 1) * HEAD_B], qg_ref[...]) * ATTN_SCALE for r in range(REP)], axis=0
    ).astype(BF16)

    def t_of(shape):
        return t0 + lax.broadcasted_iota(jnp.int32, shape, 0)

    def softmax_pv(s3, ok, v):
        s3 = jnp.where(ok[None], s3, NEG_INF)
        m = jnp.max(s3, axis=-1, keepdims=True)
        p = jnp.where(ok[None], jnp.exp(s3 - m), 0.0)
        l = jnp.sum(p, axis=-1, keepdims=True)
        pr = p / jnp.maximum(l, 1e-30)
        return pr

    s = _dot_nt(q, kc_ref[...].astype(BF16)).reshape(REP, TQ, ncmp)
    n_idx = lax.broadcasted_iota(jnp.int32, (TQ, ncmp), 1)
    ok_c = (n_idx * CMP_STRIDE + (L_CMP - 1) <= t_of((TQ, ncmp))) & (n_idx < ncmp - 1)
    pr_c = softmax_pv(s, ok_c, None)
    o_c = jnp.dot(pr_c.reshape(rows, ncmp).astype(BF16), vc_ref[...].astype(BF16), preferred_element_type=F32)
    psum = pr_c[0] + pr_c[1] + pr_c[2] + pr_c[3]
    imp = jnp.dot(psum, ov_ref[...], preferred_element_type=F32, precision=lax.Precision.HIGHEST)

    nsel = seq // L_SEL
    j_idx = lax.broadcasted_iota(jnp.int32, (TQ, LANES), 1)
    t_sel = t_of((TQ, LANES))
    cur = t_sel // L_SEL
    forced = ((j_idx == 0) | (j_idx == cur) | (j_idx == cur - 1)).astype(F32)
    causal = (j_idx * L_SEL <= t_sel) & (j_idx < nsel)
    score = jnp.where(causal, imp + FORCE_BONUS * forced, NEG_INF)
    rank = jnp.zeros((TQ, LANES), F32)
    for i in range(nsel):
        si = score[:, i:i + 1]
        ahead = (si > score) | ((si == score) & (j_idx > i))
        rank = rank + jnp.where(ahead, 1.0, 0.0)
    sel = jnp.where((rank < min(N_SEL, nsel)) & (score > 0.5 * NEG_INF), 1.0, 0.0).astype(BF16)

    def sel_chunk(c, carry):
        m_i, l_i, acc = carry
        k0 = pl.multiple_of(c * KC, KC)
        kblk = ksn_ref[pl.ds(k0, KC), :]
        vblk = vsb_ref[pl.ds(k0, KC), :]
        s3 = _dot_nt(q, kblk).reshape(REP, TQ, KC)
        member = jnp.dot(sel, ex_ref[c], preferred_element_type=F32)
        u_idx = k0 + lax.broadcasted_iota(jnp.int32, (TQ, KC), 1)
        ok = (member > 0.5) & (u_idx <= t_of((TQ, KC)))
        s3 = jnp.where(ok[None], s3, NEG_INF)
        m_new = jnp.maximum(m_i, jnp.max(s3, axis=-1, keepdims=True))
        alpha = jnp.exp(m_i - m_new)
        p = jnp.where(ok[None], jnp.exp(s3 - m_new), 0.0)
        l_new = alpha * l_i + jnp.sum(p, axis=-1, keepdims=True)
        pv = jnp.dot(p.reshape(rows, KC).astype(BF16), vblk, preferred_element_type=F32).reshape(REP, TQ, HEAD_B)
        return m_new, l_new, alpha * acc + pv

    n_chunks = (t0 + TQ + KC - 1) // KC
    init = (jnp.full((REP, TQ, 1), NEG_INF, F32), jnp.zeros((REP, TQ, 1), F32), jnp.zeros((REP, TQ, HEAD_B), F32))
    _, l_s, acc_s = lax.fori_loop(0, n_chunks, sel_chunk, init)
    o_s = (acc_s / l_s).reshape(rows, HEAD_B)

    span = WINDOW + TQ
    kwin = kwn_ref[pl.ds(t0, span), :]
    vwin = vwb_ref[pl.ds(t0, span), :]
    s3 = _dot_nt(q, kwin).reshape(REP, TQ, span)
    tk = t0 - WINDOW + lax.broadcasted_iota(jnp.int32, (TQ, span), 1)
    diff = t_of((TQ, span)) - tk
    ok_w = (diff >= 0) & (diff < WINDOW) & (tk >= 0)
    pr_w = softmax_pv(s3, ok_w, None)
    o_w = jnp.dot(pr_w.reshape(rows, span).astype(BF16), vwin, preferred_element_type=F32)

    gts = jax.nn.sigmoid(gt_ref[...])
    for r in range(REP):
        sl = slice(r * TQ, (r + 1) * TQ)
        o = (gts[:, 3 * r:3 * r + 1] * o_c[sl] + gts[:, 3 * r + 1:3 * r + 2] * o_s[sl]
             + gts[:, 3 * r + 2:3 * r + 3] * o_w[sl])
        z = z_ref[:, r * HEAD_B:(r + 1) * HEAD_B]
        o_ref[:, r * HEAD_B:(r + 1) * HEAD_B] = (o * (z * jax.nn.sigmoid(z))).astype(o_ref.dtype)


def nsa_attention(proj_b, cmp_kv, gates, q_gain, k_gain, overlap, expand, blocks, batch, seq):
    nq = seq // TQ
    ncmp = seq // CMP_STRIDE
    gw = REP * HEAD_B // LANES
    qspec = lambda off: pl.BlockSpec((TQ, REP * HEAD_B), lambda b, g, t: (b * nq + t, off // gw + g))
    kvspec = lambda off: pl.BlockSpec((seq, HEAD_B), lambda b, g, t: (b, off + g))
    cspec = lambda w: pl.BlockSpec((None, None, None, ncmp, HEAD_B), lambda b, g, t: (w, b, g, 0, 0))
    full = lambda a: pl.BlockSpec(a.shape, lambda b, g, t: (0,) * a.ndim)
    return pl.pallas_call(
        functools.partial(_attn_kernel, seq=seq),
        grid=(batch, KV_GROUPS, nq),
        in_specs=[
            qspec(blocks["q"]), kvspec(blocks["ks"]), kvspec(blocks["vs"]), kvspec(blocks["kw"]), kvspec(blocks["vw"]),
            cspec(0), cspec(1),
            pl.BlockSpec((None, None, TQ, 16), lambda b, g, t: (b, g, t, 0)),
            qspec(blocks["z"]),
            full(q_gain), full(k_gain), full(overlap), full(expand),
        ],
        out_specs=pl.BlockSpec((TQ, REP * HEAD_B), lambda b, g, t: (b * nq + t, g)),
        out_shape=jax.ShapeDtypeStruct((batch * seq, KV_GROUPS * REP * HEAD_B), BF16),
        scratch_shapes=[
            pltpu.VMEM((seq, HEAD_B), BF16), pltpu.VMEM((seq, HEAD_B), BF16),
            pltpu.VMEM((seq + WINDOW, HEAD_B), BF16), pltpu.VMEM((seq + WINDOW, HEAD_B), BF16),
        ],
        compiler_params=_params(ndim=3),
        name="nsa_attention",
    )(proj_b, proj_b, proj_b, proj_b, proj_b, cmp_kv, cmp_kv, gates, proj_b, q_gain, k_gain, overlap, expand)


def _merge_kernel(o_ref, z_ref, yb_ref, wa_ref, wb_ref, ga_ref, gb_ref, out_ref, ya_ref):
    @pl.when(pl.program_id(1) == 0)
    def _():
        z = z_ref[...]
        ya_ref[...] = (o_ref[...] * (z * jax.nn.sigmoid(z))).astype(BF16)

    ua = jnp.dot(ya_ref[...], wa_ref[...], preferred_element_type=F32)
    ub = jnp.dot(yb_ref[...], wb_ref[...], preferred_element_type=F32)
    out_ref[...] = (jax.nn.sigmoid(ga_ref[...]) * ua + jax.nn.sigmoid(gb_ref[...]) * ub).astype(out_ref.dtype)


def merge(o_a, proj_a, z_block, yb, w_a, w_b, proj_b, ga_block, gb_block, tm=512, tn=512):
    m, ka = o_a.shape
    n = w_a.shape[1]
    return pl.pallas_call(
        _merge_kernel,
        grid=(m // tm, n // tn),
        in_specs=[
            pl.BlockSpec((tm, ka), lambda i, j: (i, 0)),
            pl.BlockSpec((tm, ka), lambda i, j: (i, z_block)),
            pl.BlockSpec((tm, ka), lambda i, j: (i, 0)),
            pl.BlockSpec((ka, tn), lambda i, j: (0, j)),
            pl.BlockSpec((ka, tn), lambda i, j: (0, j)),
            pl.BlockSpec((tm, tn), lambda i, j: (i, ga_block + j)),
            pl.BlockSpec((tm, tn), lambda i, j: (i, gb_block + j)),
        ],
        out_specs=pl.BlockSpec((tm, tn), lambda i, j: (i, j)),
        out_shape=jax.ShapeDtypeStruct((m, n), BF16),
        scratch_shapes=[pltpu.VMEM((tm, ka), BF16)],
        compiler_params=_params(ndim=2),
        name="merge",
    )(o_a, proj_a, yb, w_a, w_b, proj_b, proj_b)


def _ple_embed_kernel(p_ref, w_ref, g_ref, o_ref):
    acc = jnp.dot(p_ref[...].astype(BF16), w_ref[...], preferred_element_type=F32)
    o_ref[...] = _rms(acc, g_ref[...]).astype(o_ref.dtype)


def ple_embed(p, w, g, tm=256):
    m, k = p.shape
    n = w.shape[1]
    return pl.pallas_call(
        _ple_embed_kernel,
        grid=(m // tm,),
        in_specs=[pl.BlockSpec((tm, k), lambda i: (i, 0)), pl.BlockSpec((k, n), lambda i: (0, 0)),
                  pl.BlockSpec((1, n), lambda i: (0, 0))],
        out_specs=pl.BlockSpec((tm, n), lambda i: (i, 0)),
        out_shape=jax.ShapeDtypeStruct((m, n), F32),
        compiler_params=_params(),
        name="ple_embed",
    )(p, w, g.reshape(1, n))


def head_minor(w, heads):
    lead = w.shape[:-1]
    return w.reshape(*lead, heads, HEAD_A).swapaxes(-1, -2).reshape(*lead, heads * HEAD_A)


def rwkv_branch(proj_a, batch, seq, a_width, w_lora_up, w0, a_lora_up, a0, k_k, k_a, r_k, lnx_w, lnx_b):
    tokens = batch * seq
    heads_a = a_width // HEAD_A
    assert heads_a == SCAN_HEADS and batch % SCAN_BATCH == 0
    lora = w_lora_up.shape[0]
    zeros = jnp.zeros((lora, a_width), F32)
    hm = lambda w: head_minor(w, heads_a)
    w_lora = jnp.concatenate([jnp.concatenate([hm(w_lora_up), zeros], 1),
                              jnp.concatenate([zeros, hm(a_lora_up)], 1)], 0)
    b_lora = jnp.concatenate([hm(w0), hm(a0)]).reshape(1, -1)
    wa_pre = lora_project(proj_a, 4 * a_width // LANES, w_lora, b_lora)

    def par_scan(t):
        return jnp.tile(t.reshape(heads_a, HEAD_A).T, (1, SCAN_BATCH))

    par3 = lambda t: par_scan(t).reshape(HEAD_A, 1, LANES)
    o = rwkv_scan(proj_a.reshape(batch, seq, -1), wa_pre.reshape(batch, seq, -1),
                  par3(k_k), par3(k_a), par3(r_k), par_scan(lnx_w), par_scan(lnx_b), batch, seq)
    return o.reshape(tokens, a_width)


def nsa_branch(proj_b, graw, batch, seq, b_width, q_norm_g, k_norm_g, pe_cmp_k, pe_cmp_v,
               cmp_k_w1, cmp_k_w2, cmp_v_w1, cmp_v_w2):
    kv_width = KV_GROUPS * HEAD_B
    blk = lambda cols: cols // LANES
    blocks = {"q": 0, "kc": blk(b_width), "vc": blk(b_width + kv_width), "ks": blk(b_width + 2 * kv_width),
              "vs": blk(b_width + 3 * kv_width), "kw": blk(b_width + 4 * kv_width), "vw": blk(b_width + 5 * kv_width),
              "z": blk(b_width + 6 * kv_width)}
    pe = jnp.stack([pe_cmp_k, pe_cmp_v])
    hidden = cmp_k_w1.shape[1]
    w1 = jnp.stack([cmp_k_w1, cmp_v_w1]).reshape(2, L_CMP, HEAD_B, hidden).astype(BF16)
    w2 = jnp.stack([cmp_k_w2, cmp_v_w2]).astype(BF16)
    cmp_kv = compress(proj_b, blocks["kc"], blocks["vc"], pe, w1, w2, k_norm_g[0:1], batch, seq)

    ncmp = seq // CMP_STRIDE
    nsel = seq // L_SEL
    c_start = jnp.arange(ncmp) * CMP_STRIDE
    s_start = jnp.arange(LANES) * L_SEL
    overlap = ((c_start[:, None] < (s_start + L_SEL)[None, :]) & (s_start[None, :] < (c_start + L_CMP)[:, None])
               & (jnp.arange(LANES)[None, :] < nsel) & (jnp.arange(ncmp)[:, None] < ncmp - 1)).astype(F32)
    key_blk = jnp.arange(seq) // L_SEL
    expand = (jnp.arange(LANES)[:, None] == key_blk[None, :]).astype(BF16)
    expand = expand.reshape(LANES, seq // KC, KC).transpose(1, 0, 2)
    gates = graw.reshape(batch, seq, KV_GROUPS, REP * 3).transpose(0, 2, 1, 3)
    gates = jnp.pad(gates, ((0, 0), (0, 0), (0, 0), (0, 16 - REP * 3)))
    return nsa_attention(proj_b, cmp_kv, gates, q_norm_g.reshape(1, HEAD_B), k_norm_g, overlap, expand,
                         blocks, batch, seq)


def _layer(x, p, norm_g, w_in, shift_mu, w_lora_up, w0, a_lora_up, a0, k_k, k_a, r_k, lnx_w, lnx_b,
           q_norm_g, k_norm_g, pe_cmp_k, pe_cmp_v, cmp_k_w1, cmp_k_w2, cmp_v_w1, cmp_v_w2,
           w_up_a, w_up_b, w_out, ple_pre_g, w_ple_gate, w_ple, ple_post_g):
    batch, seq, d = x.shape
    tokens = batch * seq
    a_width = w_up_a.shape[0]
    b_width = w_up_b.shape[0]
    heads_a = a_width // HEAD_A
    kv_width = KV_GROUPS * HEAD_B
    n_gate = KV_GROUPS * REP * 3
    lora = w_lora_up.shape[0]
    assert lora == HEAD_A and a_lora_up.shape[0] == HEAD_A and 2 * lora == LANES
    a_cols = 4 * a_width + 2 * lora
    g_off = a_cols + b_width + 6 * kv_width
    zb_off = g_off + n_gate
    assert w_in.shape[1] == zb_off + b_width + 2 * d

    pad_a = LANES - n_gate
    hm4 = lambda w: head_minor(w.reshape(*w.shape[:-1], 4, a_width), heads_a).reshape(*w.shape[:-1], 4 * a_width)
    w_a = jnp.concatenate([hm4(w_in[:, :4 * a_width]), w_in[:, 4 * a_width:a_cols], w_in[:, g_off:zb_off],
                           jnp.zeros((d, pad_a), F32)], axis=1).astype(BF16)
    mu_a = jnp.concatenate([hm4(shift_mu[:4 * a_width]), shift_mu[4 * a_width:], jnp.zeros((LANES,), F32)]).reshape(1, -1)
    w_b = jnp.concatenate([w_in[:, a_cols:g_off], w_in[:, zb_off:]], axis=1).astype(BF16)
    na = w_a.shape[1]
    tn_a = 768 if na % 768 == 0 else LANES

    x2 = x.reshape(tokens, d)
    h = rmsnorm_rows(x2, norm_g, BF16)
    tm = 512
    proj_a = matmul(h, w_a, mode="shift", extras=(mu_a,), tm=tm, tn=tn_a, tiles_per_seq=seq // tm, name="proj_a")
    proj_b = matmul(h, w_b, tm=tm, tn=1024, name="proj_b")

    o_a = rwkv_branch(proj_a, batch, seq, a_width, w_lora_up, w0, a_lora_up, a0, k_k, k_a, r_k, lnx_w, lnx_b)
    graw = proj_a[:, a_cols:a_cols + n_gate]
    yb = nsa_branch(proj_b, graw, batch, seq, b_width, q_norm_g, k_norm_g, pe_cmp_k, pe_cmp_v,
                    cmp_k_w1, cmp_k_w2, cmp_v_w1, cmp_v_w2)

    gate_blk = (b_width + 6 * kv_width + b_width) // 512
    w_up_a_nm = head_minor(w_up_a.T, heads_a).T
    merged = merge(o_a, proj_a, 3, yb, w_up_a_nm.astype(BF16), w_up_b.astype(BF16), proj_b,
                   gate_blk, gate_blk + d // 512)
    x1 = matmul(merged, w_out.astype(BF16), mode="resid", extras=(x2,), tm=tm, tn=1024, name="out_proj")
    hn = rmsnorm_rows(x1, ple_pre_g, BF16)
    e = ple_embed(p.reshape(tokens, -1), w_ple.astype(BF16), ple_post_g)
    out = matmul(hn, w_ple_gate.astype(BF16), mode="ple", extras=(x1, e), tm=tm, tn=1024, name="ple_gate")
    return out.reshape(batch, seq, d)


def kernel(x, p, norm_g, w_in, shift_mu, w_lora_up, w0, a_lora_up, a0, k_k, k_a, r_k, lnx_w, lnx_b, q_norm_g, k_norm_g, pe_cmp_k, pe_cmp_v, cmp_k_w1, cmp_k_w2, cmp_v_w1, cmp_v_w2, w_up_a, w_up_b, w_out, ple_pre_g, w_ple_gate, w_ple, ple_post_g):
    depth = w_in.shape[0]
    for i in range(depth):
        x = _layer(x, p[i], norm_g[i], w_in[i], shift_mu[i], w_lora_up[i], w0[i], a_lora_up[i], a0[i], k_k[i],
                   k_a[i], r_k[i], lnx_w[i], lnx_b[i], q_norm_g[i], k_norm_g[i], pe_cmp_k[i], pe_cmp_v[i],
                   cmp_k_w1[i], cmp_k_w2[i], cmp_v_w1[i], cmp_v_w2[i], w_up_a[i], w_up_b[i], w_out[i],
                   ple_pre_g[i], w_ple_gate[i], w_ple[i], ple_post_g[i])
    return x
```

```python
import functools

import jax
import jax.numpy as jnp
from jax import lax
from jax.experimental import pallas as pl
from jax.experimental.pallas import tpu as pltpu

F32 = jnp.float32
BF16 = jnp.bfloat16

LANES = 128
SUBLANES = 8
VMEM_LIMIT = 56 * 1024 * 1024

NORM_EPS = 1e-6
NEG_INF = -1e30
HEAD_A = 64
GN_EPS = 64e-5
HEAD_B = 128
KV_GROUPS = 4
REP = 4
L_CMP = 32
CMP_STRIDE = 16
L_SEL = 64
N_SEL = 16
WINDOW = 512
TQ = 128
KC = 256
FORCE_BONUS = 1e3
ATTN_SCALE = HEAD_B ** -0.5
DECAY_SCALE = 0.6065306597126334


def _params(vmem=VMEM_LIMIT, ndim=1):
    return pltpu.CompilerParams(dimension_semantics=("arbitrary",) * ndim, vmem_limit_bytes=vmem)


def _rmsnorm_kernel(x_ref, g_ref, o_ref):
    x = x_ref[...]
    ms = jnp.mean(x * x, axis=-1, keepdims=True)
    o_ref[...] = (x * lax.rsqrt(ms + NORM_EPS) * g_ref[...]).astype(o_ref.dtype)


def rmsnorm_rows(x, g, out_dtype, tm=256):
    m, d = x.shape
    return pl.pallas_call(
        _rmsnorm_kernel,
        grid=(m // tm,),
        in_specs=[pl.BlockSpec((tm, d), lambda i: (i, 0)), pl.BlockSpec((1, d), lambda i: (0, 0))],
        out_specs=pl.BlockSpec((tm, d), lambda i: (i, 0)),
        out_shape=jax.ShapeDtypeStruct((m, d), out_dtype),
        compiler_params=_params(),
        name="rmsnorm",
    )(x, g.reshape(1, d))


def _mm_kernel(*refs, mode, tiles_per_seq):
    a_ref, b_ref = refs[0], refs[1]
    acc = jnp.dot(a_ref[...], b_ref[...], preferred_element_type=F32)
    if mode == "plain":
        o_ref = refs[2]
        o_ref[...] = acc.astype(o_ref.dtype)
    elif mode == "shift":
        mu_ref, o_ref, carry_ref = refs[2], refs[3], refs[4]
        i = pl.program_id(1)
        tm = acc.shape[0]
        first = (i % tiles_per_seq) == 0
        last_prev = jnp.where(first, 0.0, carry_ref[SUBLANES - 1:SUBLANES, :])
        rolled = pltpu.roll(acc, 1, axis=0)
        row = lax.broadcasted_iota(jnp.int32, acc.shape, 0)
        prev = jnp.where(row == 0, last_prev, rolled)
        carry_ref[...] = acc[tm - SUBLANES:tm, :]
        o_ref[...] = (acc + mu_ref[...] * (prev - acc)).astype(o_ref.dtype)
    elif mode == "resid":
        r_ref, o_ref = refs[2], refs[3]
        o_ref[...] = (r_ref[...] + acc).astype(o_ref.dtype)
    elif mode == "ple":
        x_ref, e_ref, o_ref = refs[2], refs[3], refs[4]
        o_ref[...] = (x_ref[...] + jax.nn.sigmoid(acc) * e_ref[...]).astype(o_ref.dtype)
    else:
        raise ValueError(mode)


def matmul(a, b, *, mode="plain", extras=(), out_dtype=F32, tm=512, tn=1024, tiles_per_seq=1, name="mm"):
    m, k = a.shape
    n = b.shape[1]
    assert m % tm == 0 and n % tn == 0, (m, n, tm, tn)
    in_specs = [pl.BlockSpec((tm, k), lambda j, i: (i, 0)), pl.BlockSpec((k, tn), lambda j, i: (0, j))]
    scratch = []
    if mode == "shift":
        in_specs.append(pl.BlockSpec((1, tn), lambda j, i: (0, j)))
        scratch.append(pltpu.VMEM((SUBLANES, tn), F32))
    elif mode == "resid":
        in_specs.append(pl.BlockSpec((tm, tn), lambda j, i: (i, j)))
    elif mode == "ple":
        in_specs += [pl.BlockSpec((tm, tn), lambda j, i: (i, j)), pl.BlockSpec((tm, tn), lambda j, i: (i, j))]
    return pl.pallas_call(
        functools.partial(_mm_kernel, mode=mode, tiles_per_seq=tiles_per_seq),
        grid=(n // tn, m // tm),
        in_specs=in_specs,
        out_specs=pl.BlockSpec((tm, tn), lambda j, i: (i, j)),
        out_shape=jax.ShapeDtypeStruct((m, n), out_dtype),
        scratch_shapes=scratch,
        compiler_params=_params(ndim=2),
        name=name,
    )(a, b, *extras)


def _lora_kernel(x_ref, w_ref, b_ref, o_ref):
    x = x_ref[...]
    lane = lax.broadcasted_iota(jnp.int32, x.shape, 1)
    x = jnp.where(lane < HEAD_A, jnp.tanh(x), x)
    acc = jnp.dot(x, w_ref[...], preferred_element_type=F32, precision=lax.Precision.HIGHEST)
    o_ref[...] = acc + b_ref[...]


def lora_project(proj_a, col_block, w_blockdiag, bias, tm=1024, tn=1024):
    m = proj_a.shape[0]
    n = w_blockdiag.shape[1]
    tm = min(tm, m)
    assert m % tm == 0 and n % tn == 0
    return pl.pallas_call(
        _lora_kernel,
        grid=(m // tm, n // tn),
        in_specs=[
            pl.BlockSpec((tm, LANES), lambda i, j: (i, col_block)),
            pl.BlockSpec((LANES, tn), lambda i, j: (0, j)),
            pl.BlockSpec((1, tn), lambda i, j: (0, j)),
        ],
        out_specs=pl.BlockSpec((tm, tn), lambda i, j: (i, j)),
        out_shape=jax.ShapeDtypeStruct((m, n), F32),
        compiler_params=_params(ndim=2),
        name="lora",
    )(proj_a, w_blockdiag, bias)


SCAN_BATCH = 4
SCAN_HEADS = LANES // SCAN_BATCH
NGRP = LANES // SCAN_HEADS


def _segment_transpose(x):
    seg = lax.broadcasted_iota(jnp.int32, x[0].shape, 1) // SCAN_HEADS
    y = []
    for i in range(NGRP):
        out = None
        for j in range(SCAN_BATCH):
            shift = ((j - i) % NGRP) * SCAN_HEADS
            piece = x[j] if shift == 0 else pltpu.roll(x[j], shift, axis=1)
            out = piece if out is None else jnp.where(seg == j, piece, out)
        y.append(out)
    return y


def _to_lanes(x_ref, dst_ref, tc):
    for g in range(HEAD_A // NGRP):
        y = _segment_transpose([x_ref[b, :, g * LANES:(g + 1) * LANES] for b in range(SCAN_BATCH)])
        for n_lo in range(NGRP):
            n = g * NGRP + n_lo
            dst_ref[n * tc:(n + 1) * tc, :] = y[n_lo]


def _from_lanes(src_ref, o_ref, tc):
    for g in range(HEAD_A // NGRP):
        y = _segment_transpose([src_ref[(g * NGRP + n_lo) * tc:(g * NGRP + n_lo + 1) * tc, :] for n_lo in range(NGRP)])
        for b in range(SCAN_BATCH):
            o_ref[b, :, g * LANES:(g + 1) * LANES] = y[b]


def _scan_kernel(r_ref, k_ref, v_ref, wp_ref, ap_ref, kkw_ref, kaw_ref, rkw_ref, lnw_ref, lnb_ref,
                 o_ref, state_ref, rs, ks, vs, ws, ans, bvs, os, bon, *, tc):
    n_ch = HEAD_A

    @pl.when(pl.program_id(1) == 0)
    def _():
        state_ref[...] = jnp.zeros_like(state_ref)

    _to_lanes(r_ref, rs, tc)
    _to_lanes(k_ref, ks, tc)
    _to_lanes(v_ref, vs, tc)
    _to_lanes(wp_ref, ws, tc)
    _to_lanes(ap_ref, bvs, tc)

    def rows(n):
        return pl.ds(pl.multiple_of(n * tc, tc), tc)

    def norm_acc(n, acc):
        kkr = ks[rows(n), :] * kkw_ref[n]
        return acc + kkr * kkr

    nsq = lax.fori_loop(0, n_ch, norm_acc, jnp.zeros((tc, LANES), F32), unroll=8)
    inv = 1.0 / jnp.maximum(jnp.sqrt(nsq), 1e-12)

    def prep(n, bacc):
        k = ks[rows(n), :]
        a = jax.nn.sigmoid(bvs[rows(n), :])
        kk = k * kkw_ref[n] * inv
        ws[rows(n), :] = jnp.exp(-DECAY_SCALE * jax.nn.sigmoid(ws[rows(n), :]))
        ans[rows(n), :] = -kk
        bvs[rows(n), :] = kk * a
        kmod = k * (1.0 + (a - 1.0) * kaw_ref[n])
        ks[rows(n), :] = kmod
        return bacc + rs[rows(n), :] * kmod * rkw_ref[n]

    bon[...] = lax.fori_loop(0, n_ch, prep, jnp.zeros((tc, LANES), F32), unroll=4)

    def bcast(ref, row):
        return ref[pl.ds(row, 1), :][None]

    def sa_first(j, acc):
        return acc + state_ref[j] * bcast(ans, j * tc)

    slab = (n_ch // SUBLANES, SUBLANES, LANES)
    sa0 = lax.fori_loop(0, n_ch, sa_first, jnp.zeros(slab, F32), unroll=4)

    def step(t, sa):
        v = vs[pl.ds(t, n_ch, stride=tc), :].reshape(slab)
        t_next = jnp.minimum(t + 1, tc - 1)

        def jbody(j, carry):
            out, sa_next = carry
            row = j * tc + t
            s_new = state_ref[j] * bcast(ws, row) + sa * bcast(bvs, row) + v * bcast(ks, row)
            state_ref[j] = s_new
            return out + s_new * bcast(rs, row), sa_next + s_new * bcast(ans, j * tc + t_next)

        out, sa_next = lax.fori_loop(0, n_ch, jbody, (jnp.zeros(slab, F32), jnp.zeros(slab, F32)), unroll=4)
        o = out.reshape(n_ch, LANES)
        mu = jnp.mean(o, axis=0, keepdims=True)
        d = o - mu
        var = jnp.mean(d * d, axis=0, keepdims=True)
        on = d * lax.rsqrt(var + GN_EPS) * lnw_ref[...] + lnb_ref[...]
        bonus = bon[pl.ds(t, 1), :] * v.reshape(n_ch, LANES)
        os[pl.ds(t, n_ch, stride=tc), :] = on + bonus
        return sa_next

    lax.fori_loop(0, tc, step, sa0)
    _from_lanes(os, o_ref, tc)


def rwkv_scan(proj_a, wa_pre, kkw, kaw, rkw, lnw, lnb, batch, seq, tc=32):
    width = HEAD_A * SCAN_HEADS
    seqb = lambda col: pl.BlockSpec((SCAN_BATCH, tc, width), lambda g, c: (g, c, col))
    par3 = pl.BlockSpec((HEAD_A, 1, LANES), lambda g, c: (0, 0, 0))
    par2 = pl.BlockSpec((HEAD_A, LANES), lambda g, c: (0, 0))
    buf = pltpu.VMEM((HEAD_A * tc, LANES), F32)
    return pl.pallas_call(
        functools.partial(_scan_kernel, tc=tc),
        grid=(batch // SCAN_BATCH, seq // tc),
        in_specs=[seqb(0), seqb(1), seqb(2), seqb(0), seqb(1), par3, par3, par3, par2, par2],
        out_specs=seqb(0),
        out_shape=jax.ShapeDtypeStruct((batch, seq, width), F32),
        scratch_shapes=[pltpu.VMEM((HEAD_A, HEAD_A // SUBLANES, SUBLANES, LANES), F32)] + [buf] * 7
        + [pltpu.VMEM((tc, LANES), F32)],
        compiler_params=_params(ndim=2),
        name="rwkv_scan",
    )(proj_a, proj_a, proj_a, wa_pre, wa_pre, kkw, kaw, rkw, lnw, lnb)


def _compress_kernel(x_ref, pe_ref, w1_ref, w2_ref, kg_ref, o_ref):
    which = pl.program_id(0)
    half = L_CMP // 2
    nblk = x_ref.shape[0] // CMP_STRIDE
    h1 = jnp.zeros((nblk, w1_ref.shape[-1]), F32)
    h2 = jnp.zeros((nblk, w1_ref.shape[-1]), F32)
    for l in range(half):
        x = x_ref[pl.ds(l, nblk, stride=CMP_STRIDE), :]
        a1 = (x + pe_ref[l:l + 1, :]).astype(BF16)
        a2 = (x + pe_ref[half + l:half + l + 1, :]).astype(BF16)
        h1 = h1 + jnp.dot(a1, w1_ref[l], preferred_element_type=F32)
        h2 = h2 + jnp.dot(a2, w1_ref[half + l], preferred_element_type=F32)
    hid = h1 + pltpu.roll(h2, nblk - 1, axis=0)
    hid = jax.nn.gelu(hid)
    out = jnp.dot(hid.astype(BF16), w2_ref[...], preferred_element_type=F32)
    ms = jnp.mean(out * out, axis=-1, keepdims=True)
    normed = out * lax.rsqrt(ms + NORM_EPS) * kg_ref[...]
    out = jnp.where(which == 0, normed, out)
    row = lax.broadcasted_iota(jnp.int32, out.shape, 0)
    o_ref[...] = jnp.where(row < nblk - 1, out, 0.0)


def compress(proj_b, kc_block, vc_block, pe, w1, w2, kgain, batch, seq):
    nblk = seq // CMP_STRIDE
    hidden = w1.shape[-1]

    def xmap(w, b, g):
        return (b, kc_block + w * (vc_block - kc_block) + g)

    return pl.pallas_call(
        _compress_kernel,
        grid=(2, batch, KV_GROUPS),
        in_specs=[
            pl.BlockSpec((seq, HEAD_B), xmap),
            pl.BlockSpec((None, L_CMP, HEAD_B), lambda w, b, g: (w, 0, 0)),
            pl.BlockSpec((None, L_CMP, HEAD_B, hidden), lambda w, b, g: (w, 0, 0, 0)),
            pl.BlockSpec((None, hidden, HEAD_B), lambda w, b, g: (w, 0, 0)),
            pl.BlockSpec((1, HEAD_B), lambda w, b, g: (0, 0)),
        ],
        out_specs=pl.BlockSpec((None, None, None, nblk, HEAD_B), lambda w, b, g: (w, b, g, 0, 0)),
        out_shape=jax.ShapeDtypeStruct((2, batch, KV_GROUPS, nblk, HEAD_B), F32),
        compiler_params=_params(ndim=3),
        name="nsa_compress",
    )(proj_b, pe, w1, w2, kgain)


def _rms(x, gain):
    ms = jnp.mean(x * x, axis=-1, keepdims=True)
    return x * lax.rsqrt(ms + NORM_EPS) * gain


def _dot_nt(a, b):
    return lax.dot_general(a, b, (((1,), (1,)), ((), ())), preferred_element_type=F32)


LOG2E = 1.4426950408889634
VAUG = 2 * HEAD_B


def _attn_kernel(q_ref, ks_ref, vs_ref, kw_ref, vw_ref, kc_ref, vc_ref, gt_ref, z_ref, qg_ref, kg_ref,
                 ovt_ref, ex_ref, wb_ref, o_ref, ksn_ref, vs1_ref, kwn_ref, vw1_ref, q_scr, m_scr, acc_scr,
                 oc_scr, *, seq):
    qt = pl.program_id(2)
    ncmp = kc_ref.shape[0]
    nsel = seq // L_SEL

    @pl.when(qt == 0)
    def _():
        ones_col = jnp.ones((seq, HEAD_B), BF16)
        ksn_ref[...] = _rms(ks_ref[...], kg_ref[1:2, :]).astype(BF16)
        vs1_ref[:, 0:HEAD_B] = vs_ref[...].astype(BF16)
        vs1_ref[:, HEAD_B:VAUG] = ones_col
        kwn_ref[0:WINDOW, :] = jnp.zeros((WINDOW, HEAD_B), BF16)
        vw1_ref[0:WINDOW, :] = jnp.zeros((WINDOW, VAUG), BF16)
        kwn_ref[WINDOW:WINDOW + seq, :] = _rms(kw_ref[...], kg_ref[2:3, :]).astype(BF16)
        vw1_ref[WINDOW:WINDOW + seq, 0:HEAD_B] = vw_ref[...].astype(BF16)
        vw1_ref[WINDOW:WINDOW + seq, HEAD_B:VAUG] = ones_col

    t0 = pl.multiple_of(qt * TQ, TQ)
    for r in range(REP):
        q_scr[r] = (_rms(q_ref[:, r * HEAD_B:(r + 1) * HEAD_B], qg_ref[...]) * (ATTN_SCALE * LOG2E)).astype(BF16)

    def t_of(shape):
        return t0 + lax.broadcasted_iota(jnp.int32, shape, 0)

    n_idx = lax.broadcasted_iota(jnp.int32, (TQ, ncmp), 1)
    bias_c = jnp.where(n_idx * CMP_STRIDE + (L_CMP - 1) <= t_of((TQ, ncmp)), 0.0, NEG_INF)
    bias_c = jnp.where(n_idx < ncmp - 1, bias_c, NEG_INF)
    row_ok = jnp.where(t_of((TQ, 1)) >= L_CMP - 1, 1.0, 0.0)
    q_all = q_scr[...].reshape(REP * TQ, HEAD_B)
    head = lambda x, r: x[r * TQ:(r + 1) * TQ]
    s_all = _dot_nt(q_all, kc_ref[...].astype(BF16))
    prs = []
    for r in range(REP):
        s = head(s_all, r) + bias_c
        p = jnp.exp2(s - jnp.max(s, axis=-1, keepdims=True))
        prs.append(p * (row_ok / jnp.sum(p, axis=-1, keepdims=True)))
    psum = prs[0] + prs[1] + prs[2] + prs[3]
    oc_scr[...] = jnp.dot(jnp.concatenate(prs, axis=0).astype(BF16), vc_ref[...].astype(BF16),
                          preferred_element_type=F32).reshape(REP, TQ, HEAD_B)
    imp_t = lax.dot_general(ovt_ref[...], psum, (((1,), (1,)), ((), ())), preferred_element_type=F32,
                            precision=lax.Precision.HIGHEST)

    j_idx = lax.broadcasted_iota(jnp.int32, (nsel, TQ), 0)
    t_sel = t0 + lax.broadcasted_iota(jnp.int32, (nsel, TQ), 1)
    cur = t_sel // L_SEL
    forced = jnp.where((j_idx == 0) | (j_idx == cur) | (j_idx == cur - 1), FORCE_BONUS, 0.0)
    score = jnp.where(j_idx * L_SEL <= t_sel, imp_t + forced, NEG_INF)
    rank = jnp.zeros((nsel, TQ), F32)
    for i in range(nsel):
        si = score[i:i + 1, :]
        ahead = (si > score) | ((si == score) & (j_idx > i))
        rank = rank + jnp.where(ahead, 1.0, 0.0)
    sel_t = jnp.where((rank < min(N_SEL, nsel)) & (score > 0.5 * NEG_INF), 1.0, 0.0)
    sel = jnp.concatenate([sel_t, jnp.zeros((LANES - nsel, TQ), F32)], axis=0).T
    not_sel = (1.0 - sel).astype(BF16)

    m_scr[...] = jnp.full(m_scr.shape, NEG_INF, F32)
    acc_scr[...] = jnp.zeros(acc_scr.shape, F32)

    def sel_chunk(c, diagonal):
        k0 = pl.multiple_of(c * KC, KC)
        kblk = ksn_ref[pl.ds(k0, KC), :]
        v1 = vs1_ref[pl.ds(k0, KC), :]
        bias = jnp.dot(not_sel, ex_ref[c], preferred_element_type=F32)
        if diagonal:
            lane = lax.broadcasted_iota(jnp.int32, (TQ, KC), 1)
            bias = jnp.where(k0 + lane <= t_of((TQ, KC)), bias, NEG_INF)
        s_all = _dot_nt(q_scr[...].reshape(REP * TQ, HEAD_B), kblk)
        ps, alphas = [], []
        for r in range(REP):
            s = head(s_all, r) + bias
            m_old = m_scr[r]
            m_new = jnp.maximum(m_old, jnp.max(s, axis=-1, keepdims=True))
            m_scr[r] = m_new
            ps.append(jnp.exp2(s - jnp.tile(m_new, (1, KC // LANES))).astype(BF16))
            alphas.append(jnp.exp2(m_old - m_new))
        pv = jnp.dot(jnp.concatenate(ps, axis=0), v1, preferred_element_type=F32)
        for r in range(REP):
            acc_scr[r] = jnp.tile(alphas[r], (1, VAUG // LANES)) * acc_scr[r] + head(pv, r)

    n_full = t0 // KC

    def full_chunk(c, carry):
        sel_chunk(c, False)
        return carry

    lax.fori_loop(0, n_full, full_chunk, 0)
    sel_chunk(n_full, True)

    span = WINDOW + TQ
    kwin = kwn_ref[pl.ds(t0, span), :]
    vwin = vw1_ref[pl.ds(t0, span), :]
    lane_w = lax.broadcasted_iota(jnp.int32, (TQ, span), 1)
    bias_w = jnp.where(lane_w >= WINDOW - t0, wb_ref[...], NEG_INF)
    s_all = _dot_nt(q_all, kwin)
    ps = []
    for r in range(REP):
        s = head(s_all, r) + bias_w
        ps.append(jnp.exp2(s - jnp.max(s, axis=-1, keepdims=True)).astype(BF16))
    ow_all = jnp.dot(jnp.concatenate(ps, axis=0), vwin, preferred_element_type=F32)
    gts = jax.nn.sigmoid(gt_ref[...])
    for r in range(REP):
        ow = head(ow_all, r)
        o_w = ow[:, 0:HEAD_B] / ow[:, HEAD_B:VAUG]
        acc = acc_scr[r]
        o_s = acc[:, 0:HEAD_B] / acc[:, HEAD_B:VAUG]
        o = (gts[:, 3 * r:3 * r + 1] * oc_scr[r] + gts[:, 3 * r + 1:3 * r + 2] * o_s
             + gts[:, 3 * r + 2:3 * r + 3] * o_w)
        z = z_ref[:, r * HEAD_B:(r + 1) * HEAD_B]
        o_ref[:, r * HEAD_B:(r + 1) * HEAD_B] = (o * (z * jax.nn.sigmoid(z))).astype(o_ref.dtype)


def nsa_attention(proj_b, cmp_kv, gates, q_gain, k_gain, overlap_t, expand, win_bias, blocks, batch, seq):
    nq = seq // TQ
    ncmp = seq // CMP_STRIDE
    gw = REP * HEAD_B // LANES
    qspec = lambda off: pl.BlockSpec((TQ, REP * HEAD_B), lambda b, g, t: (b * nq + t, off // gw + g))
    kvspec = lambda off: pl.BlockSpec((seq, HEAD_B), lambda b, g, t: (b, off + g))
    cspec = lambda w: pl.BlockSpec((None, None, None, ncmp, HEAD_B), lambda b, g, t: (w, b, g, 0, 0))
    full = lambda a: pl.BlockSpec(a.shape, lambda b, g, t: (0,) * a.ndim)
    return pl.pallas_call(
        functools.partial(_attn_kernel, seq=seq),
        grid=(batch, KV_GROUPS, nq),
        in_specs=[
            qspec(blocks["q"]), kvspec(blocks["ks"]), kvspec(blocks["vs"]), kvspec(blocks["kw"]), kvspec(blocks["vw"]),
            cspec(0), cspec(1),
            pl.BlockSpec((None, None, TQ, 16), lambda b, g, t: (b, g, t, 0)),
            qspec(blocks["z"]),
            full(q_gain), full(k_gain), full(overlap_t), full(expand), full(win_bias),
        ],
        out_specs=pl.BlockSpec((TQ, REP * HEAD_B), lambda b, g, t: (b * nq + t, g)),
        out_shape=jax.ShapeDtypeStruct((batch * seq, KV_GROUPS * REP * HEAD_B), BF16),
        scratch_shapes=[
            pltpu.VMEM((seq, HEAD_B), BF16), pltpu.VMEM((seq, VAUG), BF16),
            pltpu.VMEM((seq + WINDOW, HEAD_B), BF16), pltpu.VMEM((seq + WINDOW, VAUG), BF16),
            pltpu.VMEM((REP, TQ, HEAD_B), BF16), pltpu.VMEM((REP, TQ, LANES), F32),
            pltpu.VMEM((REP, TQ, VAUG), F32), pltpu.VMEM((REP, TQ, HEAD_B), F32),
        ],
        compiler_params=_params(ndim=3),
        name="nsa_attention",
    )(proj_b, proj_b, proj_b, proj_b, proj_b, cmp_kv, cmp_kv, gates, proj_b, q_gain, k_gain, overlap_t, expand,
      win_bias)


def _merge_kernel(o_ref, z_ref, yb_ref, wa_ref, wb_ref, ga_ref, gb_ref, out_ref, ya_ref):
    @pl.when(pl.program_id(1) == 0)
    def _():
        z = z_ref[...]
        ya_ref[...] = (o_ref[...] * (z * jax.nn.sigmoid(z))).astype(BF16)

    ua = jnp.dot(ya_ref[...], wa_ref[...], preferred_element_type=F32)
    ub = jnp.dot(yb_ref[...], wb_ref[...], preferred_element_type=F32)
    out_ref[...] = (jax.nn.sigmoid(ga_ref[...]) * ua + jax.nn.sigmoid(gb_ref[...]) * ub).astype(out_ref.dtype)


def merge(o_a, proj_a, z_block, yb, w_a, w_b, proj_b, ga_block, gb_block, tm=512, tn=512):
    m, ka = o_a.shape
    n = w_a.shape[1]
    return pl.pallas_call(
        _merge_kernel,
        grid=(m // tm, n // tn),
        in_specs=[
            pl.BlockSpec((tm, ka), lambda i, j: (i, 0)),
            pl.BlockSpec((tm, ka), lambda i, j: (i, z_block)),
            pl.BlockSpec((tm, ka), lambda i, j: (i, 0)),
            pl.BlockSpec((ka, tn), lambda i, j: (0, j)),
            pl.BlockSpec((ka, tn), lambda i, j: (0, j)),
            pl.BlockSpec((tm, tn), lambda i, j: (i, ga_block + j)),
            pl.BlockSpec((tm, tn), lambda i, j: (i, gb_block + j)),
        ],
        out_specs=pl.BlockSpec((tm, tn), lambda i, j: (i, j)),
        out_shape=jax.ShapeDtypeStruct((m, n), BF16),
        scratch_shapes=[pltpu.VMEM((tm, ka), BF16)],
        compiler_params=_params(ndim=2),
        name="merge",
    )(o_a, proj_a, yb, w_a, w_b, proj_b, proj_b)


def _ple_embed_kernel(p_ref, w_ref, g_ref, o_ref):
    acc = jnp.dot(p_ref[...].astype(BF16), w_ref[...], preferred_element_type=F32)
    o_ref[...] = _rms(acc, g_ref[...]).astype(o_ref.dtype)


def ple_embed(p, w, g, tm=256):
    m, k = p.shape
    n = w.shape[1]
    return pl.pallas_call(
        _ple_embed_kernel,
        grid=(m // tm,),
        in_specs=[pl.BlockSpec((tm, k), lambda i: (i, 0)), pl.BlockSpec((k, n), lambda i: (0, 0)),
                  pl.BlockSpec((1, n), lambda i: (0, 0))],
        out_specs=pl.BlockSpec((tm, n), lambda i: (i, 0)),
        out_shape=jax.ShapeDtypeStruct((m, n), F32),
        compiler_params=_params(),
        name="ple_embed",
    )(p, w, g.reshape(1, n))


def head_minor(w, heads):
    lead = w.shape[:-1]
    return w.reshape(*lead, heads, HEAD_A).swapaxes(-1, -2).reshape(*lead, heads * HEAD_A)


def rwkv_branch(proj_a, batch, seq, a_width, w_lora_up, w0, a_lora_up, a0, k_k, k_a, r_k, lnx_w, lnx_b):
    tokens = batch * seq
    heads_a = a_width // HEAD_A
    assert heads_a == SCAN_HEADS and batch % SCAN_BATCH == 0
    lora = w_lora_up.shape[0]
    zeros = jnp.zeros((lora, a_width), F32)
    hm = lambda w: head_minor(w, heads_a)
    w_lora = jnp.concatenate([jnp.concatenate([hm(w_lora_up), zeros], 1),
                              jnp.concatenate([zeros, hm(a_lora_up)], 1)], 0)
    b_lora = jnp.concatenate([hm(w0), hm(a0)]).reshape(1, -1)
    wa_pre = lora_project(proj_a, 4 * a_width // LANES, w_lora, b_lora)

    def par_scan(t):
        return jnp.tile(t.reshape(heads_a, HEAD_A).T, (1, SCAN_BATCH))

    par3 = lambda t: par_scan(t).reshape(HEAD_A, 1, LANES)
    o = rwkv_scan(proj_a.reshape(batch, seq, -1), wa_pre.reshape(batch, seq, -1),
                  par3(k_k), par3(k_a), par3(r_k), par_scan(lnx_w), par_scan(lnx_b), batch, seq)
    return o.reshape(tokens, a_width)


def nsa_branch(proj_b, graw, batch, seq, b_width, q_norm_g, k_norm_g, pe_cmp_k, pe_cmp_v,
               cmp_k_w1, cmp_k_w2, cmp_v_w1, cmp_v_w2):
    kv_width = KV_GROUPS * HEAD_B
    blk = lambda cols: cols // LANES
    blocks = {"q": 0, "kc": blk(b_width), "vc": blk(b_width + kv_width), "ks": blk(b_width + 2 * kv_width),
              "vs": blk(b_width + 3 * kv_width), "kw": blk(b_width + 4 * kv_width), "vw": blk(b_width + 5 * kv_width),
              "z": blk(b_width + 6 * kv_width)}
    pe = jnp.stack([pe_cmp_k, pe_cmp_v])
    hidden = cmp_k_w1.shape[1]
    w1 = jnp.stack([cmp_k_w1, cmp_v_w1]).reshape(2, L_CMP, HEAD_B, hidden).astype(BF16)
    w2 = jnp.stack([cmp_k_w2, cmp_v_w2]).astype(BF16)
    cmp_kv = compress(proj_b, blocks["kc"], blocks["vc"], pe, w1, w2, k_norm_g[0:1], batch, seq)

    ncmp = seq // CMP_STRIDE
    nsel = seq // L_SEL
    c_start = jnp.arange(ncmp) * CMP_STRIDE
    s_start = jnp.arange(nsel) * L_SEL
    overlap_t = ((c_start[None, :] < (s_start + L_SEL)[:, None]) & (s_start[:, None] < (c_start + L_CMP)[None, :])
                 & (jnp.arange(ncmp)[None, :] < ncmp - 1)).astype(F32)
    key_blk = jnp.arange(seq) // L_SEL
    expand = jnp.where(jnp.arange(LANES)[:, None] == key_blk[None, :], NEG_INF, 0.0).astype(BF16)
    expand = expand.reshape(LANES, seq // KC, KC).transpose(1, 0, 2)
    tl = jnp.arange(TQ)[:, None]
    u = jnp.arange(WINDOW + TQ)[None, :]
    win_bias = jnp.where((u > tl) & (u <= WINDOW + tl), 0.0, NEG_INF).astype(F32)
    gates = graw.reshape(batch, seq, KV_GROUPS, REP * 3).transpose(0, 2, 1, 3)
    gates = jnp.pad(gates, ((0, 0), (0, 0), (0, 0), (0, 16 - REP * 3)))
    return nsa_attention(proj_b, cmp_kv, gates, q_norm_g.reshape(1, HEAD_B), k_norm_g, overlap_t, expand, win_bias,
                         blocks, batch, seq)


def _layer(x, p, norm_g, w_in, shift_mu, w_lora_up, w0, a_lora_up, a0, k_k, k_a, r_k, lnx_w, lnx_b,
           q_norm_g, k_norm_g, pe_cmp_k, pe_cmp_v, cmp_k_w1, cmp_k_w2, cmp_v_w1, cmp_v_w2,
           w_up_a, w_up_b, w_out, ple_pre_g, w_ple_gate, w_ple, ple_post_g):
    batch, seq, d = x.shape
    tokens = batch * seq
    a_width = w_up_a.shape[0]
    b_width = w_up_b.shape[0]
    heads_a = a_width // HEAD_A
    kv_width = KV_GROUPS * HEAD_B
    n_gate = KV_GROUPS * REP * 3
    lora = w_lora_up.shape[0]
    assert lora == HEAD_A and a_lora_up.shape[0] == HEAD_A and 2 * lora == LANES
    a_cols = 4 * a_width + 2 * lora
    g_off = a_cols + b_width + 6 * kv_width
    zb_off = g_off + n_gate
    assert w_in.shape[1] == zb_off + b_width + 2 * d

    pad_a = LANES - n_gate
    hm4 = lambda w: head_minor(w.reshape(*w.shape[:-1], 4, a_width), heads_a).reshape(*w.shape[:-1], 4 * a_width)
    w_a = jnp.concatenate([hm4(w_in[:, :4 * a_width]), w_in[:, 4 * a_width:a_cols], w_in[:, g_off:zb_off],
                           jnp.zeros((d, pad_a), F32)], axis=1).astype(BF16)
    mu_a = jnp.concatenate([hm4(shift_mu[:4 * a_width]), shift_mu[4 * a_width:], jnp.zeros((LANES,), F32)]).reshape(1, -1)
    w_b = jnp.concatenate([w_in[:, a_cols:g_off], w_in[:, zb_off:]], axis=1).astype(BF16)
    na = w_a.shape[1]
    tn_a = 768 if na % 768 == 0 else LANES

    x2 = x.reshape(tokens, d)
    h = rmsnorm_rows(x2, norm_g, BF16)
    tm = 512
    proj_a = matmul(h, w_a, mode="shift", extras=(mu_a,), tm=tm, tn=tn_a, tiles_per_seq=seq // tm, name="proj_a")
    proj_b = matmul(h, w_b, tm=tm, tn=1024, name="proj_b")

    o_a = rwkv_branch(proj_a, batch, seq, a_width, w_lora_up, w0, a_lora_up, a0, k_k, k_a, r_k, lnx_w, lnx_b)
    graw = proj_a[:, a_cols:a_cols + n_gate]
    yb = nsa_branch(proj_b, graw, batch, seq, b_width, q_norm_g, k_norm_g, pe_cmp_k, pe_cmp_v,
                    cmp_k_w1, cmp_k_w2, cmp_v_w1, cmp_v_w2)

    gate_blk = (b_width + 6 * kv_width + b_width) // 512
    w_up_a_nm = head_minor(w_up_a.T, heads_a).T
    merged = merge(o_a, proj_a, 3, yb, w_up_a_nm.astype(BF16), w_up_b.astype(BF16), proj_b,
                   gate_blk, gate_blk + d // 512)
    x1 = matmul(merged, w_out.astype(BF16), mode="resid", extras=(x2,), tm=tm, tn=1024, name="out_proj")
    hn = rmsnorm_rows(x1, ple_pre_g, BF16)
    e = ple_embed(p.reshape(tokens, -1), w_ple.astype(BF16), ple_post_g)
    out = matmul(hn, w_ple_gate.astype(BF16), mode="ple", extras=(x1, e), tm=tm, tn=1024, name="ple_gate")
    return out.reshape(batch, seq, d)


def kernel(x, p, norm_g, w_in, shift_mu, w_lora_up, w0, a_lora_up, a0, k_k, k_a, r_k, lnx_w, lnx_b, q_norm_g, k_norm_g, pe_cmp_k, pe_cmp_v, cmp_k_w1, cmp_k_w2, cmp_v_w1, cmp_v_w2, w_up_a, w_up_b, w_out, ple_pre_g, w_ple_gate, w_ple, ple_post_g):
    depth = w_in.shape[0]
    for i in range(depth):
        x = _layer(x, p[i], norm_g[i], w_in[i], shift_mu[i], w_lora_up[i], w0[i], a_lora_up[i], a0[i], k_k[i],
                   k_a[i], r_k[i], lnx_w[i], lnx_b[i], q_norm_g[i], k_norm_g[i], pe_cmp_k[i], pe_cmp_v[i],
                   cmp_k_w1[i], cmp_k_w2[i], cmp_v_w1[i], cmp_v_w2[i], w_up_a[i], w_up_b[i], w_out[i],
                   ple_pre_g[i], w_ple_gate[i], w_ple[i], ple_post_g[i])
    return x
```

```python
import functools

import jax
import jax.numpy as jnp
from jax import lax
from jax.experimental import pallas as pl
from jax.experimental.pallas import tpu as pltpu

F32 = jnp.float32
BF16 = jnp.bfloat16

LANES = 128
SUBLANES = 8
VMEM_LIMIT = 56 * 1024 * 1024

NORM_EPS = 1e-6
NEG_INF = -1e30
HEAD_A = 64
GN_EPS = 64e-5
HEAD_B = 128
KV_GROUPS = 4
REP = 4
L_CMP = 32
CMP_STRIDE = 16
L_SEL = 64
N_SEL = 16
WINDOW = 512
TQ = 128
KC = 256
FORCE_BONUS = 1e3
ATTN_SCALE = HEAD_B ** -0.5
DECAY_SCALE = 0.6065306597126334


def _params(vmem=VMEM_LIMIT, ndim=1):
    return pltpu.CompilerParams(dimension_semantics=("arbitrary",) * ndim, vmem_limit_bytes=vmem)


def _rmsnorm_kernel(x_ref, g_ref, o_ref):
    x = x_ref[...]
    ms = jnp.mean(x * x, axis=-1, keepdims=True)
    o_ref[...] = (x * lax.rsqrt(ms + NORM_EPS) * g_ref[...]).astype(o_ref.dtype)


def rmsnorm_rows(x, g, out_dtype, tm=256):
    m, d = x.shape
    return pl.pallas_call(
        _rmsnorm_kernel,
        grid=(m // tm,),
        in_specs=[pl.BlockSpec((tm, d), lambda i: (i, 0)), pl.BlockSpec((1, d), lambda i: (0, 0))],
        out_specs=pl.BlockSpec((tm, d), lambda i: (i, 0)),
        out_shape=jax.ShapeDtypeStruct((m, d), out_dtype),
        compiler_params=_params(),
        name="rmsnorm",
    )(x, g.reshape(1, d))


def _mm_kernel(*refs, mode, tiles_per_seq):
    a_ref, b_ref = refs[0], refs[1]
    acc = jnp.dot(a_ref[...], b_ref[...], preferred_element_type=F32)
    if mode == "plain":
        o_ref = refs[2]
        o_ref[...] = acc.astype(o_ref.dtype)
    elif mode == "shift":
        mu_ref, o_ref, carry_ref = refs[2], refs[3], refs[4]
        i = pl.program_id(1)
        tm = acc.shape[0]
        first = (i % tiles_per_seq) == 0
        last_prev = jnp.where(first, 0.0, carry_ref[SUBLANES - 1:SUBLANES, :])
        rolled = pltpu.roll(acc, 1, axis=0)
        row = lax.broadcasted_iota(jnp.int32, acc.shape, 0)
        prev = jnp.where(row == 0, last_prev, rolled)
        carry_ref[...] = acc[tm - SUBLANES:tm, :]
        o_ref[...] = (acc + mu_ref[...] * (prev - acc)).astype(o_ref.dtype)
    elif mode == "resid":
        r_ref, o_ref = refs[2], refs[3]
        o_ref[...] = (r_ref[...] + acc).astype(o_ref.dtype)
    elif mode == "ple":
        x_ref, e_ref, o_ref = refs[2], refs[3], refs[4]
        o_ref[...] = (x_ref[...] + jax.nn.sigmoid(acc) * e_ref[...]).astype(o_ref.dtype)
    else:
        raise ValueError(mode)


def matmul(a, b, *, mode="plain", extras=(), out_dtype=F32, tm=512, tn=1024, tiles_per_seq=1, name="mm"):
    m, k = a.shape
    n = b.shape[1]
    assert m % tm == 0 and n % tn == 0, (m, n, tm, tn)
    in_specs = [pl.BlockSpec((tm, k), lambda j, i: (i, 0)), pl.BlockSpec((k, tn), lambda j, i: (0, j))]
    scratch = []
    if mode == "shift":
        in_specs.append(pl.BlockSpec((1, tn), lambda j, i: (0, j)))
        scratch.append(pltpu.VMEM((SUBLANES, tn), F32))
    elif mode == "resid":
        in_specs.append(pl.BlockSpec((tm, tn), lambda j, i: (i, j)))
    elif mode == "ple":
        in_specs += [pl.BlockSpec((tm, tn), lambda j, i: (i, j)), pl.BlockSpec((tm, tn), lambda j, i: (i, j))]
    return pl.pallas_call(
        functools.partial(_mm_kernel, mode=mode, tiles_per_seq=tiles_per_seq),
        grid=(n // tn, m // tm),
        in_specs=in_specs,
        out_specs=pl.BlockSpec((tm, tn), lambda j, i: (i, j)),
        out_shape=jax.ShapeDtypeStruct((m, n), out_dtype),
        scratch_shapes=scratch,
        compiler_params=_params(ndim=2),
        name=name,
    )(a, b, *extras)


def _lora_kernel(x_ref, w_ref, b_ref, o_ref):
    x = x_ref[...]
    lane = lax.broadcasted_iota(jnp.int32, x.shape, 1)
    x = jnp.where(lane < HEAD_A, jnp.tanh(x), x)
    acc = jnp.dot(x, w_ref[...], preferred_element_type=F32, precision=lax.Precision.HIGHEST)
    o_ref[...] = acc + b_ref[...]


def lora_project(proj_a, col_block, w_blockdiag, bias, tm=1024, tn=1024):
    m = proj_a.shape[0]
    n = w_blockdiag.shape[1]
    tm = min(tm, m)
    assert m % tm == 0 and n % tn == 0
    return pl.pallas_call(
        _lora_kernel,
        grid=(m // tm, n // tn),
        in_specs=[
            pl.BlockSpec((tm, LANES), lambda i, j: (i, col_block)),
            pl.BlockSpec((LANES, tn), lambda i, j: (0, j)),
            pl.BlockSpec((1, tn), lambda i, j: (0, j)),
        ],
        out_specs=pl.BlockSpec((tm, tn), lambda i, j: (i, j)),
        out_shape=jax.ShapeDtypeStruct((m, n), F32),
        compiler_params=_params(ndim=2),
        name="lora",
    )(proj_a, w_blockdiag, bias)


SCAN_BATCH = 4
SCAN_HEADS = LANES // SCAN_BATCH
NGRP = LANES // SCAN_HEADS


def _segment_transpose(x):
    seg = lax.broadcasted_iota(jnp.int32, x[0].shape, 1) // SCAN_HEADS
    y = []
    for i in range(NGRP):
        out = None
        for j in range(SCAN_BATCH):
            shift = ((j - i) % NGRP) * SCAN_HEADS
            piece = x[j] if shift == 0 else pltpu.roll(x[j], shift, axis=1)
            out = piece if out is None else jnp.where(seg == j, piece, out)
        y.append(out)
    return y


def _to_lanes(x_ref, dst_ref, tc):
    for g in range(HEAD_A // NGRP):
        y = _segment_transpose([x_ref[b, :, g * LANES:(g + 1) * LANES] for b in range(SCAN_BATCH)])
        for n_lo in range(NGRP):
            n = g * NGRP + n_lo
            dst_ref[n * tc:(n + 1) * tc, :] = y[n_lo]


def _from_lanes_gated(src_ref, z_ref, o_ref, tc):
    for g in range(HEAD_A // NGRP):
        y = _segment_transpose([src_ref[(g * NGRP + n_lo) * tc:(g * NGRP + n_lo + 1) * tc, :] for n_lo in range(NGRP)])
        for b in range(SCAN_BATCH):
            z = z_ref[b, :, g * LANES:(g + 1) * LANES]
            o_ref[b, :, g * LANES:(g + 1) * LANES] = (y[b] * (z * jax.nn.sigmoid(z))).astype(o_ref.dtype)


def _scan_kernel(r_ref, k_ref, v_ref, z_ref, wp_ref, ap_ref, kkw_ref, kaw_ref, rkw_ref, lnw_ref, lnb_ref,
                 o_ref, state_ref, rs, ks, vs, ws, ans, bvs, os, bon, *, tc):
    n_ch = HEAD_A

    @pl.when(pl.program_id(1) == 0)
    def _():
        state_ref[...] = jnp.zeros_like(state_ref)

    _to_lanes(r_ref, rs, tc)
    _to_lanes(k_ref, ks, tc)
    _to_lanes(v_ref, vs, tc)
    _to_lanes(wp_ref, ws, tc)
    _to_lanes(ap_ref, bvs, tc)

    def rows(n):
        return pl.ds(pl.multiple_of(n * tc, tc), tc)

    def norm_acc(n, acc):
        kkr = ks[rows(n), :] * kkw_ref[n]
        return acc + kkr * kkr

    nsq = lax.fori_loop(0, n_ch, norm_acc, jnp.zeros((tc, LANES), F32), unroll=8)
    inv = 1.0 / jnp.maximum(jnp.sqrt(nsq), 1e-12)

    def prep(n, bacc):
        k = ks[rows(n), :]
        a = jax.nn.sigmoid(bvs[rows(n), :])
        kk = k * kkw_ref[n] * inv
        ws[rows(n), :] = jnp.exp(-DECAY_SCALE * jax.nn.sigmoid(ws[rows(n), :]))
        ans[rows(n), :] = -kk
        bvs[rows(n), :] = kk * a
        kmod = k * (1.0 + (a - 1.0) * kaw_ref[n])
        ks[rows(n), :] = kmod
        return bacc + rs[rows(n), :] * kmod * rkw_ref[n]

    bon[...] = lax.fori_loop(0, n_ch, prep, jnp.zeros((tc, LANES), F32), unroll=4)

    def bcast(ref, row):
        return ref[pl.ds(row, 1), :][None]

    def sa_first(j, acc):
        return acc + state_ref[j] * bcast(ans, j * tc)

    slab = (n_ch // SUBLANES, SUBLANES, LANES)
    sa0 = lax.fori_loop(0, n_ch, sa_first, jnp.zeros(slab, F32), unroll=4)

    def step(t, sa):
        v = vs[pl.ds(t, n_ch, stride=tc), :].reshape(slab)
        t_next = jnp.minimum(t + 1, tc - 1)

        def jbody(j, carry):
            out, sa_next = carry
            row = j * tc + t
            s_new = state_ref[j] * bcast(ws, row) + sa * bcast(bvs, row) + v * bcast(ks, row)
            state_ref[j] = s_new
            return out + s_new * bcast(rs, row), sa_next + s_new * bcast(ans, j * tc + t_next)

        out, sa_next = lax.fori_loop(0, n_ch, jbody, (jnp.zeros(slab, F32), jnp.zeros(slab, F32)), unroll=4)
        o = out.reshape(n_ch, LANES)
        mu = jnp.mean(o, axis=0, keepdims=True)
        d = o - mu
        var = jnp.mean(d * d, axis=0, keepdims=True)
        on = d * lax.rsqrt(var + GN_EPS) * lnw_ref[...] + lnb_ref[...]
        bonus = bon[pl.ds(t, 1), :] * v.reshape(n_ch, LANES)
        os[pl.ds(t, n_ch, stride=tc), :] = on + bonus
        return sa_next

    lax.fori_loop(0, tc, step, sa0)
    _from_lanes_gated(os, z_ref, o_ref, tc)


def rwkv_scan(proj_a, wa_pre, kkw, kaw, rkw, lnw, lnb, batch, seq, tc=32):
    width = HEAD_A * SCAN_HEADS
    seqb = lambda col: pl.BlockSpec((SCAN_BATCH, tc, width), lambda g, c: (g, c, col))
    par3 = pl.BlockSpec((HEAD_A, 1, LANES), lambda g, c: (0, 0, 0))
    par2 = pl.BlockSpec((HEAD_A, LANES), lambda g, c: (0, 0))
    buf = pltpu.VMEM((HEAD_A * tc, LANES), F32)
    return pl.pallas_call(
        functools.partial(_scan_kernel, tc=tc),
        grid=(batch // SCAN_BATCH, seq // tc),
        in_specs=[seqb(0), seqb(1), seqb(2), seqb(3), seqb(0), seqb(1), par3, par3, par3, par2, par2],
        out_specs=seqb(0),
        out_shape=jax.ShapeDtypeStruct((batch, seq, width), BF16),
        scratch_shapes=[pltpu.VMEM((HEAD_A, HEAD_A // SUBLANES, SUBLANES, LANES), F32)] + [buf] * 7
        + [pltpu.VMEM((tc, LANES), F32)],
        compiler_params=_params(ndim=2),
        name="rwkv_scan",
    )(proj_a, proj_a, proj_a, proj_a, wa_pre, wa_pre, kkw, kaw, rkw, lnw, lnb)


def _compress_kernel(x_ref, pe_ref, w1_ref, w2_ref, kg_ref, o_ref):
    which = pl.program_id(0)
    half = L_CMP // 2
    nblk = x_ref.shape[0] // CMP_STRIDE
    h1 = jnp.zeros((nblk, w1_ref.shape[-1]), F32)
    h2 = jnp.zeros((nblk, w1_ref.shape[-1]), F32)
    for l in range(half):
        x = x_ref[pl.ds(l, nblk, stride=CMP_STRIDE), :]
        a1 = (x + pe_ref[l:l + 1, :]).astype(BF16)
        a2 = (x + pe_ref[half + l:half + l + 1, :]).astype(BF16)
        h1 = h1 + jnp.dot(a1, w1_ref[l], preferred_element_type=F32)
        h2 = h2 + jnp.dot(a2, w1_ref[half + l], preferred_element_type=F32)
    hid = h1 + pltpu.roll(h2, nblk - 1, axis=0)
    hid = jax.nn.gelu(hid)
    out = jnp.dot(hid.astype(BF16), w2_ref[...], preferred_element_type=F32)
    ms = jnp.mean(out * out, axis=-1, keepdims=True)
    normed = out * lax.rsqrt(ms + NORM_EPS) * kg_ref[...]
    out = jnp.where(which == 0, normed, out)
    row = lax.broadcasted_iota(jnp.int32, out.shape, 0)
    o_ref[...] = jnp.where(row < nblk - 1, out, 0.0)


def compress(proj_b, kc_block, vc_block, pe, w1, w2, kgain, batch, seq):
    nblk = seq // CMP_STRIDE
    hidden = w1.shape[-1]

    def xmap(w, b, g):
        return (b, kc_block + w * (vc_block - kc_block) + g)

    return pl.pallas_call(
        _compress_kernel,
        grid=(2, batch, KV_GROUPS),
        in_specs=[
            pl.BlockSpec((seq, HEAD_B), xmap),
            pl.BlockSpec((None, L_CMP, HEAD_B), lambda w, b, g: (w, 0, 0)),
            pl.BlockSpec((None, L_CMP, HEAD_B, hidden), lambda w, b, g: (w, 0, 0, 0)),
            pl.BlockSpec((None, hidden, HEAD_B), lambda w, b, g: (w, 0, 0)),
            pl.BlockSpec((1, HEAD_B), lambda w, b, g: (0, 0)),
        ],
        out_specs=pl.BlockSpec((None, None, None, nblk, HEAD_B), lambda w, b, g: (w, b, g, 0, 0)),
        out_shape=jax.ShapeDtypeStruct((2, batch, KV_GROUPS, nblk, HEAD_B), F32),
        compiler_params=_params(ndim=3),
        name="nsa_compress",
    )(proj_b, pe, w1, w2, kgain)


def _rms(x, gain):
    ms = jnp.mean(x * x, axis=-1, keepdims=True)
    return x * lax.rsqrt(ms + NORM_EPS) * gain


def _dot_nt(a, b):
    return lax.dot_general(a, b, (((1,), (1,)), ((), ())), preferred_element_type=F32)


LOG2E = 1.4426950408889634
VAUG = 2 * HEAD_B


def _attn_kernel(q_ref, ks_ref, vs_ref, kw_ref, vw_ref, kc_ref, vc_ref, gt_ref, z_ref, qg_ref, kg_ref,
                 ovt_ref, ex_ref, wb_ref, o_ref, ksn_ref, vs1_ref, kwn_ref, vw1_ref, q_scr, m_scr, acc_scr,
                 oc_scr, ow_scr, *, seq):
    qt = pl.program_id(2)
    ncmp = kc_ref.shape[0]
    nsel = seq // L_SEL

    @pl.when(qt == 0)
    def _():
        ones_col = jnp.ones((seq, HEAD_B), BF16)
        ksn_ref[...] = _rms(ks_ref[...], kg_ref[1:2, :]).astype(BF16)
        vs1_ref[:, 0:HEAD_B] = vs_ref[...].astype(BF16)
        vs1_ref[:, HEAD_B:VAUG] = ones_col
        kwn_ref[0:WINDOW, :] = jnp.zeros((WINDOW, HEAD_B), BF16)
        vw1_ref[0:WINDOW, :] = jnp.zeros((WINDOW, VAUG), BF16)
        kwn_ref[WINDOW:WINDOW + seq, :] = _rms(kw_ref[...], kg_ref[2:3, :]).astype(BF16)
        vw1_ref[WINDOW:WINDOW + seq, 0:HEAD_B] = vw_ref[...].astype(BF16)
        vw1_ref[WINDOW:WINDOW + seq, HEAD_B:VAUG] = ones_col

    t0 = pl.multiple_of(qt * TQ, TQ)
    for r in range(REP):
        q_scr[r] = (_rms(q_ref[:, r * HEAD_B:(r + 1) * HEAD_B], qg_ref[...]) * (ATTN_SCALE * LOG2E)).astype(BF16)

    def t_of(shape):
        return t0 + lax.broadcasted_iota(jnp.int32, shape, 0)

    n_idx = lax.broadcasted_iota(jnp.int32, (TQ, ncmp), 1)
    bias_c = jnp.where(n_idx * CMP_STRIDE + (L_CMP - 1) <= t_of((TQ, ncmp)), 0.0, NEG_INF)
    bias_c = jnp.where(n_idx < ncmp - 1, bias_c, NEG_INF)
    row_ok = jnp.where(t_of((TQ, 1)) >= L_CMP - 1, 1.0, 0.0)
    q_all = q_scr[...].reshape(REP * TQ, HEAD_B)
    head = lambda x, r: x[r * TQ:(r + 1) * TQ]
    s_all = _dot_nt(q_all, kc_ref[...].astype(BF16))
    prs = []
    for r in range(REP):
        s = head(s_all, r) + bias_c
        p = jnp.exp2(s - jnp.max(s, axis=-1, keepdims=True))
        prs.append(p * (row_ok / jnp.sum(p, axis=-1, keepdims=True)))
    psum = prs[0] + prs[1] + prs[2] + prs[3]
    oc_scr[...] = jnp.dot(jnp.concatenate(prs, axis=0).astype(BF16), vc_ref[...].astype(BF16),
                          preferred_element_type=F32).reshape(REP, TQ, HEAD_B)
    imp_t = lax.dot_general(ovt_ref[...], psum, (((1,), (1,)), ((), ())), preferred_element_type=F32,
                            precision=lax.Precision.HIGHEST)

    j_idx = lax.broadcasted_iota(jnp.int32, (nsel, TQ), 0)
    t_sel = t0 + lax.broadcasted_iota(jnp.int32, (nsel, TQ), 1)
    cur = t_sel // L_SEL
    forced = jnp.where((j_idx == 0) | (j_idx == cur) | (j_idx == cur - 1), FORCE_BONUS, 0.0)
    score = jnp.where(j_idx * L_SEL <= t_sel, imp_t + forced, NEG_INF)
    rank = jnp.zeros((nsel, TQ), F32)
    for i in range(nsel):
        si = score[i:i + 1, :]
        ahead = (si > score) | ((si == score) & (j_idx > i))
        rank = rank + jnp.where(ahead, 1.0, 0.0)
    sel_t = jnp.where((rank < min(N_SEL, nsel)) & (score > 0.5 * NEG_INF), 1.0, 0.0)
    sel = jnp.concatenate([sel_t, jnp.zeros((LANES - nsel, TQ), F32)], axis=0).T
    not_sel = (1.0 - sel).astype(BF16)

    span = WINDOW + TQ
    kwin = kwn_ref[pl.ds(t0, span), :]
    vwin = vw1_ref[pl.ds(t0, span), :]
    lane_w = lax.broadcasted_iota(jnp.int32, (TQ, span), 1)
    bias_w = jnp.where(lane_w >= WINDOW - t0, wb_ref[...], NEG_INF)
    s_all = _dot_nt(q_all, kwin)
    ps = []
    for r in range(REP):
        s = head(s_all, r) + bias_w
        ps.append(jnp.exp2(s - jnp.max(s, axis=-1, keepdims=True)).astype(BF16))
    ow_scr[...] = jnp.dot(jnp.concatenate(ps, axis=0), vwin, preferred_element_type=F32)

    m_scr[...] = jnp.full(m_scr.shape, NEG_INF, F32)
    acc_scr[...] = jnp.zeros(acc_scr.shape, F32)

    def scores(c):
        k0 = pl.multiple_of(c * KC, KC)
        return _dot_nt(q_scr[...].reshape(REP * TQ, HEAD_B), ksn_ref[pl.ds(k0, KC), :])

    def sel_bias(c):
        return jnp.dot(not_sel, ex_ref[c], preferred_element_type=F32)

    def sel_chunk(c, s_all, bias, diagonal):
        k0 = pl.multiple_of(c * KC, KC)
        v1 = vs1_ref[pl.ds(k0, KC), :]
        if diagonal:
            lane = lax.broadcasted_iota(jnp.int32, (TQ, KC), 1)
            bias = jnp.where(k0 + lane <= t_of((TQ, KC)), bias, NEG_INF)
        ps, alphas = [], []
        for r in range(REP):
            s = head(s_all, r) + bias
            m_old = m_scr[r]
            m_new = jnp.maximum(m_old, jnp.max(s, axis=-1, keepdims=True))
            m_scr[r] = m_new
            ps.append(jnp.exp2(s - jnp.tile(m_new, (1, KC // LANES))).astype(BF16))
            alphas.append(jnp.exp2(m_old - m_new))
        pv = jnp.dot(jnp.concatenate(ps, axis=0), v1, preferred_element_type=F32)
        for r in range(REP):
            acc_scr[r] = jnp.tile(alphas[r], (1, VAUG // LANES)) * acc_scr[r] + head(pv, r)

    n_full = t0 // KC

    def full_chunk(c, carry):
        s_cur, bias_cur = carry
        nxt = (scores(c + 1), sel_bias(c + 1))
        sel_chunk(c, s_cur, bias_cur, False)
        return nxt

    s_last, bias_last = lax.fori_loop(0, n_full, full_chunk, (scores(0), sel_bias(0)))
    sel_chunk(n_full, s_last, bias_last, True)

    gts = jax.nn.sigmoid(gt_ref[...])
    for r in range(REP):
        ow = ow_scr[r * TQ:(r + 1) * TQ, :]
        o_w = ow[:, 0:HEAD_B] / ow[:, HEAD_B:VAUG]
        acc = acc_scr[r]
        o_s = acc[:, 0:HEAD_B] / acc[:, HEAD_B:VAUG]
        o = (gts[:, 3 * r:3 * r + 1] * oc_scr[r] + gts[:, 3 * r + 1:3 * r + 2] * o_s
             + gts[:, 3 * r + 2:3 * r + 3] * o_w)
        z = z_ref[:, r * HEAD_B:(r + 1) * HEAD_B]
        o_ref[:, r * HEAD_B:(r + 1) * HEAD_B] = (o * (z * jax.nn.sigmoid(z))).astype(o_ref.dtype)


def nsa_attention(proj_b, cmp_kv, gates, q_gain, k_gain, overlap_t, expand, win_bias, blocks, batch, seq):
    nq = seq // TQ
    ncmp = seq // CMP_STRIDE
    gw = REP * HEAD_B // LANES
    qspec = lambda off: pl.BlockSpec((TQ, REP * HEAD_B), lambda b, g, t: (b * nq + t, off // gw + g))
    kvspec = lambda off: pl.BlockSpec((seq, HEAD_B), lambda b, g, t: (b, off + g))
    cspec = lambda w: pl.BlockSpec((None, None, None, ncmp, HEAD_B), lambda b, g, t: (w, b, g, 0, 0))
    full = lambda a: pl.BlockSpec(a.shape, lambda b, g, t: (0,) * a.ndim)
    return pl.pallas_call(
        functools.partial(_attn_kernel, seq=seq),
        grid=(batch, KV_GROUPS, nq),
        in_specs=[
            qspec(blocks["q"]), kvspec(blocks["ks"]), kvspec(blocks["vs"]), kvspec(blocks["kw"]), kvspec(blocks["vw"]),
            cspec(0), cspec(1),
            pl.BlockSpec((None, None, TQ, 16), lambda b, g, t: (b, g, t, 0)),
            qspec(blocks["z"]),
            full(q_gain), full(k_gain), full(overlap_t), full(expand), full(win_bias),
        ],
        out_specs=pl.BlockSpec((TQ, REP * HEAD_B), lambda b, g, t: (b * nq + t, g)),
        out_shape=jax.ShapeDtypeStruct((batch * seq, KV_GROUPS * REP * HEAD_B), BF16),
        scratch_shapes=[
            pltpu.VMEM((seq, HEAD_B), BF16), pltpu.VMEM((seq, VAUG), BF16),
            pltpu.VMEM((seq + WINDOW, HEAD_B), BF16), pltpu.VMEM((seq + WINDOW, VAUG), BF16),
            pltpu.VMEM((REP, TQ, HEAD_B), BF16), pltpu.VMEM((REP, TQ, LANES), F32),
            pltpu.VMEM((REP, TQ, VAUG), F32), pltpu.VMEM((REP, TQ, HEAD_B), F32),
            pltpu.VMEM((REP * TQ, VAUG), F32),
        ],
        compiler_params=_params(ndim=3),
        name="nsa_attention",
    )(proj_b, proj_b, proj_b, proj_b, proj_b, cmp_kv, cmp_kv, gates, proj_b, q_gain, k_gain, overlap_t, expand,
      win_bias)


def _merge_kernel(ya_ref, yb_ref, wa_ref, wb_ref, ga_ref, gb_ref, out_ref):
    ua = jnp.dot(ya_ref[...], wa_ref[...], preferred_element_type=F32)
    ub = jnp.dot(yb_ref[...], wb_ref[...], preferred_element_type=F32)
    out_ref[...] = (jax.nn.sigmoid(ga_ref[...]) * ua + jax.nn.sigmoid(gb_ref[...]) * ub).astype(out_ref.dtype)


def merge(ya, yb, w_a, w_b, proj_b, ga_block, gb_block, tm=1024, tn=512):
    m, ka = ya.shape
    n = w_a.shape[1]
    tm = min(tm, m)
    return pl.pallas_call(
        _merge_kernel,
        grid=(m // tm, n // tn),
        in_specs=[
            pl.BlockSpec((tm, ka), lambda i, j: (i, 0)),
            pl.BlockSpec((tm, ka), lambda i, j: (i, 0)),
            pl.BlockSpec((ka, tn), lambda i, j: (0, j)),
            pl.BlockSpec((ka, tn), lambda i, j: (0, j)),
            pl.BlockSpec((tm, tn), lambda i, j: (i, ga_block + j)),
            pl.BlockSpec((tm, tn), lambda i, j: (i, gb_block + j)),
        ],
        out_specs=pl.BlockSpec((tm, tn), lambda i, j: (i, j)),
        out_shape=jax.ShapeDtypeStruct((m, n), BF16),
        compiler_params=_params(ndim=2),
        name="merge",
    )(ya, yb, w_a, w_b, proj_b, proj_b)


def _ple_embed_kernel(p_ref, w_ref, g_ref, o_ref):
    acc = jnp.dot(p_ref[...].astype(BF16), w_ref[...], preferred_element_type=F32)
    o_ref[...] = _rms(acc, g_ref[...]).astype(o_ref.dtype)


def ple_embed(p, w, g, tm=256):
    m, k = p.shape
    n = w.shape[1]
    return pl.pallas_call(
        _ple_embed_kernel,
        grid=(m // tm,),
        in_specs=[pl.BlockSpec((tm, k), lambda i: (i, 0)), pl.BlockSpec((k, n), lambda i: (0, 0)),
                  pl.BlockSpec((1, n), lambda i: (0, 0))],
        out_specs=pl.BlockSpec((tm, n), lambda i: (i, 0)),
        out_shape=jax.ShapeDtypeStruct((m, n), F32),
        compiler_params=_params(),
        name="ple_embed",
    )(p, w, g.reshape(1, n))


def head_minor(w, heads):
    lead = w.shape[:-1]
    return w.reshape(*lead, heads, HEAD_A).swapaxes(-1, -2).reshape(*lead, heads * HEAD_A)


def rwkv_branch(proj_a, batch, seq, a_width, w_lora_up, w0, a_lora_up, a0, k_k, k_a, r_k, lnx_w, lnx_b):
    tokens = batch * seq
    heads_a = a_width // HEAD_A
    assert heads_a == SCAN_HEADS and batch % SCAN_BATCH == 0
    lora = w_lora_up.shape[0]
    zeros = jnp.zeros((lora, a_width), F32)
    hm = lambda w: head_minor(w, heads_a)
    w_lora = jnp.concatenate([jnp.concatenate([hm(w_lora_up), zeros], 1),
                              jnp.concatenate([zeros, hm(a_lora_up)], 1)], 0)
    b_lora = jnp.concatenate([hm(w0), hm(a0)]).reshape(1, -1)
    wa_pre = lora_project(proj_a, 4 * a_width // LANES, w_lora, b_lora)

    def par_scan(t):
        return jnp.tile(t.reshape(heads_a, HEAD_A).T, (1, SCAN_BATCH))

    par3 = lambda t: par_scan(t).reshape(HEAD_A, 1, LANES)
    o = rwkv_scan(proj_a.reshape(batch, seq, -1), wa_pre.reshape(batch, seq, -1),
                  par3(k_k), par3(k_a), par3(r_k), par_scan(lnx_w), par_scan(lnx_b), batch, seq)
    return o.reshape(tokens, a_width)


def nsa_branch(proj_b, graw, batch, seq, b_width, q_norm_g, k_norm_g, pe_cmp_k, pe_cmp_v,
               cmp_k_w1, cmp_k_w2, cmp_v_w1, cmp_v_w2):
    kv_width = KV_GROUPS * HEAD_B
    blk = lambda cols: cols // LANES
    blocks = {"q": 0, "kc": blk(b_width), "vc": blk(b_width + kv_width), "ks": blk(b_width + 2 * kv_width),
              "vs": blk(b_width + 3 * kv_width), "kw": blk(b_width + 4 * kv_width), "vw": blk(b_width + 5 * kv_width),
              "z": blk(b_width + 6 * kv_width)}
    pe = jnp.stack([pe_cmp_k, pe_cmp_v])
    hidden = cmp_k_w1.shape[1]
    w1 = jnp.stack([cmp_k_w1, cmp_v_w1]).reshape(2, L_CMP, HEAD_B, hidden).astype(BF16)
    w2 = jnp.stack([cmp_k_w2, cmp_v_w2]).astype(BF16)
    cmp_kv = compress(proj_b, blocks["kc"], blocks["vc"], pe, w1, w2, k_norm_g[0:1], batch, seq)

    ncmp = seq // CMP_STRIDE
    nsel = seq // L_SEL
    c_start = jnp.arange(ncmp) * CMP_STRIDE
    s_start = jnp.arange(nsel) * L_SEL
    overlap_t = ((c_start[None, :] < (s_start + L_SEL)[:, None]) & (s_start[:, None] < (c_start + L_CMP)[None, :])
                 & (jnp.arange(ncmp)[None, :] < ncmp - 1)).astype(F32)
    key_blk = jnp.arange(seq) // L_SEL
    expand = jnp.where(jnp.arange(LANES)[:, None] == key_blk[None, :], NEG_INF, 0.0).astype(BF16)
    expand = expand.reshape(LANES, seq // KC, KC).transpose(1, 0, 2)
    tl = jnp.arange(TQ)[:, None]
    u = jnp.arange(WINDOW + TQ)[None, :]
    win_bias = jnp.where((u > tl) & (u <= WINDOW + tl), 0.0, NEG_INF).astype(F32)
    gates = graw.reshape(batch, seq, KV_GROUPS, REP * 3).transpose(0, 2, 1, 3)
    gates = jnp.pad(gates, ((0, 0), (0, 0), (0, 0), (0, 16 - REP * 3)))
    return nsa_attention(proj_b, cmp_kv, gates, q_norm_g.reshape(1, HEAD_B), k_norm_g, overlap_t, expand, win_bias,
                         blocks, batch, seq)


def _layer(x, p, norm_g, w_in, shift_mu, w_lora_up, w0, a_lora_up, a0, k_k, k_a, r_k, lnx_w, lnx_b,
           q_norm_g, k_norm_g, pe_cmp_k, pe_cmp_v, cmp_k_w1, cmp_k_w2, cmp_v_w1, cmp_v_w2,
           w_up_a, w_up_b, w_out, ple_pre_g, w_ple_gate, w_ple, ple_post_g):
    batch, seq, d = x.shape
    tokens = batch * seq
    a_width = w_up_a.shape[0]
    b_width = w_up_b.shape[0]
    heads_a = a_width // HEAD_A
    kv_width = KV_GROUPS * HEAD_B
    n_gate = KV_GROUPS * REP * 3
    lora = w_lora_up.shape[0]
    assert lora == HEAD_A and a_lora_up.shape[0] == HEAD_A and 2 * lora == LANES
    a_cols = 4 * a_width + 2 * lora
    g_off = a_cols + b_width + 6 * kv_width
    zb_off = g_off + n_gate
    assert w_in.shape[1] == zb_off + b_width + 2 * d

    pad_a = LANES - n_gate
    hm4 = lambda w: head_minor(w.reshape(*w.shape[:-1], 4, a_width), heads_a).reshape(*w.shape[:-1], 4 * a_width)
    w_a = jnp.concatenate([hm4(w_in[:, :4 * a_width]), w_in[:, 4 * a_width:a_cols], w_in[:, g_off:zb_off],
                           jnp.zeros((d, pad_a), F32)], axis=1).astype(BF16)
    mu_a = jnp.concatenate([hm4(shift_mu[:4 * a_width]), shift_mu[4 * a_width:], jnp.zeros((LANES,), F32)]).reshape(1, -1)
    w_b = jnp.concatenate([w_in[:, a_cols:g_off], w_in[:, zb_off:]], axis=1).astype(BF16)
    na = w_a.shape[1]
    tn_a = 768 if na % 768 == 0 else LANES

    x2 = x.reshape(tokens, d)
    h = rmsnorm_rows(x2, norm_g, BF16)
    tm = 512
    proj_a = matmul(h, w_a, mode="shift", extras=(mu_a,), tm=tm, tn=tn_a, tiles_per_seq=seq // tm, name="proj_a")
    proj_b = matmul(h, w_b, tm=tm, tn=1024, name="proj_b")

    ya = rwkv_branch(proj_a, batch, seq, a_width, w_lora_up, w0, a_lora_up, a0, k_k, k_a, r_k, lnx_w, lnx_b)
    graw = proj_a[:, a_cols:a_cols + n_gate]
    yb = nsa_branch(proj_b, graw, batch, seq, b_width, q_norm_g, k_norm_g, pe_cmp_k, pe_cmp_v,
                    cmp_k_w1, cmp_k_w2, cmp_v_w1, cmp_v_w2)

    gate_blk = (b_width + 6 * kv_width + b_width) // 512
    w_up_a_nm = head_minor(w_up_a.T, heads_a).T
    merged = merge(ya, yb, w_up_a_nm.astype(BF16), w_up_b.astype(BF16), proj_b, gate_blk, gate_blk + d // 512)
    x1 = matmul(merged, w_out.astype(BF16), mode="resid", extras=(x2,), tm=tm, tn=1024, name="out_proj")
    hn = rmsnorm_rows(x1, ple_pre_g, BF16)
    e = ple_embed(p.reshape(tokens, -1), w_ple.astype(BF16), ple_post_g)
    out = matmul(hn, w_ple_gate.astype(BF16), mode="ple", extras=(x1, e), tm=tm, tn=1024, name="ple_gate")
    return out.reshape(batch, seq, d)


def kernel(x, p, norm_g, w_in, shift_mu, w_lora_up, w0, a_lora_up, a0, k_k, k_a, r_k, lnx_w, lnx_b, q_norm_g, k_norm_g, pe_cmp_k, pe_cmp_v, cmp_k_w1, cmp_k_w2, cmp_v_w1, cmp_v_w2, w_up_a, w_up_b, w_out, ple_pre_g, w_ple_gate, w_ple, ple_post_g):
    depth = w_in.shape[0]
    for i in range(depth):
        x = _layer(x, p[i], norm_g[i], w_in[i], shift_mu[i], w_lora_up[i], w0[i], a_lora_up[i], a0[i], k_k[i],
                   k_a[i], r_k[i], lnx_w[i], lnx_b[i], q_norm_g[i], k_norm_g[i], pe_cmp_k[i], pe_cmp_v[i],
                   cmp_k_w1[i], cmp_k_w2[i], cmp_v_w1[i], cmp_v_w2[i], w_up_a[i], w_up_b[i], w_out[i],
                   ple_pre_g[i], w_ple_gate[i], w_ple[i], ple_post_g[i])
    return x
```

```python
import functools

import jax
import jax.numpy as jnp
from jax import lax
from jax.experimental import pallas as pl
from jax.experimental.pallas import tpu as pltpu

F32 = jnp.float32
BF16 = jnp.bfloat16

LANES = 128
SUBLANES = 8
VMEM_LIMIT = 56 * 1024 * 1024

NORM_EPS = 1e-6
NEG_INF = -1e30
HEAD_A = 64
GN_EPS = 64e-5
HEAD_B = 128
KV_GROUPS = 4
REP = 4
L_CMP = 32
CMP_STRIDE = 16
L_SEL = 64
N_SEL = 16
WINDOW = 512
TQ = 128
KC = 256
FORCE_BONUS = 1e3
ATTN_SCALE = HEAD_B ** -0.5
DECAY_SCALE = 0.6065306597126334


def _params(vmem=VMEM_LIMIT, ndim=1):
    return pltpu.CompilerParams(dimension_semantics=("arbitrary",) * ndim, vmem_limit_bytes=vmem)


def _rmsnorm_kernel(x_ref, g_ref, o_ref):
    x = x_ref[...]
    ms = jnp.mean(x * x, axis=-1, keepdims=True)
    o_ref[...] = (x * lax.rsqrt(ms + NORM_EPS) * g_ref[...]).astype(o_ref.dtype)


def rmsnorm_rows(x, g, out_dtype, tm=256):
    m, d = x.shape
    return pl.pallas_call(
        _rmsnorm_kernel,
        grid=(m // tm,),
        in_specs=[pl.BlockSpec((tm, d), lambda i: (i, 0)), pl.BlockSpec((1, d), lambda i: (0, 0))],
        out_specs=pl.BlockSpec((tm, d), lambda i: (i, 0)),
        out_shape=jax.ShapeDtypeStruct((m, d), out_dtype),
        compiler_params=_params(),
        name="rmsnorm",
    )(x, g.reshape(1, d))


def _mm_kernel(*refs, mode, tiles_per_seq):
    a_ref, b_ref = refs[0], refs[1]
    acc = jnp.dot(a_ref[...], b_ref[...], preferred_element_type=F32)
    if mode == "plain":
        o_ref = refs[2]
        o_ref[...] = acc.astype(o_ref.dtype)
    elif mode == "shift":
        mu_ref, o_ref, carry_ref = refs[2], refs[3], refs[4]
        i = pl.program_id(1)
        tm = acc.shape[0]
        first = (i % tiles_per_seq) == 0
        last_prev = jnp.where(first, 0.0, carry_ref[SUBLANES - 1:SUBLANES, :])
        rolled = pltpu.roll(acc, 1, axis=0)
        row = lax.broadcasted_iota(jnp.int32, acc.shape, 0)
        prev = jnp.where(row == 0, last_prev, rolled)
        carry_ref[...] = acc[tm - SUBLANES:tm, :]
        o_ref[...] = (acc + mu_ref[...] * (prev - acc)).astype(o_ref.dtype)
    elif mode == "resid":
        r_ref, o_ref = refs[2], refs[3]
        o_ref[...] = (r_ref[...] + acc).astype(o_ref.dtype)
    elif mode == "ple":
        x_ref, e_ref, o_ref = refs[2], refs[3], refs[4]
        o_ref[...] = (x_ref[...] + jax.nn.sigmoid(acc) * e_ref[...]).astype(o_ref.dtype)
    else:
        raise ValueError(mode)


def matmul(a, b, *, mode="plain", extras=(), out_dtype=F32, tm=512, tn=1024, tiles_per_seq=1, name="mm"):
    m, k = a.shape
    n = b.shape[1]
    assert m % tm == 0 and n % tn == 0, (m, n, tm, tn)
    in_specs = [pl.BlockSpec((tm, k), lambda j, i: (i, 0)), pl.BlockSpec((k, tn), lambda j, i: (0, j))]
    scratch = []
    if mode == "shift":
        in_specs.append(pl.BlockSpec((1, tn), lambda j, i: (0, j)))
        scratch.append(pltpu.VMEM((SUBLANES, tn), F32))
    elif mode == "resid":
        in_specs.append(pl.BlockSpec((tm, tn), lambda j, i: (i, j)))
    elif mode == "ple":
        in_specs += [pl.BlockSpec((tm, tn), lambda j, i: (i, j)), pl.BlockSpec((tm, tn), lambda j, i: (i, j))]
    return pl.pallas_call(
        functools.partial(_mm_kernel, mode=mode, tiles_per_seq=tiles_per_seq),
        grid=(n // tn, m // tm),
        in_specs=in_specs,
        out_specs=pl.BlockSpec((tm, tn), lambda j, i: (i, j)),
        out_shape=jax.ShapeDtypeStruct((m, n), out_dtype),
        scratch_shapes=scratch,
        compiler_params=_params(ndim=2),
        name=name,
    )(a, b, *extras)


def _lora_kernel(x_ref, w_ref, b_ref, o_ref):
    x = x_ref[...]
    lane = lax.broadcasted_iota(jnp.int32, x.shape, 1)
    x = jnp.where(lane < HEAD_A, jnp.tanh(x), x)
    acc = jnp.dot(x, w_ref[...], preferred_element_type=F32, precision=lax.Precision.HIGHEST)
    o_ref[...] = acc + b_ref[...]


def lora_project(proj_a, col_block, w_blockdiag, bias, tm=1024, tn=1024):
    m = proj_a.shape[0]
    n = w_blockdiag.shape[1]
    tm = min(tm, m)
    assert m % tm == 0 and n % tn == 0
    return pl.pallas_call(
        _lora_kernel,
        grid=(m // tm, n // tn),
        in_specs=[
            pl.BlockSpec((tm, LANES), lambda i, j: (i, col_block)),
            pl.BlockSpec((LANES, tn), lambda i, j: (0, j)),
            pl.BlockSpec((1, tn), lambda i, j: (0, j)),
        ],
        out_specs=pl.BlockSpec((tm, tn), lambda i, j: (i, j)),
        out_shape=jax.ShapeDtypeStruct((m, n), F32),
        compiler_params=_params(ndim=2),
        name="lora",
    )(proj_a, w_blockdiag, bias)


SCAN_BATCH = 4
SCAN_HEADS = LANES // SCAN_BATCH
NGRP = LANES // SCAN_HEADS


def _segment_transpose(x):
    seg = lax.broadcasted_iota(jnp.int32, x[0].shape, 1) // SCAN_HEADS
    y = []
    for i in range(NGRP):
        out = None
        for j in range(SCAN_BATCH):
            shift = ((j - i) % NGRP) * SCAN_HEADS
            piece = x[j] if shift == 0 else pltpu.roll(x[j], shift, axis=1)
            out = piece if out is None else jnp.where(seg == j, piece, out)
        y.append(out)
    return y


def _to_lanes(x_ref, dst_ref, tc):
    for g in range(HEAD_A // NGRP):
        y = _segment_transpose([x_ref[b, :, g * LANES:(g + 1) * LANES] for b in range(SCAN_BATCH)])
        for n_lo in range(NGRP):
            n = g * NGRP + n_lo
            dst_ref[n * tc:(n + 1) * tc, :] = y[n_lo]


def _from_lanes_gated(src_ref, z_ref, o_ref, tc):
    for g in range(HEAD_A // NGRP):
        y = _segment_transpose([src_ref[(g * NGRP + n_lo) * tc:(g * NGRP + n_lo + 1) * tc, :] for n_lo in range(NGRP)])
        for b in range(SCAN_BATCH):
            z = z_ref[b, :, g * LANES:(g + 1) * LANES]
            o_ref[b, :, g * LANES:(g + 1) * LANES] = (y[b] * (z * jax.nn.sigmoid(z))).astype(o_ref.dtype)


GATE_TM = 512
GATE_TN = 1024
GATE_SM = 128
GATE_SN = 256


def _scan_kernel(r_ref, k_ref, v_ref, z_ref, wp_ref, ap_ref, kkw_ref, kaw_ref, rkw_ref, lnw_ref, lnb_ref,
                 h_ref, wg_ref, o_ref, gate_ref, state_ref, rs, ks, vs, ws, ans, bvs, os, bon, h_scr, wg_scr,
                 *, tc, steps_per_wg):
    n_ch = HEAD_A

    @pl.when(pl.program_id(1) == 0)
    def _():
        state_ref[...] = jnp.zeros_like(state_ref)

    h_scr[...] = h_ref[...]
    step_id = pl.program_id(0) * pl.num_programs(1) + pl.program_id(1)

    @pl.when(step_id % steps_per_wg == 0)
    def _():
        wg_scr[...] = wg_ref[...]

    _to_lanes(r_ref, rs, tc)
    _to_lanes(k_ref, ks, tc)
    _to_lanes(v_ref, vs, tc)
    _to_lanes(wp_ref, ws, tc)
    _to_lanes(ap_ref, bvs, tc)

    def rows(n):
        return pl.ds(pl.multiple_of(n * tc, tc), tc)

    def norm_acc(n, acc):
        kkr = ks[rows(n), :] * kkw_ref[n]
        return acc + kkr * kkr

    nsq = lax.fori_loop(0, n_ch, norm_acc, jnp.zeros((tc, LANES), F32), unroll=8)
    inv = 1.0 / jnp.maximum(jnp.sqrt(nsq), 1e-12)

    def prep(n, bacc):
        k = ks[rows(n), :]
        a = jax.nn.sigmoid(bvs[rows(n), :])
        kk = k * kkw_ref[n] * inv
        ws[rows(n), :] = jnp.exp(-DECAY_SCALE * jax.nn.sigmoid(ws[rows(n), :]))
        ans[rows(n), :] = -kk
        bvs[rows(n), :] = kk * a
        kmod = k * (1.0 + (a - 1.0) * kaw_ref[n])
        ks[rows(n), :] = kmod
        return bacc + rs[rows(n), :] * kmod * rkw_ref[n]

    bon[...] = lax.fori_loop(0, n_ch, prep, jnp.zeros((tc, LANES), F32), unroll=4)

    def bcast(ref, row):
        return ref[pl.ds(row, 1), :][None]

    def sa_first(j, acc):
        return acc + state_ref[j] * bcast(ans, j * tc)

    slab = (n_ch // SUBLANES, SUBLANES, LANES)
    sa0 = lax.fori_loop(0, n_ch, sa_first, jnp.zeros(slab, F32), unroll=4)

    def step(t, sa):
        v = vs[pl.ds(t, n_ch, stride=tc), :].reshape(slab)
        t_next = jnp.minimum(t + 1, tc - 1)

        r0 = pl.multiple_of((t // (GATE_TN // GATE_SN)) * GATE_SM, GATE_SM)
        c0 = pl.multiple_of((t % (GATE_TN // GATE_SN)) * GATE_SN, GATE_SN)
        logits = jnp.dot(h_scr[pl.ds(r0, GATE_SM), :], wg_scr[:, pl.ds(c0, GATE_SN)], preferred_element_type=F32)
        gate_ref[pl.ds(r0, GATE_SM), pl.ds(c0, GATE_SN)] = jax.nn.sigmoid(logits).astype(gate_ref.dtype)

        out = jnp.zeros(slab, F32)
        sa_next = jnp.zeros(slab, F32)
        for j in range(n_ch):
            row = j * tc + t
            s_new = state_ref[j] * bcast(ws, row) + sa * bcast(bvs, row) + v * bcast(ks, row)
            state_ref[j] = s_new
            out = out + s_new * bcast(rs, row)
            sa_next = sa_next + s_new * bcast(ans, j * tc + t_next)
        o = out.reshape(n_ch, LANES)
        mu = jnp.mean(o, axis=0, keepdims=True)
        d = o - mu
        var = jnp.mean(d * d, axis=0, keepdims=True)
        on = d * lax.rsqrt(var + GN_EPS) * lnw_ref[...] + lnb_ref[...]
        bonus = bon[pl.ds(t, 1), :] * v.reshape(n_ch, LANES)
        os[pl.ds(t, n_ch, stride=tc), :] = on + bonus
        return sa_next

    lax.fori_loop(0, tc, step, sa0)
    _from_lanes_gated(os, z_ref, o_ref, tc)


def rwkv_scan(proj_a, wa_pre, kkw, kaw, rkw, lnw, lnb, h, w_gate, batch, seq):
    width = HEAD_A * SCAN_HEADS
    tc = (GATE_TM // GATE_SM) * (GATE_TN // GATE_SN)
    n_chunks = seq // tc
    tokens, d = h.shape
    n_gate = w_gate.shape[1]
    m_tiles = tokens // GATE_TM
    assert tokens % GATE_TM == 0 and n_gate % GATE_TN == 0
    assert (batch // SCAN_BATCH) * n_chunks == m_tiles * (n_gate // GATE_TN), "one gate tile per scan grid step"

    def gate_tile(g, c):
        s = g * n_chunks + c
        return s % m_tiles, s // m_tiles

    seqb = lambda col: pl.BlockSpec((SCAN_BATCH, tc, width), lambda g, c: (g, c, col))
    par3 = pl.BlockSpec((HEAD_A, 1, LANES), lambda g, c: (0, 0, 0))
    par2 = pl.BlockSpec((HEAD_A, LANES), lambda g, c: (0, 0))
    buf = pltpu.VMEM((HEAD_A * tc, LANES), F32)
    return pl.pallas_call(
        functools.partial(_scan_kernel, tc=tc, steps_per_wg=m_tiles),
        grid=(batch // SCAN_BATCH, n_chunks),
        in_specs=[seqb(0), seqb(1), seqb(2), seqb(3), seqb(0), seqb(1), par3, par3, par3, par2, par2,
                  pl.BlockSpec((GATE_TM, d), lambda g, c: (gate_tile(g, c)[0], 0)),
                  pl.BlockSpec((d, GATE_TN), lambda g, c: (0, gate_tile(g, c)[1]), pipeline_mode=pl.Buffered(1))],
        out_specs=[seqb(0), pl.BlockSpec((GATE_TM, GATE_TN), lambda g, c: gate_tile(g, c))],
        out_shape=[jax.ShapeDtypeStruct((batch, seq, width), BF16), jax.ShapeDtypeStruct((tokens, n_gate), BF16)],
        scratch_shapes=[pltpu.VMEM((HEAD_A, HEAD_A // SUBLANES, SUBLANES, LANES), F32)] + [buf] * 7
        + [pltpu.VMEM((tc, LANES), F32), pltpu.VMEM((GATE_TM, d), BF16), pltpu.VMEM((d, GATE_TN), BF16)],
        compiler_params=_params(ndim=2),
        name="rwkv_scan",
    )(proj_a, proj_a, proj_a, proj_a, wa_pre, wa_pre, kkw, kaw, rkw, lnw, lnb, h, w_gate)


def _compress_kernel(x_ref, pe_ref, w1_ref, w2_ref, kg_ref, o_ref):
    which = pl.program_id(0)
    half = L_CMP // 2
    nblk = x_ref.shape[0] // CMP_STRIDE
    h1 = jnp.zeros((nblk, w1_ref.shape[-1]), F32)
    h2 = jnp.zeros((nblk, w1_ref.shape[-1]), F32)
    for l in range(half):
        x = x_ref[pl.ds(l, nblk, stride=CMP_STRIDE), :]
        a1 = (x + pe_ref[l:l + 1, :]).astype(BF16)
        a2 = (x + pe_ref[half + l:half + l + 1, :]).astype(BF16)
        h1 = h1 + jnp.dot(a1, w1_ref[l], preferred_element_type=F32)
        h2 = h2 + jnp.dot(a2, w1_ref[half + l], preferred_element_type=F32)
    hid = h1 + pltpu.roll(h2, nblk - 1, axis=0)
    hid = jax.nn.gelu(hid)
    out = jnp.dot(hid.astype(BF16), w2_ref[...], preferred_element_type=F32)
    ms = jnp.mean(out * out, axis=-1, keepdims=True)
    normed = out * lax.rsqrt(ms + NORM_EPS) * kg_ref[...]
    out = jnp.where(which == 0, normed, out)
    row = lax.broadcasted_iota(jnp.int32, out.shape, 0)
    o_ref[...] = jnp.where(row < nblk - 1, out, 0.0)


def compress(proj_b, kc_block, vc_block, pe, w1, w2, kgain, batch, seq):
    nblk = seq // CMP_STRIDE
    hidden = w1.shape[-1]

    def xmap(w, b, g):
        return (b, kc_block + w * (vc_block - kc_block) + g)

    return pl.pallas_call(
        _compress_kernel,
        grid=(2, batch, KV_GROUPS),
        in_specs=[
            pl.BlockSpec((seq, HEAD_B), xmap),
            pl.BlockSpec((None, L_CMP, HEAD_B), lambda w, b, g: (w, 0, 0)),
            pl.BlockSpec((None, L_CMP, HEAD_B, hidden), lambda w, b, g: (w, 0, 0, 0)),
            pl.BlockSpec((None, hidden, HEAD_B), lambda w, b, g: (w, 0, 0)),
            pl.BlockSpec((1, HEAD_B), lambda w, b, g: (0, 0)),
        ],
        out_specs=pl.BlockSpec((None, None, None, nblk, HEAD_B), lambda w, b, g: (w, b, g, 0, 0)),
        out_shape=jax.ShapeDtypeStruct((2, batch, KV_GROUPS, nblk, HEAD_B), F32),
        compiler_params=_params(ndim=3),
        name="nsa_compress",
    )(proj_b, pe, w1, w2, kgain)


def _rms(x, gain):
    ms = jnp.mean(x * x, axis=-1, keepdims=True)
    return x * lax.rsqrt(ms + NORM_EPS) * gain


def _dot_nt(a, b):
    return lax.dot_general(a, b, (((1,), (1,)), ((), ())), preferred_element_type=F32)


LOG2E = 1.4426950408889634
VAUG = 2 * HEAD_B


def _attn_kernel(q_ref, ks_ref, vs_ref, kw_ref, vw_ref, kc_ref, vc_ref, gt_ref, z_ref, qg_ref, kg_ref,
                 ovt_ref, ex_ref, wb_ref, o_ref, ksn_ref, vs1_ref, kwn_ref, vw1_ref, q_scr, m_scr, acc_scr,
                 oc_scr, ow_scr, *, seq):
    qt = pl.program_id(2)
    ncmp = kc_ref.shape[0]
    nsel = seq // L_SEL

    @pl.when(qt == 0)
    def _():
        ones_col = jnp.ones((seq, HEAD_B), BF16)
        ksn_ref[...] = _rms(ks_ref[...], kg_ref[1:2, :]).astype(BF16)
        vs1_ref[:, 0:HEAD_B] = vs_ref[...].astype(BF16)
        vs1_ref[:, HEAD_B:VAUG] = ones_col
        kwn_ref[0:WINDOW, :] = jnp.zeros((WINDOW, HEAD_B), BF16)
        vw1_ref[0:WINDOW, :] = jnp.zeros((WINDOW, VAUG), BF16)
        kwn_ref[WINDOW:WINDOW + seq, :] = _rms(kw_ref[...], kg_ref[2:3, :]).astype(BF16)
        vw1_ref[WINDOW:WINDOW + seq, 0:HEAD_B] = vw_ref[...].astype(BF16)
        vw1_ref[WINDOW:WINDOW + seq, HEAD_B:VAUG] = ones_col

    t0 = pl.multiple_of(qt * TQ, TQ)
    for r in range(REP):
        q_scr[r] = (_rms(q_ref[:, r * HEAD_B:(r + 1) * HEAD_B], qg_ref[...]) * (ATTN_SCALE * LOG2E)).astype(BF16)

    def t_of(shape):
        return t0 + lax.broadcasted_iota(jnp.int32, shape, 0)

    n_idx = lax.broadcasted_iota(jnp.int32, (TQ, ncmp), 1)
    bias_c = jnp.where(n_idx * CMP_STRIDE + (L_CMP - 1) <= t_of((TQ, ncmp)), 0.0, NEG_INF)
    bias_c = jnp.where(n_idx < ncmp - 1, bias_c, NEG_INF)
    row_ok = jnp.where(t_of((TQ, 1)) >= L_CMP - 1, 1.0, 0.0)
    q_all = q_scr[...].reshape(REP * TQ, HEAD_B)
    head = lambda x, r: x[r * TQ:(r + 1) * TQ]
    s_all = _dot_nt(q_all, kc_ref[...].astype(BF16))
    prs = []
    for r in range(REP):
        s = head(s_all, r) + bias_c
        p = jnp.exp2(s - jnp.max(s, axis=-1, keepdims=True))
        prs.append(p * (row_ok / jnp.sum(p, axis=-1, keepdims=True)))
    psum = prs[0] + prs[1] + prs[2] + prs[3]
    oc_scr[...] = jnp.dot(jnp.concatenate(prs, axis=0).astype(BF16), vc_ref[...].astype(BF16),
                          preferred_element_type=F32).reshape(REP, TQ, HEAD_B)
    imp_t = lax.dot_general(ovt_ref[...], psum, (((1,), (1,)), ((), ())), preferred_element_type=F32,
                            precision=lax.Precision.HIGHEST)

    j_idx = lax.broadcasted_iota(jnp.int32, (nsel, TQ), 0)
    t_sel = t0 + lax.broadcasted_iota(jnp.int32, (nsel, TQ), 1)
    cur = t_sel // L_SEL
    forced = jnp.where((j_idx == 0) | (j_idx == cur) | (j_idx == cur - 1), FORCE_BONUS, 0.0)
    score = jnp.where(j_idx * L_SEL <= t_sel, imp_t + forced, NEG_INF)
    rank = jnp.zeros((nsel, TQ), F32)
    for i in range(nsel):
        si = score[i:i + 1, :]
        ahead = (si > score) | ((si == score) & (j_idx > i))
        rank = rank + jnp.where(ahead, 1.0, 0.0)
    sel_t = jnp.where((rank < min(N_SEL, nsel)) & (score > 0.5 * NEG_INF), 1.0, 0.0)
    sel = jnp.concatenate([sel_t, jnp.zeros((LANES - nsel, TQ), F32)], axis=0).T
    not_sel = (1.0 - sel).astype(BF16)

    span = WINDOW + TQ
    kwin = kwn_ref[pl.ds(t0, span), :]
    vwin = vw1_ref[pl.ds(t0, span), :]
    lane_w = lax.broadcasted_iota(jnp.int32, (TQ, span), 1)
    bias_w = jnp.where(lane_w >= WINDOW - t0, wb_ref[...], NEG_INF)
    s_all = _dot_nt(q_all, kwin)
    ps = []
    for r in range(REP):
        s = head(s_all, r) + bias_w
        ps.append(jnp.exp2(s - jnp.max(s, axis=-1, keepdims=True)).astype(BF16))
    ow_scr[...] = jnp.dot(jnp.concatenate(ps, axis=0), vwin, preferred_element_type=F32)

    m_scr[...] = jnp.full(m_scr.shape, NEG_INF, F32)
    acc_scr[...] = jnp.zeros(acc_scr.shape, F32)

    def scores(c):
        k0 = pl.multiple_of(c * KC, KC)
        return _dot_nt(q_scr[...].reshape(REP * TQ, HEAD_B), ksn_ref[pl.ds(k0, KC), :])

    def sel_bias(c):
        return jnp.dot(not_sel, ex_ref[c], preferred_element_type=F32)

    def sel_chunk(c, s_all, bias, diagonal):
        k0 = pl.multiple_of(c * KC, KC)
        v1 = vs1_ref[pl.ds(k0, KC), :]
        if diagonal:
            lane = lax.broadcasted_iota(jnp.int32, (TQ, KC), 1)
            bias = jnp.where(k0 + lane <= t_of((TQ, KC)), bias, NEG_INF)
        ps, alphas = [], []
        for r in range(REP):
            s = head(s_all, r) + bias
            m_old = m_scr[r]
            m_new = jnp.maximum(m_old, jnp.max(s, axis=-1, keepdims=True))
            m_scr[r] = m_new
            ps.append(jnp.exp2(s - jnp.tile(m_new, (1, KC // LANES))).astype(BF16))
            alphas.append(jnp.exp2(m_old - m_new))
        pv = jnp.dot(jnp.concatenate(ps, axis=0), v1, preferred_element_type=F32)
        for r in range(REP):
            acc_scr[r] = jnp.tile(alphas[r], (1, VAUG // LANES)) * acc_scr[r] + head(pv, r)

    n_full = t0 // KC

    def full_chunk(c, carry):
        s_cur, bias_cur = carry
        nxt = (scores(c + 1), sel_bias(c + 1))
        sel_chunk(c, s_cur, bias_cur, False)
        return nxt

    s_last, bias_last = lax.fori_loop(0, n_full, full_chunk, (scores(0), sel_bias(0)))
    sel_chunk(n_full, s_last, bias_last, True)

    gts = jax.nn.sigmoid(gt_ref[...])
    for r in range(REP):
        ow = ow_scr[r * TQ:(r + 1) * TQ, :]
        o_w = ow[:, 0:HEAD_B] / ow[:, HEAD_B:VAUG]
        acc = acc_scr[r]
        o_s = acc[:, 0:HEAD_B] / acc[:, HEAD_B:VAUG]
        o = (gts[:, 3 * r:3 * r + 1] * oc_scr[r] + gts[:, 3 * r + 1:3 * r + 2] * o_s
             + gts[:, 3 * r + 2:3 * r + 3] * o_w)
        z = z_ref[:, r * HEAD_B:(r + 1) * HEAD_B]
        o_ref[:, r * HEAD_B:(r + 1) * HEAD_B] = (o * (z * jax.nn.sigmoid(z))).astype(o_ref.dtype)


def nsa_attention(proj_b, cmp_kv, gates, q_gain, k_gain, overlap_t, expand, win_bias, blocks, batch, seq):
    nq = seq // TQ
    ncmp = seq // CMP_STRIDE
    gw = REP * HEAD_B // LANES
    qspec = lambda off: pl.BlockSpec((TQ, REP * HEAD_B), lambda b, g, t: (b * nq + t, off // gw + g))
    kvspec = lambda off: pl.BlockSpec((seq, HEAD_B), lambda b, g, t: (b, off + g))
    cspec = lambda w: pl.BlockSpec((None, None, None, ncmp, HEAD_B), lambda b, g, t: (w, b, g, 0, 0))
    full = lambda a: pl.BlockSpec(a.shape, lambda b, g, t: (0,) * a.ndim)
    return pl.pallas_call(
        functools.partial(_attn_kernel, seq=seq),
        grid=(batch, KV_GROUPS, nq),
        in_specs=[
            qspec(blocks["q"]), kvspec(blocks["ks"]), kvspec(blocks["vs"]), kvspec(blocks["kw"]), kvspec(blocks["vw"]),
            cspec(0), cspec(1),
            pl.BlockSpec((None, None, TQ, 16), lambda b, g, t: (b, g, t, 0)),
            qspec(blocks["z"]),
            full(q_gain), full(k_gain), full(overlap_t), full(expand), full(win_bias),
        ],
        out_specs=pl.BlockSpec((TQ, REP * HEAD_B), lambda b, g, t: (b * nq + t, g)),
        out_shape=jax.ShapeDtypeStruct((batch * seq, KV_GROUPS * REP * HEAD_B), BF16),
        scratch_shapes=[
            pltpu.VMEM((seq, HEAD_B), BF16), pltpu.VMEM((seq, VAUG), BF16),
            pltpu.VMEM((seq + WINDOW, HEAD_B), BF16), pltpu.VMEM((seq + WINDOW, VAUG), BF16),
            pltpu.VMEM((REP, TQ, HEAD_B), BF16), pltpu.VMEM((REP, TQ, LANES), F32),
            pltpu.VMEM((REP, TQ, VAUG), F32), pltpu.VMEM((REP, TQ, HEAD_B), F32),
            pltpu.VMEM((REP * TQ, VAUG), F32),
        ],
        compiler_params=_params(ndim=3),
        name="nsa_attention",
    )(proj_b, proj_b, proj_b, proj_b, proj_b, cmp_kv, cmp_kv, gates, proj_b, q_gain, k_gain, overlap_t, expand,
      win_bias)


def _merge_kernel(ya_ref, yb_ref, wa_ref, wb_ref, ga_ref, gb_ref, out_ref):
    ua = jnp.dot(ya_ref[...], wa_ref[...], preferred_element_type=F32)
    ub = jnp.dot(yb_ref[...], wb_ref[...], preferred_element_type=F32)
    out_ref[...] = (ga_ref[...].astype(F32) * ua + gb_ref[...].astype(F32) * ub).astype(out_ref.dtype)


def merge(ya, yb, w_a, w_b, gates, tm=1024, tn=512):
    m, ka = ya.shape
    n = w_a.shape[1]
    tm = min(tm, m)
    ga_block, gb_block = 0, n // tn
    return pl.pallas_call(
        _merge_kernel,
        grid=(m // tm, n // tn),
        in_specs=[
            pl.BlockSpec((tm, ka), lambda i, j: (i, 0)),
            pl.BlockSpec((tm, ka), lambda i, j: (i, 0)),
            pl.BlockSpec((ka, tn), lambda i, j: (0, j)),
            pl.BlockSpec((ka, tn), lambda i, j: (0, j)),
            pl.BlockSpec((tm, tn), lambda i, j: (i, ga_block + j)),
            pl.BlockSpec((tm, tn), lambda i, j: (i, gb_block + j)),
        ],
        out_specs=pl.BlockSpec((tm, tn), lambda i, j: (i, j)),
        out_shape=jax.ShapeDtypeStruct((m, n), BF16),
        compiler_params=_params(ndim=2),
        name="merge",
    )(ya, yb, w_a, w_b, gates, gates)


def _ple_embed_kernel(p_ref, w_ref, g_ref, o_ref):
    acc = jnp.dot(p_ref[...].astype(BF16), w_ref[...], preferred_element_type=F32)
    o_ref[...] = _rms(acc, g_ref[...]).astype(o_ref.dtype)


def ple_embed(p, w, g, tm=256):
    m, k = p.shape
    n = w.shape[1]
    return pl.pallas_call(
        _ple_embed_kernel,
        grid=(m // tm,),
        in_specs=[pl.BlockSpec((tm, k), lambda i: (i, 0)), pl.BlockSpec((k, n), lambda i: (0, 0)),
                  pl.BlockSpec((1, n), lambda i: (0, 0))],
        out_specs=pl.BlockSpec((tm, n), lambda i: (i, 0)),
        out_shape=jax.ShapeDtypeStruct((m, n), F32),
        compiler_params=_params(),
        name="ple_embed",
    )(p, w, g.reshape(1, n))


def head_minor(w, heads):
    lead = w.shape[:-1]
    return w.reshape(*lead, heads, HEAD_A).swapaxes(-1, -2).reshape(*lead, heads * HEAD_A)


def rwkv_branch(proj_a, batch, seq, a_width, w_lora_up, w0, a_lora_up, a0, k_k, k_a, r_k, lnx_w, lnx_b, h, w_gate):
    tokens = batch * seq
    heads_a = a_width // HEAD_A
    assert heads_a == SCAN_HEADS and batch % SCAN_BATCH == 0
    lora = w_lora_up.shape[0]
    zeros = jnp.zeros((lora, a_width), F32)
    hm = lambda w: head_minor(w, heads_a)
    w_lora = jnp.concatenate([jnp.concatenate([hm(w_lora_up), zeros], 1),
                              jnp.concatenate([zeros, hm(a_lora_up)], 1)], 0)
    b_lora = jnp.concatenate([hm(w0), hm(a0)]).reshape(1, -1)
    wa_pre = lora_project(proj_a, 4 * a_width // LANES, w_lora, b_lora)

    def par_scan(t):
        return jnp.tile(t.reshape(heads_a, HEAD_A).T, (1, SCAN_BATCH))

    par3 = lambda t: par_scan(t).reshape(HEAD_A, 1, LANES)
    ya, gates = rwkv_scan(proj_a.reshape(batch, seq, -1), wa_pre.reshape(batch, seq, -1),
                          par3(k_k), par3(k_a), par3(r_k), par_scan(lnx_w), par_scan(lnx_b), h, w_gate, batch, seq)
    return ya.reshape(tokens, a_width), gates


def nsa_branch(proj_b, graw, batch, seq, b_width, q_norm_g, k_norm_g, pe_cmp_k, pe_cmp_v,
               cmp_k_w1, cmp_k_w2, cmp_v_w1, cmp_v_w2):
    kv_width = KV_GROUPS * HEAD_B
    blk = lambda cols: cols // LANES
    blocks = {"q": 0, "kc": blk(b_width), "vc": blk(b_width + kv_width), "ks": blk(b_width + 2 * kv_width),
              "vs": blk(b_width + 3 * kv_width), "kw": blk(b_width + 4 * kv_width), "vw": blk(b_width + 5 * kv_width),
              "z": blk(b_width + 6 * kv_width)}
    pe = jnp.stack([pe_cmp_k, pe_cmp_v])
    hidden = cmp_k_w1.shape[1]
    w1 = jnp.stack([cmp_k_w1, cmp_v_w1]).reshape(2, L_CMP, HEAD_B, hidden).astype(BF16)
    w2 = jnp.stack([cmp_k_w2, cmp_v_w2]).astype(BF16)
    cmp_kv = compress(proj_b, blocks["kc"], blocks["vc"], pe, w1, w2, k_norm_g[0:1], batch, seq)

    ncmp = seq // CMP_STRIDE
    nsel = seq // L_SEL
    c_start = jnp.arange(ncmp) * CMP_STRIDE
    s_start = jnp.arange(nsel) * L_SEL
    overlap_t = ((c_start[None, :] < (s_start + L_SEL)[:, None]) & (s_start[:, None] < (c_start + L_CMP)[None, :])
                 & (jnp.arange(ncmp)[None, :] < ncmp - 1)).astype(F32)
    key_blk = jnp.arange(seq) // L_SEL
    expand = jnp.where(jnp.arange(LANES)[:, None] == key_blk[None, :], NEG_INF, 0.0).astype(BF16)
    expand = expand.reshape(LANES, seq // KC, KC).transpose(1, 0, 2)
    tl = jnp.arange(TQ)[:, None]
    u = jnp.arange(WINDOW + TQ)[None, :]
    win_bias = jnp.where((u > tl) & (u <= WINDOW + tl), 0.0, NEG_INF).astype(F32)
    gates = graw.reshape(batch, seq, KV_GROUPS, REP * 3).transpose(0, 2, 1, 3)
    gates = jnp.pad(gates, ((0, 0), (0, 0), (0, 0), (0, 16 - REP * 3)))
    return nsa_attention(proj_b, cmp_kv, gates, q_norm_g.reshape(1, HEAD_B), k_norm_g, overlap_t, expand, win_bias,
                         blocks, batch, seq)


def _layer(x, p, norm_g, w_in, shift_mu, w_lora_up, w0, a_lora_up, a0, k_k, k_a, r_k, lnx_w, lnx_b,
           q_norm_g, k_norm_g, pe_cmp_k, pe_cmp_v, cmp_k_w1, cmp_k_w2, cmp_v_w1, cmp_v_w2,
           w_up_a, w_up_b, w_out, ple_pre_g, w_ple_gate, w_ple, ple_post_g):
    batch, seq, d = x.shape
    tokens = batch * seq
    a_width = w_up_a.shape[0]
    b_width = w_up_b.shape[0]
    heads_a = a_width // HEAD_A
    kv_width = KV_GROUPS * HEAD_B
    n_gate = KV_GROUPS * REP * 3
    lora = w_lora_up.shape[0]
    assert lora == HEAD_A and a_lora_up.shape[0] == HEAD_A and 2 * lora == LANES
    a_cols = 4 * a_width + 2 * lora
    g_off = a_cols + b_width + 6 * kv_width
    zb_off = g_off + n_gate
    assert w_in.shape[1] == zb_off + b_width + 2 * d

    pad_a = LANES - n_gate
    hm4 = lambda w: head_minor(w.reshape(*w.shape[:-1], 4, a_width), heads_a).reshape(*w.shape[:-1], 4 * a_width)
    w_a = jnp.concatenate([hm4(w_in[:, :4 * a_width]), w_in[:, 4 * a_width:a_cols], w_in[:, g_off:zb_off],
                           jnp.zeros((d, pad_a), F32)], axis=1).astype(BF16)
    mu_a = jnp.concatenate([hm4(shift_mu[:4 * a_width]), shift_mu[4 * a_width:], jnp.zeros((LANES,), F32)]).reshape(1, -1)
    gates_off = zb_off + b_width
    w_b = jnp.concatenate([w_in[:, a_cols:g_off], w_in[:, zb_off:gates_off]], axis=1).astype(BF16)
    w_gate = w_in[:, gates_off:].astype(BF16)
    na = w_a.shape[1]
    tn_a = 768 if na % 768 == 0 else LANES

    x2 = x.reshape(tokens, d)
    h = rmsnorm_rows(x2, norm_g, BF16)
    tm = 512
    proj_a = matmul(h, w_a, mode="shift", extras=(mu_a,), tm=tm, tn=tn_a, tiles_per_seq=seq // tm, name="proj_a")
    proj_b = matmul(h, w_b, tm=tm, tn=1024, name="proj_b")

    ya, gates = rwkv_branch(proj_a, batch, seq, a_width, w_lora_up, w0, a_lora_up, a0, k_k, k_a, r_k, lnx_w, lnx_b,
                            h, w_gate)
    graw = proj_a[:, a_cols:a_cols + n_gate]
    yb = nsa_branch(proj_b, graw, batch, seq, b_width, q_norm_g, k_norm_g, pe_cmp_k, pe_cmp_v,
                    cmp_k_w1, cmp_k_w2, cmp_v_w1, cmp_v_w2)

    w_up_a_nm = head_minor(w_up_a.T, heads_a).T
    merged = merge(ya, yb, w_up_a_nm.astype(BF16), w_up_b.astype(BF16), gates)
    x1 = matmul(merged, w_out.astype(BF16), mode="resid", extras=(x2,), tm=tm, tn=1024, name="out_proj")
    hn = rmsnorm_rows(x1, ple_pre_g, BF16)
    e = ple_embed(p.reshape(tokens, -1), w_ple.astype(BF16), ple_post_g)
    out = matmul(hn, w_ple_gate.astype(BF16), mode="ple", extras=(x1, e), tm=tm, tn=1024, name="ple_gate")
    return out.reshape(batch, seq, d)


def kernel(x, p, norm_g, w_in, shift_mu, w_lora_up, w0, a_lora_up, a0, k_k, k_a, r_k, lnx_w, lnx_b, q_norm_g, k_norm_g, pe_cmp_k, pe_cmp_v, cmp_k_w1, cmp_k_w2, cmp_v_w1, cmp_v_w2, w_up_a, w_up_b, w_out, ple_pre_g, w_ple_gate, w_ple, ple_post_g):
    depth = w_in.shape[0]
    for i in range(depth):
        x = _layer(x, p[i], norm_g[i], w_in[i], shift_mu[i], w_lora_up[i], w0[i], a_lora_up[i], a0[i], k_k[i],
                   k_a[i], r_k[i], lnx_w[i], lnx_b[i], q_norm_g[i], k_norm_g[i], pe_cmp_k[i], pe_cmp_v[i],
                   cmp_k_w1[i], cmp_k_w2[i], cmp_v_w1[i], cmp_v_w2[i], w_up_a[i], w_up_b[i], w_out[i],
                   ple_pre_g[i], w_ple_gate[i], w_ple[i], ple_post_g[i])
    return x
```

```python
import functools

import jax
import jax.numpy as jnp
from jax import lax
from jax.experimental import pallas as pl
from jax.experimental.pallas import tpu as pltpu

F32 = jnp.float32
BF16 = jnp.bfloat16

LANES = 128
SUBLANES = 8
VMEM_LIMIT = 56 * 1024 * 1024

NORM_EPS = 1e-6
NEG_INF = -1e30
HEAD_A = 64
GN_EPS = 64e-5
HEAD_B = 128
KV_GROUPS = 4
REP = 4
L_CMP = 32
CMP_STRIDE = 16
L_SEL = 64
N_SEL = 16
WINDOW = 512
TQ = 128
KC = 256
FORCE_BONUS = 1e3
ATTN_SCALE = HEAD_B ** -0.5
DECAY_SCALE = 0.6065306597126334


def _params(vmem=VMEM_LIMIT, ndim=1):
    return pltpu.CompilerParams(dimension_semantics=("arbitrary",) * ndim, vmem_limit_bytes=vmem)


def _rmsnorm_kernel(x_ref, g_ref, o_ref):
    x = x_ref[...]
    ms = jnp.mean(x * x, axis=-1, keepdims=True)
    o_ref[...] = (x * lax.rsqrt(ms + NORM_EPS) * g_ref[...]).astype(o_ref.dtype)


def rmsnorm_rows(x, g, out_dtype, tm=256):
    m, d = x.shape
    return pl.pallas_call(
        _rmsnorm_kernel,
        grid=(m // tm,),
        in_specs=[pl.BlockSpec((tm, d), lambda i: (i, 0)), pl.BlockSpec((1, d), lambda i: (0, 0))],
        out_specs=pl.BlockSpec((tm, d), lambda i: (i, 0)),
        out_shape=jax.ShapeDtypeStruct((m, d), out_dtype),
        compiler_params=_params(),
        name="rmsnorm",
    )(x, g.reshape(1, d))


def _mm_kernel(*refs, mode, tiles_per_seq):
    a_ref, b_ref = refs[0], refs[1]
    acc = jnp.dot(a_ref[...], b_ref[...], preferred_element_type=F32)
    if mode == "plain":
        o_ref = refs[2]
        o_ref[...] = acc.astype(o_ref.dtype)
    elif mode == "shift":
        mu_ref, o_ref, carry_ref = refs[2], refs[3], refs[4]
        i = pl.program_id(1)
        tm = acc.shape[0]
        first = (i % tiles_per_seq) == 0
        last_prev = jnp.where(first, 0.0, carry_ref[SUBLANES - 1:SUBLANES, :])
        rolled = pltpu.roll(acc, 1, axis=0)
        row = lax.broadcasted_iota(jnp.int32, acc.shape, 0)
        prev = jnp.where(row == 0, last_prev, rolled)
        carry_ref[...] = acc[tm - SUBLANES:tm, :]
        o_ref[...] = (acc + mu_ref[...] * (prev - acc)).astype(o_ref.dtype)
    elif mode == "resid":
        r_ref, o_ref = refs[2], refs[3]
        o_ref[...] = (r_ref[...] + acc).astype(o_ref.dtype)
    elif mode == "ple":
        x_ref, e_ref, o_ref = refs[2], refs[3], refs[4]
        o_ref[...] = (x_ref[...] + jax.nn.sigmoid(acc) * e_ref[...]).astype(o_ref.dtype)
    else:
        raise ValueError(mode)


def matmul(a, b, *, mode="plain", extras=(), out_dtype=F32, tm=512, tn=1024, tiles_per_seq=1, name="mm"):
    m, k = a.shape
    n = b.shape[1]
    assert m % tm == 0 and n % tn == 0, (m, n, tm, tn)
    in_specs = [pl.BlockSpec((tm, k), lambda j, i: (i, 0)), pl.BlockSpec((k, tn), lambda j, i: (0, j))]
    scratch = []
    if mode == "shift":
        in_specs.append(pl.BlockSpec((1, tn), lambda j, i: (0, j)))
        scratch.append(pltpu.VMEM((SUBLANES, tn), F32))
    elif mode == "resid":
        in_specs.append(pl.BlockSpec((tm, tn), lambda j, i: (i, j)))
    elif mode == "ple":
        in_specs += [pl.BlockSpec((tm, tn), lambda j, i: (i, j)), pl.BlockSpec((tm, tn), lambda j, i: (i, j))]
    return pl.pallas_call(
        functools.partial(_mm_kernel, mode=mode, tiles_per_seq=tiles_per_seq),
        grid=(n // tn, m // tm),
        in_specs=in_specs,
        out_specs=pl.BlockSpec((tm, tn), lambda j, i: (i, j)),
        out_shape=jax.ShapeDtypeStruct((m, n), out_dtype),
        scratch_shapes=scratch,
        compiler_params=_params(ndim=2),
        name=name,
    )(a, b, *extras)


def _lora_kernel(x_ref, w_ref, b_ref, o_ref):
    x = x_ref[...]
    lane = lax.broadcasted_iota(jnp.int32, x.shape, 1)
    x = jnp.where(lane < HEAD_A, jnp.tanh(x), x)
    acc = jnp.dot(x, w_ref[...], preferred_element_type=F32, precision=lax.Precision.HIGHEST)
    o_ref[...] = acc + b_ref[...]


def lora_project(proj_a, col_block, w_blockdiag, bias, tm=1024, tn=1024):
    m = proj_a.shape[0]
    n = w_blockdiag.shape[1]
    tm = min(tm, m)
    assert m % tm == 0 and n % tn == 0
    return pl.pallas_call(
        _lora_kernel,
        grid=(m // tm, n // tn),
        in_specs=[
            pl.BlockSpec((tm, LANES), lambda i, j: (i, col_block)),
            pl.BlockSpec((LANES, tn), lambda i, j: (0, j)),
            pl.BlockSpec((1, tn), lambda i, j: (0, j)),
        ],
        out_specs=pl.BlockSpec((tm, tn), lambda i, j: (i, j)),
        out_shape=jax.ShapeDtypeStruct((m, n), F32),
        compiler_params=_params(ndim=2),
        name="lora",
    )(proj_a, w_blockdiag, bias)


SCAN_BATCH = 4
SCAN_HEADS = LANES // SCAN_BATCH
NGRP = LANES // SCAN_HEADS


def _segment_transpose(x):
    seg = lax.broadcasted_iota(jnp.int32, x[0].shape, 1) // SCAN_HEADS
    y = []
    for i in range(NGRP):
        out = None
        for j in range(SCAN_BATCH):
            shift = ((j - i) % NGRP) * SCAN_HEADS
            piece = x[j] if shift == 0 else pltpu.roll(x[j], shift, axis=1)
            out = piece if out is None else jnp.where(seg == j, piece, out)
        y.append(out)
    return y


def _to_lanes(x_ref, dst_ref, tc):
    for g in range(HEAD_A // NGRP):
        y = _segment_transpose([x_ref[b, :, g * LANES:(g + 1) * LANES] for b in range(SCAN_BATCH)])
        for n_lo in range(NGRP):
            n = g * NGRP + n_lo
            dst_ref[n * tc:(n + 1) * tc, :] = y[n_lo]


def _from_lanes_gated(src_ref, z_ref, o_ref, tc):
    for g in range(HEAD_A // NGRP):
        y = _segment_transpose([src_ref[(g * NGRP + n_lo) * tc:(g * NGRP + n_lo + 1) * tc, :] for n_lo in range(NGRP)])
        for b in range(SCAN_BATCH):
            z = z_ref[b, :, g * LANES:(g + 1) * LANES]
            o_ref[b, :, g * LANES:(g + 1) * LANES] = (y[b] * (z * jax.nn.sigmoid(z))).astype(o_ref.dtype)


GATE_TM = 512
GATE_TN = 1024
GATE_SN = 256
GATE_KSPLIT = 4


def _scan_kernel(r_ref, k_ref, v_ref, z_ref, wp_ref, ap_ref, kkw_ref, kaw_ref, rkw_ref, lnw_ref, lnb_ref,
                 h_ref, wg_ref, o_ref, gate_ref, state_ref, rs, ks, vs, ws, ans, bvs, os, bon, h_scr, wg_scr, gacc,
                 *, tc, steps_per_wg):
    n_ch = HEAD_A

    @pl.when(pl.program_id(1) == 0)
    def _():
        state_ref[...] = jnp.zeros_like(state_ref)
        gacc[...] = jnp.zeros_like(gacc)

    gate_k = h_scr.shape[1] // GATE_KSPLIT
    h_scr[...] = h_ref[...]
    step_id = pl.program_id(0) * pl.num_programs(1) + pl.program_id(1)

    @pl.when(step_id % steps_per_wg == 0)
    def _():
        wg_scr[...] = wg_ref[...]

    _to_lanes(r_ref, rs, tc)
    _to_lanes(k_ref, ks, tc)
    _to_lanes(v_ref, vs, tc)
    _to_lanes(wp_ref, ws, tc)
    _to_lanes(ap_ref, bvs, tc)

    def rows(n):
        return pl.ds(pl.multiple_of(n * tc, tc), tc)

    def norm_acc(n, acc):
        kkr = ks[rows(n), :] * kkw_ref[n]
        return acc + kkr * kkr

    nsq = lax.fori_loop(0, n_ch, norm_acc, jnp.zeros((tc, LANES), F32), unroll=8)
    inv = 1.0 / jnp.maximum(jnp.sqrt(nsq), 1e-12)

    def prep(n, bacc):
        k = ks[rows(n), :]
        a = jax.nn.sigmoid(bvs[rows(n), :])
        kk = k * kkw_ref[n] * inv
        ws[rows(n), :] = jnp.exp(-DECAY_SCALE * jax.nn.sigmoid(ws[rows(n), :]))
        ans[rows(n), :] = -kk
        bvs[rows(n), :] = kk * a
        kmod = k * (1.0 + (a - 1.0) * kaw_ref[n])
        ks[rows(n), :] = kmod
        return bacc + rs[rows(n), :] * kmod * rkw_ref[n]

    bon[...] = lax.fori_loop(0, n_ch, prep, jnp.zeros((tc, LANES), F32), unroll=4)

    def bcast(ref, row):
        return ref[pl.ds(row, 1), :][None]

    def sa_first(j, acc):
        return acc + state_ref[j] * bcast(ans, j * tc)

    slab = (n_ch // SUBLANES, SUBLANES, LANES)
    sa0 = lax.fori_loop(0, n_ch, sa_first, jnp.zeros(slab, F32), unroll=4)

    def step(t, sa):
        v = vs[pl.ds(t, n_ch, stride=tc), :].reshape(slab)
        t_next = jnp.minimum(t + 1, tc - 1)

        kq = t % GATE_KSPLIT
        k0 = pl.multiple_of(kq * gate_k, gate_k)
        c0 = pl.multiple_of((t // GATE_KSPLIT) * GATE_SN, GATE_SN)
        gacc[...] += jnp.dot(h_scr[:, pl.ds(k0, gate_k)], wg_scr[pl.ds(k0, gate_k), pl.ds(c0, GATE_SN)],
                             preferred_element_type=F32)

        out = jnp.zeros(slab, F32)
        sa_next = jnp.zeros(slab, F32)
        for j in range(n_ch):
            row = j * tc + t
            s_new = state_ref[j] * bcast(ws, row) + sa * bcast(bvs, row) + v * bcast(ks, row)
            state_ref[j] = s_new
            out = out + s_new * bcast(rs, row)
            sa_next = sa_next + s_new * bcast(ans, j * tc + t_next)
        o = out.reshape(n_ch, LANES)
        mu = jnp.mean(o, axis=0, keepdims=True)
        d = o - mu
        var = jnp.mean(d * d, axis=0, keepdims=True)
        on = d * lax.rsqrt(var + GN_EPS) * lnw_ref[...] + lnb_ref[...]
        bonus = bon[pl.ds(t, 1), :] * v.reshape(n_ch, LANES)
        os[pl.ds(t, n_ch, stride=tc), :] = on + bonus

        @pl.when(kq == GATE_KSPLIT - 1)
        def _():
            gate_ref[:, pl.ds(c0, GATE_SN)] = jax.nn.sigmoid(gacc[...]).astype(gate_ref.dtype)
            gacc[...] = jnp.zeros_like(gacc)

        return sa_next

    lax.fori_loop(0, tc, step, sa0)
    _from_lanes_gated(os, z_ref, o_ref, tc)


def rwkv_scan(proj_a, wa_pre, kkw, kaw, rkw, lnw, lnb, h, w_gate, batch, seq):
    width = HEAD_A * SCAN_HEADS
    tc = GATE_KSPLIT * (GATE_TN // GATE_SN)
    n_chunks = seq // tc
    tokens, d = h.shape
    n_gate = w_gate.shape[1]
    m_tiles = tokens // GATE_TM
    assert tokens % GATE_TM == 0 and n_gate % GATE_TN == 0
    assert (batch // SCAN_BATCH) * n_chunks == m_tiles * (n_gate // GATE_TN), "one gate tile per scan grid step"

    def gate_tile(g, c):
        s = g * n_chunks + c
        return s % m_tiles, s // m_tiles

    seqb = lambda col: pl.BlockSpec((SCAN_BATCH, tc, width), lambda g, c: (g, c, col))
    par3 = pl.BlockSpec((HEAD_A, 1, LANES), lambda g, c: (0, 0, 0))
    par2 = pl.BlockSpec((HEAD_A, LANES), lambda g, c: (0, 0))
    buf = pltpu.VMEM((HEAD_A * tc, LANES), F32)
    return pl.pallas_call(
        functools.partial(_scan_kernel, tc=tc, steps_per_wg=m_tiles),
        grid=(batch // SCAN_BATCH, n_chunks),
        in_specs=[seqb(0), seqb(1), seqb(2), seqb(3), seqb(0), seqb(1), par3, par3, par3, par2, par2,
                  pl.BlockSpec((GATE_TM, d), lambda g, c: (gate_tile(g, c)[0], 0)),
                  pl.BlockSpec((d, GATE_TN), lambda g, c: (0, gate_tile(g, c)[1]), pipeline_mode=pl.Buffered(1))],
        out_specs=[seqb(0), pl.BlockSpec((GATE_TM, GATE_TN), lambda g, c: gate_tile(g, c))],
        out_shape=[jax.ShapeDtypeStruct((batch, seq, width), BF16), jax.ShapeDtypeStruct((tokens, n_gate), BF16)],
        scratch_shapes=[pltpu.VMEM((HEAD_A, HEAD_A // SUBLANES, SUBLANES, LANES), F32)] + [buf] * 7
        + [pltpu.VMEM((tc, LANES), F32), pltpu.VMEM((GATE_TM, d), BF16), pltpu.VMEM((d, GATE_TN), BF16),
           pltpu.VMEM((GATE_TM, GATE_SN), F32)],
        compiler_params=_params(ndim=2),
        name="rwkv_scan",
    )(proj_a, proj_a, proj_a, proj_a, wa_pre, wa_pre, kkw, kaw, rkw, lnw, lnb, h, w_gate)


def _compress_kernel(x_ref, pe_ref, w1_ref, w2_ref, kg_ref, o_ref):
    which = pl.program_id(0)
    half = L_CMP // 2
    nblk = x_ref.shape[0] // CMP_STRIDE
    h1 = jnp.zeros((nblk, w1_ref.shape[-1]), F32)
    h2 = jnp.zeros((nblk, w1_ref.shape[-1]), F32)
    for l in range(half):
        x = x_ref[pl.ds(l, nblk, stride=CMP_STRIDE), :]
        a1 = (x + pe_ref[l:l + 1, :]).astype(BF16)
        a2 = (x + pe_ref[half + l:half + l + 1, :]).astype(BF16)
        h1 = h1 + jnp.dot(a1, w1_ref[l], preferred_element_type=F32)
        h2 = h2 + jnp.dot(a2, w1_ref[half + l], preferred_element_type=F32)
    hid = h1 + pltpu.roll(h2, nblk - 1, axis=0)
    hid = jax.nn.gelu(hid)
    out = jnp.dot(hid.astype(BF16), w2_ref[...], preferred_element_type=F32)
    ms = jnp.mean(out * out, axis=-1, keepdims=True)
    normed = out * lax.rsqrt(ms + NORM_EPS) * kg_ref[...]
    out = jnp.where(which == 0, normed, out)
    row = lax.broadcasted_iota(jnp.int32, out.shape, 0)
    o_ref[...] = jnp.where(row < nblk - 1, out, 0.0)


def compress(proj_b, kc_block, vc_block, pe, w1, w2, kgain, batch, seq):
    nblk = seq // CMP_STRIDE
    hidden = w1.shape[-1]

    def xmap(w, b, g):
        return (b, kc_block + w * (vc_block - kc_block) + g)

    return pl.pallas_call(
        _compress_kernel,
        grid=(2, batch, KV_GROUPS),
        in_specs=[
            pl.BlockSpec((seq, HEAD_B), xmap),
            pl.BlockSpec((None, L_CMP, HEAD_B), lambda w, b, g: (w, 0, 0)),
            pl.BlockSpec((None, L_CMP, HEAD_B, hidden), lambda w, b, g: (w, 0, 0, 0)),
            pl.BlockSpec((None, hidden, HEAD_B), lambda w, b, g: (w, 0, 0)),
            pl.BlockSpec((1, HEAD_B), lambda w, b, g: (0, 0)),
        ],
        out_specs=pl.BlockSpec((None, None, None, nblk, HEAD_B), lambda w, b, g: (w, b, g, 0, 0)),
        out_shape=jax.ShapeDtypeStruct((2, batch, KV_GROUPS, nblk, HEAD_B), F32),
        compiler_params=_params(ndim=3),
        name="nsa_compress",
    )(proj_b, pe, w1, w2, kgain)


def _rms(x, gain):
    ms = jnp.mean(x * x, axis=-1, keepdims=True)
    return x * lax.rsqrt(ms + NORM_EPS) * gain


def _dot_nt(a, b):
    return lax.dot_general(a, b, (((1,), (1,)), ((), ())), preferred_element_type=F32)


LOG2E = 1.4426950408889634
VAUG = 2 * HEAD_B


def _attn_kernel(q_ref, ks_ref, vs_ref, kw_ref, vw_ref, kc_ref, vc_ref, gt_ref, z_ref, qg_ref, kg_ref,
                 ovt_ref, ex_ref, wb_ref, o_ref, ksn_ref, vs1_ref, kwn_ref, vw1_ref, q_scr, m_scr, acc_scr,
                 oc_scr, ow_scr, *, seq):
    qt = pl.program_id(2)
    ncmp = kc_ref.shape[0]
    nsel = seq // L_SEL

    @pl.when(qt == 0)
    def _():
        ones_col = jnp.ones((seq, HEAD_B), BF16)
        ksn_ref[...] = _rms(ks_ref[...], kg_ref[1:2, :]).astype(BF16)
        vs1_ref[:, 0:HEAD_B] = vs_ref[...].astype(BF16)
        vs1_ref[:, HEAD_B:VAUG] = ones_col
        kwn_ref[0:WINDOW, :] = jnp.zeros((WINDOW, HEAD_B), BF16)
        vw1_ref[0:WINDOW, :] = jnp.zeros((WINDOW, VAUG), BF16)
        kwn_ref[WINDOW:WINDOW + seq, :] = _rms(kw_ref[...], kg_ref[2:3, :]).astype(BF16)
        vw1_ref[WINDOW:WINDOW + seq, 0:HEAD_B] = vw_ref[...].astype(BF16)
        vw1_ref[WINDOW:WINDOW + seq, HEAD_B:VAUG] = ones_col

    t0 = pl.multiple_of(qt * TQ, TQ)
    for r in range(REP):
        q_scr[r] = (_rms(q_ref[:, r * HEAD_B:(r + 1) * HEAD_B], qg_ref[...]) * (ATTN_SCALE * LOG2E)).astype(BF16)

    def t_of(shape):
        return t0 + lax.broadcasted_iota(jnp.int32, shape, 0)

    n_idx = lax.broadcasted_iota(jnp.int32, (TQ, ncmp), 1)
    bias_c = jnp.where(n_idx * CMP_STRIDE + (L_CMP - 1) <= t_of((TQ, ncmp)), 0.0, NEG_INF)
    bias_c = jnp.where(n_idx < ncmp - 1, bias_c, NEG_INF)
    row_ok = jnp.where(t_of((TQ, 1)) >= L_CMP - 1, 1.0, 0.0)
    q_all = q_scr[...].reshape(REP * TQ, HEAD_B)
    head = lambda x, r: x[r * TQ:(r + 1) * TQ]
    s_all = _dot_nt(q_all, kc_ref[...].astype(BF16))
    prs = []
    for r in range(REP):
        s = head(s_all, r) + bias_c
        p = jnp.exp2(s - jnp.max(s, axis=-1, keepdims=True))
        prs.append(p * (row_ok / jnp.sum(p, axis=-1, keepdims=True)))
    psum = prs[0] + prs[1] + prs[2] + prs[3]
    oc_scr[...] = jnp.dot(jnp.concatenate(prs, axis=0).astype(BF16), vc_ref[...].astype(BF16),
                          preferred_element_type=F32).reshape(REP, TQ, HEAD_B)
    imp_t = lax.dot_general(ovt_ref[...], psum, (((1,), (1,)), ((), ())), preferred_element_type=F32,
                            precision=lax.Precision.HIGHEST)

    j_idx = lax.broadcasted_iota(jnp.int32, (nsel, TQ), 0)
    t_sel = t0 + lax.broadcasted_iota(jnp.int32, (nsel, TQ), 1)
    cur = t_sel // L_SEL
    forced = jnp.where((j_idx == 0) | (j_idx == cur) | (j_idx == cur - 1), FORCE_BONUS, 0.0)
    score = jnp.where(j_idx * L_SEL <= t_sel, imp_t + forced, NEG_INF)
    rank = jnp.zeros((nsel, TQ), F32)
    for i in range(nsel):
        si = score[i:i + 1, :]
        ahead = (si > score) | ((si == score) & (j_idx > i))
        rank = rank + jnp.where(ahead, 1.0, 0.0)
    sel_t = jnp.where((rank < min(N_SEL, nsel)) & (score > 0.5 * NEG_INF), 1.0, 0.0)
    sel = jnp.concatenate([sel_t, jnp.zeros((LANES - nsel, TQ), F32)], axis=0).T
    not_sel = (1.0 - sel).astype(BF16)

    span = WINDOW + TQ
    kwin = kwn_ref[pl.ds(t0, span), :]
    vwin = vw1_ref[pl.ds(t0, span), :]
    lane_w = lax.broadcasted_iota(jnp.int32, (TQ, span), 1)
    bias_w = jnp.where(lane_w >= WINDOW - t0, wb_ref[...], NEG_INF)
    s_all = _dot_nt(q_all, kwin)
    ps = []
    for r in range(REP):
        s = head(s_all, r) + bias_w
        ps.append(jnp.exp2(s - jnp.max(s, axis=-1, keepdims=True)).astype(BF16))
    ow_scr[...] = jnp.dot(jnp.concatenate(ps, axis=0), vwin, preferred_element_type=F32)

    m_scr[...] = jnp.full(m_scr.shape, NEG_INF, F32)
    acc_scr[...] = jnp.zeros(acc_scr.shape, F32)

    def scores(c):
        k0 = pl.multiple_of(c * KC, KC)
        return _dot_nt(q_scr[...].reshape(REP * TQ, HEAD_B), ksn_ref[pl.ds(k0, KC), :])

    def sel_bias(c):
        return jnp.dot(not_sel, ex_ref[c], preferred_element_type=F32)

    def sel_chunk(c, s_all, bias, diagonal):
        k0 = pl.multiple_of(c * KC, KC)
        v1 = vs1_ref[pl.ds(k0, KC), :]
        if diagonal:
            lane = lax.broadcasted_iota(jnp.int32, (TQ, KC), 1)
            bias = jnp.where(k0 + lane <= t_of((TQ, KC)), bias, NEG_INF)
        ps, alphas = [], []
        for r in range(REP):
            s = head(s_all, r) + bias
            m_old = m_scr[r]
            m_new = jnp.maximum(m_old, jnp.max(s, axis=-1, keepdims=True))
            m_scr[r] = m_new
            ps.append(jnp.exp2(s - jnp.tile(m_new, (1, KC // LANES))).astype(BF16))
            alphas.append(jnp.exp2(m_old - m_new))
        pv = jnp.dot(jnp.concatenate(ps, axis=0), v1, preferred_element_type=F32)
        for r in range(REP):
            acc_scr[r] = jnp.tile(alphas[r], (1, VAUG // LANES)) * acc_scr[r] + head(pv, r)

    n_full = t0 // KC

    def full_chunk(c, carry):
        s_cur, bias_cur = carry
        nxt = (scores(c + 1), sel_bias(c + 1))
        sel_chunk(c, s_cur, bias_cur, False)
        return nxt

    s_last, bias_last = lax.fori_loop(0, n_full, full_chunk, (scores(0), sel_bias(0)))
    sel_chunk(n_full, s_last, bias_last, True)

    gts = jax.nn.sigmoid(gt_ref[...])
    for r in range(REP):
        ow = ow_scr[r * TQ:(r + 1) * TQ, :]
        o_w = ow[:, 0:HEAD_B] / ow[:, HEAD_B:VAUG]
        acc = acc_scr[r]
        o_s = acc[:, 0:HEAD_B] / acc[:, HEAD_B:VAUG]
        o = (gts[:, 3 * r:3 * r + 1] * oc_scr[r] + gts[:, 3 * r + 1:3 * r + 2] * o_s
             + gts[:, 3 * r + 2:3 * r + 3] * o_w)
        z = z_ref[:, r * HEAD_B:(r + 1) * HEAD_B]
        o_ref[:, r * HEAD_B:(r + 1) * HEAD_B] = (o * (z * jax.nn.sigmoid(z))).astype(o_ref.dtype)


def nsa_attention(proj_b, cmp_kv, gates, q_gain, k_gain, overlap_t, expand, win_bias, blocks, batch, seq):
    nq = seq // TQ
    ncmp = seq // CMP_STRIDE
    gw = REP * HEAD_B // LANES
    qspec = lambda off: pl.BlockSpec((TQ, REP * HEAD_B), lambda b, g, t: (b * nq + t, off // gw + g))
    kvspec = lambda off: pl.BlockSpec((seq, HEAD_B), lambda b, g, t: (b, off + g))
    cspec = lambda w: pl.BlockSpec((None, None, None, ncmp, HEAD_B), lambda b, g, t: (w, b, g, 0, 0))
    full = lambda a: pl.BlockSpec(a.shape, lambda b, g, t: (0,) * a.ndim)
    return pl.pallas_call(
        functools.partial(_attn_kernel, seq=seq),
        grid=(batch, KV_GROUPS, nq),
        in_specs=[
            qspec(blocks["q"]), kvspec(blocks["ks"]), kvspec(blocks["vs"]), kvspec(blocks["kw"]), kvspec(blocks["vw"]),
            cspec(0), cspec(1),
            pl.BlockSpec((None, None, TQ, 16), lambda b, g, t: (b, g, t, 0)),
            qspec(blocks["z"]),
            full(q_gain), full(k_gain), full(overlap_t), full(expand), full(win_bias),
        ],
        out_specs=pl.BlockSpec((TQ, REP * HEAD_B), lambda b, g, t: (b * nq + t, g)),
        out_shape=jax.ShapeDtypeStruct((batch * seq, KV_GROUPS * REP * HEAD_B), BF16),
        scratch_shapes=[
            pltpu.VMEM((seq, HEAD_B), BF16), pltpu.VMEM((seq, VAUG), BF16),
            pltpu.VMEM((seq + WINDOW, HEAD_B), BF16), pltpu.VMEM((seq + WINDOW, VAUG), BF16),
            pltpu.VMEM((REP, TQ, HEAD_B), BF16), pltpu.VMEM((REP, TQ, LANES), F32),
            pltpu.VMEM((REP, TQ, VAUG), F32), pltpu.VMEM((REP, TQ, HEAD_B), F32),
            pltpu.VMEM((REP * TQ, VAUG), F32),
        ],
        compiler_params=_params(ndim=3),
        name="nsa_attention",
    )(proj_b, proj_b, proj_b, proj_b, proj_b, cmp_kv, cmp_kv, gates, proj_b, q_gain, k_gain, overlap_t, expand,
      win_bias)


def _merge_kernel(ya_ref, yb_ref, wa_ref, wb_ref, ga_ref, gb_ref, out_ref):
    ua = jnp.dot(ya_ref[...], wa_ref[...], preferred_element_type=F32)
    ub = jnp.dot(yb_ref[...], wb_ref[...], preferred_element_type=F32)
    out_ref[...] = (ga_ref[...].astype(F32) * ua + gb_ref[...].astype(F32) * ub).astype(out_ref.dtype)


def merge(ya, yb, w_a, w_b, gates, tm=1024, tn=512):
    m, ka = ya.shape
    n = w_a.shape[1]
    tm = min(tm, m)
    ga_block, gb_block = 0, n // tn
    return pl.pallas_call(
        _merge_kernel,
        grid=(m // tm, n // tn),
        in_specs=[
            pl.BlockSpec((tm, ka), lambda i, j: (i, 0)),
            pl.BlockSpec((tm, ka), lambda i, j: (i, 0)),
            pl.BlockSpec((ka, tn), lambda i, j: (0, j)),
            pl.BlockSpec((ka, tn), lambda i, j: (0, j)),
            pl.BlockSpec((tm, tn), lambda i, j: (i, ga_block + j)),
            pl.BlockSpec((tm, tn), lambda i, j: (i, gb_block + j)),
        ],
        out_specs=pl.BlockSpec((tm, tn), lambda i, j: (i, j)),
        out_shape=jax.ShapeDtypeStruct((m, n), BF16),
        compiler_params=_params(ndim=2),
        name="merge",
    )(ya, yb, w_a, w_b, gates, gates)


def _ple_embed_kernel(p_ref, w_ref, g_ref, o_ref):
    acc = jnp.dot(p_ref[...].astype(BF16), w_ref[...], preferred_element_type=F32)
    o_ref[...] = _rms(acc, g_ref[...]).astype(o_ref.dtype)


def ple_embed(p, w, g, tm=256):
    m, k = p.shape
    n = w.shape[1]
    return pl.pallas_call(
        _ple_embed_kernel,
        grid=(m // tm,),
        in_specs=[pl.BlockSpec((tm, k), lambda i: (i, 0)), pl.BlockSpec((k, n), lambda i: (0, 0)),
                  pl.BlockSpec((1, n), lambda i: (0, 0))],
        out_specs=pl.BlockSpec((tm, n), lambda i: (i, 0)),
        out_shape=jax.ShapeDtypeStruct((m, n), F32),
        compiler_params=_params(),
        name="ple_embed",
    )(p, w, g.reshape(1, n))


def head_minor(w, heads):
    lead = w.shape[:-1]
    return w.reshape(*lead, heads, HEAD_A).swapaxes(-1, -2).reshape(*lead, heads * HEAD_A)


def rwkv_branch(proj_a, batch, seq, a_width, w_lora_up, w0, a_lora_up, a0, k_k, k_a, r_k, lnx_w, lnx_b, h, w_gate):
    tokens = batch * seq
    heads_a = a_width // HEAD_A
    assert heads_a == SCAN_HEADS and batch % SCAN_BATCH == 0
    lora = w_lora_up.shape[0]
    zeros = jnp.zeros((lora, a_width), F32)
    hm = lambda w: head_minor(w, heads_a)
    w_lora = jnp.concatenate([jnp.concatenate([hm(w_lora_up), zeros], 1),
                              jnp.concatenate([zeros, hm(a_lora_up)], 1)], 0)
    b_lora = jnp.concatenate([hm(w0), hm(a0)]).reshape(1, -1)
    wa_pre = lora_project(proj_a, 4 * a_width // LANES, w_lora, b_lora)

    def par_scan(t):
        return jnp.tile(t.reshape(heads_a, HEAD_A).T, (1, SCAN_BATCH))

    par3 = lambda t: par_scan(t).reshape(HEAD_A, 1, LANES)
    ya, gates = rwkv_scan(proj_a.reshape(batch, seq, -1), wa_pre.reshape(batch, seq, -1),
                          par3(k_k), par3(k_a), par3(r_k), par_scan(lnx_w), par_scan(lnx_b), h, w_gate, batch, seq)
    return ya.reshape(tokens, a_width), gates


def nsa_branch(proj_b, graw, batch, seq, b_width, q_norm_g, k_norm_g, pe_cmp_k, pe_cmp_v,
               cmp_k_w1, cmp_k_w2, cmp_v_w1, cmp_v_w2):
    kv_width = KV_GROUPS * HEAD_B
    blk = lambda cols: cols // LANES
    blocks = {"q": 0, "kc": blk(b_width), "vc": blk(b_width + kv_width), "ks": blk(b_width + 2 * kv_width),
              "vs": blk(b_width + 3 * kv_width), "kw": blk(b_width + 4 * kv_width), "vw": blk(b_width + 5 * kv_width),
              "z": blk(b_width + 6 * kv_width)}
    pe = jnp.stack([pe_cmp_k, pe_cmp_v])
    hidden = cmp_k_w1.shape[1]
    w1 = jnp.stack([cmp_k_w1, cmp_v_w1]).reshape(2, L_CMP, HEAD_B, hidden).astype(BF16)
    w2 = jnp.stack([cmp_k_w2, cmp_v_w2]).astype(BF16)
    cmp_kv = compress(proj_b, blocks["kc"], blocks["vc"], pe, w1, w2, k_norm_g[0:1], batch, seq)

    ncmp = seq // CMP_STRIDE
    nsel = seq // L_SEL
    c_start = jnp.arange(ncmp) * CMP_STRIDE
    s_start = jnp.arange(nsel) * L_SEL
    overlap_t = ((c_start[None, :] < (s_start + L_SEL)[:, None]) & (s_start[:, None] < (c_start + L_CMP)[None, :])
                 & (jnp.arange(ncmp)[None, :] < ncmp - 1)).astype(F32)
    key_blk = jnp.arange(seq) // L_SEL
    expand = jnp.where(jnp.arange(LANES)[:, None] == key_blk[None, :], NEG_INF, 0.0).astype(BF16)
    expand = expand.reshape(LANES, seq // KC, KC).transpose(1, 0, 2)
    tl = jnp.arange(TQ)[:, None]
    u = jnp.arange(WINDOW + TQ)[None, :]
    win_bias = jnp.where((u > tl) & (u <= WINDOW + tl), 0.0, NEG_INF).astype(F32)
    gates = graw.reshape(batch, seq, KV_GROUPS, REP * 3).transpose(0, 2, 1, 3)
    gates = jnp.pad(gates, ((0, 0), (0, 0), (0, 0), (0, 16 - REP * 3)))
    return nsa_attention(proj_b, cmp_kv, gates, q_norm_g.reshape(1, HEAD_B), k_norm_g, overlap_t, expand, win_bias,
                         blocks, batch, seq)


def _layer(x, p, norm_g, w_in, shift_mu, w_lora_up, w0, a_lora_up, a0, k_k, k_a, r_k, lnx_w, lnx_b,
           q_norm_g, k_norm_g, pe_cmp_k, pe_cmp_v, cmp_k_w1, cmp_k_w2, cmp_v_w1, cmp_v_w2,
           w_up_a, w_up_b, w_out, ple_pre_g, w_ple_gate, w_ple, ple_post_g):
    batch, seq, d = x.shape
    tokens = batch * seq
    a_width = w_up_a.shape[0]
    b_width = w_up_b.shape[0]
    heads_a = a_width // HEAD_A
    kv_width = KV_GROUPS * HEAD_B
    n_gate = KV_GROUPS * REP * 3
    lora = w_lora_up.shape[0]
    assert lora == HEAD_A and a_lora_up.shape[0] == HEAD_A and 2 * lora == LANES
    a_cols = 4 * a_width + 2 * lora
    g_off = a_cols + b_width + 6 * kv_width
    zb_off = g_off + n_gate
    assert w_in.shape[1] == zb_off + b_width + 2 * d

    pad_a = LANES - n_gate
    hm4 = lambda w: head_minor(w.reshape(*w.shape[:-1], 4, a_width), heads_a).reshape(*w.shape[:-1], 4 * a_width)
    w_a = jnp.concatenate([hm4(w_in[:, :4 * a_width]), w_in[:, 4 * a_width:a_cols], w_in[:, g_off:zb_off],
                           jnp.zeros((d, pad_a), F32)], axis=1).astype(BF16)
    mu_a = jnp.concatenate([hm4(shift_mu[:4 * a_width]), shift_mu[4 * a_width:], jnp.zeros((LANES,), F32)]).reshape(1, -1)
    gates_off = zb_off + b_width
    w_b = jnp.concatenate([w_in[:, a_cols:g_off], w_in[:, zb_off:gates_off]], axis=1).astype(BF16)
    w_gate = w_in[:, gates_off:].astype(BF16)
    na = w_a.shape[1]
    tn_a = 768 if na % 768 == 0 else LANES

    x2 = x.reshape(tokens, d)
    h = rmsnorm_rows(x2, norm_g, BF16)
    tm = 512
    proj_a = matmul(h, w_a, mode="shift", extras=(mu_a,), tm=tm, tn=tn_a, tiles_per_seq=seq // tm, name="proj_a")
    proj_b = matmul(h, w_b, tm=tm, tn=1024, name="proj_b")

    ya, gates = rwkv_branch(proj_a, batch, seq, a_width, w_lora_up, w0, a_lora_up, a0, k_k, k_a, r_k, lnx_w, lnx_b,
                            h, w_gate)
    graw = proj_a[:, a_cols:a_cols + n_gate]
    yb = nsa_branch(proj_b, graw, batch, seq, b_width, q_norm_g, k_norm_g, pe_cmp_k, pe_cmp_v,
                    cmp_k_w1, cmp_k_w2, cmp_v_w1, cmp_v_w2)

    w_up_a_nm = head_minor(w_up_a.T, heads_a).T
    merged = merge(ya, yb, w_up_a_nm.astype(BF16), w_up_b.astype(BF16), gates)
    x1 = matmul(merged, w_out.astype(BF16), mode="resid", extras=(x2,), tm=tm, tn=1024, name="out_proj")
    hn = rmsnorm_rows(x1, ple_pre_g, BF16)
    e = ple_embed(p.reshape(tokens, -1), w_ple.astype(BF16), ple_post_g)
    out = matmul(hn, w_ple_gate.astype(BF16), mode="ple", extras=(x1, e), tm=tm, tn=1024, name="ple_gate")
    return out.reshape(batch, seq, d)


def kernel(x, p, norm_g, w_in, shift_mu, w_lora_up, w0, a_lora_up, a0, k_k, k_a, r_k, lnx_w, lnx_b, q_norm_g, k_norm_g, pe_cmp_k, pe_cmp_v, cmp_k_w1, cmp_k_w2, cmp_v_w1, cmp_v_w2, w_up_a, w_up_b, w_out, ple_pre_g, w_ple_gate, w_ple, ple_post_g):
    depth = w_in.shape[0]
    for i in range(depth):
        x = _layer(x, p[i], norm_g[i], w_in[i], shift_mu[i], w_lora_up[i], w0[i], a_lora_up[i], a0[i], k_k[i],
                   k_a[i], r_k[i], lnx_w[i], lnx_b[i], q_norm_g[i], k_norm_g[i], pe_cmp_k[i], pe_cmp_v[i],
                   cmp_k_w1[i], cmp_k_w2[i], cmp_v_w1[i], cmp_v_w2[i], w_up_a[i], w_up_b[i], w_out[i],
                   ple_pre_g[i], w_ple_gate[i], w_ple[i], ple_post_g[i])
    return x
```

```python
import functools

import jax
import jax.numpy as jnp
from jax import lax
from jax.experimental import pallas as pl
from jax.experimental.pallas import tpu as pltpu

F32 = jnp.float32
BF16 = jnp.bfloat16

LANES = 128
SUBLANES = 8
VMEM_LIMIT = 56 * 1024 * 1024

NORM_EPS = 1e-6
NEG_INF = -1e30
HEAD_A = 64
GN_EPS = 64e-5
HEAD_B = 128
KV_GROUPS = 4
REP = 4
N_BRANCH = 3
GATE_PAD = 16
L_CMP = 32
CMP_STRIDE = 16
L_SEL = 64
N_SEL = 16
WINDOW = 512
TQ = 128
KC = 256
FORCE_BONUS = 1e3
ATTN_SCALE = HEAD_B ** -0.5
DECAY_SCALE = 0.6065306597126334


def _params(vmem=VMEM_LIMIT, ndim=1):
    return pltpu.CompilerParams(dimension_semantics=("arbitrary",) * ndim, vmem_limit_bytes=vmem)


def _rmsnorm_kernel(x_ref, g_ref, o_ref):
    x = x_ref[...]
    ms = jnp.mean(x * x, axis=-1, keepdims=True)
    o_ref[...] = (x * lax.rsqrt(ms + NORM_EPS) * g_ref[...]).astype(o_ref.dtype)


def rmsnorm_rows(x, g, out_dtype, tm=256):
    m, d = x.shape
    return pl.pallas_call(
        _rmsnorm_kernel,
        grid=(m // tm,),
        in_specs=[pl.BlockSpec((tm, d), lambda i: (i, 0)), pl.BlockSpec((1, d), lambda i: (0, 0))],
        out_specs=pl.BlockSpec((tm, d), lambda i: (i, 0)),
        out_shape=jax.ShapeDtypeStruct((m, d), out_dtype),
        compiler_params=_params(),
        name="rmsnorm",
    )(x, g.reshape(1, d))


def _mm_kernel(*refs, mode, tiles_per_seq):
    a_ref, b_ref = refs[0], refs[1]
    acc = jnp.dot(a_ref[...], b_ref[...], preferred_element_type=F32)
    if mode == "plain":
        o_ref = refs[2]
        o_ref[...] = acc.astype(o_ref.dtype)
    elif mode == "shift":
        mu_ref, o_ref, carry_ref = refs[2], refs[3], refs[4]
        i = pl.program_id(1)
        tm = acc.shape[0]
        first = (i % tiles_per_seq) == 0
        last_prev = jnp.where(first, 0.0, carry_ref[SUBLANES - 1:SUBLANES, :])
        rolled = pltpu.roll(acc, 1, axis=0)
        row = lax.broadcasted_iota(jnp.int32, acc.shape, 0)
        prev = jnp.where(row == 0, last_prev, rolled)
        carry_ref[...] = acc[tm - SUBLANES:tm, :]
        o_ref[...] = (acc + mu_ref[...] * (prev - acc)).astype(o_ref.dtype)
    elif mode == "resid":
        r_ref, o_ref = refs[2], refs[3]
        o_ref[...] = (r_ref[...] + acc).astype(o_ref.dtype)
    elif mode == "ple":
        x_ref, e_ref, o_ref = refs[2], refs[3], refs[4]
        o_ref[...] = (x_ref[...] + jax.nn.sigmoid(acc) * e_ref[...]).astype(o_ref.dtype)
    else:
        raise ValueError(mode)


def matmul(a, b, *, mode="plain", extras=(), out_dtype=F32, tm=512, tn=1024, tiles_per_seq=1, name="mm"):
    m, k = a.shape
    n = b.shape[1]
    assert m % tm == 0 and n % tn == 0, (m, n, tm, tn)
    in_specs = [pl.BlockSpec((tm, k), lambda j, i: (i, 0)), pl.BlockSpec((k, tn), lambda j, i: (0, j))]
    scratch = []
    if mode == "shift":
        in_specs.append(pl.BlockSpec((1, tn), lambda j, i: (0, j)))
        scratch.append(pltpu.VMEM((SUBLANES, tn), F32))
    elif mode == "resid":
        in_specs.append(pl.BlockSpec((tm, tn), lambda j, i: (i, j)))
    elif mode == "ple":
        in_specs += [pl.BlockSpec((tm, tn), lambda j, i: (i, j)), pl.BlockSpec((tm, tn), lambda j, i: (i, j))]
    return pl.pallas_call(
        functools.partial(_mm_kernel, mode=mode, tiles_per_seq=tiles_per_seq),
        grid=(n // tn, m // tm),
        in_specs=in_specs,
        out_specs=pl.BlockSpec((tm, tn), lambda j, i: (i, j)),
        out_shape=jax.ShapeDtypeStruct((m, n), out_dtype),
        scratch_shapes=scratch,
        compiler_params=_params(ndim=2),
        name=name,
    )(a, b, *extras)


def _lora_kernel(x_ref, w_ref, b_ref, o_ref):
    x = x_ref[...]
    lane = lax.broadcasted_iota(jnp.int32, x.shape, 1)
    x = jnp.where(lane < HEAD_A, jnp.tanh(x), x)
    acc = jnp.dot(x, w_ref[...], preferred_element_type=F32, precision=lax.Precision.HIGHEST)
    o_ref[...] = acc + b_ref[...]


def lora_project(proj_a, col_block, w_blockdiag, bias, tm=1024, tn=1024):
    m = proj_a.shape[0]
    n = w_blockdiag.shape[1]
    tm = min(tm, m)
    assert m % tm == 0 and n % tn == 0
    return pl.pallas_call(
        _lora_kernel,
        grid=(m // tm, n // tn),
        in_specs=[
            pl.BlockSpec((tm, LANES), lambda i, j: (i, col_block)),
            pl.BlockSpec((LANES, tn), lambda i, j: (0, j)),
            pl.BlockSpec((1, tn), lambda i, j: (0, j)),
        ],
        out_specs=pl.BlockSpec((tm, tn), lambda i, j: (i, j)),
        out_shape=jax.ShapeDtypeStruct((m, n), F32),
        compiler_params=_params(ndim=2),
        name="lora",
    )(proj_a, w_blockdiag, bias)


SCAN_BATCH = 4
SCAN_HEADS = LANES // SCAN_BATCH
NGRP = LANES // SCAN_HEADS


def _segment_transpose(x):
    seg = lax.broadcasted_iota(jnp.int32, x[0].shape, 1) // SCAN_HEADS
    y = []
    for i in range(NGRP):
        out = None
        for j in range(SCAN_BATCH):
            shift = ((j - i) % NGRP) * SCAN_HEADS
            piece = x[j] if shift == 0 else pltpu.roll(x[j], shift, axis=1)
            out = piece if out is None else jnp.where(seg == j, piece, out)
        y.append(out)
    return y


def _to_lanes(x_ref, dst_ref, tc):
    for g in range(HEAD_A // NGRP):
        y = _segment_transpose([x_ref[b, :, g * LANES:(g + 1) * LANES] for b in range(SCAN_BATCH)])
        for n_lo in range(NGRP):
            n = g * NGRP + n_lo
            dst_ref[n * tc:(n + 1) * tc, :] = y[n_lo]


def _from_lanes_gated(src_ref, z_ref, o_ref, tc):
    for g in range(HEAD_A // NGRP):
        y = _segment_transpose([src_ref[(g * NGRP + n_lo) * tc:(g * NGRP + n_lo + 1) * tc, :] for n_lo in range(NGRP)])
        for b in range(SCAN_BATCH):
            z = z_ref[b, :, g * LANES:(g + 1) * LANES]
            o_ref[b, :, g * LANES:(g + 1) * LANES] = (y[b] * (z * jax.nn.sigmoid(z))).astype(o_ref.dtype)


def _scan_kernel(r_ref, k_ref, v_ref, z_ref, wp_ref, ap_ref, kkw_ref, kaw_ref, rkw_ref, lnw_ref, lnb_ref,
                 o_ref, state_ref, rs, ks, vs, ws, ans, bvs, os, bon, *, tc):
    n_ch = HEAD_A

    @pl.when(pl.program_id(1) == 0)
    def _():
        state_ref[...] = jnp.zeros_like(state_ref)

    _to_lanes(r_ref, rs, tc)
    _to_lanes(k_ref, ks, tc)
    _to_lanes(v_ref, vs, tc)
    _to_lanes(wp_ref, ws, tc)
    _to_lanes(ap_ref, bvs, tc)

    def rows(n):
        return pl.ds(pl.multiple_of(n * tc, tc), tc)

    def norm_acc(n, acc):
        kkr = ks[rows(n), :] * kkw_ref[n]
        return acc + kkr * kkr

    nsq = lax.fori_loop(0, n_ch, norm_acc, jnp.zeros((tc, LANES), F32), unroll=8)
    inv = 1.0 / jnp.maximum(jnp.sqrt(nsq), 1e-12)

    def prep(n, bacc):
        k = ks[rows(n), :]
        a = jax.nn.sigmoid(bvs[rows(n), :])
        kk = k * kkw_ref[n] * inv
        ws[rows(n), :] = jnp.exp(-DECAY_SCALE * jax.nn.sigmoid(ws[rows(n), :]))
        ans[rows(n), :] = -kk
        bvs[rows(n), :] = kk * a
        kmod = k * (1.0 + (a - 1.0) * kaw_ref[n])
        ks[rows(n), :] = kmod
        return bacc + rs[rows(n), :] * kmod * rkw_ref[n]

    bon[...] = lax.fori_loop(0, n_ch, prep, jnp.zeros((tc, LANES), F32), unroll=4)

    def bcast(ref, row):
        return ref[pl.ds(row, 1), :][None]

    def sa_first(j, acc):
        return acc + state_ref[j] * bcast(ans, j * tc)

    slab = (n_ch // SUBLANES, SUBLANES, LANES)
    sa0 = lax.fori_loop(0, n_ch, sa_first, jnp.zeros(slab, F32), unroll=4)

    def step(t, sa):
        v = vs[pl.ds(t, n_ch, stride=tc), :].reshape(slab)
        t_next = jnp.minimum(t + 1, tc - 1)

        out = jnp.zeros(slab, F32)
        sa_next = jnp.zeros(slab, F32)
        for j in range(n_ch):
            row = j * tc + t
            s_new = state_ref[j] * bcast(ws, row) + sa * bcast(bvs, row) + v * bcast(ks, row)
            state_ref[j] = s_new
            out = out + s_new * bcast(rs, row)
            sa_next = sa_next + s_new * bcast(ans, j * tc + t_next)
        o = out.reshape(n_ch, LANES)
        mu = jnp.mean(o, axis=0, keepdims=True)
        d = o - mu
        var = jnp.mean(d * d, axis=0, keepdims=True)
        on = d * lax.rsqrt(var + GN_EPS) * lnw_ref[...] + lnb_ref[...]
        bonus = bon[pl.ds(t, 1), :] * v.reshape(n_ch, LANES)
        os[pl.ds(t, n_ch, stride=tc), :] = on + bonus
        return sa_next

    lax.fori_loop(0, tc, step, sa0)
    _from_lanes_gated(os, z_ref, o_ref, tc)


def rwkv_scan(proj_a, wa_pre, kkw, kaw, rkw, lnw, lnb, batch, seq, tc=32):
    width = HEAD_A * SCAN_HEADS
    seqb = lambda col: pl.BlockSpec((SCAN_BATCH, tc, width), lambda g, c: (g, c, col))
    par3 = pl.BlockSpec((HEAD_A, 1, LANES), lambda g, c: (0, 0, 0))
    par2 = pl.BlockSpec((HEAD_A, LANES), lambda g, c: (0, 0))
    buf = pltpu.VMEM((HEAD_A * tc, LANES), F32)
    return pl.pallas_call(
        functools.partial(_scan_kernel, tc=tc),
        grid=(batch // SCAN_BATCH, seq // tc),
        in_specs=[seqb(0), seqb(1), seqb(2), seqb(3), seqb(0), seqb(1), par3, par3, par3, par2, par2],
        out_specs=seqb(0),
        out_shape=jax.ShapeDtypeStruct((batch, seq, width), BF16),
        scratch_shapes=[pltpu.VMEM((HEAD_A, HEAD_A // SUBLANES, SUBLANES, LANES), F32)] + [buf] * 7
        + [pltpu.VMEM((tc, LANES), F32)],
        compiler_params=_params(ndim=2),
        name="rwkv_scan",
    )(proj_a, proj_a, proj_a, proj_a, wa_pre, wa_pre, kkw, kaw, rkw, lnw, lnb)


def _compress_kernel(x_ref, pe_ref, w1_ref, w2_ref, kg_ref, o_ref):
    which = pl.program_id(0)
    half = L_CMP // 2
    nblk = x_ref.shape[0] // CMP_STRIDE
    h1 = jnp.zeros((nblk, w1_ref.shape[-1]), F32)
    h2 = jnp.zeros((nblk, w1_ref.shape[-1]), F32)
    for l in range(half):
        x = x_ref[pl.ds(l, nblk, stride=CMP_STRIDE), :]
        a1 = (x + pe_ref[l:l + 1, :]).astype(BF16)
        a2 = (x + pe_ref[half + l:half + l + 1, :]).astype(BF16)
        h1 = h1 + jnp.dot(a1, w1_ref[l], preferred_element_type=F32)
        h2 = h2 + jnp.dot(a2, w1_ref[half + l], preferred_element_type=F32)
    hid = h1 + pltpu.roll(h2, nblk - 1, axis=0)
    hid = jax.nn.gelu(hid)
    out = jnp.dot(hid.astype(BF16), w2_ref[...], preferred_element_type=F32)
    ms = jnp.mean(out * out, axis=-1, keepdims=True)
    normed = out * lax.rsqrt(ms + NORM_EPS) * kg_ref[...]
    out = jnp.where(which == 0, normed, out)
    row = lax.broadcasted_iota(jnp.int32, out.shape, 0)
    o_ref[...] = jnp.where(row < nblk - 1, out, 0.0)


def compress(proj_b, kc_block, vc_block, pe, w1, w2, kgain, batch, seq):
    nblk = seq // CMP_STRIDE
    hidden = w1.shape[-1]

    def xmap(w, b, g):
        return (b, kc_block + w * (vc_block - kc_block) + g)

    return pl.pallas_call(
        _compress_kernel,
        grid=(2, batch, KV_GROUPS),
        in_specs=[
            pl.BlockSpec((seq, HEAD_B), xmap),
            pl.BlockSpec((None, L_CMP, HEAD_B), lambda w, b, g: (w, 0, 0)),
            pl.BlockSpec((None, L_CMP, HEAD_B, hidden), lambda w, b, g: (w, 0, 0, 0)),
            pl.BlockSpec((None, hidden, HEAD_B), lambda w, b, g: (w, 0, 0)),
            pl.BlockSpec((1, HEAD_B), lambda w, b, g: (0, 0)),
        ],
        out_specs=pl.BlockSpec((None, None, None, nblk, HEAD_B), lambda w, b, g: (w, b, g, 0, 0)),
        out_shape=jax.ShapeDtypeStruct((2, batch, KV_GROUPS, nblk, HEAD_B), F32),
        compiler_params=_params(ndim=3),
        name="nsa_compress",
    )(proj_b, pe, w1, w2, kgain)


def _rms(x, gain):
    ms = jnp.mean(x * x, axis=-1, keepdims=True)
    return x * lax.rsqrt(ms + NORM_EPS) * gain


def _dot_nt(a, b):
    return lax.dot_general(a, b, (((1,), (1,)), ((), ())), preferred_element_type=F32)


LOG2E = 1.4426950408889634
VAUG = 2 * HEAD_B


def _attn_kernel(q_ref, ks_ref, vs_ref, kw_ref, vw_ref, kc_ref, vc_ref, gt_ref, z_ref, qg_ref, kg_ref,
                 ovt_ref, ex_ref, wb_ref, o_ref, ksn_ref, vs1_ref, kwn_ref, vw1_ref, q_scr, m_scr, acc_scr,
                 oc_scr, ow_scr, *, seq):
    qt = pl.program_id(2)
    ncmp = kc_ref.shape[0]
    nsel = seq // L_SEL

    @pl.when(qt == 0)
    def _():
        ones_col = jnp.ones((seq, HEAD_B), BF16)
        ksn_ref[...] = _rms(ks_ref[...], kg_ref[1:2, :]).astype(BF16)
        vs1_ref[:, 0:HEAD_B] = vs_ref[...].astype(BF16)
        vs1_ref[:, HEAD_B:VAUG] = ones_col
        kwn_ref[0:WINDOW, :] = jnp.zeros((WINDOW, HEAD_B), BF16)
        vw1_ref[0:WINDOW, :] = jnp.zeros((WINDOW, VAUG), BF16)
        kwn_ref[WINDOW:WINDOW + seq, :] = _rms(kw_ref[...], kg_ref[2:3, :]).astype(BF16)
        vw1_ref[WINDOW:WINDOW + seq, 0:HEAD_B] = vw_ref[...].astype(BF16)
        vw1_ref[WINDOW:WINDOW + seq, HEAD_B:VAUG] = ones_col

    t0 = pl.multiple_of(qt * TQ, TQ)
    for r in range(REP):
        q_scr[r] = (_rms(q_ref[:, r * HEAD_B:(r + 1) * HEAD_B], qg_ref[...]) * (ATTN_SCALE * LOG2E)).astype(BF16)

    def t_of(shape):
        return t0 + lax.broadcasted_iota(jnp.int32, shape, 0)

    n_idx = lax.broadcasted_iota(jnp.int32, (TQ, ncmp), 1)
    bias_c = jnp.where(n_idx * CMP_STRIDE + (L_CMP - 1) <= t_of((TQ, ncmp)), 0.0, NEG_INF)
    bias_c = jnp.where(n_idx < ncmp - 1, bias_c, NEG_INF)
    row_ok = jnp.where(t_of((TQ, 1)) >= L_CMP - 1, 1.0, 0.0)
    q_all = q_scr[...].reshape(REP * TQ, HEAD_B)
    head = lambda x, r: x[r * TQ:(r + 1) * TQ]
    s_all = _dot_nt(q_all, kc_ref[...].astype(BF16))
    prs = []
    for r in range(REP):
        s = head(s_all, r) + bias_c
        p = jnp.exp2(s - jnp.max(s, axis=-1, keepdims=True))
        prs.append(p * (row_ok / jnp.sum(p, axis=-1, keepdims=True)))
    psum = prs[0] + prs[1] + prs[2] + prs[3]
    oc_scr[...] = jnp.dot(jnp.concatenate(prs, axis=0).astype(BF16), vc_ref[...].astype(BF16),
                          preferred_element_type=F32).reshape(REP, TQ, HEAD_B)
    imp_t = lax.dot_general(ovt_ref[...], psum, (((1,), (1,)), ((), ())), preferred_element_type=F32,
                            precision=lax.Precision.HIGHEST)

    j_idx = lax.broadcasted_iota(jnp.int32, (nsel, TQ), 0)
    t_sel = t0 + lax.broadcasted_iota(jnp.int32, (nsel, TQ), 1)
    cur = t_sel // L_SEL
    forced = jnp.where((j_idx == 0) | (j_idx == cur) | (j_idx == cur - 1), FORCE_BONUS, 0.0)
    score = jnp.where(j_idx * L_SEL <= t_sel, imp_t + forced, NEG_INF)
    rank = jnp.zeros((nsel, TQ), F32)
    for i in range(nsel):
        si = score[i:i + 1, :]
        ahead = (si > score) | ((si == score) & (j_idx > i))
        rank = rank + jnp.where(ahead, 1.0, 0.0)
    sel_t = jnp.where((rank < min(N_SEL, nsel)) & (score > 0.5 * NEG_INF), 1.0, 0.0)
    sel = jnp.concatenate([sel_t, jnp.zeros((LANES - nsel, TQ), F32)], axis=0).T
    not_sel = (1.0 - sel).astype(BF16)

    span = WINDOW + TQ
    kwin = kwn_ref[pl.ds(t0, span), :]
    vwin = vw1_ref[pl.ds(t0, span), :]
    lane_w = lax.broadcasted_iota(jnp.int32, (TQ, span), 1)
    bias_w = jnp.where(lane_w >= WINDOW - t0, wb_ref[...], NEG_INF)
    s_all = _dot_nt(q_all, kwin)
    ps = []
    for r in range(REP):
        s = head(s_all, r) + bias_w
        ps.append(jnp.exp2(s - jnp.max(s, axis=-1, keepdims=True)).astype(BF16))
    ow_scr[...] = jnp.dot(jnp.concatenate(ps, axis=0), vwin, preferred_element_type=F32)

    m_scr[...] = jnp.full(m_scr.shape, NEG_INF, F32)
    acc_scr[...] = jnp.zeros(acc_scr.shape, F32)

    def scores(c):
        k0 = pl.multiple_of(c * KC, KC)
        return _dot_nt(q_scr[...].reshape(REP * TQ, HEAD_B), ksn_ref[pl.ds(k0, KC), :])

    def sel_bias(c):
        return jnp.dot(not_sel, ex_ref[c], preferred_element_type=F32)

    def sel_chunk(c, s_all, bias, diagonal):
        k0 = pl.multiple_of(c * KC, KC)
        v1 = vs1_ref[pl.ds(k0, KC), :]
        if diagonal:
            lane = lax.broadcasted_iota(jnp.int32, (TQ, KC), 1)
            bias = jnp.where(k0 + lane <= t_of((TQ, KC)), bias, NEG_INF)
        ps, alphas = [], []
        for r in range(REP):
            s = head(s_all, r) + bias
            m_old = m_scr[r]
            m_new = jnp.maximum(m_old, jnp.max(s, axis=-1, keepdims=True))
            m_scr[r] = m_new
            ps.append(jnp.exp2(s - jnp.tile(m_new, (1, KC // LANES))).astype(BF16))
            alphas.append(jnp.exp2(m_old - m_new))
        pv = jnp.dot(jnp.concatenate(ps, axis=0), v1, preferred_element_type=F32)
        for r in range(REP):
            acc_scr[r] = jnp.tile(alphas[r], (1, VAUG // LANES)) * acc_scr[r] + head(pv, r)

    n_full = t0 // KC

    def full_chunk(c, carry):
        s_cur, bias_cur = carry
        nxt = (scores(c + 1), sel_bias(c + 1))
        sel_chunk(c, s_cur, bias_cur, False)
        return nxt

    s_last, bias_last = lax.fori_loop(0, n_full, full_chunk, (scores(0), sel_bias(0)))
    sel_chunk(n_full, s_last, bias_last, True)

    gts = jax.nn.sigmoid(gt_ref[...])
    for r in range(REP):
        ow = ow_scr[r * TQ:(r + 1) * TQ, :]
        o_w = ow[:, 0:HEAD_B] / ow[:, HEAD_B:VAUG]
        acc = acc_scr[r]
        o_s = acc[:, 0:HEAD_B] / acc[:, HEAD_B:VAUG]
        g0 = N_BRANCH * r
        o = gts[:, g0:g0 + 1] * oc_scr[r] + gts[:, g0 + 1:g0 + 2] * o_s + gts[:, g0 + 2:g0 + 3] * o_w
        z = z_ref[:, r * HEAD_B:(r + 1) * HEAD_B]
        o_ref[:, r * HEAD_B:(r + 1) * HEAD_B] = (o * (z * jax.nn.sigmoid(z))).astype(o_ref.dtype)


def nsa_attention(proj_b, cmp_kv, gates, q_gain, k_gain, overlap_t, expand, win_bias, blocks, batch, seq):
    nq = seq // TQ
    ncmp = seq // CMP_STRIDE
    gw = REP * HEAD_B // LANES
    qspec = lambda off: pl.BlockSpec((TQ, REP * HEAD_B), lambda b, g, t: (b * nq + t, off // gw + g))
    kvspec = lambda off: pl.BlockSpec((seq, HEAD_B), lambda b, g, t: (b, off + g))
    cspec = lambda w: pl.BlockSpec((None, None, None, ncmp, HEAD_B), lambda b, g, t: (w, b, g, 0, 0))
    full = lambda a: pl.BlockSpec(a.shape, lambda b, g, t: (0,) * a.ndim)
    return pl.pallas_call(
        functools.partial(_attn_kernel, seq=seq),
        grid=(batch, KV_GROUPS, nq),
        in_specs=[
            qspec(blocks["q"]), kvspec(blocks["ks"]), kvspec(blocks["vs"]), kvspec(blocks["kw"]), kvspec(blocks["vw"]),
            cspec(0), cspec(1),
            pl.BlockSpec((None, None, TQ, GATE_PAD), lambda b, g, t: (b, g, t, 0)),
            qspec(blocks["z"]),
            full(q_gain), full(k_gain), full(overlap_t), full(expand), full(win_bias),
        ],
        out_specs=pl.BlockSpec((TQ, REP * HEAD_B), lambda b, g, t: (b * nq + t, g)),
        out_shape=jax.ShapeDtypeStruct((batch * seq, KV_GROUPS * REP * HEAD_B), BF16),
        scratch_shapes=[
            pltpu.VMEM((seq, HEAD_B), BF16), pltpu.VMEM((seq, VAUG), BF16),
            pltpu.VMEM((seq + WINDOW, HEAD_B), BF16), pltpu.VMEM((seq + WINDOW, VAUG), BF16),
            pltpu.VMEM((REP, TQ, HEAD_B), BF16), pltpu.VMEM((REP, TQ, LANES), F32),
            pltpu.VMEM((REP, TQ, VAUG), F32), pltpu.VMEM((REP, TQ, HEAD_B), F32),
            pltpu.VMEM((REP * TQ, VAUG), F32),
        ],
        compiler_params=_params(ndim=3),
        name="nsa_attention",
    )(proj_b, proj_b, proj_b, proj_b, proj_b, cmp_kv, cmp_kv, gates, proj_b, q_gain, k_gain, overlap_t, expand,
      win_bias)


def _merge_kernel(ya_ref, yb_ref, wa_ref, wb_ref, ga_ref, gb_ref, out_ref):
    ua = jnp.dot(ya_ref[...], wa_ref[...], preferred_element_type=F32)
    ub = jnp.dot(yb_ref[...], wb_ref[...], preferred_element_type=F32)
    out_ref[...] = (jax.nn.sigmoid(ga_ref[...]) * ua + jax.nn.sigmoid(gb_ref[...]) * ub).astype(out_ref.dtype)


def merge(ya, yb, w_a, w_b, proj_b, gate_col, tm=1024, tn=512):
    m, ka = ya.shape
    n = w_a.shape[1]
    tm = min(tm, m)
    assert gate_col % tn == 0 and n % tn == 0
    ga_block, gb_block = gate_col // tn, (gate_col + n) // tn
    return pl.pallas_call(
        _merge_kernel,
        grid=(m // tm, n // tn),
        in_specs=[
            pl.BlockSpec((tm, ka), lambda i, j: (i, 0)),
            pl.BlockSpec((tm, ka), lambda i, j: (i, 0)),
            pl.BlockSpec((ka, tn), lambda i, j: (0, j)),
            pl.BlockSpec((ka, tn), lambda i, j: (0, j)),
            pl.BlockSpec((tm, tn), lambda i, j: (i, ga_block + j)),
            pl.BlockSpec((tm, tn), lambda i, j: (i, gb_block + j)),
        ],
        out_specs=pl.BlockSpec((tm, tn), lambda i, j: (i, j)),
        out_shape=jax.ShapeDtypeStruct((m, n), BF16),
        compiler_params=_params(ndim=2),
        name="merge",
    )(ya, yb, w_a, w_b, proj_b, proj_b)


def _ple_embed_kernel(p_ref, w_ref, g_ref, o_ref):
    acc = jnp.dot(p_ref[...].astype(BF16), w_ref[...], preferred_element_type=F32)
    o_ref[...] = _rms(acc, g_ref[...]).astype(o_ref.dtype)


def ple_embed(p, w, g, tm=256):
    m, k = p.shape
    n = w.shape[1]
    return pl.pallas_call(
        _ple_embed_kernel,
        grid=(m // tm,),
        in_specs=[pl.BlockSpec((tm, k), lambda i: (i, 0)), pl.BlockSpec((k, n), lambda i: (0, 0)),
                  pl.BlockSpec((1, n), lambda i: (0, 0))],
        out_specs=pl.BlockSpec((tm, n), lambda i: (i, 0)),
        out_shape=jax.ShapeDtypeStruct((m, n), F32),
        compiler_params=_params(),
        name="ple_embed",
    )(p, w, g.reshape(1, n))


def head_minor(w, heads):
    lead = w.shape[:-1]
    return w.reshape(*lead, heads, HEAD_A).swapaxes(-1, -2).reshape(*lead, heads * HEAD_A)


def rwkv_branch(proj_a, batch, seq, a_width, w_lora_up, w0, a_lora_up, a0, k_k, k_a, r_k, lnx_w, lnx_b):
    tokens = batch * seq
    heads_a = a_width // HEAD_A
    assert heads_a == SCAN_HEADS and batch % SCAN_BATCH == 0
    lora = w_lora_up.shape[0]
    zeros = jnp.zeros((lora, a_width), F32)
    hm = lambda w: head_minor(w, heads_a)
    w_lora = jnp.concatenate([jnp.concatenate([hm(w_lora_up), zeros], 1),
                              jnp.concatenate([zeros, hm(a_lora_up)], 1)], 0)
    b_lora = jnp.concatenate([hm(w0), hm(a0)]).reshape(1, -1)
    wa_pre = lora_project(proj_a, 4 * a_width // LANES, w_lora, b_lora)

    def par_scan(t):
        return jnp.tile(t.reshape(heads_a, HEAD_A).T, (1, SCAN_BATCH))

    par3 = lambda t: par_scan(t).reshape(HEAD_A, 1, LANES)
    o = rwkv_scan(proj_a.reshape(batch, seq, -1), wa_pre.reshape(batch, seq, -1),
                  par3(k_k), par3(k_a), par3(r_k), par_scan(lnx_w), par_scan(lnx_b), batch, seq)
    return o.reshape(tokens, a_width)


def nsa_branch(proj_b, graw, batch, seq, b_width, q_norm_g, k_norm_g, pe_cmp_k, pe_cmp_v,
               cmp_k_w1, cmp_k_w2, cmp_v_w1, cmp_v_w2):
    kv_width = KV_GROUPS * HEAD_B
    blk = lambda cols: cols // LANES
    blocks = {"q": 0, "kc": blk(b_width), "vc": blk(b_width + kv_width), "ks": blk(b_width + 2 * kv_width),
              "vs": blk(b_width + 3 * kv_width), "kw": blk(b_width + 4 * kv_width), "vw": blk(b_width + 5 * kv_width),
              "z": blk(b_width + 6 * kv_width)}
    pe = jnp.stack([pe_cmp_k, pe_cmp_v])
    hidden = cmp_k_w1.shape[1]
    w1 = jnp.stack([cmp_k_w1, cmp_v_w1]).reshape(2, L_CMP, HEAD_B, hidden).astype(BF16)
    w2 = jnp.stack([cmp_k_w2, cmp_v_w2]).astype(BF16)
    cmp_kv = compress(proj_b, blocks["kc"], blocks["vc"], pe, w1, w2, k_norm_g[0:1], batch, seq)

    ncmp = seq // CMP_STRIDE
    nsel = seq // L_SEL
    c_start = jnp.arange(ncmp) * CMP_STRIDE
    s_start = jnp.arange(nsel) * L_SEL
    overlap_t = ((c_start[None, :] < (s_start + L_SEL)[:, None]) & (s_start[:, None] < (c_start + L_CMP)[None, :])
                 & (jnp.arange(ncmp)[None, :] < ncmp - 1)).astype(F32)
    key_blk = jnp.arange(seq) // L_SEL
    expand = jnp.where(jnp.arange(LANES)[:, None] == key_blk[None, :], NEG_INF, 0.0).astype(BF16)
    expand = expand.reshape(LANES, seq // KC, KC).transpose(1, 0, 2)
    tl = jnp.arange(TQ)[:, None]
    u = jnp.arange(WINDOW + TQ)[None, :]
    win_bias = jnp.where((u > tl) & (u <= WINDOW + tl), 0.0, NEG_INF).astype(F32)
    gates = graw.reshape(batch, seq, KV_GROUPS, REP * N_BRANCH).transpose(0, 2, 1, 3)
    gates = jnp.pad(gates, ((0, 0), (0, 0), (0, 0), (0, GATE_PAD - REP * N_BRANCH)))
    return nsa_attention(proj_b, cmp_kv, gates, q_norm_g.reshape(1, HEAD_B), k_norm_g, overlap_t, expand, win_bias,
                         blocks, batch, seq)


def _layer(x, p, norm_g, w_in, shift_mu, w_lora_up, w0, a_lora_up, a0, k_k, k_a, r_k, lnx_w, lnx_b,
           q_norm_g, k_norm_g, pe_cmp_k, pe_cmp_v, cmp_k_w1, cmp_k_w2, cmp_v_w1, cmp_v_w2,
           w_up_a, w_up_b, w_out, ple_pre_g, w_ple_gate, w_ple, ple_post_g):
    batch, seq, d = x.shape
    tokens = batch * seq
    a_width = w_up_a.shape[0]
    b_width = w_up_b.shape[0]
    heads_a = a_width // HEAD_A
    kv_width = KV_GROUPS * HEAD_B
    n_gate = KV_GROUPS * REP * N_BRANCH
    lora = w_lora_up.shape[0]
    assert lora == HEAD_A and a_lora_up.shape[0] == HEAD_A and 2 * lora == LANES
    a_cols = 4 * a_width + 2 * lora
    g_off = a_cols + b_width + 6 * kv_width
    zb_off = g_off + n_gate
    assert w_in.shape[1] == zb_off + b_width + 2 * d

    pad_a = LANES - n_gate
    hm4 = lambda w: head_minor(w.reshape(*w.shape[:-1], 4, a_width), heads_a).reshape(*w.shape[:-1], 4 * a_width)
    w_a = jnp.concatenate([hm4(w_in[:, :4 * a_width]), w_in[:, 4 * a_width:a_cols], w_in[:, g_off:zb_off],
                           jnp.zeros((d, pad_a), F32)], axis=1).astype(BF16)
    mu_a = jnp.concatenate([hm4(shift_mu[:4 * a_width]), shift_mu[4 * a_width:], jnp.zeros((LANES,), F32)]).reshape(1, -1)
    w_b = jnp.concatenate([w_in[:, a_cols:g_off], w_in[:, zb_off:]], axis=1).astype(BF16)
    na = w_a.shape[1]
    tn_a = 768 if na % 768 == 0 else LANES

    x2 = x.reshape(tokens, d)
    h = rmsnorm_rows(x2, norm_g, BF16)
    tm = 512
    proj_a = matmul(h, w_a, mode="shift", extras=(mu_a,), tm=tm, tn=tn_a, tiles_per_seq=seq // tm, name="proj_a")
    proj_b = matmul(h, w_b, tm=tm, tn=1024, name="proj_b")

    ya = rwkv_branch(proj_a, batch, seq, a_width, w_lora_up, w0, a_lora_up, a0, k_k, k_a, r_k, lnx_w, lnx_b)
    graw = proj_a[:, a_cols:a_cols + n_gate]
    yb = nsa_branch(proj_b, graw, batch, seq, b_width, q_norm_g, k_norm_g, pe_cmp_k, pe_cmp_v,
                    cmp_k_w1, cmp_k_w2, cmp_v_w1, cmp_v_w2)

    w_up_a_nm = head_minor(w_up_a.T, heads_a).T
    merged = merge(ya, yb, w_up_a_nm.astype(BF16), w_up_b.astype(BF16), proj_b, 2 * b_width + 6 * kv_width)
    x1 = matmul(merged, w_out.astype(BF16), mode="resid", extras=(x2,), tm=tm, tn=1024, name="out_proj")
    hn = rmsnorm_rows(x1, ple_pre_g, BF16)
    e = ple_embed(p.reshape(tokens, -1), w_ple.astype(BF16), ple_post_g)
    out = matmul(hn, w_ple_gate.astype(BF16), mode="ple", extras=(x1, e), tm=tm, tn=1024, name="ple_gate")
    return out.reshape(batch, seq, d)


def kernel(x, p, norm_g, w_in, shift_mu, w_lora_up, w0, a_lora_up, a0, k_k, k_a, r_k, lnx_w, lnx_b, q_norm_g, k_norm_g, pe_cmp_k, pe_cmp_v, cmp_k_w1, cmp_k_w2, cmp_v_w1, cmp_v_w2, w_up_a, w_up_b, w_out, ple_pre_g, w_ple_gate, w_ple, ple_post_g):
    depth = w_in.shape[0]
    for i in range(depth):
        x = _layer(x, p[i], norm_g[i], w_in[i], shift_mu[i], w_lora_up[i], w0[i], a_lora_up[i], a0[i], k_k[i],
                   k_a[i], r_k[i], lnx_w[i], lnx_b[i], q_norm_g[i], k_norm_g[i], pe_cmp_k[i], pe_cmp_v[i],
                   cmp_k_w1[i], cmp_k_w2[i], cmp_v_w1[i], cmp_v_w2[i], w_up_a[i], w_up_b[i], w_out[i],
                   ple_pre_g[i], w_ple_gate[i], w_ple[i], ple_post_g[i])
    return x
```

```python
import functools

import jax
import jax.numpy as jnp
from jax import lax
from jax.experimental import pallas as pl
from jax.experimental.pallas import tpu as pltpu

F32 = jnp.float32
BF16 = jnp.bfloat16

LANES = 128
SUBLANES = 8
VMEM_LIMIT = 56 * 1024 * 1024

NORM_EPS = 1e-6
NEG_INF = -1e30
HEAD_A = 64
GN_EPS = 64e-5
HEAD_B = 128
KV_GROUPS = 4
REP = 4
N_BRANCH = 3
GATE_PAD = 16
L_CMP = 32
CMP_STRIDE = 16
L_SEL = 64
N_SEL = 16
WINDOW = 512
TQ = 128
KC = 256
FORCE_BONUS = 1e3
ATTN_SCALE = HEAD_B ** -0.5
DECAY_SCALE = 0.6065306597126334


def _params(vmem=VMEM_LIMIT, ndim=1):
    return pltpu.CompilerParams(dimension_semantics=("arbitrary",) * ndim, vmem_limit_bytes=vmem)


def _rmsnorm_kernel(x_ref, g_ref, o_ref):
    x = x_ref[...]
    ms = jnp.mean(x * x, axis=-1, keepdims=True)
    o_ref[...] = (x * lax.rsqrt(ms + NORM_EPS) * g_ref[...]).astype(o_ref.dtype)


def rmsnorm_rows(x, g, out_dtype, tm=256):
    m, d = x.shape
    return pl.pallas_call(
        _rmsnorm_kernel,
        grid=(m // tm,),
        in_specs=[pl.BlockSpec((tm, d), lambda i: (i, 0)), pl.BlockSpec((1, d), lambda i: (0, 0))],
        out_specs=pl.BlockSpec((tm, d), lambda i: (i, 0)),
        out_shape=jax.ShapeDtypeStruct((m, d), out_dtype),
        compiler_params=_params(),
        name="rmsnorm",
    )(x, g.reshape(1, d))


def _mm_kernel(*refs, mode, tiles_per_seq):
    a_ref, b_ref = refs[0], refs[1]
    acc = jnp.dot(a_ref[...], b_ref[...], preferred_element_type=F32)
    if mode == "plain":
        o_ref = refs[2]
        o_ref[...] = acc.astype(o_ref.dtype)
    elif mode == "shift":
        mu_ref, o_ref, carry_ref = refs[2], refs[3], refs[4]
        i = pl.program_id(1)
        tm = acc.shape[0]
        first = (i % tiles_per_seq) == 0
        last_prev = jnp.where(first, 0.0, carry_ref[SUBLANES - 1:SUBLANES, :])
        rolled = pltpu.roll(acc, 1, axis=0)
        row = lax.broadcasted_iota(jnp.int32, acc.shape, 0)
        prev = jnp.where(row == 0, last_prev, rolled)
        carry_ref[...] = acc[tm - SUBLANES:tm, :]
        o_ref[...] = (acc + mu_ref[...] * (prev - acc)).astype(o_ref.dtype)
    elif mode == "resid":
        r_ref, o_ref = refs[2], refs[3]
        o_ref[...] = (r_ref[...] + acc).astype(o_ref.dtype)
    elif mode == "ple":
        x_ref, e_ref, o_ref = refs[2], refs[3], refs[4]
        o_ref[...] = (x_ref[...] + jax.nn.sigmoid(acc) * e_ref[...]).astype(o_ref.dtype)
    else:
        raise ValueError(mode)


def matmul(a, b, *, mode="plain", extras=(), out_dtype=F32, tm=512, tn=1024, tiles_per_seq=1, name="mm"):
    m, k = a.shape
    n = b.shape[1]
    assert m % tm == 0 and n % tn == 0, (m, n, tm, tn)
    in_specs = [pl.BlockSpec((tm, k), lambda j, i: (i, 0)), pl.BlockSpec((k, tn), lambda j, i: (0, j))]
    scratch = []
    if mode == "shift":
        in_specs.append(pl.BlockSpec((1, tn), lambda j, i: (0, j)))
        scratch.append(pltpu.VMEM((SUBLANES, tn), F32))
    elif mode == "resid":
        in_specs.append(pl.BlockSpec((tm, tn), lambda j, i: (i, j)))
    elif mode == "ple":
        in_specs += [pl.BlockSpec((tm, tn), lambda j, i: (i, j)), pl.BlockSpec((tm, tn), lambda j, i: (i, j))]
    return pl.pallas_call(
        functools.partial(_mm_kernel, mode=mode, tiles_per_seq=tiles_per_seq),
        grid=(n // tn, m // tm),
        in_specs=in_specs,
        out_specs=pl.BlockSpec((tm, tn), lambda j, i: (i, j)),
        out_shape=jax.ShapeDtypeStruct((m, n), out_dtype),
        scratch_shapes=scratch,
        compiler_params=_params(ndim=2),
        name=name,
    )(a, b, *extras)


def _lora_kernel(x_ref, w_ref, b_ref, o_ref):
    x = x_ref[...]
    lane = lax.broadcasted_iota(jnp.int32, x.shape, 1)
    x = jnp.where(lane < HEAD_A, jnp.tanh(x), x)
    acc = jnp.dot(x, w_ref[...], preferred_element_type=F32, precision=lax.Precision.HIGHEST)
    o_ref[...] = acc + b_ref[...]


def lora_project(proj_a, col_block, w_blockdiag, bias, tm=1024, tn=1024):
    m = proj_a.shape[0]
    n = w_blockdiag.shape[1]
    tm = min(tm, m)
    assert m % tm == 0 and n % tn == 0
    return pl.pallas_call(
        _lora_kernel,
        grid=(m // tm, n // tn),
        in_specs=[
            pl.BlockSpec((tm, LANES), lambda i, j: (i, col_block)),
            pl.BlockSpec((LANES, tn), lambda i, j: (0, j)),
            pl.BlockSpec((1, tn), lambda i, j: (0, j)),
        ],
        out_specs=pl.BlockSpec((tm, tn), lambda i, j: (i, j)),
        out_shape=jax.ShapeDtypeStruct((m, n), F32),
        compiler_params=_params(ndim=2),
        name="lora",
    )(proj_a, w_blockdiag, bias)


SCAN_BATCH = 4
SCAN_HEADS = LANES // SCAN_BATCH
NGRP = LANES // SCAN_HEADS


def _segment_transpose(x):
    seg = lax.broadcasted_iota(jnp.int32, x[0].shape, 1) // SCAN_HEADS
    y = []
    for i in range(NGRP):
        out = None
        for j in range(SCAN_BATCH):
            shift = ((j - i) % NGRP) * SCAN_HEADS
            piece = x[j] if shift == 0 else pltpu.roll(x[j], shift, axis=1)
            out = piece if out is None else jnp.where(seg == j, piece, out)
        y.append(out)
    return y


def _to_lanes(x_ref, dst_ref, tc):
    for g in range(HEAD_A // NGRP):
        y = _segment_transpose([x_ref[b, :, g * LANES:(g + 1) * LANES] for b in range(SCAN_BATCH)])
        for n_lo in range(NGRP):
            n = g * NGRP + n_lo
            dst_ref[n * tc:(n + 1) * tc, :] = y[n_lo]


def _from_lanes_gated(src_ref, z_ref, o_ref, tc):
    for g in range(HEAD_A // NGRP):
        y = _segment_transpose([src_ref[(g * NGRP + n_lo) * tc:(g * NGRP + n_lo + 1) * tc, :] for n_lo in range(NGRP)])
        for b in range(SCAN_BATCH):
            z = z_ref[b, :, g * LANES:(g + 1) * LANES]
            o_ref[b, :, g * LANES:(g + 1) * LANES] = (y[b] * (z * jax.nn.sigmoid(z))).astype(o_ref.dtype)


def _scan_kernel(r_ref, k_ref, v_ref, wp_ref, ap_ref, z_ref, kkw_ref, kaw_ref, rkw_ref, lnw_ref, lnb_ref,
                 o_ref, state_ref, set_a, set_b, ans, os, bon, *, tc):
    s = pl.program_id(1)
    srcs = (r_ref, k_ref, v_ref, wp_ref, ap_ref)

    @pl.when(s == 0)
    def _():
        state_ref[...] = jnp.zeros_like(state_ref)
        for src, dst in zip(srcs, set_a):
            _to_lanes(src, dst, tc)

    @pl.when(s % 2 == 1)
    def _():
        _scan_chunk(set_a, set_b, srcs, z_ref, kkw_ref, kaw_ref, rkw_ref, lnw_ref, lnb_ref, o_ref, state_ref,
                    ans, os, bon, tc)

    @pl.when((s % 2 == 0) & (s > 0))
    def _():
        _scan_chunk(set_b, set_a, srcs, z_ref, kkw_ref, kaw_ref, rkw_ref, lnw_ref, lnb_ref, o_ref, state_ref,
                    ans, os, bon, tc)


def _scan_chunk(cur, nxt, srcs, z_ref, kkw_ref, kaw_ref, rkw_ref, lnw_ref, lnb_ref, o_ref, state_ref,
                ans, os, bon, tc):
    n_ch = HEAD_A
    rs, ks, vs, ws, bvs = cur
    n_groups = HEAD_A // NGRP
    groups_per_step = 2
    steps_per_rowblock = n_groups // groups_per_step
    assert tc * groups_per_step == n_groups * (tc // SUBLANES)

    def rows(n):
        return pl.ds(pl.multiple_of(n * tc, tc), tc)

    def norm_acc(n, acc):
        kkr = ks[rows(n), :] * kkw_ref[n]
        return acc + kkr * kkr

    nsq = lax.fori_loop(0, n_ch, norm_acc, jnp.zeros((tc, LANES), F32), unroll=8)
    inv = 1.0 / jnp.maximum(jnp.sqrt(nsq), 1e-12)

    def prep(n, bacc):
        k = ks[rows(n), :]
        a = jax.nn.sigmoid(bvs[rows(n), :])
        kk = k * kkw_ref[n] * inv
        ws[rows(n), :] = jnp.exp(-DECAY_SCALE * jax.nn.sigmoid(ws[rows(n), :]))
        ans[rows(n), :] = -kk
        bvs[rows(n), :] = kk * a
        kmod = k * (1.0 + (a - 1.0) * kaw_ref[n])
        ks[rows(n), :] = kmod
        return bacc + rs[rows(n), :] * kmod * rkw_ref[n]

    bon[...] = lax.fori_loop(0, n_ch, prep, jnp.zeros((tc, LANES), F32), unroll=4)

    def bcast(ref, row):
        return ref[pl.ds(row, 1), :][None]

    def sa_first(j, acc):
        return acc + state_ref[j] * bcast(ans, j * tc)

    slab = (n_ch // SUBLANES, SUBLANES, LANES)
    sa0 = lax.fori_loop(0, n_ch, sa_first, jnp.zeros(slab, F32), unroll=4)

    def step(t, sa):
        v = vs[pl.ds(t, n_ch, stride=tc), :].reshape(slab)
        t_next = jnp.minimum(t + 1, tc - 1)

        out = jnp.zeros(slab, F32)
        sa_next = jnp.zeros(slab, F32)
        for j in range(n_ch):
            row = j * tc + t
            s_new = state_ref[j] * bcast(ws, row) + sa * bcast(bvs, row) + v * bcast(ks, row)
            state_ref[j] = s_new
            out = out + s_new * bcast(rs, row)
            sa_next = sa_next + s_new * bcast(ans, j * tc + t_next)
        o = out.reshape(n_ch, LANES)
        mu = jnp.mean(o, axis=0, keepdims=True)
        d = o - mu
        var = jnp.mean(d * d, axis=0, keepdims=True)
        on = d * lax.rsqrt(var + GN_EPS) * lnw_ref[...] + lnb_ref[...]
        bonus = bon[pl.ds(t, 1), :] * v.reshape(n_ch, LANES)
        os[pl.ds(t, n_ch, stride=tc), :] = on + bonus

        row0 = pl.multiple_of((t // steps_per_rowblock) * SUBLANES, SUBLANES)
        for src, dst in zip(srcs, nxt):
            for gi in range(groups_per_step):
                g = (t % steps_per_rowblock) * groups_per_step + gi
                lane0 = pl.multiple_of(g * LANES, LANES)
                y = _segment_transpose([src[b, pl.ds(row0, SUBLANES), pl.ds(lane0, LANES)] for b in range(SCAN_BATCH)])
                for n_lo in range(NGRP):
                    dst[pl.ds(pl.multiple_of((g * NGRP + n_lo) * tc + row0, SUBLANES), SUBLANES), :] = y[n_lo]
        return sa_next

    lax.fori_loop(0, tc, step, sa0)
    _from_lanes_gated(os, z_ref, o_ref, tc)


def rwkv_scan(proj_a, wa_pre, kkw, kaw, rkw, lnw, lnb, batch, seq, tc=32):
    width = HEAD_A * SCAN_HEADS
    n_chunks = seq // tc
    seq_in = lambda col: pl.BlockSpec((SCAN_BATCH, tc, width), lambda g, s: (g, jnp.minimum(s, n_chunks - 1), col))
    seq_out = lambda col: pl.BlockSpec((SCAN_BATCH, tc, width), lambda g, s: (g, jnp.maximum(s - 1, 0), col))
    par3 = pl.BlockSpec((HEAD_A, 1, LANES), lambda g, s: (0, 0, 0))
    par2 = pl.BlockSpec((HEAD_A, LANES), lambda g, s: (0, 0))
    buf = pltpu.VMEM((HEAD_A * tc, LANES), F32)
    return pl.pallas_call(
        functools.partial(_scan_kernel, tc=tc),
        grid=(batch // SCAN_BATCH, n_chunks + 1),
        in_specs=[seq_in(0), seq_in(1), seq_in(2), seq_in(0), seq_in(1), seq_out(3), par3, par3, par3, par2, par2],
        out_specs=seq_out(0),
        out_shape=jax.ShapeDtypeStruct((batch, seq, width), BF16),
        scratch_shapes=[pltpu.VMEM((HEAD_A, HEAD_A // SUBLANES, SUBLANES, LANES), F32), [buf] * 5, [buf] * 5,
                        buf, buf, pltpu.VMEM((tc, LANES), F32)],
        compiler_params=_params(ndim=2),
        name="rwkv_scan",
    )(proj_a, proj_a, proj_a, wa_pre, wa_pre, proj_a, kkw, kaw, rkw, lnw, lnb)


def _compress_kernel(x_ref, pe_ref, w1_ref, w2_ref, kg_ref, o_ref):
    which = pl.program_id(0)
    half = L_CMP // 2
    nblk = x_ref.shape[0] // CMP_STRIDE
    h1 = jnp.zeros((nblk, w1_ref.shape[-1]), F32)
    h2 = jnp.zeros((nblk, w1_ref.shape[-1]), F32)
    for l in range(half):
        x = x_ref[pl.ds(l, nblk, stride=CMP_STRIDE), :]
        a1 = (x + pe_ref[l:l + 1, :]).astype(BF16)
        a2 = (x + pe_ref[half + l:half + l + 1, :]).astype(BF16)
        h1 = h1 + jnp.dot(a1, w1_ref[l], preferred_element_type=F32)
        h2 = h2 + jnp.dot(a2, w1_ref[half + l], preferred_element_type=F32)
    hid = h1 + pltpu.roll(h2, nblk - 1, axis=0)
    hid = jax.nn.gelu(hid)
    out = jnp.dot(hid.astype(BF16), w2_ref[...], preferred_element_type=F32)
    ms = jnp.mean(out * out, axis=-1, keepdims=True)
    normed = out * lax.rsqrt(ms + NORM_EPS) * kg_ref[...]
    out = jnp.where(which == 0, normed, out)
    row = lax.broadcasted_iota(jnp.int32, out.shape, 0)
    o_ref[...] = jnp.where(row < nblk - 1, out, 0.0)


def compress(proj_b, kc_block, vc_block, pe, w1, w2, kgain, batch, seq):
    nblk = seq // CMP_STRIDE
    hidden = w1.shape[-1]

    def xmap(w, b, g):
        return (b, kc_block + w * (vc_block - kc_block) + g)

    return pl.pallas_call(
        _compress_kernel,
        grid=(2, batch, KV_GROUPS),
        in_specs=[
            pl.BlockSpec((seq, HEAD_B), xmap),
            pl.BlockSpec((None, L_CMP, HEAD_B), lambda w, b, g: (w, 0, 0)),
            pl.BlockSpec((None, L_CMP, HEAD_B, hidden), lambda w, b, g: (w, 0, 0, 0)),
            pl.BlockSpec((None, hidden, HEAD_B), lambda w, b, g: (w, 0, 0)),
            pl.BlockSpec((1, HEAD_B), lambda w, b, g: (0, 0)),
        ],
        out_specs=pl.BlockSpec((None, None, None, nblk, HEAD_B), lambda w, b, g: (w, b, g, 0, 0)),
        out_shape=jax.ShapeDtypeStruct((2, batch, KV_GROUPS, nblk, HEAD_B), F32),
        compiler_params=_params(ndim=3),
        name="nsa_compress",
    )(proj_b, pe, w1, w2, kgain)


def _rms(x, gain):
    ms = jnp.mean(x * x, axis=-1, keepdims=True)
    return x * lax.rsqrt(ms + NORM_EPS) * gain


def _dot_nt(a, b):
    return lax.dot_general(a, b, (((1,), (1,)), ((), ())), preferred_element_type=F32)


LOG2E = 1.4426950408889634
VAUG = 2 * HEAD_B


def _attn_kernel(q_ref, ks_ref, vs_ref, kw_ref, vw_ref, kc_ref, vc_ref, gt_ref, z_ref, qg_ref, kg_ref,
                 ovt_ref, ex_ref, wb_ref, o_ref, ksn_ref, vs1_ref, kwn_ref, vw1_ref, q_scr, m_scr, acc_scr,
                 oc_scr, ow_scr, *, seq):
    qt = pl.program_id(2)
    ncmp = kc_ref.shape[0]
    nsel = seq // L_SEL

    @pl.when(qt == 0)
    def _():
        ones_col = jnp.ones((seq, HEAD_B), BF16)
        ksn_ref[...] = _rms(ks_ref[...], kg_ref[1:2, :]).astype(BF16)
        vs1_ref[:, 0:HEAD_B] = vs_ref[...].astype(BF16)
        vs1_ref[:, HEAD_B:VAUG] = ones_col
        kwn_ref[0:WINDOW, :] = jnp.zeros((WINDOW, HEAD_B), BF16)
        vw1_ref[0:WINDOW, :] = jnp.zeros((WINDOW, VAUG), BF16)
        kwn_ref[WINDOW:WINDOW + seq, :] = _rms(kw_ref[...], kg_ref[2:3, :]).astype(BF16)
        vw1_ref[WINDOW:WINDOW + seq, 0:HEAD_B] = vw_ref[...].astype(BF16)
        vw1_ref[WINDOW:WINDOW + seq, HEAD_B:VAUG] = ones_col

    t0 = pl.multiple_of(qt * TQ, TQ)
    for r in range(REP):
        q_scr[r] = (_rms(q_ref[:, r * HEAD_B:(r + 1) * HEAD_B], qg_ref[...]) * (ATTN_SCALE * LOG2E)).astype(BF16)

    def t_of(shape):
        return t0 + lax.broadcasted_iota(jnp.int32, shape, 0)

    n_idx = lax.broadcasted_iota(jnp.int32, (TQ, ncmp), 1)
    bias_c = jnp.where(n_idx * CMP_STRIDE + (L_CMP - 1) <= t_of((TQ, ncmp)), 0.0, NEG_INF)
    bias_c = jnp.where(n_idx < ncmp - 1, bias_c, NEG_INF)
    row_ok = jnp.where(t_of((TQ, 1)) >= L_CMP - 1, 1.0, 0.0)
    q_all = q_scr[...].reshape(REP * TQ, HEAD_B)
    head = lambda x, r: x[r * TQ:(r + 1) * TQ]
    s_all = _dot_nt(q_all, kc_ref[...].astype(BF16))
    prs = []
    for r in range(REP):
        s = head(s_all, r) + bias_c
        p = jnp.exp2(s - jnp.max(s, axis=-1, keepdims=True))
        prs.append(p * (row_ok / jnp.sum(p, axis=-1, keepdims=True)))
    psum = prs[0] + prs[1] + prs[2] + prs[3]
    oc_scr[...] = jnp.dot(jnp.concatenate(prs, axis=0).astype(BF16), vc_ref[...].astype(BF16),
                          preferred_element_type=F32).reshape(REP, TQ, HEAD_B)
    imp_t = lax.dot_general(ovt_ref[...], psum, (((1,), (1,)), ((), ())), preferred_element_type=F32,
                            precision=lax.Precision.HIGHEST)

    j_idx = lax.broadcasted_iota(jnp.int32, (nsel, TQ), 0)
    t_sel = t0 + lax.broadcasted_iota(jnp.int32, (nsel, TQ), 1)
    cur = t_sel // L_SEL
    forced = jnp.where((j_idx == 0) | (j_idx == cur) | (j_idx == cur - 1), FORCE_BONUS, 0.0)
    score = jnp.where(j_idx * L_SEL <= t_sel, imp_t + forced, NEG_INF)
    rank = jnp.zeros((nsel, TQ), F32)
    for i in range(nsel):
        si = score[i:i + 1, :]
        ahead = (si > score) | ((si == score) & (j_idx > i))
        rank = rank + jnp.where(ahead, 1.0, 0.0)
    sel_t = jnp.where((rank < min(N_SEL, nsel)) & (score > 0.5 * NEG_INF), 1.0, 0.0)
    sel = jnp.concatenate([sel_t, jnp.zeros((LANES - nsel, TQ), F32)], axis=0).T
    not_sel = (1.0 - sel).astype(BF16)

    span = WINDOW + TQ
    kwin = kwn_ref[pl.ds(t0, span), :]
    vwin = vw1_ref[pl.ds(t0, span), :]
    lane_w = lax.broadcasted_iota(jnp.int32, (TQ, span), 1)
    bias_w = jnp.where(lane_w >= WINDOW - t0, wb_ref[...], NEG_INF)
    s_all = _dot_nt(q_all, kwin)
    ps = []
    for r in range(REP):
        s = head(s_all, r) + bias_w
        ps.append(jnp.exp2(s - jnp.max(s, axis=-1, keepdims=True)).astype(BF16))
    ow_scr[...] = jnp.dot(jnp.concatenate(ps, axis=0), vwin, preferred_element_type=F32)

    m_scr[...] = jnp.full(m_scr.shape, NEG_INF, F32)
    acc_scr[...] = jnp.zeros(acc_scr.shape, F32)

    def scores(c):
        k0 = pl.multiple_of(c * KC, KC)
        return _dot_nt(q_scr[...].reshape(REP * TQ, HEAD_B), ksn_ref[pl.ds(k0, KC), :])

    def sel_bias(c):
        return jnp.dot(not_sel, ex_ref[c], preferred_element_type=F32)

    def sel_chunk(c, s_all, bias, diagonal):
        k0 = pl.multiple_of(c * KC, KC)
        v1 = vs1_ref[pl.ds(k0, KC), :]
        if diagonal:
            lane = lax.broadcasted_iota(jnp.int32, (TQ, KC), 1)
            bias = jnp.where(k0 + lane <= t_of((TQ, KC)), bias, NEG_INF)
        ps, alphas = [], []
        for r in range(REP):
            s = head(s_all, r) + bias
            m_old = m_scr[r]
            m_new = jnp.maximum(m_old, jnp.max(s, axis=-1, keepdims=True))
            m_scr[r] = m_new
            ps.append(jnp.exp2(s - jnp.tile(m_new, (1, KC // LANES))).astype(BF16))
            alphas.append(jnp.exp2(m_old - m_new))
        pv = jnp.dot(jnp.concatenate(ps, axis=0), v1, preferred_element_type=F32)
        for r in range(REP):
            acc_scr[r] = jnp.tile(alphas[r], (1, VAUG // LANES)) * acc_scr[r] + head(pv, r)

    n_full = t0 // KC

    def full_chunk(c, carry):
        s_cur, bias_cur = carry
        nxt = (scores(c + 1), sel_bias(c + 1))
        sel_chunk(c, s_cur, bias_cur, False)
        return nxt

    s_last, bias_last = lax.fori_loop(0, n_full, full_chunk, (scores(0), sel_bias(0)))
    sel_chunk(n_full, s_last, bias_last, True)

    gts = jax.nn.sigmoid(gt_ref[...])
    for r in range(REP):
        ow = ow_scr[r * TQ:(r + 1) * TQ, :]
        o_w = ow[:, 0:HEAD_B] / ow[:, HEAD_B:VAUG]
        acc = acc_scr[r]
        o_s = acc[:, 0:HEAD_B] / acc[:, HEAD_B:VAUG]
        g0 = N_BRANCH * r
        o = gts[:, g0:g0 + 1] * oc_scr[r] + gts[:, g0 + 1:g0 + 2] * o_s + gts[:, g0 + 2:g0 + 3] * o_w
        z = z_ref[:, r * HEAD_B:(r + 1) * HEAD_B]
        o_ref[:, r * HEAD_B:(r + 1) * HEAD_B] = (o * (z * jax.nn.sigmoid(z))).astype(o_ref.dtype)


def nsa_attention(proj_b, cmp_kv, gates, q_gain, k_gain, overlap_t, expand, win_bias, blocks, batch, seq):
    nq = seq // TQ
    ncmp = seq // CMP_STRIDE
    gw = REP * HEAD_B // LANES
    qspec = lambda off: pl.BlockSpec((TQ, REP * HEAD_B), lambda b, g, t: (b * nq + t, off // gw + g))
    kvspec = lambda off: pl.BlockSpec((seq, HEAD_B), lambda b, g, t: (b, off + g))
    cspec = lambda w: pl.BlockSpec((None, None, None, ncmp, HEAD_B), lambda b, g, t: (w, b, g, 0, 0))
    full = lambda a: pl.BlockSpec(a.shape, lambda b, g, t: (0,) * a.ndim)
    return pl.pallas_call(
        functools.partial(_attn_kernel, seq=seq),
        grid=(batch, KV_GROUPS, nq),
        in_specs=[
            qspec(blocks["q"]), kvspec(blocks["ks"]), kvspec(blocks["vs"]), kvspec(blocks["kw"]), kvspec(blocks["vw"]),
            cspec(0), cspec(1),
            pl.BlockSpec((None, None, TQ, GATE_PAD), lambda b, g, t: (b, g, t, 0)),
            qspec(blocks["z"]),
            full(q_gain), full(k_gain), full(overlap_t), full(expand), full(win_bias),
        ],
        out_specs=pl.BlockSpec((TQ, REP * HEAD_B), lambda b, g, t: (b * nq + t, g)),
        out_shape=jax.ShapeDtypeStruct((batch * seq, KV_GROUPS * REP * HEAD_B), BF16),
        scratch_shapes=[
            pltpu.VMEM((seq, HEAD_B), BF16), pltpu.VMEM((seq, VAUG), BF16),
            pltpu.VMEM((seq + WINDOW, HEAD_B), BF16), pltpu.VMEM((seq + WINDOW, VAUG), BF16),
            pltpu.VMEM((REP, TQ, HEAD_B), BF16), pltpu.VMEM((REP, TQ, LANES), F32),
            pltpu.VMEM((REP, TQ, VAUG), F32), pltpu.VMEM((REP, TQ, HEAD_B), F32),
            pltpu.VMEM((REP * TQ, VAUG), F32),
        ],
        compiler_params=_params(ndim=3),
        name="nsa_attention",
    )(proj_b, proj_b, proj_b, proj_b, proj_b, cmp_kv, cmp_kv, gates, proj_b, q_gain, k_gain, overlap_t, expand,
      win_bias)


def _merge_kernel(ya_ref, yb_ref, wa_ref, wb_ref, ga_ref, gb_ref, out_ref):
    ua = jnp.dot(ya_ref[...], wa_ref[...], preferred_element_type=F32)
    ub = jnp.dot(yb_ref[...], wb_ref[...], preferred_element_type=F32)
    out_ref[...] = (jax.nn.sigmoid(ga_ref[...]) * ua + jax.nn.sigmoid(gb_ref[...]) * ub).astype(out_ref.dtype)


def merge(ya, yb, w_a, w_b, proj_b, gate_col, tm=1024, tn=512):
    m, ka = ya.shape
    n = w_a.shape[1]
    tm = min(tm, m)
    assert gate_col % tn == 0 and n % tn == 0
    ga_block, gb_block = gate_col // tn, (gate_col + n) // tn
    return pl.pallas_call(
        _merge_kernel,
        grid=(m // tm, n // tn),
        in_specs=[
            pl.BlockSpec((tm, ka), lambda i, j: (i, 0)),
            pl.BlockSpec((tm, ka), lambda i, j: (i, 0)),
            pl.BlockSpec((ka, tn), lambda i, j: (0, j)),
            pl.BlockSpec((ka, tn), lambda i, j: (0, j)),
            pl.BlockSpec((tm, tn), lambda i, j: (i, ga_block + j)),
            pl.BlockSpec((tm, tn), lambda i, j: (i, gb_block + j)),
        ],
        out_specs=pl.BlockSpec((tm, tn), lambda i, j: (i, j)),
        out_shape=jax.ShapeDtypeStruct((m, n), BF16),
        compiler_params=_params(ndim=2),
        name="merge",
    )(ya, yb, w_a, w_b, proj_b, proj_b)


def _ple_embed_kernel(p_ref, w_ref, g_ref, o_ref):
    acc = jnp.dot(p_ref[...].astype(BF16), w_ref[...], preferred_element_type=F32)
    o_ref[...] = _rms(acc, g_ref[...]).astype(o_ref.dtype)


def ple_embed(p, w, g, tm=256):
    m, k = p.shape
    n = w.shape[1]
    return pl.pallas_call(
        _ple_embed_kernel,
        grid=(m // tm,),
        in_specs=[pl.BlockSpec((tm, k), lambda i: (i, 0)), pl.BlockSpec((k, n), lambda i: (0, 0)),
                  pl.BlockSpec((1, n), lambda i: (0, 0))],
        out_specs=pl.BlockSpec((tm, n), lambda i: (i, 0)),
        out_shape=jax.ShapeDtypeStruct((m, n), F32),
        compiler_params=_params(),
        name="ple_embed",
    )(p, w, g.reshape(1, n))


def head_minor(w, heads):
    lead = w.shape[:-1]
    return w.reshape(*lead, heads, HEAD_A).swapaxes(-1, -2).reshape(*lead, heads * HEAD_A)


def rwkv_branch(proj_a, batch, seq, a_width, w_lora_up, w0, a_lora_up, a0, k_k, k_a, r_k, lnx_w, lnx_b):
    tokens = batch * seq
    heads_a = a_width // HEAD_A
    assert heads_a == SCAN_HEADS and batch % SCAN_BATCH == 0
    lora = w_lora_up.shape[0]
    zeros = jnp.zeros((lora, a_width), F32)
    hm = lambda w: head_minor(w, heads_a)
    w_lora = jnp.concatenate([jnp.concatenate([hm(w_lora_up), zeros], 1),
                              jnp.concatenate([zeros, hm(a_lora_up)], 1)], 0)
    b_lora = jnp.concatenate([hm(w0), hm(a0)]).reshape(1, -1)
    wa_pre = lora_project(proj_a, 4 * a_width // LANES, w_lora, b_lora)

    def par_scan(t):
        return jnp.tile(t.reshape(heads_a, HEAD_A).T, (1, SCAN_BATCH))

    par3 = lambda t: par_scan(t).reshape(HEAD_A, 1, LANES)
    o = rwkv_scan(proj_a.reshape(batch, seq, -1), wa_pre.reshape(batch, seq, -1),
                  par3(k_k), par3(k_a), par3(r_k), par_scan(lnx_w), par_scan(lnx_b), batch, seq)
    return o.reshape(tokens, a_width)


def nsa_branch(proj_b, graw, batch, seq, b_width, q_norm_g, k_norm_g, pe_cmp_k, pe_cmp_v,
               cmp_k_w1, cmp_k_w2, cmp_v_w1, cmp_v_w2):
    kv_width = KV_GROUPS * HEAD_B
    blk = lambda cols: cols // LANES
    blocks = {"q": 0, "kc": blk(b_width), "vc": blk(b_width + kv_width), "ks": blk(b_width + 2 * kv_width),
              "vs": blk(b_width + 3 * kv_width), "kw": blk(b_width + 4 * kv_width), "vw": blk(b_width + 5 * kv_width),
              "z": blk(b_width + 6 * kv_width)}
    pe = jnp.stack([pe_cmp_k, pe_cmp_v])
    hidden = cmp_k_w1.shape[1]
    w1 = jnp.stack([cmp_k_w1, cmp_v_w1]).reshape(2, L_CMP, HEAD_B, hidden).astype(BF16)
    w2 = jnp.stack([cmp_k_w2, cmp_v_w2]).astype(BF16)
    cmp_kv = compress(proj_b, blocks["kc"], blocks["vc"], pe, w1, w2, k_norm_g[0:1], batch, seq)

    ncmp = seq // CMP_STRIDE
    nsel = seq // L_SEL
    c_start = jnp.arange(ncmp) * CMP_STRIDE
    s_start = jnp.arange(nsel) * L_SEL
    overlap_t = ((c_start[None, :] < (s_start + L_SEL)[:, None]) & (s_start[:, None] < (c_start + L_CMP)[None, :])
                 & (jnp.arange(ncmp)[None, :] < ncmp - 1)).astype(F32)
    key_blk = jnp.arange(seq) // L_SEL
    expand = jnp.where(jnp.arange(LANES)[:, None] == key_blk[None, :], NEG_INF, 0.0).astype(BF16)
    expand = expand.reshape(LANES, seq // KC, KC).transpose(1, 0, 2)
    tl = jnp.arange(TQ)[:, None]
    u = jnp.arange(WINDOW + TQ)[None, :]
    win_bias = jnp.where((u > tl) & (u <= WINDOW + tl), 0.0, NEG_INF).astype(F32)
    gates = graw.reshape(batch, seq, KV_GROUPS, REP * N_BRANCH).transpose(0, 2, 1, 3)
    gates = jnp.pad(gates, ((0, 0), (0, 0), (0, 0), (0, GATE_PAD - REP * N_BRANCH)))
    return nsa_attention(proj_b, cmp_kv, gates, q_norm_g.reshape(1, HEAD_B), k_norm_g, overlap_t, expand, win_bias,
                         blocks, batch, seq)


def _layer(x, p, norm_g, w_in, shift_mu, w_lora_up, w0, a_lora_up, a0, k_k, k_a, r_k, lnx_w, lnx_b,
           q_norm_g, k_norm_g, pe_cmp_k, pe_cmp_v, cmp_k_w1, cmp_k_w2, cmp_v_w1, cmp_v_w2,
           w_up_a, w_up_b, w_out, ple_pre_g, w_ple_gate, w_ple, ple_post_g):
    batch, seq, d = x.shape
    tokens = batch * seq
    a_width = w_up_a.shape[0]
    b_width = w_up_b.shape[0]
    heads_a = a_width // HEAD_A
    kv_width = KV_GROUPS * HEAD_B
    n_gate = KV_GROUPS * REP * N_BRANCH
    lora = w_lora_up.shape[0]
    assert lora == HEAD_A and a_lora_up.shape[0] == HEAD_A and 2 * lora == LANES
    a_cols = 4 * a_width + 2 * lora
    g_off = a_cols + b_width + 6 * kv_width
    zb_off = g_off + n_gate
    assert w_in.shape[1] == zb_off + b_width + 2 * d

    pad_a = LANES - n_gate
    hm4 = lambda w: head_minor(w.reshape(*w.shape[:-1], 4, a_width), heads_a).reshape(*w.shape[:-1], 4 * a_width)
    w_a = jnp.concatenate([hm4(w_in[:, :4 * a_width]), w_in[:, 4 * a_width:a_cols], w_in[:, g_off:zb_off],
                           jnp.zeros((d, pad_a), F32)], axis=1).astype(BF16)
    mu_a = jnp.concatenate([hm4(shift_mu[:4 * a_width]), shift_mu[4 * a_width:], jnp.zeros((LANES,), F32)]).reshape(1, -1)
    w_b = jnp.concatenate([w_in[:, a_cols:g_off], w_in[:, zb_off:]], axis=1).astype(BF16)
    na = w_a.shape[1]
    tn_a = 768 if na % 768 == 0 else LANES

    x2 = x.reshape(tokens, d)
    h = rmsnorm_rows(x2, norm_g, BF16)
    tm = 512
    proj_a = matmul(h, w_a, mode="shift", extras=(mu_a,), tm=tm, tn=tn_a, tiles_per_seq=seq // tm, name="proj_a")
    proj_b = matmul(h, w_b, tm=tm, tn=1024, name="proj_b")

    ya = rwkv_branch(proj_a, batch, seq, a_width, w_lora_up, w0, a_lora_up, a0, k_k, k_a, r_k, lnx_w, lnx_b)
    graw = proj_a[:, a_cols:a_cols + n_gate]
    yb = nsa_branch(proj_b, graw, batch, seq, b_width, q_norm_g, k_norm_g, pe_cmp_k, pe_cmp_v,
                    cmp_k_w1, cmp_k_w2, cmp_v_w1, cmp_v_w2)

    w_up_a_nm = head_minor(w_up_a.T, heads_a).T
    merged = merge(ya, yb, w_up_a_nm.astype(BF16), w_up_b.astype(BF16), proj_b, 2 * b_width + 6 * kv_width)
    x1 = matmul(merged, w_out.astype(BF16), mode="resid", extras=(x2,), tm=tm, tn=1024, name="out_proj")
    hn = rmsnorm_rows(x1, ple_pre_g, BF16)
    e = ple_embed(p.reshape(tokens, -1), w_ple.astype(BF16), ple_post_g)
    out = matmul(hn, w_ple_gate.astype(BF16), mode="ple", extras=(x1, e), tm=tm, tn=1024, name="ple_gate")
    return out.reshape(batch, seq, d)


def kernel(x, p, norm_g, w_in, shift_mu, w_lora_up, w0, a_lora_up, a0, k_k, k_a, r_k, lnx_w, lnx_b, q_norm_g, k_norm_g, pe_cmp_k, pe_cmp_v, cmp_k_w1, cmp_k_w2, cmp_v_w1, cmp_v_w2, w_up_a, w_up_b, w_out, ple_pre_g, w_ple_gate, w_ple, ple_post_g):
    depth = w_in.shape[0]
    for i in range(depth):
        x = _layer(x, p[i], norm_g[i], w_in[i], shift_mu[i], w_lora_up[i], w0[i], a_lora_up[i], a0[i], k_k[i],
                   k_a[i], r_k[i], lnx_w[i], lnx_b[i], q_norm_g[i], k_norm_g[i], pe_cmp_k[i], pe_cmp_v[i],
                   cmp_k_w1[i], cmp_k_w2[i], cmp_v_w1[i], cmp_v_w2[i], w_up_a[i], w_up_b[i], w_out[i],
                   ple_pre_g[i], w_ple_gate[i], w_ple[i], ple_post_g[i])
    return x
```

```python
import functools

import jax
import jax.numpy as jnp
from jax import lax
from jax.experimental import pallas as pl
from jax.experimental.pallas import tpu as pltpu

F32 = jnp.float32
BF16 = jnp.bfloat16

LANES = 128
SUBLANES = 8
VMEM_LIMIT = 56 * 1024 * 1024

NORM_EPS = 1e-6
NEG_INF = -1e30
HEAD_A = 64
GN_EPS = 64e-5
HEAD_B = 128
KV_GROUPS = 4
REP = 4
N_BRANCH = 3
GATE_PAD = 16
L_CMP = 32
CMP_STRIDE = 16
L_SEL = 64
N_SEL = 16
WINDOW = 512
TQ = 128
KC = 256
FORCE_BONUS = 1e3
ATTN_SCALE = HEAD_B ** -0.5
DECAY_SCALE = 0.6065306597126334


def _params(vmem=VMEM_LIMIT, ndim=1):
    return pltpu.CompilerParams(dimension_semantics=("arbitrary",) * ndim, vmem_limit_bytes=vmem)


def _rms(x, gain):
    ms = jnp.mean(x * x, axis=-1, keepdims=True)
    return x * lax.rsqrt(ms + NORM_EPS) * gain


def _mm_kernel(*refs, mode, tiles_per_seq, norm_a):
    a_ref, b_ref = refs[0], refs[1]
    if norm_a:
        x = a_ref[...]
        acc = jnp.dot((x * refs[2][...]).astype(BF16), b_ref[...], preferred_element_type=F32)
        acc = acc * lax.rsqrt(jnp.mean(x * x, axis=-1, keepdims=True) + NORM_EPS)
        refs = refs[:2] + refs[3:]
    else:
        acc = jnp.dot(a_ref[...], b_ref[...], preferred_element_type=F32)
    if mode == "plain":
        o_ref = refs[2]
        o_ref[...] = acc.astype(o_ref.dtype)
    elif mode == "shift":
        mu_ref, o_ref, carry_ref = refs[2], refs[3], refs[4]
        i = pl.program_id(1)
        tm = acc.shape[0]
        first = (i % tiles_per_seq) == 0
        last_prev = jnp.where(first, 0.0, carry_ref[SUBLANES - 1:SUBLANES, :])
        rolled = pltpu.roll(acc, 1, axis=0)
        row = lax.broadcasted_iota(jnp.int32, acc.shape, 0)
        prev = jnp.where(row == 0, last_prev, rolled)
        carry_ref[...] = acc[tm - SUBLANES:tm, :]
        o_ref[...] = (acc + mu_ref[...] * (prev - acc)).astype(o_ref.dtype)
    elif mode == "resid":
        r_ref, o_ref = refs[2], refs[3]
        o_ref[...] = (r_ref[...] + acc).astype(o_ref.dtype)
    elif mode == "ple":
        x_ref, e_ref, o_ref = refs[2], refs[3], refs[4]
        o_ref[...] = (x_ref[...] + jax.nn.sigmoid(acc) * e_ref[...]).astype(o_ref.dtype)
    else:
        raise ValueError(mode)


def matmul(a, b, *, mode="plain", extras=(), norm_gain=None, out_dtype=F32, tm=512, tn=1024, tiles_per_seq=1,
           name="mm"):
    m, k = a.shape
    n = b.shape[1]
    assert m % tm == 0 and n % tn == 0, (m, n, tm, tn)
    in_specs = [pl.BlockSpec((tm, k), lambda j, i: (i, 0)), pl.BlockSpec((k, tn), lambda j, i: (0, j))]
    if norm_gain is not None:
        in_specs.append(pl.BlockSpec((1, k), lambda j, i: (0, 0)))
        extras = (norm_gain.reshape(1, k),) + tuple(extras)
    scratch = []
    if mode == "shift":
        in_specs.append(pl.BlockSpec((1, tn), lambda j, i: (0, j)))
        scratch.append(pltpu.VMEM((SUBLANES, tn), F32))
    elif mode == "resid":
        in_specs.append(pl.BlockSpec((tm, tn), lambda j, i: (i, j)))
    elif mode == "ple":
        in_specs += [pl.BlockSpec((tm, tn), lambda j, i: (i, j)), pl.BlockSpec((tm, tn), lambda j, i: (i, j))]
    return pl.pallas_call(
        functools.partial(_mm_kernel, mode=mode, tiles_per_seq=tiles_per_seq, norm_a=norm_gain is not None),
        grid=(n // tn, m // tm),
        in_specs=in_specs,
        out_specs=pl.BlockSpec((tm, tn), lambda j, i: (i, j)),
        out_shape=jax.ShapeDtypeStruct((m, n), out_dtype),
        scratch_shapes=scratch,
        compiler_params=_params(ndim=2),
        name=name,
    )(a, b, *extras)


def _lora_kernel(x_ref, w_ref, b_ref, o_ref):
    x = x_ref[...]
    lane = lax.broadcasted_iota(jnp.int32, x.shape, 1)
    x = jnp.where(lane < HEAD_A, jnp.tanh(x), x)
    acc = jnp.dot(x, w_ref[...], preferred_element_type=F32, precision=lax.Precision.HIGHEST)
    o_ref[...] = acc + b_ref[...]


def lora_project(proj_a, col_block, w_blockdiag, bias, tm=1024, tn=1024):
    m = proj_a.shape[0]
    n = w_blockdiag.shape[1]
    tm = min(tm, m)
    assert m % tm == 0 and n % tn == 0
    return pl.pallas_call(
        _lora_kernel,
        grid=(m // tm, n // tn),
        in_specs=[
            pl.BlockSpec((tm, LANES), lambda i, j: (i, col_block)),
            pl.BlockSpec((LANES, tn), lambda i, j: (0, j)),
            pl.BlockSpec((1, tn), lambda i, j: (0, j)),
        ],
        out_specs=pl.BlockSpec((tm, tn), lambda i, j: (i, j)),
        out_shape=jax.ShapeDtypeStruct((m, n), F32),
        compiler_params=_params(ndim=2),
        name="lora",
    )(proj_a, w_blockdiag, bias)


SCAN_BATCH = 4
SCAN_HEADS = LANES // SCAN_BATCH
NGRP = LANES // SCAN_HEADS


def _segment_transpose(x):
    seg = lax.broadcasted_iota(jnp.int32, x[0].shape, 1) // SCAN_HEADS
    y = []
    for i in range(NGRP):
        out = None
        for j in range(SCAN_BATCH):
            shift = ((j - i) % NGRP) * SCAN_HEADS
            piece = x[j] if shift == 0 else pltpu.roll(x[j], shift, axis=1)
            out = piece if out is None else jnp.where(seg == j, piece, out)
        y.append(out)
    return y


def _to_lanes(x_ref, dst_ref, tc):
    for g in range(HEAD_A // NGRP):
        y = _segment_transpose([x_ref[b, :, g * LANES:(g + 1) * LANES] for b in range(SCAN_BATCH)])
        for n_lo in range(NGRP):
            n = g * NGRP + n_lo
            dst_ref[n * tc:(n + 1) * tc, :] = y[n_lo]


def _from_lanes_gated(src_ref, z_ref, o_ref, tc):
    for g in range(HEAD_A // NGRP):
        y = _segment_transpose([src_ref[(g * NGRP + n_lo) * tc:(g * NGRP + n_lo + 1) * tc, :] for n_lo in range(NGRP)])
        for b in range(SCAN_BATCH):
            z = z_ref[b, :, g * LANES:(g + 1) * LANES]
            o_ref[b, :, g * LANES:(g + 1) * LANES] = (y[b] * (z * jax.nn.sigmoid(z))).astype(o_ref.dtype)


def _scan_kernel(r_ref, k_ref, v_ref, wp_ref, ap_ref, z_ref, kkw_ref, kaw_ref, rkw_ref, lnw_ref, lnb_ref,
                 o_ref, state_ref, set_a, set_b, ans, os, bon, *, tc):
    s = pl.program_id(1)
    srcs = (r_ref, k_ref, v_ref, wp_ref, ap_ref)

    @pl.when(s == 0)
    def _():
        state_ref[...] = jnp.zeros_like(state_ref)
        for src, dst in zip(srcs, set_a):
            _to_lanes(src, dst, tc)

    @pl.when(s % 2 == 1)
    def _():
        _scan_chunk(set_a, set_b, srcs, z_ref, kkw_ref, kaw_ref, rkw_ref, lnw_ref, lnb_ref, o_ref, state_ref,
                    ans, os, bon, tc)

    @pl.when((s % 2 == 0) & (s > 0))
    def _():
        _scan_chunk(set_b, set_a, srcs, z_ref, kkw_ref, kaw_ref, rkw_ref, lnw_ref, lnb_ref, o_ref, state_ref,
                    ans, os, bon, tc)


def _scan_chunk(cur, nxt, srcs, z_ref, kkw_ref, kaw_ref, rkw_ref, lnw_ref, lnb_ref, o_ref, state_ref,
                ans, os, bon, tc):
    n_ch = HEAD_A
    rs, ks, vs, ws, bvs = cur
    n_groups = HEAD_A // NGRP
    groups_per_step = 2
    steps_per_rowblock = n_groups // groups_per_step
    assert tc * groups_per_step == n_groups * (tc // SUBLANES)

    def rows(n):
        return pl.ds(pl.multiple_of(n * tc, tc), tc)

    def norm_acc(n, acc):
        kkr = ks[rows(n), :] * kkw_ref[n]
        return acc + kkr * kkr

    nsq = lax.fori_loop(0, n_ch, norm_acc, jnp.zeros((tc, LANES), F32), unroll=8)
    inv = 1.0 / jnp.maximum(jnp.sqrt(nsq), 1e-12)

    def prep(n, bacc):
        k = ks[rows(n), :]
        a = jax.nn.sigmoid(bvs[rows(n), :])
        kk = k * kkw_ref[n] * inv
        ws[rows(n), :] = jnp.exp(-DECAY_SCALE * jax.nn.sigmoid(ws[rows(n), :]))
        ans[rows(n), :] = -kk
        bvs[rows(n), :] = kk * a
        kmod = k * (1.0 + (a - 1.0) * kaw_ref[n])
        ks[rows(n), :] = kmod
        return bacc + rs[rows(n), :] * kmod * rkw_ref[n]

    bon[...] = lax.fori_loop(0, n_ch, prep, jnp.zeros((tc, LANES), F32), unroll=4)

    def bcast(ref, row):
        return ref[pl.ds(row, 1), :][None]

    def sa_first(j, acc):
        return acc + state_ref[j] * bcast(ans, j * tc)

    slab = (n_ch // SUBLANES, SUBLANES, LANES)
    sa0 = lax.fori_loop(0, n_ch, sa_first, jnp.zeros(slab, F32), unroll=4)

    def step(t, sa):
        v = vs[pl.ds(t, n_ch, stride=tc), :].reshape(slab)
        t_next = jnp.minimum(t + 1, tc - 1)

        out = jnp.zeros(slab, F32)
        sa_next = jnp.zeros(slab, F32)
        for j in range(n_ch):
            row = j * tc + t
            s_new = state_ref[j] * bcast(ws, row) + sa * bcast(bvs, row) + v * bcast(ks, row)
            state_ref[j] = s_new
            out = out + s_new * bcast(rs, row)
            sa_next = sa_next + s_new * bcast(ans, j * tc + t_next)
        o = out.reshape(n_ch, LANES)
        mu = jnp.mean(o, axis=0, keepdims=True)
        d = o - mu
        var = jnp.mean(d * d, axis=0, keepdims=True)
        on = d * lax.rsqrt(var + GN_EPS) * lnw_ref[...] + lnb_ref[...]
        bonus = bon[pl.ds(t, 1), :] * v.reshape(n_ch, LANES)
        os[pl.ds(t, n_ch, stride=tc), :] = on + bonus

        row0 = pl.multiple_of((t // steps_per_rowblock) * SUBLANES, SUBLANES)
        for src, dst in zip(srcs, nxt):
            for gi in range(groups_per_step):
                g = (t % steps_per_rowblock) * groups_per_step + gi
                lane0 = pl.multiple_of(g * LANES, LANES)
                y = _segment_transpose([src[b, pl.ds(row0, SUBLANES), pl.ds(lane0, LANES)] for b in range(SCAN_BATCH)])
                for n_lo in range(NGRP):
                    dst[pl.ds(pl.multiple_of((g * NGRP + n_lo) * tc + row0, SUBLANES), SUBLANES), :] = y[n_lo]
        return sa_next

    lax.fori_loop(0, tc, step, sa0)
    _from_lanes_gated(os, z_ref, o_ref, tc)


def rwkv_scan(proj_a, wa_pre, kkw, kaw, rkw, lnw, lnb, batch, seq, tc=32):
    width = HEAD_A * SCAN_HEADS
    n_chunks = seq // tc
    seq_in = lambda col: pl.BlockSpec((SCAN_BATCH, tc, width), lambda g, s: (g, jnp.minimum(s, n_chunks - 1), col))
    seq_out = lambda col: pl.BlockSpec((SCAN_BATCH, tc, width), lambda g, s: (g, jnp.maximum(s - 1, 0), col))
    par3 = pl.BlockSpec((HEAD_A, 1, LANES), lambda g, s: (0, 0, 0))
    par2 = pl.BlockSpec((HEAD_A, LANES), lambda g, s: (0, 0))
    buf = pltpu.VMEM((HEAD_A * tc, LANES), F32)
    return pl.pallas_call(
        functools.partial(_scan_kernel, tc=tc),
        grid=(batch // SCAN_BATCH, n_chunks + 1),
        in_specs=[seq_in(0), seq_in(1), seq_in(2), seq_in(0), seq_in(1), seq_out(3), par3, par3, par3, par2, par2],
        out_specs=seq_out(0),
        out_shape=jax.ShapeDtypeStruct((batch, seq, width), BF16),
        scratch_shapes=[pltpu.VMEM((HEAD_A, HEAD_A // SUBLANES, SUBLANES, LANES), F32), [buf] * 5, [buf] * 5,
                        buf, buf, pltpu.VMEM((tc, LANES), F32)],
        compiler_params=_params(ndim=2),
        name="rwkv_scan",
    )(proj_a, proj_a, proj_a, wa_pre, wa_pre, proj_a, kkw, kaw, rkw, lnw, lnb)


def _compress_kernel(x_ref, pe_ref, w1_ref, w2_ref, kg_ref, o_ref):
    which = pl.program_id(0)
    half = L_CMP // 2
    nblk = x_ref.shape[0] // CMP_STRIDE
    h1 = jnp.zeros((nblk, w1_ref.shape[-1]), F32)
    h2 = jnp.zeros((nblk, w1_ref.shape[-1]), F32)
    for l in range(half):
        x = x_ref[pl.ds(l, nblk, stride=CMP_STRIDE), :]
        a1 = (x + pe_ref[l:l + 1, :]).astype(BF16)
        a2 = (x + pe_ref[half + l:half + l + 1, :]).astype(BF16)
        h1 = h1 + jnp.dot(a1, w1_ref[l], preferred_element_type=F32)
        h2 = h2 + jnp.dot(a2, w1_ref[half + l], preferred_element_type=F32)
    hid = h1 + pltpu.roll(h2, nblk - 1, axis=0)
    hid = jax.nn.gelu(hid)
    out = jnp.dot(hid.astype(BF16), w2_ref[...], preferred_element_type=F32)
    ms = jnp.mean(out * out, axis=-1, keepdims=True)
    normed = out * lax.rsqrt(ms + NORM_EPS) * kg_ref[...]
    out = jnp.where(which == 0, normed, out)
    row = lax.broadcasted_iota(jnp.int32, out.shape, 0)
    o_ref[...] = jnp.where(row < nblk - 1, out, 0.0)


def compress(proj_b, kc_block, vc_block, pe, w1, w2, kgain, batch, seq):
    nblk = seq // CMP_STRIDE
    hidden = w1.shape[-1]

    def xmap(w, b, g):
        return (b, kc_block + w * (vc_block - kc_block) + g)

    return pl.pallas_call(
        _compress_kernel,
        grid=(2, batch, KV_GROUPS),
        in_specs=[
            pl.BlockSpec((seq, HEAD_B), xmap),
            pl.BlockSpec((None, L_CMP, HEAD_B), lambda w, b, g: (w, 0, 0)),
            pl.BlockSpec((None, L_CMP, HEAD_B, hidden), lambda w, b, g: (w, 0, 0, 0)),
            pl.BlockSpec((None, hidden, HEAD_B), lambda w, b, g: (w, 0, 0)),
            pl.BlockSpec((1, HEAD_B), lambda w, b, g: (0, 0)),
        ],
        out_specs=pl.BlockSpec((None, None, None, nblk, HEAD_B), lambda w, b, g: (w, b, g, 0, 0)),
        out_shape=jax.ShapeDtypeStruct((2, batch, KV_GROUPS, nblk, HEAD_B), F32),
        compiler_params=_params(ndim=3),
        name="nsa_compress",
    )(proj_b, pe, w1, w2, kgain)


def _dot_nt(a, b):
    return lax.dot_general(a, b, (((1,), (1,)), ((), ())), preferred_element_type=F32)


LOG2E = 1.4426950408889634
VAUG = 2 * HEAD_B


def _attn_kernel(q_ref, ks_ref, vs_ref, kw_ref, vw_ref, kc_ref, vc_ref, gt_ref, z_ref, qg_ref, kg_ref,
                 ovt_ref, ex_ref, wb_ref, o_ref, ksn_ref, vs1_ref, kwn_ref, vw1_ref, q_scr, m_scr, acc_scr,
                 oc_scr, ow_scr, *, seq):
    qt = pl.program_id(2)
    ncmp = kc_ref.shape[0]
    nsel = seq // L_SEL

    @pl.when(qt == 0)
    def _():
        ones_col = jnp.ones((seq, HEAD_B), BF16)
        ksn_ref[...] = _rms(ks_ref[...], kg_ref[1:2, :]).astype(BF16)
        vs1_ref[:, 0:HEAD_B] = vs_ref[...].astype(BF16)
        vs1_ref[:, HEAD_B:VAUG] = ones_col
        kwn_ref[0:WINDOW, :] = jnp.zeros((WINDOW, HEAD_B), BF16)
        vw1_ref[0:WINDOW, :] = jnp.zeros((WINDOW, VAUG), BF16)
        kwn_ref[WINDOW:WINDOW + seq, :] = _rms(kw_ref[...], kg_ref[2:3, :]).astype(BF16)
        vw1_ref[WINDOW:WINDOW + seq, 0:HEAD_B] = vw_ref[...].astype(BF16)
        vw1_ref[WINDOW:WINDOW + seq, HEAD_B:VAUG] = ones_col

    t0 = pl.multiple_of(qt * TQ, TQ)
    for r in range(REP):
        q_scr[r] = (_rms(q_ref[:, r * HEAD_B:(r + 1) * HEAD_B], qg_ref[...]) * (ATTN_SCALE * LOG2E)).astype(BF16)

    def t_of(shape):
        return t0 + lax.broadcasted_iota(jnp.int32, shape, 0)

    n_idx = lax.broadcasted_iota(jnp.int32, (TQ, ncmp), 1)
    bias_c = jnp.where(n_idx * CMP_STRIDE + (L_CMP - 1) <= t_of((TQ, ncmp)), 0.0, NEG_INF)
    bias_c = jnp.where(n_idx < ncmp - 1, bias_c, NEG_INF)
    row_ok = jnp.where(t_of((TQ, 1)) >= L_CMP - 1, 1.0, 0.0)
    q_all = q_scr[...].reshape(REP * TQ, HEAD_B)
    head = lambda x, r: x[r * TQ:(r + 1) * TQ]
    s_all = _dot_nt(q_all, kc_ref[...].astype(BF16))
    prs = []
    for r in range(REP):
        s = head(s_all, r) + bias_c
        p = jnp.exp2(s - jnp.max(s, axis=-1, keepdims=True))
        prs.append(p * (row_ok / jnp.sum(p, axis=-1, keepdims=True)))
    psum = prs[0] + prs[1] + prs[2] + prs[3]
    oc_scr[...] = jnp.dot(jnp.concatenate(prs, axis=0).astype(BF16), vc_ref[...].astype(BF16),
                          preferred_element_type=F32).reshape(REP, TQ, HEAD_B)
    imp_t = lax.dot_general(ovt_ref[...], psum, (((1,), (1,)), ((), ())), preferred_element_type=F32,
                            precision=lax.Precision.HIGHEST)

    j_idx = lax.broadcasted_iota(jnp.int32, (nsel, TQ), 0)
    t_sel = t0 + lax.broadcasted_iota(jnp.int32, (nsel, TQ), 1)
    cur = t_sel // L_SEL
    forced = jnp.where((j_idx == 0) | (j_idx == cur) | (j_idx == cur - 1), FORCE_BONUS, 0.0)
    score = jnp.where(j_idx * L_SEL <= t_sel, imp_t + forced, NEG_INF)
    rank = jnp.zeros((nsel, TQ), F32)
    for i in range(nsel):
        si = score[i:i + 1, :]
        ahead = (si > score) | ((si == score) & (j_idx > i))
        rank = rank + jnp.where(ahead, 1.0, 0.0)
    sel_t = jnp.where((rank < min(N_SEL, nsel)) & (score > 0.5 * NEG_INF), 1.0, 0.0)
    sel = jnp.concatenate([sel_t, jnp.zeros((LANES - nsel, TQ), F32)], axis=0).T
    not_sel = (1.0 - sel).astype(BF16)

    span = WINDOW + TQ
    kwin = kwn_ref[pl.ds(t0, span), :]
    vwin = vw1_ref[pl.ds(t0, span), :]
    lane_w = lax.broadcasted_iota(jnp.int32, (TQ, span), 1)
    bias_w = jnp.where(lane_w >= WINDOW - t0, wb_ref[...], NEG_INF)
    s_all = _dot_nt(q_all, kwin)
    ps = []
    for r in range(REP):
        s = head(s_all, r) + bias_w
        ps.append(jnp.exp2(s - jnp.max(s, axis=-1, keepdims=True)).astype(BF16))
    ow_scr[...] = jnp.dot(jnp.concatenate(ps, axis=0), vwin, preferred_element_type=F32)

    m_scr[...] = jnp.full(m_scr.shape, NEG_INF, F32)
    acc_scr[...] = jnp.zeros(acc_scr.shape, F32)

    def scores(c):
        k0 = pl.multiple_of(c * KC, KC)
        return _dot_nt(q_scr[...].reshape(REP * TQ, HEAD_B), ksn_ref[pl.ds(k0, KC), :])

    def sel_bias(c):
        return jnp.dot(not_sel, ex_ref[c], preferred_element_type=F32)

    def sel_chunk(c, s_all, bias, diagonal):
        k0 = pl.multiple_of(c * KC, KC)
        v1 = vs1_ref[pl.ds(k0, KC), :]
        if diagonal:
            lane = lax.broadcasted_iota(jnp.int32, (TQ, KC), 1)
            bias = jnp.where(k0 + lane <= t_of((TQ, KC)), bias, NEG_INF)
        ps, alphas = [], []
        for r in range(REP):
            s = head(s_all, r) + bias
            m_old = m_scr[r]
            m_new = jnp.maximum(m_old, jnp.max(s, axis=-1, keepdims=True))
            m_scr[r] = m_new
            ps.append(jnp.exp2(s - jnp.tile(m_new, (1, KC // LANES))).astype(BF16))
            alphas.append(jnp.exp2(m_old - m_new))
        pv = jnp.dot(jnp.concatenate(ps, axis=0), v1, preferred_element_type=F32)
        for r in range(REP):
            acc_scr[r] = jnp.tile(alphas[r], (1, VAUG // LANES)) * acc_scr[r] + head(pv, r)

    n_full = t0 // KC

    def full_chunk(c, carry):
        s_cur, bias_cur = carry
        nxt = (scores(c + 1), sel_bias(c + 1))
        sel_chunk(c, s_cur, bias_cur, False)
        return nxt

    s_last, bias_last = lax.fori_loop(0, n_full, full_chunk, (scores(0), sel_bias(0)))
    sel_chunk(n_full, s_last, bias_last, True)

    gts = jax.nn.sigmoid(gt_ref[...])
    for r in range(REP):
        ow = ow_scr[r * TQ:(r + 1) * TQ, :]
        o_w = ow[:, 0:HEAD_B] / ow[:, HEAD_B:VAUG]
        acc = acc_scr[r]
        o_s = acc[:, 0:HEAD_B] / acc[:, HEAD_B:VAUG]
        g0 = N_BRANCH * r
        o = gts[:, g0:g0 + 1] * oc_scr[r] + gts[:, g0 + 1:g0 + 2] * o_s + gts[:, g0 + 2:g0 + 3] * o_w
        z = z_ref[:, r * HEAD_B:(r + 1) * HEAD_B]
        o_ref[:, r * HEAD_B:(r + 1) * HEAD_B] = (o * (z * jax.nn.sigmoid(z))).astype(o_ref.dtype)


def nsa_attention(proj_b, cmp_kv, gates, q_gain, k_gain, overlap_t, expand, win_bias, blocks, batch, seq):
    nq = seq // TQ
    ncmp = seq // CMP_STRIDE
    gw = REP * HEAD_B // LANES
    qspec = lambda off: pl.BlockSpec((TQ, REP * HEAD_B), lambda b, g, t: (b * nq + t, off // gw + g))
    kvspec = lambda off: pl.BlockSpec((seq, HEAD_B), lambda b, g, t: (b, off + g))
    cspec = lambda w: pl.BlockSpec((None, None, None, ncmp, HEAD_B), lambda b, g, t: (w, b, g, 0, 0))
    full = lambda a: pl.BlockSpec(a.shape, lambda b, g, t: (0,) * a.ndim)
    return pl.pallas_call(
        functools.partial(_attn_kernel, seq=seq),
        grid=(batch, KV_GROUPS, nq),
        in_specs=[
            qspec(blocks["q"]), kvspec(blocks["ks"]), kvspec(blocks["vs"]), kvspec(blocks["kw"]), kvspec(blocks["vw"]),
            cspec(0), cspec(1),
            pl.BlockSpec((None, None, TQ, GATE_PAD), lambda b, g, t: (b, g, t, 0)),
            qspec(blocks["z"]),
            full(q_gain), full(k_gain), full(overlap_t), full(expand), full(win_bias),
        ],
        out_specs=pl.BlockSpec((TQ, REP * HEAD_B), lambda b, g, t: (b * nq + t, g)),
        out_shape=jax.ShapeDtypeStruct((batch * seq, KV_GROUPS * REP * HEAD_B), BF16),
        scratch_shapes=[
            pltpu.VMEM((seq, HEAD_B), BF16), pltpu.VMEM((seq, VAUG), BF16),
            pltpu.VMEM((seq + WINDOW, HEAD_B), BF16), pltpu.VMEM((seq + WINDOW, VAUG), BF16),
            pltpu.VMEM((REP, TQ, HEAD_B), BF16), pltpu.VMEM((REP, TQ, LANES), F32),
            pltpu.VMEM((REP, TQ, VAUG), F32), pltpu.VMEM((REP, TQ, HEAD_B), F32),
            pltpu.VMEM((REP * TQ, VAUG), F32),
        ],
        compiler_params=_params(ndim=3),
        name="nsa_attention",
    )(proj_b, proj_b, proj_b, proj_b, proj_b, cmp_kv, cmp_kv, gates, proj_b, q_gain, k_gain, overlap_t, expand,
      win_bias)


def _merge_kernel(ya_ref, yb_ref, wa_ref, wb_ref, ga_ref, gb_ref, out_ref):
    ua = jnp.dot(ya_ref[...], wa_ref[...], preferred_element_type=F32)
    ub = jnp.dot(yb_ref[...], wb_ref[...], preferred_element_type=F32)
    out_ref[...] = (jax.nn.sigmoid(ga_ref[...]) * ua + jax.nn.sigmoid(gb_ref[...]) * ub).astype(out_ref.dtype)


def merge(ya, yb, w_a, w_b, proj_b, gate_col, tm=1024, tn=512):
    m, ka = ya.shape
    n = w_a.shape[1]
    tm = min(tm, m)
    assert gate_col % tn == 0 and n % tn == 0
    ga_block, gb_block = gate_col // tn, (gate_col + n) // tn
    return pl.pallas_call(
        _merge_kernel,
        grid=(m // tm, n // tn),
        in_specs=[
            pl.BlockSpec((tm, ka), lambda i, j: (i, 0)),
            pl.BlockSpec((tm, ka), lambda i, j: (i, 0)),
            pl.BlockSpec((ka, tn), lambda i, j: (0, j)),
            pl.BlockSpec((ka, tn), lambda i, j: (0, j)),
            pl.BlockSpec((tm, tn), lambda i, j: (i, ga_block + j)),
            pl.BlockSpec((tm, tn), lambda i, j: (i, gb_block + j)),
        ],
        out_specs=pl.BlockSpec((tm, tn), lambda i, j: (i, j)),
        out_shape=jax.ShapeDtypeStruct((m, n), BF16),
        compiler_params=_params(ndim=2),
        name="merge",
    )(ya, yb, w_a, w_b, proj_b, proj_b)


def _ple_embed_kernel(p_ref, w_ref, g_ref, o_ref):
    acc = jnp.dot(p_ref[...].astype(BF16), w_ref[...], preferred_element_type=F32)
    o_ref[...] = _rms(acc, g_ref[...]).astype(o_ref.dtype)


def ple_embed(p, w, g, tm=256):
    m, k = p.shape
    n = w.shape[1]
    return pl.pallas_call(
        _ple_embed_kernel,
        grid=(m // tm,),
        in_specs=[pl.BlockSpec((tm, k), lambda i: (i, 0)), pl.BlockSpec((k, n), lambda i: (0, 0)),
                  pl.BlockSpec((1, n), lambda i: (0, 0))],
        out_specs=pl.BlockSpec((tm, n), lambda i: (i, 0)),
        out_shape=jax.ShapeDtypeStruct((m, n), F32),
        compiler_params=_params(),
        name="ple_embed",
    )(p, w, g.reshape(1, n))


def head_minor(w, heads):
    lead = w.shape[:-1]
    return w.reshape(*lead, heads, HEAD_A).swapaxes(-1, -2).reshape(*lead, heads * HEAD_A)


def rwkv_branch(proj_a, batch, seq, a_width, w_lora_up, w0, a_lora_up, a0, k_k, k_a, r_k, lnx_w, lnx_b):
    tokens = batch * seq
    heads_a = a_width // HEAD_A
    assert heads_a == SCAN_HEADS and batch % SCAN_BATCH == 0
    lora = w_lora_up.shape[0]
    zeros = jnp.zeros((lora, a_width), F32)
    hm = lambda w: head_minor(w, heads_a)
    w_lora = jnp.concatenate([jnp.concatenate([hm(w_lora_up), zeros], 1),
                              jnp.concatenate([zeros, hm(a_lora_up)], 1)], 0)
    b_lora = jnp.concatenate([hm(w0), hm(a0)]).reshape(1, -1)
    wa_pre = lora_project(proj_a, 4 * a_width // LANES, w_lora, b_lora)

    def par_scan(t):
        return jnp.tile(t.reshape(heads_a, HEAD_A).T, (1, SCAN_BATCH))

    par3 = lambda t: par_scan(t).reshape(HEAD_A, 1, LANES)
    o = rwkv_scan(proj_a.reshape(batch, seq, -1), wa_pre.reshape(batch, seq, -1),
                  par3(k_k), par3(k_a), par3(r_k), par_scan(lnx_w), par_scan(lnx_b), batch, seq)
    return o.reshape(tokens, a_width)


def nsa_branch(proj_b, graw, batch, seq, b_width, q_norm_g, k_norm_g, pe_cmp_k, pe_cmp_v,
               cmp_k_w1, cmp_k_w2, cmp_v_w1, cmp_v_w2):
    kv_width = KV_GROUPS * HEAD_B
    blk = lambda cols: cols // LANES
    blocks = {"q": 0, "kc": blk(b_width), "vc": blk(b_width + kv_width), "ks": blk(b_width + 2 * kv_width),
              "vs": blk(b_width + 3 * kv_width), "kw": blk(b_width + 4 * kv_width), "vw": blk(b_width + 5 * kv_width),
              "z": blk(b_width + 6 * kv_width)}
    pe = jnp.stack([pe_cmp_k, pe_cmp_v])
    hidden = cmp_k_w1.shape[1]
    w1 = jnp.stack([cmp_k_w1, cmp_v_w1]).reshape(2, L_CMP, HEAD_B, hidden).astype(BF16)
    w2 = jnp.stack([cmp_k_w2, cmp_v_w2]).astype(BF16)
    cmp_kv = compress(proj_b, blocks["kc"], blocks["vc"], pe, w1, w2, k_norm_g[0:1], batch, seq)

    ncmp = seq // CMP_STRIDE
    nsel = seq // L_SEL
    c_start = jnp.arange(ncmp) * CMP_STRIDE
    s_start = jnp.arange(nsel) * L_SEL
    overlap_t = ((c_start[None, :] < (s_start + L_SEL)[:, None]) & (s_start[:, None] < (c_start + L_CMP)[None, :])
                 & (jnp.arange(ncmp)[None, :] < ncmp - 1)).astype(F32)
    key_blk = jnp.arange(seq) // L_SEL
    expand = jnp.where(jnp.arange(LANES)[:, None] == key_blk[None, :], NEG_INF, 0.0).astype(BF16)
    expand = expand.reshape(LANES, seq // KC, KC).transpose(1, 0, 2)
    tl = jnp.arange(TQ)[:, None]
    u = jnp.arange(WINDOW + TQ)[None, :]
    win_bias = jnp.where((u > tl) & (u <= WINDOW + tl), 0.0, NEG_INF).astype(F32)
    gates = graw.reshape(batch, seq, KV_GROUPS, REP * N_BRANCH).transpose(0, 2, 1, 3)
    gates = jnp.pad(gates, ((0, 0), (0, 0), (0, 0), (0, GATE_PAD - REP * N_BRANCH)))
    return nsa_attention(proj_b, cmp_kv, gates, q_norm_g.reshape(1, HEAD_B), k_norm_g, overlap_t, expand, win_bias,
                         blocks, batch, seq)


def _layer(x, p, norm_g, w_in, shift_mu, w_lora_up, w0, a_lora_up, a0, k_k, k_a, r_k, lnx_w, lnx_b,
           q_norm_g, k_norm_g, pe_cmp_k, pe_cmp_v, cmp_k_w1, cmp_k_w2, cmp_v_w1, cmp_v_w2,
           w_up_a, w_up_b, w_out, ple_pre_g, w_ple_gate, w_ple, ple_post_g):
    batch, seq, d = x.shape
    tokens = batch * seq
    a_width = w_up_a.shape[0]
    b_width = w_up_b.shape[0]
    heads_a = a_width // HEAD_A
    kv_width = KV_GROUPS * HEAD_B
    n_gate = KV_GROUPS * REP * N_BRANCH
    lora = w_lora_up.shape[0]
    assert lora == HEAD_A and a_lora_up.shape[0] == HEAD_A and 2 * lora == LANES
    a_cols = 4 * a_width + 2 * lora
    g_off = a_cols + b_width + 6 * kv_width
    zb_off = g_off + n_gate
    assert w_in.shape[1] == zb_off + b_width + 2 * d

    pad_a = LANES - n_gate
    hm4 = lambda w: head_minor(w.reshape(*w.shape[:-1], 4, a_width), heads_a).reshape(*w.shape[:-1], 4 * a_width)
    w_a = jnp.concatenate([hm4(w_in[:, :4 * a_width]), w_in[:, 4 * a_width:a_cols], w_in[:, g_off:zb_off],
                           jnp.zeros((d, pad_a), F32)], axis=1).astype(BF16)
    mu_a = jnp.concatenate([hm4(shift_mu[:4 * a_width]), shift_mu[4 * a_width:], jnp.zeros((LANES,), F32)]).reshape(1, -1)
    w_b = jnp.concatenate([w_in[:, a_cols:g_off], w_in[:, zb_off:]], axis=1).astype(BF16)
    na = w_a.shape[1]
    tn_a = 768 if na % 768 == 0 else LANES

    x2 = x.reshape(tokens, d)
    tm = 512
    proj_a = matmul(x2, w_a, mode="shift", extras=(mu_a,), norm_gain=norm_g, tm=tm, tn=tn_a,
                    tiles_per_seq=seq // tm, name="proj_a")
    proj_b = matmul(x2, w_b, norm_gain=norm_g, tm=tm, tn=1024, name="proj_b")

    ya = rwkv_branch(proj_a, batch, seq, a_width, w_lora_up, w0, a_lora_up, a0, k_k, k_a, r_k, lnx_w, lnx_b)
    graw = proj_a[:, a_cols:a_cols + n_gate]
    yb = nsa_branch(proj_b, graw, batch, seq, b_width, q_norm_g, k_norm_g, pe_cmp_k, pe_cmp_v,
                    cmp_k_w1, cmp_k_w2, cmp_v_w1, cmp_v_w2)

    w_up_a_nm = head_minor(w_up_a.T, heads_a).T
    merged = merge(ya, yb, w_up_a_nm.astype(BF16), w_up_b.astype(BF16), proj_b, 2 * b_width + 6 * kv_width)
    x1 = matmul(merged, w_out.astype(BF16), mode="resid", extras=(x2,), tm=tm, tn=1024, name="out_proj")
    e = ple_embed(p.reshape(tokens, -1), w_ple.astype(BF16), ple_post_g)
    out = matmul(x1, w_ple_gate.astype(BF16), mode="ple", extras=(x1, e), norm_gain=ple_pre_g, tm=tm, tn=1024,
                 name="ple_gate")
    return out.reshape(batch, seq, d)


def kernel(x, p, norm_g, w_in, shift_mu, w_lora_up, w0, a_lora_up, a0, k_k, k_a, r_k, lnx_w, lnx_b, q_norm_g, k_norm_g, pe_cmp_k, pe_cmp_v, cmp_k_w1, cmp_k_w2, cmp_v_w1, cmp_v_w2, w_up_a, w_up_b, w_out, ple_pre_g, w_ple_gate, w_ple, ple_post_g):
    depth = w_in.shape[0]
    for i in range(depth):
        x = _layer(x, p[i], norm_g[i], w_in[i], shift_mu[i], w_lora_up[i], w0[i], a_lora_up[i], a0[i], k_k[i],
                   k_a[i], r_k[i], lnx_w[i], lnx_b[i], q_norm_g[i], k_norm_g[i], pe_cmp_k[i], pe_cmp_v[i],
                   cmp_k_w1[i], cmp_k_w2[i], cmp_v_w1[i], cmp_v_w2[i], w_up_a[i], w_up_b[i], w_out[i],
                   ple_pre_g[i], w_ple_gate[i], w_ple[i], ple_post_g[i])
    return x
```

```python
import functools

import jax
import jax.numpy as jnp
from jax import lax
from jax.experimental import pallas as pl
from jax.experimental.pallas import tpu as pltpu

F32 = jnp.float32
BF16 = jnp.bfloat16

LANES = 128
SUBLANES = 8
VMEM_LIMIT = 56 * 1024 * 1024

NORM_EPS = 1e-6
NEG_INF = -1e30
HEAD_A = 64
GN_EPS = 64e-5
HEAD_B = 128
KV_GROUPS = 4
REP = 4
N_BRANCH = 3
GATE_PAD = 16
L_CMP = 32
CMP_STRIDE = 16
L_SEL = 64
N_SEL = 16
WINDOW = 512
TQ = 128
KC = 256
FORCE_BONUS = 1e3
ATTN_SCALE = HEAD_B ** -0.5
DECAY_SCALE = 0.6065306597126334


def _params(vmem=VMEM_LIMIT, ndim=1):
    return pltpu.CompilerParams(dimension_semantics=("arbitrary",) * ndim, vmem_limit_bytes=vmem)


def _rms(x, gain):
    ms = jnp.mean(x * x, axis=-1, keepdims=True)
    return x * lax.rsqrt(ms + NORM_EPS) * gain


def _rmsnorm_kernel(x_ref, g_ref, o_ref):
    o_ref[...] = _rms(x_ref[...], g_ref[...]).astype(o_ref.dtype)


def rmsnorm_rows(x, g, out_dtype, tm=256):
    m, d = x.shape
    return pl.pallas_call(
        _rmsnorm_kernel,
        grid=(m // tm,),
        in_specs=[pl.BlockSpec((tm, d), lambda i: (i, 0)), pl.BlockSpec((1, d), lambda i: (0, 0))],
        out_specs=pl.BlockSpec((tm, d), lambda i: (i, 0)),
        out_shape=jax.ShapeDtypeStruct((m, d), out_dtype),
        compiler_params=_params(),
        name="rmsnorm",
    )(x, g.reshape(1, d))


def _mm_kernel(*refs, mode, tiles_per_seq, norm_a):
    a_ref, b_ref = refs[0], refs[1]
    if norm_a:
        x = a_ref[...]
        acc = jnp.dot((x * refs[2][...]).astype(BF16), b_ref[...], preferred_element_type=F32)
        acc = acc * lax.rsqrt(jnp.mean(x * x, axis=-1, keepdims=True) + NORM_EPS)
        refs = refs[:2] + refs[3:]
    else:
        acc = jnp.dot(a_ref[...], b_ref[...], preferred_element_type=F32)
    if mode == "plain":
        o_ref = refs[2]
        o_ref[...] = acc.astype(o_ref.dtype)
    elif mode == "shift":
        mu_ref, o_ref, carry_ref = refs[2], refs[3], refs[4]
        i = pl.program_id(1)
        tm = acc.shape[0]
        first = (i % tiles_per_seq) == 0
        last_prev = jnp.where(first, 0.0, carry_ref[SUBLANES - 1:SUBLANES, :])
        rolled = pltpu.roll(acc, 1, axis=0)
        row = lax.broadcasted_iota(jnp.int32, acc.shape, 0)
        prev = jnp.where(row == 0, last_prev, rolled)
        carry_ref[...] = acc[tm - SUBLANES:tm, :]
        o_ref[...] = (acc + mu_ref[...] * (prev - acc)).astype(o_ref.dtype)
    elif mode == "resid":
        r_ref, o_ref = refs[2], refs[3]
        o_ref[...] = (r_ref[...] + acc).astype(o_ref.dtype)
    elif mode == "ple":
        x_ref, e_ref, o_ref = refs[2], refs[3], refs[4]
        o_ref[...] = (x_ref[...] + jax.nn.sigmoid(acc) * e_ref[...]).astype(o_ref.dtype)
    else:
        raise ValueError(mode)


def matmul(a, b, *, mode="plain", extras=(), norm_gain=None, out_dtype=F32, tm=512, tn=1024, tiles_per_seq=1,
           name="mm"):
    m, k = a.shape
    n = b.shape[1]
    assert m % tm == 0 and n % tn == 0, (m, n, tm, tn)
    in_specs = [pl.BlockSpec((tm, k), lambda j, i: (i, 0)), pl.BlockSpec((k, tn), lambda j, i: (0, j))]
    if norm_gain is not None:
        in_specs.append(pl.BlockSpec((1, k), lambda j, i: (0, 0)))
        extras = (norm_gain.reshape(1, k),) + tuple(extras)
    scratch = []
    if mode == "shift":
        in_specs.append(pl.BlockSpec((1, tn), lambda j, i: (0, j)))
        scratch.append(pltpu.VMEM((SUBLANES, tn), F32))
    elif mode == "resid":
        in_specs.append(pl.BlockSpec((tm, tn), lambda j, i: (i, j)))
    elif mode == "ple":
        in_specs += [pl.BlockSpec((tm, tn), lambda j, i: (i, j)), pl.BlockSpec((tm, tn), lambda j, i: (i, j))]
    return pl.pallas_call(
        functools.partial(_mm_kernel, mode=mode, tiles_per_seq=tiles_per_seq, norm_a=norm_gain is not None),
        grid=(n // tn, m // tm),
        in_specs=in_specs,
        out_specs=pl.BlockSpec((tm, tn), lambda j, i: (i, j)),
        out_shape=jax.ShapeDtypeStruct((m, n), out_dtype),
        scratch_shapes=scratch,
        compiler_params=_params(ndim=2),
        name=name,
    )(a, b, *extras)


def _lora_kernel(x_ref, w_ref, b_ref, o_ref, *, tiles_per_half):
    x = x_ref[...]
    rank = w_ref.shape[0]
    decay_half = pl.program_id(1) < tiles_per_half
    xs = jnp.where(decay_half, jnp.tanh(x[:, :rank]), x[:, rank:])
    acc = jnp.dot(xs, w_ref[...], preferred_element_type=F32, precision=lax.Precision.HIGHEST)
    o_ref[...] = acc + b_ref[...]


def lora_project(proj_s, w_pair, bias, tm=1024, tn=1024):
    m = proj_s.shape[0]
    _, rank, n = w_pair.shape
    tm = min(tm, m)
    assert m % tm == 0 and n % tn == 0 and 2 * rank == LANES
    half = n // tn
    return pl.pallas_call(
        functools.partial(_lora_kernel, tiles_per_half=half),
        grid=(m // tm, 2 * half),
        in_specs=[
            pl.BlockSpec((tm, LANES), lambda i, j: (i, 0)),
            pl.BlockSpec((None, rank, tn), lambda i, j: (j // half, 0, j % half)),
            pl.BlockSpec((1, tn), lambda i, j: (0, j)),
        ],
        out_specs=pl.BlockSpec((tm, tn), lambda i, j: (i, j)),
        out_shape=jax.ShapeDtypeStruct((m, 2 * n), F32),
        compiler_params=_params(ndim=2),
        name="lora",
    )(proj_s, w_pair, bias)


SCAN_BATCH = 4
SCAN_HEADS = LANES // SCAN_BATCH
NGRP = LANES // SCAN_HEADS


def _segment_transpose(x):
    seg = lax.broadcasted_iota(jnp.int32, x[0].shape, 1) // SCAN_HEADS
    y = []
    for i in range(NGRP):
        out = None
        for j in range(SCAN_BATCH):
            shift = ((j - i) % NGRP) * SCAN_HEADS
            piece = x[j] if shift == 0 else pltpu.roll(x[j], shift, axis=1)
            out = piece if out is None else jnp.where(seg == j, piece, out)
        y.append(out)
    return y


def _to_lanes(x_ref, dst_ref, tc):
    for g in range(HEAD_A // NGRP):
        y = _segment_transpose([x_ref[b, :, g * LANES:(g + 1) * LANES] for b in range(SCAN_BATCH)])
        for n_lo in range(NGRP):
            n = g * NGRP + n_lo
            dst_ref[n * tc:(n + 1) * tc, :] = y[n_lo]


def _from_lanes_gated(src_ref, z_ref, o_ref, tc):
    for g in range(HEAD_A // NGRP):
        y = _segment_transpose([src_ref[(g * NGRP + n_lo) * tc:(g * NGRP + n_lo + 1) * tc, :] for n_lo in range(NGRP)])
        for b in range(SCAN_BATCH):
            z = z_ref[b, :, g * LANES:(g + 1) * LANES]
            o_ref[b, :, g * LANES:(g + 1) * LANES] = (y[b] * (z * jax.nn.sigmoid(z))).astype(o_ref.dtype)


def _scan_kernel(r_ref, k_ref, v_ref, wp_ref, ap_ref, z_ref, kkw_ref, kaw_ref, rkw_ref, lnw_ref, lnb_ref,
                 o_ref, state_ref, set_a, set_b, ans, os, bon, *, tc):
    s = pl.program_id(1)
    srcs = (r_ref, k_ref, v_ref, wp_ref, ap_ref)

    @pl.when(s == 0)
    def _():
        state_ref[...] = jnp.zeros_like(state_ref)
        for src, dst in zip(srcs, set_a):
            _to_lanes(src, dst, tc)

    @pl.when(s % 2 == 1)
    def _():
        _scan_chunk(set_a, set_b, srcs, z_ref, kkw_ref, kaw_ref, rkw_ref, lnw_ref, lnb_ref, o_ref, state_ref,
                    ans, os, bon, tc)

    @pl.when((s % 2 == 0) & (s > 0))
    def _():
        _scan_chunk(set_b, set_a, srcs, z_ref, kkw_ref, kaw_ref, rkw_ref, lnw_ref, lnb_ref, o_ref, state_ref,
                    ans, os, bon, tc)


def _scan_chunk(cur, nxt, srcs, z_ref, kkw_ref, kaw_ref, rkw_ref, lnw_ref, lnb_ref, o_ref, state_ref,
                ans, os, bon, tc):
    n_ch = HEAD_A
    rs, ks, vs, ws, bvs = cur
    n_groups = HEAD_A // NGRP
    groups_per_step = 2
    steps_per_rowblock = n_groups // groups_per_step
    assert tc * groups_per_step == n_groups * (tc // SUBLANES)

    def rows(n):
        return pl.ds(pl.multiple_of(n * tc, tc), tc)

    def norm_acc(n, acc):
        kkr = ks[rows(n), :] * kkw_ref[n]
        return acc + kkr * kkr

    nsq = lax.fori_loop(0, n_ch, norm_acc, jnp.zeros((tc, LANES), F32), unroll=8)
    inv = 1.0 / jnp.maximum(jnp.sqrt(nsq), 1e-12)

    def prep(n, bacc):
        k = ks[rows(n), :]
        a = jax.nn.sigmoid(bvs[rows(n), :])
        kk = k * kkw_ref[n] * inv
        ws[rows(n), :] = jnp.exp(-DECAY_SCALE * jax.nn.sigmoid(ws[rows(n), :]))
        ans[rows(n), :] = -kk
        bvs[rows(n), :] = kk * a
        kmod = k * (1.0 + (a - 1.0) * kaw_ref[n])
        ks[rows(n), :] = kmod
        return bacc + rs[rows(n), :] * kmod * rkw_ref[n]

    bon[...] = lax.fori_loop(0, n_ch, prep, jnp.zeros((tc, LANES), F32), unroll=4)

    def bcast(ref, row):
        return ref[pl.ds(row, 1), :][None]

    def sa_first(j, acc):
        return acc + state_ref[j] * bcast(ans, j * tc)

    slab = (n_ch // SUBLANES, SUBLANES, LANES)
    sa0 = lax.fori_loop(0, n_ch, sa_first, jnp.zeros(slab, F32), unroll=4)

    def step(t, sa):
        v = vs[pl.ds(t, n_ch, stride=tc), :].reshape(slab)
        t_next = jnp.minimum(t + 1, tc - 1)

        out = jnp.zeros(slab, F32)
        sa_next = jnp.zeros(slab, F32)
        for j in range(n_ch):
            row = j * tc + t
            s_new = state_ref[j] * bcast(ws, row) + sa * bcast(bvs, row) + v * bcast(ks, row)
            state_ref[j] = s_new
            out = out + s_new * bcast(rs, row)
            sa_next = sa_next + s_new * bcast(ans, j * tc + t_next)
        o = out.reshape(n_ch, LANES)
        mu = jnp.mean(o, axis=0, keepdims=True)
        d = o - mu
        var = jnp.mean(d * d, axis=0, keepdims=True)
        on = d * lax.rsqrt(var + GN_EPS) * lnw_ref[...] + lnb_ref[...]
        bonus = bon[pl.ds(t, 1), :] * v.reshape(n_ch, LANES)
        os[pl.ds(t, n_ch, stride=tc), :] = on + bonus

        row0 = pl.multiple_of((t // steps_per_rowblock) * SUBLANES, SUBLANES)
        for src, dst in zip(srcs, nxt):
            for gi in range(groups_per_step):
                g = (t % steps_per_rowblock) * groups_per_step + gi
                lane0 = pl.multiple_of(g * LANES, LANES)
                y = _segment_transpose([src[b, pl.ds(row0, SUBLANES), pl.ds(lane0, LANES)] for b in range(SCAN_BATCH)])
                for n_lo in range(NGRP):
                    dst[pl.ds(pl.multiple_of((g * NGRP + n_lo) * tc + row0, SUBLANES), SUBLANES), :] = y[n_lo]
        return sa_next

    lax.fori_loop(0, tc, step, sa0)
    _from_lanes_gated(os, z_ref, o_ref, tc)


def rwkv_scan(proj_a, wa_pre, kkw, kaw, rkw, lnw, lnb, batch, seq, tc=32):
    width = HEAD_A * SCAN_HEADS
    n_chunks = seq // tc
    seq_in = lambda col: pl.BlockSpec((SCAN_BATCH, tc, width), lambda g, s: (g, jnp.minimum(s, n_chunks - 1), col))
    seq_out = lambda col: pl.BlockSpec((SCAN_BATCH, tc, width), lambda g, s: (g, jnp.maximum(s - 1, 0), col))
    par3 = pl.BlockSpec((HEAD_A, 1, LANES), lambda g, s: (0, 0, 0))
    par2 = pl.BlockSpec((HEAD_A, LANES), lambda g, s: (0, 0))
    buf = pltpu.VMEM((HEAD_A * tc, LANES), F32)
    return pl.pallas_call(
        functools.partial(_scan_kernel, tc=tc),
        grid=(batch // SCAN_BATCH, n_chunks + 1),
        in_specs=[seq_in(0), seq_in(1), seq_in(2), seq_in(0), seq_in(1), seq_out(3), par3, par3, par3, par2, par2],
        out_specs=seq_out(0),
        out_shape=jax.ShapeDtypeStruct((batch, seq, width), BF16),
        scratch_shapes=[pltpu.VMEM((HEAD_A, HEAD_A // SUBLANES, SUBLANES, LANES), F32), [buf] * 5, [buf] * 5,
                        buf, buf, pltpu.VMEM((tc, LANES), F32)],
        compiler_params=_params(ndim=2),
        name="rwkv_scan",
    )(proj_a, proj_a, proj_a, wa_pre, wa_pre, proj_a, kkw, kaw, rkw, lnw, lnb)


def _compress_kernel(x_ref, pe_ref, w1_ref, w2_ref, kg_ref, o_ref):
    which = pl.program_id(0)
    half = L_CMP // 2
    nblk = x_ref.shape[0] // CMP_STRIDE
    h1 = jnp.zeros((nblk, w1_ref.shape[-1]), F32)
    h2 = jnp.zeros((nblk, w1_ref.shape[-1]), F32)
    for l in range(half):
        x = x_ref[pl.ds(l, nblk, stride=CMP_STRIDE), :]
        a1 = (x + pe_ref[l:l + 1, :]).astype(BF16)
        a2 = (x + pe_ref[half + l:half + l + 1, :]).astype(BF16)
        h1 = h1 + jnp.dot(a1, w1_ref[l], preferred_element_type=F32)
        h2 = h2 + jnp.dot(a2, w1_ref[half + l], preferred_element_type=F32)
    hid = h1 + pltpu.roll(h2, nblk - 1, axis=0)
    hid = jax.nn.gelu(hid)
    out = jnp.dot(hid.astype(BF16), w2_ref[...], preferred_element_type=F32)
    ms = jnp.mean(out * out, axis=-1, keepdims=True)
    normed = out * lax.rsqrt(ms + NORM_EPS) * kg_ref[...]
    out = jnp.where(which == 0, normed, out)
    row = lax.broadcasted_iota(jnp.int32, out.shape, 0)
    o_ref[...] = jnp.where(row < nblk - 1, out, 0.0)


def compress(proj_b, kc_block, vc_block, pe, w1, w2, kgain, batch, seq):
    nblk = seq // CMP_STRIDE
    hidden = w1.shape[-1]

    def xmap(w, b, g):
        return (b, kc_block + w * (vc_block - kc_block) + g)

    return pl.pallas_call(
        _compress_kernel,
        grid=(2, batch, KV_GROUPS),
        in_specs=[
            pl.BlockSpec((seq, HEAD_B), xmap),
            pl.BlockSpec((None, L_CMP, HEAD_B), lambda w, b, g: (w, 0, 0)),
            pl.BlockSpec((None, L_CMP, HEAD_B, hidden), lambda w, b, g: (w, 0, 0, 0)),
            pl.BlockSpec((None, hidden, HEAD_B), lambda w, b, g: (w, 0, 0)),
            pl.BlockSpec((1, HEAD_B), lambda w, b, g: (0, 0)),
        ],
        out_specs=pl.BlockSpec((None, None, None, nblk, HEAD_B), lambda w, b, g: (w, b, g, 0, 0)),
        out_shape=jax.ShapeDtypeStruct((2, batch, KV_GROUPS, nblk, HEAD_B), F32),
        compiler_params=_params(ndim=3),
        name="nsa_compress",
    )(proj_b, pe, w1, w2, kgain)


def _dot_nt(a, b):
    return lax.dot_general(a, b, (((1,), (1,)), ((), ())), preferred_element_type=F32)


LOG2E = 1.4426950408889634
VAUG = 2 * HEAD_B


def _attn_kernel(q_ref, ks_ref, vs_ref, kw_ref, vw_ref, kc_ref, vc_ref, gt_ref, z_ref, qg_ref, kg_ref,
                 ovt_ref, ex_ref, wb_ref, o_ref, ksn_ref, vs1_ref, kwn_ref, vw1_ref, q_scr, m_scr, acc_scr,
                 oc_scr, ow_scr, *, seq):
    qt = pl.program_id(2)
    ncmp = kc_ref.shape[0]
    nsel = seq // L_SEL

    @pl.when(qt == 0)
    def _():
        ones_col = jnp.ones((seq, HEAD_B), BF16)
        ksn_ref[...] = _rms(ks_ref[...], kg_ref[1:2, :]).astype(BF16)
        vs1_ref[:, 0:HEAD_B] = vs_ref[...].astype(BF16)
        vs1_ref[:, HEAD_B:VAUG] = ones_col
        kwn_ref[0:WINDOW, :] = jnp.zeros((WINDOW, HEAD_B), BF16)
        vw1_ref[0:WINDOW, :] = jnp.zeros((WINDOW, VAUG), BF16)
        kwn_ref[WINDOW:WINDOW + seq, :] = _rms(kw_ref[...], kg_ref[2:3, :]).astype(BF16)
        vw1_ref[WINDOW:WINDOW + seq, 0:HEAD_B] = vw_ref[...].astype(BF16)
        vw1_ref[WINDOW:WINDOW + seq, HEAD_B:VAUG] = ones_col

    t0 = pl.multiple_of(qt * TQ, TQ)
    for r in range(REP):
        q_scr[r] = (_rms(q_ref[:, r * HEAD_B:(r + 1) * HEAD_B], qg_ref[...]) * (ATTN_SCALE * LOG2E)).astype(BF16)

    def t_of(shape):
        return t0 + lax.broadcasted_iota(jnp.int32, shape, 0)

    n_idx = lax.broadcasted_iota(jnp.int32, (TQ, ncmp), 1)
    bias_c = jnp.where(n_idx * CMP_STRIDE + (L_CMP - 1) <= t_of((TQ, ncmp)), 0.0, NEG_INF)
    bias_c = jnp.where(n_idx < ncmp - 1, bias_c, NEG_INF)
    row_ok = jnp.where(t_of((TQ, 1)) >= L_CMP - 1, 1.0, 0.0)
    q_all = q_scr[...].reshape(REP * TQ, HEAD_B)
    head = lambda x, r: x[r * TQ:(r + 1) * TQ]
    s_all = _dot_nt(q_all, kc_ref[...].astype(BF16))
    prs = []
    for r in range(REP):
        s = head(s_all, r) + bias_c
        p = jnp.exp2(s - jnp.max(s, axis=-1, keepdims=True))
        prs.append(p * (row_ok / jnp.sum(p, axis=-1, keepdims=True)))
    psum = prs[0] + prs[1] + prs[2] + prs[3]
    oc_scr[...] = jnp.dot(jnp.concatenate(prs, axis=0).astype(BF16), vc_ref[...].astype(BF16),
                          preferred_element_type=F32).reshape(REP, TQ, HEAD_B)
    imp_t = lax.dot_general(ovt_ref[...], psum, (((1,), (1,)), ((), ())), preferred_element_type=F32,
                            precision=lax.Precision.HIGHEST)

    j_idx = lax.broadcasted_iota(jnp.int32, (nsel, TQ), 0)
    t_sel = t0 + lax.broadcasted_iota(jnp.int32, (nsel, TQ), 1)
    cur = t_sel // L_SEL
    forced = jnp.where((j_idx == 0) | (j_idx == cur) | (j_idx == cur - 1), FORCE_BONUS, 0.0)
    score = jnp.where(j_idx * L_SEL <= t_sel, imp_t + forced, NEG_INF)
    rank = jnp.zeros((nsel, TQ), F32)
    for i in range(nsel):
        si = score[i:i + 1, :]
        ahead = (si > score) | ((si == score) & (j_idx > i))
        rank = rank + jnp.where(ahead, 1.0, 0.0)
    sel_t = jnp.where((rank < min(N_SEL, nsel)) & (score > 0.5 * NEG_INF), 1.0, 0.0)
    sel = jnp.concatenate([sel_t, jnp.zeros((LANES - nsel, TQ), F32)], axis=0).T
    not_sel = (1.0 - sel).astype(BF16)

    span = WINDOW + TQ
    kwin = kwn_ref[pl.ds(t0, span), :]
    vwin = vw1_ref[pl.ds(t0, span), :]
    lane_w = lax.broadcasted_iota(jnp.int32, (TQ, span), 1)
    bias_w = jnp.where(lane_w >= WINDOW - t0, wb_ref[...], NEG_INF)
    s_all = _dot_nt(q_all, kwin)
    ps = []
    for r in range(REP):
        s = head(s_all, r) + bias_w
        ps.append(jnp.exp2(s - jnp.max(s, axis=-1, keepdims=True)).astype(BF16))
    ow_scr[...] = jnp.dot(jnp.concatenate(ps, axis=0), vwin, preferred_element_type=F32)

    m_scr[...] = jnp.full(m_scr.shape, NEG_INF, F32)
    acc_scr[...] = jnp.zeros(acc_scr.shape, F32)

    def scores(c):
        k0 = pl.multiple_of(c * KC, KC)
        return _dot_nt(q_scr[...].reshape(REP * TQ, HEAD_B), ksn_ref[pl.ds(k0, KC), :])

    def sel_bias(c):
        return jnp.dot(not_sel, ex_ref[c], preferred_element_type=F32)

    def sel_chunk(c, s_all, bias, diagonal):
        k0 = pl.multiple_of(c * KC, KC)
        v1 = vs1_ref[pl.ds(k0, KC), :]
        if diagonal:
            lane = lax.broadcasted_iota(jnp.int32, (TQ, KC), 1)
            bias = jnp.where(k0 + lane <= t_of((TQ, KC)), bias, NEG_INF)
        ps, alphas = [], []
        for r in range(REP):
            s = head(s_all, r) + bias
            m_old = m_scr[r]
            m_new = jnp.maximum(m_old, jnp.max(s, axis=-1, keepdims=True))
            m_scr[r] = m_new
            ps.append(jnp.exp2(s - jnp.tile(m_new, (1, KC // LANES))).astype(BF16))
            alphas.append(jnp.exp2(m_old - m_new))
        pv = jnp.dot(jnp.concatenate(ps, axis=0), v1, preferred_element_type=F32)
        for r in range(REP):
            acc_scr[r] = jnp.tile(alphas[r], (1, VAUG // LANES)) * acc_scr[r] + head(pv, r)

    n_full = t0 // KC

    def full_chunk(c, carry):
        s_cur, bias_cur = carry
        nxt = (scores(c + 1), sel_bias(c + 1))
        sel_chunk(c, s_cur, bias_cur, False)
        return nxt

    s_last, bias_last = lax.fori_loop(0, n_full, full_chunk, (scores(0), sel_bias(0)))
    sel_chunk(n_full, s_last, bias_last, True)

    gts = jax.nn.sigmoid(gt_ref[...])
    for r in range(REP):
        ow = ow_scr[r * TQ:(r + 1) * TQ, :]
        o_w = ow[:, 0:HEAD_B] / ow[:, HEAD_B:VAUG]
        acc = acc_scr[r]
        o_s = acc[:, 0:HEAD_B] / acc[:, HEAD_B:VAUG]
        g0 = N_BRANCH * r
        o = gts[:, g0:g0 + 1] * oc_scr[r] + gts[:, g0 + 1:g0 + 2] * o_s + gts[:, g0 + 2:g0 + 3] * o_w
        z = z_ref[:, r * HEAD_B:(r + 1) * HEAD_B]
        o_ref[:, r * HEAD_B:(r + 1) * HEAD_B] = (o * (z * jax.nn.sigmoid(z))).astype(o_ref.dtype)


def nsa_attention(proj_b, proj_z, cmp_kv, gates, q_gain, k_gain, overlap_t, expand, win_bias, blocks, batch, seq):
    nq = seq // TQ
    ncmp = seq // CMP_STRIDE
    gw = REP * HEAD_B // LANES
    qspec = lambda off: pl.BlockSpec((TQ, REP * HEAD_B), lambda b, g, t: (b * nq + t, off // gw + g))
    kvspec = lambda off: pl.BlockSpec((seq, HEAD_B), lambda b, g, t: (b, off + g))
    cspec = lambda w: pl.BlockSpec((None, None, None, ncmp, HEAD_B), lambda b, g, t: (w, b, g, 0, 0))
    full = lambda a: pl.BlockSpec(a.shape, lambda b, g, t: (0,) * a.ndim)
    return pl.pallas_call(
        functools.partial(_attn_kernel, seq=seq),
        grid=(batch, KV_GROUPS, nq),
        in_specs=[
            qspec(blocks["q"]), kvspec(blocks["ks"]), kvspec(blocks["vs"]), kvspec(blocks["kw"]), kvspec(blocks["vw"]),
            cspec(0), cspec(1),
            pl.BlockSpec((None, None, TQ, GATE_PAD), lambda b, g, t: (b, g, t, 0)),
            qspec(blocks["z"]),
            full(q_gain), full(k_gain), full(overlap_t), full(expand), full(win_bias),
        ],
        out_specs=pl.BlockSpec((TQ, REP * HEAD_B), lambda b, g, t: (b * nq + t, g)),
        out_shape=jax.ShapeDtypeStruct((batch * seq, KV_GROUPS * REP * HEAD_B), BF16),
        scratch_shapes=[
            pltpu.VMEM((seq, HEAD_B), BF16), pltpu.VMEM((seq, VAUG), BF16),
            pltpu.VMEM((seq + WINDOW, HEAD_B), BF16), pltpu.VMEM((seq + WINDOW, VAUG), BF16),
            pltpu.VMEM((REP, TQ, HEAD_B), BF16), pltpu.VMEM((REP, TQ, LANES), F32),
            pltpu.VMEM((REP, TQ, VAUG), F32), pltpu.VMEM((REP, TQ, HEAD_B), F32),
            pltpu.VMEM((REP * TQ, VAUG), F32),
        ],
        compiler_params=_params(ndim=3),
        name="nsa_attention",
    )(proj_b, proj_b, proj_b, proj_b, proj_b, cmp_kv, cmp_kv, gates, proj_z, q_gain, k_gain, overlap_t, expand,
      win_bias)


def _merge_kernel(ya_ref, yb_ref, wa_ref, wb_ref, ga_ref, gb_ref, out_ref):
    ua = jnp.dot(ya_ref[...], wa_ref[...], preferred_element_type=F32)
    ub = jnp.dot(yb_ref[...], wb_ref[...], preferred_element_type=F32)
    out_ref[...] = (jax.nn.sigmoid(ga_ref[...]) * ua + jax.nn.sigmoid(gb_ref[...]) * ub).astype(out_ref.dtype)


def merge(ya, yb, w_a, w_b, proj_b, gate_col, tm=1024, tn=512):
    m, ka = ya.shape
    n = w_a.shape[1]
    tm = min(tm, m)
    assert gate_col % tn == 0 and n % tn == 0
    ga_block, gb_block = gate_col // tn, (gate_col + n) // tn
    return pl.pallas_call(
        _merge_kernel,
        grid=(m // tm, n // tn),
        in_specs=[
            pl.BlockSpec((tm, ka), lambda i, j: (i, 0)),
            pl.BlockSpec((tm, ka), lambda i, j: (i, 0)),
            pl.BlockSpec((ka, tn), lambda i, j: (0, j)),
            pl.BlockSpec((ka, tn), lambda i, j: (0, j)),
            pl.BlockSpec((tm, tn), lambda i, j: (i, ga_block + j)),
            pl.BlockSpec((tm, tn), lambda i, j: (i, gb_block + j)),
        ],
        out_specs=pl.BlockSpec((tm, tn), lambda i, j: (i, j)),
        out_shape=jax.ShapeDtypeStruct((m, n), BF16),
        compiler_params=_params(ndim=2),
        name="merge",
    )(ya, yb, w_a, w_b, proj_b, proj_b)


def _ple_embed_kernel(p_ref, w_ref, g_ref, o_ref):
    acc = jnp.dot(p_ref[...].astype(BF16), w_ref[...], preferred_element_type=F32)
    o_ref[...] = _rms(acc, g_ref[...]).astype(o_ref.dtype)


def ple_embed(p, w, g, tm=256):
    m, k = p.shape
    n = w.shape[1]
    return pl.pallas_call(
        _ple_embed_kernel,
        grid=(m // tm,),
        in_specs=[pl.BlockSpec((tm, k), lambda i: (i, 0)), pl.BlockSpec((k, n), lambda i: (0, 0)),
                  pl.BlockSpec((1, n), lambda i: (0, 0))],
        out_specs=pl.BlockSpec((tm, n), lambda i: (i, 0)),
        out_shape=jax.ShapeDtypeStruct((m, n), F32),
        compiler_params=_params(),
        name="ple_embed",
    )(p, w, g.reshape(1, n))


def head_minor(w, heads):
    lead = w.shape[:-1]
    return w.reshape(*lead, heads, HEAD_A).swapaxes(-1, -2).reshape(*lead, heads * HEAD_A)


def rwkv_branch(proj_a, proj_s, batch, seq, a_width, w_lora_up, w0, a_lora_up, a0, k_k, k_a, r_k, lnx_w, lnx_b):
    tokens = batch * seq
    heads_a = a_width // HEAD_A
    assert heads_a == SCAN_HEADS and batch % SCAN_BATCH == 0
    lora = w_lora_up.shape[0]
    hm = lambda w: head_minor(w, heads_a)
    w_lora = jnp.stack([hm(w_lora_up), hm(a_lora_up)])
    b_lora = jnp.concatenate([hm(w0), hm(a0)]).reshape(1, -1)
    wa_pre = lora_project(proj_s, w_lora, b_lora)

    def par_scan(t):
        return jnp.tile(t.reshape(heads_a, HEAD_A).T, (1, SCAN_BATCH))

    par3 = lambda t: par_scan(t).reshape(HEAD_A, 1, LANES)
    o = rwkv_scan(proj_a.reshape(batch, seq, -1), wa_pre.reshape(batch, seq, -1),
                  par3(k_k), par3(k_a), par3(r_k), par_scan(lnx_w), par_scan(lnx_b), batch, seq)
    return o.reshape(tokens, a_width)


def nsa_branch(proj_b, proj_z, graw, batch, seq, b_width, q_norm_g, k_norm_g, pe_cmp_k, pe_cmp_v,
               cmp_k_w1, cmp_k_w2, cmp_v_w1, cmp_v_w2):
    kv_width = KV_GROUPS * HEAD_B
    blk = lambda cols: cols // LANES
    blocks = {"q": 0, "kc": blk(b_width), "vc": blk(b_width + kv_width), "ks": blk(b_width + 2 * kv_width),
              "vs": blk(b_width + 3 * kv_width), "kw": blk(b_width + 4 * kv_width), "vw": blk(b_width + 5 * kv_width),
              "z": 0}
    pe = jnp.stack([pe_cmp_k, pe_cmp_v])
    hidden = cmp_k_w1.shape[1]
    w1 = jnp.stack([cmp_k_w1, cmp_v_w1]).reshape(2, L_CMP, HEAD_B, hidden).astype(BF16)
    w2 = jnp.stack([cmp_k_w2, cmp_v_w2]).astype(BF16)
    cmp_kv = compress(proj_b, blocks["kc"], blocks["vc"], pe, w1, w2, k_norm_g[0:1], batch, seq)

    ncmp = seq // CMP_STRIDE
    nsel = seq // L_SEL
    c_start = jnp.arange(ncmp) * CMP_STRIDE
    s_start = jnp.arange(nsel) * L_SEL
    overlap_t = ((c_start[None, :] < (s_start + L_SEL)[:, None]) & (s_start[:, None] < (c_start + L_CMP)[None, :])
                 & (jnp.arange(ncmp)[None, :] < ncmp - 1)).astype(F32)
    key_blk = jnp.arange(seq) // L_SEL
    expand = jnp.where(jnp.arange(LANES)[:, None] == key_blk[None, :], NEG_INF, 0.0).astype(BF16)
    expand = expand.reshape(LANES, seq // KC, KC).transpose(1, 0, 2)
    tl = jnp.arange(TQ)[:, None]
    u = jnp.arange(WINDOW + TQ)[None, :]
    win_bias = jnp.where((u > tl) & (u <= WINDOW + tl), 0.0, NEG_INF).astype(F32)
    gates = graw.reshape(batch, seq, KV_GROUPS, REP * N_BRANCH).transpose(0, 2, 1, 3)
    gates = jnp.pad(gates, ((0, 0), (0, 0), (0, 0), (0, GATE_PAD - REP * N_BRANCH)))
    return nsa_attention(proj_b, proj_z, cmp_kv, gates, q_norm_g.reshape(1, HEAD_B), k_norm_g, overlap_t, expand,
                         win_bias, blocks, batch, seq)


def _layer(x, p, norm_g, w_in, shift_mu, w_lora_up, w0, a_lora_up, a0, k_k, k_a, r_k, lnx_w, lnx_b,
           q_norm_g, k_norm_g, pe_cmp_k, pe_cmp_v, cmp_k_w1, cmp_k_w2, cmp_v_w1, cmp_v_w2,
           w_up_a, w_up_b, w_out, ple_pre_g, w_ple_gate, w_ple, ple_post_g):
    batch, seq, d = x.shape
    tokens = batch * seq
    a_width = w_up_a.shape[0]
    b_width = w_up_b.shape[0]
    heads_a = a_width // HEAD_A
    kv_width = KV_GROUPS * HEAD_B
    n_gate = KV_GROUPS * REP * N_BRANCH
    lora = w_lora_up.shape[0]
    assert lora == HEAD_A and a_lora_up.shape[0] == HEAD_A and 2 * lora == LANES
    a_cols = 4 * a_width + 2 * lora
    g_off = a_cols + b_width + 6 * kv_width
    zb_off = g_off + n_gate
    assert w_in.shape[1] == zb_off + b_width + 2 * d

    hm4 = lambda w: head_minor(w.reshape(*w.shape[:-1], 4, a_width), heads_a).reshape(*w.shape[:-1], 4 * a_width)
    w_a = hm4(w_in[:, :4 * a_width]).astype(BF16)
    mu_a = hm4(shift_mu[:4 * a_width]).reshape(1, -1)
    pad_s = LANES - n_gate
    w_s = jnp.concatenate([w_in[:, 4 * a_width:a_cols], w_in[:, g_off:zb_off], jnp.zeros((d, pad_s), F32)],
                          axis=1).astype(BF16)
    mu_s = jnp.concatenate([shift_mu[4 * a_width:], jnp.zeros((LANES,), F32)]).reshape(1, -1)
    w_b = w_in[:, a_cols:g_off].astype(BF16)
    w_z = w_in[:, zb_off:].astype(BF16)

    x2 = x.reshape(tokens, d)
    h = rmsnorm_rows(x2, norm_g, BF16)
    tm = 512
    shift = dict(mode="shift", tm=tm, tiles_per_seq=seq // tm)
    proj_a = matmul(h, w_a, extras=(mu_a,), tn=1024, name="proj_a", **shift)
    proj_s = matmul(h, w_s, extras=(mu_s,), tn=w_s.shape[1], name="proj_s", **shift)
    proj_b = matmul(h, w_b, tm=tm, tn=1024, name="proj_b")
    proj_z = matmul(h, w_z, tm=tm, tn=1024, name="proj_z")

    ya = rwkv_branch(proj_a, proj_s, batch, seq, a_width, w_lora_up, w0, a_lora_up, a0, k_k, k_a, r_k, lnx_w, lnx_b)
    graw = proj_s[:, 2 * lora:2 * lora + n_gate]
    yb = nsa_branch(proj_b, proj_z, graw, batch, seq, b_width, q_norm_g, k_norm_g, pe_cmp_k, pe_cmp_v,
                    cmp_k_w1, cmp_k_w2, cmp_v_w1, cmp_v_w2)

    w_up_a_nm = head_minor(w_up_a.T, heads_a).T
    merged = merge(ya, yb, w_up_a_nm.astype(BF16), w_up_b.astype(BF16), proj_z, b_width)
    x1 = matmul(merged, w_out.astype(BF16), mode="resid", extras=(x2,), tm=tm, tn=1024, name="out_proj")
    e = ple_embed(p.reshape(tokens, -1), w_ple.astype(BF16), ple_post_g)
    out = matmul(x1, w_ple_gate.astype(BF16), mode="ple", extras=(x1, e), norm_gain=ple_pre_g, tm=tm, tn=1024,
                 name="ple_gate")
    return out.reshape(batch, seq, d)


def kernel(x, p, norm_g, w_in, shift_mu, w_lora_up, w0, a_lora_up, a0, k_k, k_a, r_k, lnx_w, lnx_b, q_norm_g, k_norm_g, pe_cmp_k, pe_cmp_v, cmp_k_w1, cmp_k_w2, cmp_v_w1, cmp_v_w2, w_up_a, w_up_b, w_out, ple_pre_g, w_ple_gate, w_ple, ple_post_g):
    depth = w_in.shape[0]
    for i in range(depth):
        x = _layer(x, p[i], norm_g[i], w_in[i], shift_mu[i], w_lora_up[i], w0[i], a_lora_up[i], a0[i], k_k[i],
                   k_a[i], r_k[i], lnx_w[i], lnx_b[i], q_norm_g[i], k_norm_g[i], pe_cmp_k[i], pe_cmp_v[i],
                   cmp_k_w1[i], cmp_k_w2[i], cmp_v_w1[i], cmp_v_w2[i], w_up_a[i], w_up_b[i], w_out[i],
                   ple_pre_g[i], w_ple_gate[i], w_ple[i], ple_post_g[i])
    return x
```

```python
import functools

import jax
import jax.numpy as jnp
from jax import lax
from jax.experimental import pallas as pl
from jax.experimental.pallas import tpu as pltpu

F32 = jnp.float32
BF16 = jnp.bfloat16

LANES = 128
SUBLANES = 8
VMEM_LIMIT = 56 * 1024 * 1024

NORM_EPS = 1e-6
NEG_INF = -1e30
HEAD_A = 64
GN_EPS = 64e-5
HEAD_B = 128
KV_GROUPS = 4
REP = 4
N_BRANCH = 3
GATE_PAD = 16
L_CMP = 32
CMP_STRIDE = 16
L_SEL = 64
N_SEL = 16
WINDOW = 512
TQ = 256
KC = 256
FORCE_BONUS = 1e3
ATTN_SCALE = HEAD_B ** -0.5
DECAY_SCALE = 0.6065306597126334


def _params(vmem=VMEM_LIMIT, ndim=1):
    return pltpu.CompilerParams(dimension_semantics=("arbitrary",) * ndim, vmem_limit_bytes=vmem)


def _rms(x, gain):
    ms = jnp.mean(x * x, axis=-1, keepdims=True)
    return x * lax.rsqrt(ms + NORM_EPS) * gain


def _rmsnorm_kernel(x_ref, g_ref, o_ref):
    o_ref[...] = _rms(x_ref[...], g_ref[...]).astype(o_ref.dtype)


def rmsnorm_rows(x, g, out_dtype, tm=256):
    m, d = x.shape
    return pl.pallas_call(
        _rmsnorm_kernel,
        grid=(m // tm,),
        in_specs=[pl.BlockSpec((tm, d), lambda i: (i, 0)), pl.BlockSpec((1, d), lambda i: (0, 0))],
        out_specs=pl.BlockSpec((tm, d), lambda i: (i, 0)),
        out_shape=jax.ShapeDtypeStruct((m, d), out_dtype),
        compiler_params=_params(),
        name="rmsnorm",
    )(x, g.reshape(1, d))


def _mm_kernel(*refs, mode, tiles_per_seq, norm_a):
    a_ref, b_ref = refs[0], refs[1]
    if norm_a:
        x = a_ref[...]
        acc = jnp.dot((x * refs[2][...]).astype(BF16), b_ref[...], preferred_element_type=F32)
        acc = acc * lax.rsqrt(jnp.mean(x * x, axis=-1, keepdims=True) + NORM_EPS)
        refs = refs[:2] + refs[3:]
    else:
        acc = jnp.dot(a_ref[...], b_ref[...], preferred_element_type=F32)
    if mode == "plain":
        o_ref = refs[2]
        o_ref[...] = acc.astype(o_ref.dtype)
    elif mode == "shift":
        mu_ref, o_ref, carry_ref = refs[2], refs[3], refs[4]
        i = pl.program_id(1)
        tm = acc.shape[0]
        first = (i % tiles_per_seq) == 0
        last_prev = jnp.where(first, 0.0, carry_ref[SUBLANES - 1:SUBLANES, :])
        rolled = pltpu.roll(acc, 1, axis=0)
        row = lax.broadcasted_iota(jnp.int32, acc.shape, 0)
        prev = jnp.where(row == 0, last_prev, rolled)
        carry_ref[...] = acc[tm - SUBLANES:tm, :]
        o_ref[...] = (acc + mu_ref[...] * (prev - acc)).astype(o_ref.dtype)
    elif mode == "resid":
        r_ref, o_ref = refs[2], refs[3]
        o_ref[...] = (r_ref[...] + acc).astype(o_ref.dtype)
    elif mode == "ple":
        x_ref, e_ref, o_ref = refs[2], refs[3], refs[4]
        o_ref[...] = (x_ref[...] + jax.nn.sigmoid(acc) * e_ref[...]).astype(o_ref.dtype)
    else:
        raise ValueError(mode)


def matmul(a, b, *, mode="plain", extras=(), norm_gain=None, out_dtype=F32, tm=512, tn=1024, tiles_per_seq=1,
           name="mm"):
    m, k = a.shape
    n = b.shape[1]
    assert m % tm == 0 and n % tn == 0, (m, n, tm, tn)
    in_specs = [pl.BlockSpec((tm, k), lambda j, i: (i, 0)), pl.BlockSpec((k, tn), lambda j, i: (0, j))]
    if norm_gain is not None:
        in_specs.append(pl.BlockSpec((1, k), lambda j, i: (0, 0)))
        extras = (norm_gain.reshape(1, k),) + tuple(extras)
    scratch = []
    if mode == "shift":
        in_specs.append(pl.BlockSpec((1, tn), lambda j, i: (0, j)))
        scratch.append(pltpu.VMEM((SUBLANES, tn), F32))
    elif mode == "resid":
        in_specs.append(pl.BlockSpec((tm, tn), lambda j, i: (i, j)))
    elif mode == "ple":
        in_specs += [pl.BlockSpec((tm, tn), lambda j, i: (i, j)), pl.BlockSpec((tm, tn), lambda j, i: (i, j))]
    return pl.pallas_call(
        functools.partial(_mm_kernel, mode=mode, tiles_per_seq=tiles_per_seq, norm_a=norm_gain is not None),
        grid=(n // tn, m // tm),
        in_specs=in_specs,
        out_specs=pl.BlockSpec((tm, tn), lambda j, i: (i, j)),
        out_shape=jax.ShapeDtypeStruct((m, n), out_dtype),
        scratch_shapes=scratch,
        compiler_params=_params(ndim=2),
        name=name,
    )(a, b, *extras)


def _lora_kernel(x_ref, w_ref, b_ref, o_ref, *, tiles_per_half):
    x = x_ref[...]
    rank = w_ref.shape[0]
    decay_half = pl.program_id(1) < tiles_per_half
    xs = jnp.where(decay_half, jnp.tanh(x[:, :rank]), x[:, rank:])
    acc = jnp.dot(xs, w_ref[...], preferred_element_type=F32, precision=lax.Precision.HIGHEST)
    o_ref[...] = acc + b_ref[...]


def lora_project(proj_s, w_pair, bias, tm=1024, tn=1024):
    m = proj_s.shape[0]
    _, rank, n = w_pair.shape
    tm = min(tm, m)
    assert m % tm == 0 and n % tn == 0 and 2 * rank == LANES
    half = n // tn
    return pl.pallas_call(
        functools.partial(_lora_kernel, tiles_per_half=half),
        grid=(m // tm, 2 * half),
        in_specs=[
            pl.BlockSpec((tm, LANES), lambda i, j: (i, 0)),
            pl.BlockSpec((None, rank, tn), lambda i, j: (j // half, 0, j % half)),
            pl.BlockSpec((1, tn), lambda i, j: (0, j)),
        ],
        out_specs=pl.BlockSpec((tm, tn), lambda i, j: (i, j)),
        out_shape=jax.ShapeDtypeStruct((m, 2 * n), F32),
        compiler_params=_params(ndim=2),
        name="lora",
    )(proj_s, w_pair, bias)


SCAN_BATCH = 4
SCAN_HEADS = LANES // SCAN_BATCH
NGRP = LANES // SCAN_HEADS


def _segment_transpose(x):
    seg = lax.broadcasted_iota(jnp.int32, x[0].shape, 1) // SCAN_HEADS
    y = []
    for i in range(NGRP):
        out = None
        for j in range(SCAN_BATCH):
            shift = ((j - i) % NGRP) * SCAN_HEADS
            piece = x[j] if shift == 0 else pltpu.roll(x[j], shift, axis=1)
            out = piece if out is None else jnp.where(seg == j, piece, out)
        y.append(out)
    return y


def _to_lanes(x_ref, dst_ref, tc):
    for g in range(HEAD_A // NGRP):
        y = _segment_transpose([x_ref[b, :, g * LANES:(g + 1) * LANES] for b in range(SCAN_BATCH)])
        for n_lo in range(NGRP):
            n = g * NGRP + n_lo
            dst_ref[n * tc:(n + 1) * tc, :] = y[n_lo]


def _from_lanes_gated(src_ref, z_ref, o_ref, tc):
    for g in range(HEAD_A // NGRP):
        y = _segment_transpose([src_ref[(g * NGRP + n_lo) * tc:(g * NGRP + n_lo + 1) * tc, :] for n_lo in range(NGRP)])
        for b in range(SCAN_BATCH):
            z = z_ref[b, :, g * LANES:(g + 1) * LANES]
            o_ref[b, :, g * LANES:(g + 1) * LANES] = (y[b] * (z * jax.nn.sigmoid(z))).astype(o_ref.dtype)


def _scan_kernel(r_ref, k_ref, v_ref, wp_ref, ap_ref, z_ref, kkw_ref, kaw_ref, rkw_ref, lnw_ref, lnb_ref,
                 o_ref, state_ref, set_a, set_b, ans, os, bon, *, tc):
    s = pl.program_id(1)
    srcs = (r_ref, k_ref, v_ref, wp_ref, ap_ref)

    @pl.when(s == 0)
    def _():
        state_ref[...] = jnp.zeros_like(state_ref)
        for src, dst in zip(srcs, set_a):
            _to_lanes(src, dst, tc)

    @pl.when(s % 2 == 1)
    def _():
        _scan_chunk(set_a, set_b, srcs, z_ref, kkw_ref, kaw_ref, rkw_ref, lnw_ref, lnb_ref, o_ref, state_ref,
                    ans, os, bon, tc)

    @pl.when((s % 2 == 0) & (s > 0))
    def _():
        _scan_chunk(set_b, set_a, srcs, z_ref, kkw_ref, kaw_ref, rkw_ref, lnw_ref, lnb_ref, o_ref, state_ref,
                    ans, os, bon, tc)


def _scan_chunk(cur, nxt, srcs, z_ref, kkw_ref, kaw_ref, rkw_ref, lnw_ref, lnb_ref, o_ref, state_ref,
                ans, os, bon, tc):
    n_ch = HEAD_A
    rs, ks, vs, ws, bvs = cur
    n_groups = HEAD_A // NGRP
    groups_per_step = 2
    steps_per_rowblock = n_groups // groups_per_step
    assert tc * groups_per_step == n_groups * (tc // SUBLANES)

    def rows(n):
        return pl.ds(pl.multiple_of(n * tc, tc), tc)

    def norm_acc(n, acc):
        kkr = ks[rows(n), :] * kkw_ref[n]
        return acc + kkr * kkr

    nsq = lax.fori_loop(0, n_ch, norm_acc, jnp.zeros((tc, LANES), F32), unroll=8)
    inv = 1.0 / jnp.maximum(jnp.sqrt(nsq), 1e-12)

    def prep(n, bacc):
        k = ks[rows(n), :]
        a = jax.nn.sigmoid(bvs[rows(n), :])
        kk = k * kkw_ref[n] * inv
        ws[rows(n), :] = jnp.exp(-DECAY_SCALE * jax.nn.sigmoid(ws[rows(n), :]))
        ans[rows(n), :] = -kk
        bvs[rows(n), :] = kk * a
        kmod = k * (1.0 + (a - 1.0) * kaw_ref[n])
        ks[rows(n), :] = kmod
        return bacc + rs[rows(n), :] * kmod * rkw_ref[n]

    bon[...] = lax.fori_loop(0, n_ch, prep, jnp.zeros((tc, LANES), F32), unroll=4)

    def bcast(ref, row):
        return ref[pl.ds(row, 1), :][None]

    def sa_first(j, acc):
        return acc + state_ref[j] * bcast(ans, j * tc)

    slab = (n_ch // SUBLANES, SUBLANES, LANES)
    sa0 = lax.fori_loop(0, n_ch, sa_first, jnp.zeros(slab, F32), unroll=4)

    def step(t, sa):
        v = vs[pl.ds(t, n_ch, stride=tc), :].reshape(slab)
        t_next = jnp.minimum(t + 1, tc - 1)

        out = jnp.zeros(slab, F32)
        sa_next = jnp.zeros(slab, F32)
        for j in range(n_ch):
            row = j * tc + t
            s_new = state_ref[j] * bcast(ws, row) + sa * bcast(bvs, row) + v * bcast(ks, row)
            state_ref[j] = s_new
            out = out + s_new * bcast(rs, row)
            sa_next = sa_next + s_new * bcast(ans, j * tc + t_next)
        o = out.reshape(n_ch, LANES)
        mu = jnp.mean(o, axis=0, keepdims=True)
        d = o - mu
        var = jnp.mean(d * d, axis=0, keepdims=True)
        on = d * lax.rsqrt(var + GN_EPS) * lnw_ref[...] + lnb_ref[...]
        bonus = bon[pl.ds(t, 1), :] * v.reshape(n_ch, LANES)
        os[pl.ds(t, n_ch, stride=tc), :] = on + bonus

        row0 = pl.multiple_of((t // steps_per_rowblock) * SUBLANES, SUBLANES)
        for src, dst in zip(srcs, nxt):
            for gi in range(groups_per_step):
                g = (t % steps_per_rowblock) * groups_per_step + gi
                lane0 = pl.multiple_of(g * LANES, LANES)
                y = _segment_transpose([src[b, pl.ds(row0, SUBLANES), pl.ds(lane0, LANES)] for b in range(SCAN_BATCH)])
                for n_lo in range(NGRP):
                    dst[pl.ds(pl.multiple_of((g * NGRP + n_lo) * tc + row0, SUBLANES), SUBLANES), :] = y[n_lo]
        return sa_next

    lax.fori_loop(0, tc, step, sa0)
    _from_lanes_gated(os, z_ref, o_ref, tc)


def rwkv_scan(proj_a, wa_pre, kkw, kaw, rkw, lnw, lnb, batch, seq, tc=32):
    width = HEAD_A * SCAN_HEADS
    n_chunks = seq // tc
    seq_in = lambda col: pl.BlockSpec((SCAN_BATCH, tc, width), lambda g, s: (g, jnp.minimum(s, n_chunks - 1), col))
    seq_out = lambda col: pl.BlockSpec((SCAN_BATCH, tc, width), lambda g, s: (g, jnp.maximum(s - 1, 0), col))
    par3 = pl.BlockSpec((HEAD_A, 1, LANES), lambda g, s: (0, 0, 0))
    par2 = pl.BlockSpec((HEAD_A, LANES), lambda g, s: (0, 0))
    buf = pltpu.VMEM((HEAD_A * tc, LANES), F32)
    return pl.pallas_call(
        functools.partial(_scan_kernel, tc=tc),
        grid=(batch // SCAN_BATCH, n_chunks + 1),
        in_specs=[seq_in(0), seq_in(1), seq_in(2), seq_in(0), seq_in(1), seq_out(3), par3, par3, par3, par2, par2],
        out_specs=seq_out(0),
        out_shape=jax.ShapeDtypeStruct((batch, seq, width), BF16),
        scratch_shapes=[pltpu.VMEM((HEAD_A, HEAD_A // SUBLANES, SUBLANES, LANES), F32), [buf] * 5, [buf] * 5,
                        buf, buf, pltpu.VMEM((tc, LANES), F32)],
        compiler_params=_params(ndim=2),
        name="rwkv_scan",
    )(proj_a, proj_a, proj_a, wa_pre, wa_pre, proj_a, kkw, kaw, rkw, lnw, lnb)


def _compress_kernel(x_ref, pe_ref, w1_ref, w2_ref, kg_ref, o_ref):
    which = pl.program_id(0)
    half = L_CMP // 2
    nblk = x_ref.shape[0] // CMP_STRIDE
    h1 = jnp.zeros((nblk, w1_ref.shape[-1]), F32)
    h2 = jnp.zeros((nblk, w1_ref.shape[-1]), F32)
    for l in range(half):
        x = x_ref[pl.ds(l, nblk, stride=CMP_STRIDE), :]
        a1 = (x + pe_ref[l:l + 1, :]).astype(BF16)
        a2 = (x + pe_ref[half + l:half + l + 1, :]).astype(BF16)
        h1 = h1 + jnp.dot(a1, w1_ref[l], preferred_element_type=F32)
        h2 = h2 + jnp.dot(a2, w1_ref[half + l], preferred_element_type=F32)
    hid = h1 + pltpu.roll(h2, nblk - 1, axis=0)
    hid = jax.nn.gelu(hid)
    out = jnp.dot(hid.astype(BF16), w2_ref[...], preferred_element_type=F32)
    ms = jnp.mean(out * out, axis=-1, keepdims=True)
    normed = out * lax.rsqrt(ms + NORM_EPS) * kg_ref[...]
    out = jnp.where(which == 0, normed, out)
    row = lax.broadcasted_iota(jnp.int32, out.shape, 0)
    o_ref[...] = jnp.where(row < nblk - 1, out, 0.0)


def compress(proj_b, kc_block, vc_block, pe, w1, w2, kgain, batch, seq):
    nblk = seq // CMP_STRIDE
    hidden = w1.shape[-1]

    def xmap(w, b, g):
        return (b, kc_block + w * (vc_block - kc_block) + g)

    return pl.pallas_call(
        _compress_kernel,
        grid=(2, batch, KV_GROUPS),
        in_specs=[
            pl.BlockSpec((seq, HEAD_B), xmap),
            pl.BlockSpec((None, L_CMP, HEAD_B), lambda w, b, g: (w, 0, 0)),
            pl.BlockSpec((None, L_CMP, HEAD_B, hidden), lambda w, b, g: (w, 0, 0, 0)),
            pl.BlockSpec((None, hidden, HEAD_B), lambda w, b, g: (w, 0, 0)),
            pl.BlockSpec((1, HEAD_B), lambda w, b, g: (0, 0)),
        ],
        out_specs=pl.BlockSpec((None, None, None, nblk, HEAD_B), lambda w, b, g: (w, b, g, 0, 0)),
        out_shape=jax.ShapeDtypeStruct((2, batch, KV_GROUPS, nblk, HEAD_B), F32),
        compiler_params=_params(ndim=3),
        name="nsa_compress",
    )(proj_b, pe, w1, w2, kgain)


def _dot_nt(a, b):
    return lax.dot_general(a, b, (((1,), (1,)), ((), ())), preferred_element_type=F32)


LOG2E = 1.4426950408889634
VAUG = 2 * HEAD_B


def _attn_kernel(q_ref, ks_ref, vs_ref, kw_ref, vw_ref, kc_ref, vc_ref, gt_ref, z_ref, qg_ref, kg_ref,
                 ovt_ref, ex_ref, wb_ref, o_ref, ksn_ref, vs1_ref, kwn_ref, vw1_ref, q_scr, m_scr, acc_scr,
                 oc_scr, ow_scr, *, seq):
    qt = pl.program_id(2)
    ncmp = kc_ref.shape[0]
    nsel = seq // L_SEL

    @pl.when(qt == 0)
    def _():
        ones_col = jnp.ones((seq, HEAD_B), BF16)
        ksn_ref[...] = _rms(ks_ref[...], kg_ref[1:2, :]).astype(BF16)
        vs1_ref[:, 0:HEAD_B] = vs_ref[...].astype(BF16)
        vs1_ref[:, HEAD_B:VAUG] = ones_col
        kwn_ref[0:WINDOW, :] = jnp.zeros((WINDOW, HEAD_B), BF16)
        vw1_ref[0:WINDOW, :] = jnp.zeros((WINDOW, VAUG), BF16)
        kwn_ref[WINDOW:WINDOW + seq, :] = _rms(kw_ref[...], kg_ref[2:3, :]).astype(BF16)
        vw1_ref[WINDOW:WINDOW + seq, 0:HEAD_B] = vw_ref[...].astype(BF16)
        vw1_ref[WINDOW:WINDOW + seq, HEAD_B:VAUG] = ones_col

    t0 = pl.multiple_of(qt * TQ, TQ)
    for r in range(REP):
        q_scr[r] = (_rms(q_ref[:, r * HEAD_B:(r + 1) * HEAD_B], qg_ref[...]) * (ATTN_SCALE * LOG2E)).astype(BF16)

    def t_of(shape):
        return t0 + lax.broadcasted_iota(jnp.int32, shape, 0)

    n_idx = lax.broadcasted_iota(jnp.int32, (TQ, ncmp), 1)
    bias_c = jnp.where(n_idx * CMP_STRIDE + (L_CMP - 1) <= t_of((TQ, ncmp)), 0.0, NEG_INF)
    bias_c = jnp.where(n_idx < ncmp - 1, bias_c, NEG_INF)
    row_ok = jnp.where(t_of((TQ, 1)) >= L_CMP - 1, 1.0, 0.0)
    q_all = q_scr[...].reshape(REP * TQ, HEAD_B)
    head = lambda x, r: x[r * TQ:(r + 1) * TQ]
    s_all = _dot_nt(q_all, kc_ref[...].astype(BF16))
    prs = []
    for r in range(REP):
        s = head(s_all, r) + bias_c
        p = jnp.exp2(s - jnp.max(s, axis=-1, keepdims=True))
        prs.append(p * (row_ok / jnp.sum(p, axis=-1, keepdims=True)))
    psum = prs[0] + prs[1] + prs[2] + prs[3]
    oc_scr[...] = jnp.dot(jnp.concatenate(prs, axis=0).astype(BF16), vc_ref[...].astype(BF16),
                          preferred_element_type=F32).reshape(REP, TQ, HEAD_B)
    imp_t = lax.dot_general(ovt_ref[...], psum, (((1,), (1,)), ((), ())), preferred_element_type=F32,
                            precision=lax.Precision.HIGHEST)

    j_idx = lax.broadcasted_iota(jnp.int32, (nsel, TQ), 0)
    t_sel = t0 + lax.broadcasted_iota(jnp.int32, (nsel, TQ), 1)
    cur = t_sel // L_SEL
    forced = jnp.where((j_idx == 0) | (j_idx == cur) | (j_idx == cur - 1), FORCE_BONUS, 0.0)
    score = jnp.where(j_idx * L_SEL <= t_sel, imp_t + forced, NEG_INF)
    rank = jnp.zeros((nsel, TQ), F32)
    for i in range(nsel):
        si = score[i:i + 1, :]
        ahead = (si > score) | ((si == score) & (j_idx > i))
        rank = rank + jnp.where(ahead, 1.0, 0.0)
    sel_t = jnp.where((rank < min(N_SEL, nsel)) & (score > 0.5 * NEG_INF), 1.0, 0.0)
    sel = jnp.concatenate([sel_t, jnp.zeros((LANES - nsel, TQ), F32)], axis=0).T
    not_sel = (1.0 - sel).astype(BF16)

    span = WINDOW + TQ
    kwin = kwn_ref[pl.ds(t0, span), :]
    vwin = vw1_ref[pl.ds(t0, span), :]
    lane_w = lax.broadcasted_iota(jnp.int32, (TQ, span), 1)
    bias_w = jnp.where(lane_w >= WINDOW - t0, wb_ref[...], NEG_INF)
    s_all = _dot_nt(q_all, kwin)
    ps = []
    for r in range(REP):
        s = head(s_all, r) + bias_w
        ps.append(jnp.exp2(s - jnp.max(s, axis=-1, keepdims=True)).astype(BF16))
    ow_scr[...] = jnp.dot(jnp.concatenate(ps, axis=0), vwin, preferred_element_type=F32)

    m_scr[...] = jnp.full(m_scr.shape, NEG_INF, F32)
    acc_scr[...] = jnp.zeros(acc_scr.shape, F32)

    def scores(c):
        k0 = pl.multiple_of(c * KC, KC)
        return _dot_nt(q_scr[...].reshape(REP * TQ, HEAD_B), ksn_ref[pl.ds(k0, KC), :])

    def sel_bias(c):
        return jnp.dot(not_sel, ex_ref[c], preferred_element_type=F32)

    def sel_chunk(c, s_all, bias, diagonal):
        k0 = pl.multiple_of(c * KC, KC)
        v1 = vs1_ref[pl.ds(k0, KC), :]
        if diagonal:
            lane = lax.broadcasted_iota(jnp.int32, (TQ, KC), 1)
            bias = jnp.where(k0 + lane <= t_of((TQ, KC)), bias, NEG_INF)
        ps, alphas = [], []
        for r in range(REP):
            s = head(s_all, r) + bias
            m_old = m_scr[r]
            m_new = jnp.maximum(m_old, jnp.max(s, axis=-1, keepdims=True))
            m_scr[r] = m_new
            ps.append(jnp.exp2(s - jnp.tile(m_new, (1, KC // LANES))).astype(BF16))
            alphas.append(jnp.exp2(m_old - m_new))
        pv = jnp.dot(jnp.concatenate(ps, axis=0), v1, preferred_element_type=F32)
        for r in range(REP):
            acc_scr[r] = jnp.tile(alphas[r], (1, VAUG // LANES)) * acc_scr[r] + head(pv, r)

    n_full = t0 // KC

    def full_chunk(c, carry):
        s_cur, bias_cur = carry
        nxt = (scores(c + 1), sel_bias(c + 1))
        sel_chunk(c, s_cur, bias_cur, False)
        return nxt

    s_last, bias_last = lax.fori_loop(0, n_full, full_chunk, (scores(0), sel_bias(0)))
    sel_chunk(n_full, s_last, bias_last, True)

    gts = jax.nn.sigmoid(gt_ref[...])
    for r in range(REP):
        ow = ow_scr[r * TQ:(r + 1) * TQ, :]
        o_w = ow[:, 0:HEAD_B] / ow[:, HEAD_B:VAUG]
        acc = acc_scr[r]
        o_s = acc[:, 0:HEAD_B] / acc[:, HEAD_B:VAUG]
        g0 = N_BRANCH * r
        o = gts[:, g0:g0 + 1] * oc_scr[r] + gts[:, g0 + 1:g0 + 2] * o_s + gts[:, g0 + 2:g0 + 3] * o_w
        z = z_ref[:, r * HEAD_B:(r + 1) * HEAD_B]
        o_ref[:, r * HEAD_B:(r + 1) * HEAD_B] = (o * (z * jax.nn.sigmoid(z))).astype(o_ref.dtype)


def nsa_attention(proj_b, proj_z, cmp_kv, gates, q_gain, k_gain, overlap_t, expand, win_bias, blocks, batch, seq):
    nq = seq // TQ
    ncmp = seq // CMP_STRIDE
    gw = REP * HEAD_B // LANES
    qspec = lambda off: pl.BlockSpec((TQ, REP * HEAD_B), lambda b, g, t: (b * nq + t, off // gw + g))
    kvspec = lambda off: pl.BlockSpec((seq, HEAD_B), lambda b, g, t: (b, off + g))
    cspec = lambda w: pl.BlockSpec((None, None, None, ncmp, HEAD_B), lambda b, g, t: (w, b, g, 0, 0))
    full = lambda a: pl.BlockSpec(a.shape, lambda b, g, t: (0,) * a.ndim)
    return pl.pallas_call(
        functools.partial(_attn_kernel, seq=seq),
        grid=(batch, KV_GROUPS, nq),
        in_specs=[
            qspec(blocks["q"]), kvspec(blocks["ks"]), kvspec(blocks["vs"]), kvspec(blocks["kw"]), kvspec(blocks["vw"]),
            cspec(0), cspec(1),
            pl.BlockSpec((None, None, TQ, GATE_PAD), lambda b, g, t: (b, g, t, 0)),
            qspec(blocks["z"]),
            full(q_gain), full(k_gain), full(overlap_t), full(expand), full(win_bias),
        ],
        out_specs=pl.BlockSpec((TQ, REP * HEAD_B), lambda b, g, t: (b * nq + t, g)),
        out_shape=jax.ShapeDtypeStruct((batch * seq, KV_GROUPS * REP * HEAD_B), BF16),
        scratch_shapes=[
            pltpu.VMEM((seq, HEAD_B), BF16), pltpu.VMEM((seq, VAUG), BF16),
            pltpu.VMEM((seq + WINDOW, HEAD_B), BF16), pltpu.VMEM((seq + WINDOW, VAUG), BF16),
            pltpu.VMEM((REP, TQ, HEAD_B), BF16), pltpu.VMEM((REP, TQ, LANES), F32),
            pltpu.VMEM((REP, TQ, VAUG), F32), pltpu.VMEM((REP, TQ, HEAD_B), F32),
            pltpu.VMEM((REP * TQ, VAUG), F32),
        ],
        compiler_params=_params(ndim=3),
        name="nsa_attention",
    )(proj_b, proj_b, proj_b, proj_b, proj_b, cmp_kv, cmp_kv, gates, proj_z, q_gain, k_gain, overlap_t, expand,
      win_bias)


def _merge_kernel(ya_ref, yb_ref, wa_ref, wb_ref, ga_ref, gb_ref, out_ref):
    ua = jnp.dot(ya_ref[...], wa_ref[...], preferred_element_type=F32)
    ub = jnp.dot(yb_ref[...], wb_ref[...], preferred_element_type=F32)
    out_ref[...] = (jax.nn.sigmoid(ga_ref[...]) * ua + jax.nn.sigmoid(gb_ref[...]) * ub).astype(out_ref.dtype)


def merge(ya, yb, w_a, w_b, proj_b, gate_col, tm=1024, tn=512):
    m, ka = ya.shape
    n = w_a.shape[1]
    tm = min(tm, m)
    assert gate_col % tn == 0 and n % tn == 0
    ga_block, gb_block = gate_col // tn, (gate_col + n) // tn
    return pl.pallas_call(
        _merge_kernel,
        grid=(m // tm, n // tn),
        in_specs=[
            pl.BlockSpec((tm, ka), lambda i, j: (i, 0)),
            pl.BlockSpec((tm, ka), lambda i, j: (i, 0)),
            pl.BlockSpec((ka, tn), lambda i, j: (0, j)),
            pl.BlockSpec((ka, tn), lambda i, j: (0, j)),
            pl.BlockSpec((tm, tn), lambda i, j: (i, ga_block + j)),
            pl.BlockSpec((tm, tn), lambda i, j: (i, gb_block + j)),
        ],
        out_specs=pl.BlockSpec((tm, tn), lambda i, j: (i, j)),
        out_shape=jax.ShapeDtypeStruct((m, n), BF16),
        compiler_params=_params(ndim=2),
        name="merge",
    )(ya, yb, w_a, w_b, proj_b, proj_b)


def _ple_embed_kernel(p_ref, w_ref, g_ref, o_ref):
    acc = jnp.dot(p_ref[...].astype(BF16), w_ref[...], preferred_element_type=F32)
    o_ref[...] = _rms(acc, g_ref[...]).astype(o_ref.dtype)


def ple_embed(p, w, g, tm=256):
    m, k = p.shape
    n = w.shape[1]
    return pl.pallas_call(
        _ple_embed_kernel,
        grid=(m // tm,),
        in_specs=[pl.BlockSpec((tm, k), lambda i: (i, 0)), pl.BlockSpec((k, n), lambda i: (0, 0)),
                  pl.BlockSpec((1, n), lambda i: (0, 0))],
        out_specs=pl.BlockSpec((tm, n), lambda i: (i, 0)),
        out_shape=jax.ShapeDtypeStruct((m, n), F32),
        compiler_params=_params(),
        name="ple_embed",
    )(p, w, g.reshape(1, n))


def head_minor(w, heads):
    lead = w.shape[:-1]
    return w.reshape(*lead, heads, HEAD_A).swapaxes(-1, -2).reshape(*lead, heads * HEAD_A)


def rwkv_branch(proj_a, proj_s, batch, seq, a_width, w_lora_up, w0, a_lora_up, a0, k_k, k_a, r_k, lnx_w, lnx_b):
    tokens = batch * seq
    heads_a = a_width // HEAD_A
    assert heads_a == SCAN_HEADS and batch % SCAN_BATCH == 0
    lora = w_lora_up.shape[0]
    hm = lambda w: head_minor(w, heads_a)
    w_lora = jnp.stack([hm(w_lora_up), hm(a_lora_up)])
    b_lora = jnp.concatenate([hm(w0), hm(a0)]).reshape(1, -1)
    wa_pre = lora_project(proj_s, w_lora, b_lora)

    def par_scan(t):
        return jnp.tile(t.reshape(heads_a, HEAD_A).T, (1, SCAN_BATCH))

    par3 = lambda t: par_scan(t).reshape(HEAD_A, 1, LANES)
    o = rwkv_scan(proj_a.reshape(batch, seq, -1), wa_pre.reshape(batch, seq, -1),
                  par3(k_k), par3(k_a), par3(r_k), par_scan(lnx_w), par_scan(lnx_b), batch, seq)
    return o.reshape(tokens, a_width)


def nsa_branch(proj_b, proj_z, graw, batch, seq, b_width, q_norm_g, k_norm_g, pe_cmp_k, pe_cmp_v,
               cmp_k_w1, cmp_k_w2, cmp_v_w1, cmp_v_w2):
    kv_width = KV_GROUPS * HEAD_B
    blk = lambda cols: cols // LANES
    blocks = {"q": 0, "kc": blk(b_width), "vc": blk(b_width + kv_width), "ks": blk(b_width + 2 * kv_width),
              "vs": blk(b_width + 3 * kv_width), "kw": blk(b_width + 4 * kv_width), "vw": blk(b_width + 5 * kv_width),
              "z": 0}
    pe = jnp.stack([pe_cmp_k, pe_cmp_v])
    hidden = cmp_k_w1.shape[1]
    w1 = jnp.stack([cmp_k_w1, cmp_v_w1]).reshape(2, L_CMP, HEAD_B, hidden).astype(BF16)
    w2 = jnp.stack([cmp_k_w2, cmp_v_w2]).astype(BF16)
    cmp_kv = compress(proj_b, blocks["kc"], blocks["vc"], pe, w1, w2, k_norm_g[0:1], batch, seq)

    ncmp = seq // CMP_STRIDE
    nsel = seq // L_SEL
    c_start = jnp.arange(ncmp) * CMP_STRIDE
    s_start = jnp.arange(nsel) * L_SEL
    overlap_t = ((c_start[None, :] < (s_start + L_SEL)[:, None]) & (s_start[:, None] < (c_start + L_CMP)[None, :])
                 & (jnp.arange(ncmp)[None, :] < ncmp - 1)).astype(F32)
    key_blk = jnp.arange(seq) // L_SEL
    expand = jnp.where(jnp.arange(LANES)[:, None] == key_blk[None, :], NEG_INF, 0.0).astype(BF16)
    expand = expand.reshape(LANES, seq // KC, KC).transpose(1, 0, 2)
    tl = jnp.arange(TQ)[:, None]
    u = jnp.arange(WINDOW + TQ)[None, :]
    win_bias = jnp.where((u > tl) & (u <= WINDOW + tl), 0.0, NEG_INF).astype(F32)
    gates = graw.reshape(batch, seq, KV_GROUPS, REP * N_BRANCH).transpose(0, 2, 1, 3)
    gates = jnp.pad(gates, ((0, 0), (0, 0), (0, 0), (0, GATE_PAD - REP * N_BRANCH)))
    return nsa_attention(proj_b, proj_z, cmp_kv, gates, q_norm_g.reshape(1, HEAD_B), k_norm_g, overlap_t, expand,
                         win_bias, blocks, batch, seq)


def _layer(x, p, norm_g, w_in, shift_mu, w_lora_up, w0, a_lora_up, a0, k_k, k_a, r_k, lnx_w, lnx_b,
           q_norm_g, k_norm_g, pe_cmp_k, pe_cmp_v, cmp_k_w1, cmp_k_w2, cmp_v_w1, cmp_v_w2,
           w_up_a, w_up_b, w_out, ple_pre_g, w_ple_gate, w_ple, ple_post_g):
    batch, seq, d = x.shape
    tokens = batch * seq
    a_width = w_up_a.shape[0]
    b_width = w_up_b.shape[0]
    heads_a = a_width // HEAD_A
    kv_width = KV_GROUPS * HEAD_B
    n_gate = KV_GROUPS * REP * N_BRANCH
    lora = w_lora_up.shape[0]
    assert lora == HEAD_A and a_lora_up.shape[0] == HEAD_A and 2 * lora == LANES
    a_cols = 4 * a_width + 2 * lora
    g_off = a_cols + b_width + 6 * kv_width
    zb_off = g_off + n_gate
    assert w_in.shape[1] == zb_off + b_width + 2 * d

    hm4 = lambda w: head_minor(w.reshape(*w.shape[:-1], 4, a_width), heads_a).reshape(*w.shape[:-1], 4 * a_width)
    w_a = hm4(w_in[:, :4 * a_width]).astype(BF16)
    mu_a = hm4(shift_mu[:4 * a_width]).reshape(1, -1)
    pad_s = LANES - n_gate
    w_s = jnp.concatenate([w_in[:, 4 * a_width:a_cols], w_in[:, g_off:zb_off], jnp.zeros((d, pad_s), F32)],
                          axis=1).astype(BF16)
    mu_s = jnp.concatenate([shift_mu[4 * a_width:], jnp.zeros((LANES,), F32)]).reshape(1, -1)
    w_b = w_in[:, a_cols:g_off].astype(BF16)
    w_z = w_in[:, zb_off:].astype(BF16)

    x2 = x.reshape(tokens, d)
    h = rmsnorm_rows(x2, norm_g, BF16)
    tm = 512
    shift = dict(mode="shift", tm=tm, tiles_per_seq=seq // tm)
    proj_a = matmul(h, w_a, extras=(mu_a,), tn=1024, name="proj_a", **shift)
    proj_s = matmul(h, w_s, extras=(mu_s,), tn=w_s.shape[1], name="proj_s", **shift)
    proj_b = matmul(h, w_b, tm=tm, tn=1024, name="proj_b")
    proj_z = matmul(h, w_z, tm=tm, tn=1024, name="proj_z")

    ya = rwkv_branch(proj_a, proj_s, batch, seq, a_width, w_lora_up, w0, a_lora_up, a0, k_k, k_a, r_k, lnx_w, lnx_b)
    graw = proj_s[:, 2 * lora:2 * lora + n_gate]
    yb = nsa_branch(proj_b, proj_z, graw, batch, seq, b_width, q_norm_g, k_norm_g, pe_cmp_k, pe_cmp_v,
                    cmp_k_w1, cmp_k_w2, cmp_v_w1, cmp_v_w2)

    w_up_a_nm = head_minor(w_up_a.T, heads_a).T
    merged = merge(ya, yb, w_up_a_nm.astype(BF16), w_up_b.astype(BF16), proj_z, b_width)
    x1 = matmul(merged, w_out.astype(BF16), mode="resid", extras=(x2,), tm=tm, tn=1024, name="out_proj")
    e = ple_embed(p.reshape(tokens, -1), w_ple.astype(BF16), ple_post_g)
    out = matmul(x1, w_ple_gate.astype(BF16), mode="ple", extras=(x1, e), norm_gain=ple_pre_g, tm=tm, tn=1024,
                 name="ple_gate")
    return out.reshape(batch, seq, d)


def kernel(x, p, norm_g, w_in, shift_mu, w_lora_up, w0, a_lora_up, a0, k_k, k_a, r_k, lnx_w, lnx_b, q_norm_g, k_norm_g, pe_cmp_k, pe_cmp_v, cmp_k_w1, cmp_k_w2, cmp_v_w1, cmp_v_w2, w_up_a, w_up_b, w_out, ple_pre_g, w_ple_gate, w_ple, ple_post_g):
    depth = w_in.shape[0]
    for i in range(depth):
        x = _layer(x, p[i], norm_g[i], w_in[i], shift_mu[i], w_lora_up[i], w0[i], a_lora_up[i], a0[i], k_k[i],
                   k_a[i], r_k[i], lnx_w[i], lnx_b[i], q_norm_g[i], k_norm_g[i], pe_cmp_k[i], pe_cmp_v[i],
                   cmp_k_w1[i], cmp_k_w2[i], cmp_v_w1[i], cmp_v_w2[i], w_up_a[i], w_up_b[i], w_out[i],
                   ple_pre_g[i], w_ple_gate[i], w_ple[i], ple_post_g[i])
    return x
```

```python
import functools

import jax
import jax.numpy as jnp
from jax import lax
from jax.experimental import pallas as pl
from jax.experimental.pallas import tpu as pltpu

F32 = jnp.float32
BF16 = jnp.bfloat16

LANES = 128
SUBLANES = 8
VMEM_LIMIT = 56 * 1024 * 1024

NORM_EPS = 1e-6
NEG_INF = -1e30
HEAD_A = 64
GN_EPS = 64e-5
HEAD_B = 128
KV_GROUPS = 4
REP = 4
N_BRANCH = 3
GATE_PAD = 16
L_CMP = 32
CMP_STRIDE = 16
L_SEL = 64
N_SEL = 16
WINDOW = 512
TQ = 256
KC = 256
FORCE_BONUS = 1e3
ATTN_SCALE = HEAD_B ** -0.5
DECAY_SCALE = 0.6065306597126334


def _params(vmem=VMEM_LIMIT, ndim=1):
    return pltpu.CompilerParams(dimension_semantics=("arbitrary",) * ndim, vmem_limit_bytes=vmem)


def _rms(x, gain):
    ms = jnp.mean(x * x, axis=-1, keepdims=True)
    return x * lax.rsqrt(ms + NORM_EPS) * gain


def _rmsnorm_kernel(x_ref, g_ref, o_ref):
    o_ref[...] = _rms(x_ref[...], g_ref[...]).astype(o_ref.dtype)


def rmsnorm_rows(x, g, out_dtype, tm=256):
    m, d = x.shape
    return pl.pallas_call(
        _rmsnorm_kernel,
        grid=(m // tm,),
        in_specs=[pl.BlockSpec((tm, d), lambda i: (i, 0)), pl.BlockSpec((1, d), lambda i: (0, 0))],
        out_specs=pl.BlockSpec((tm, d), lambda i: (i, 0)),
        out_shape=jax.ShapeDtypeStruct((m, d), out_dtype),
        compiler_params=_params(),
        name="rmsnorm",
    )(x, g.reshape(1, d))


def _mm_kernel(*refs, mode, tiles_per_seq, norm_a):
    a_ref, b_ref = refs[0], refs[1]
    if norm_a:
        x = a_ref[...]
        acc = jnp.dot((x * refs[2][...]).astype(BF16), b_ref[...], preferred_element_type=F32)
        acc = acc * lax.rsqrt(jnp.mean(x * x, axis=-1, keepdims=True) + NORM_EPS)
        refs = refs[:2] + refs[3:]
    else:
        acc = jnp.dot(a_ref[...], b_ref[...], preferred_element_type=F32)
    if mode == "plain":
        o_ref = refs[2]
        o_ref[...] = acc.astype(o_ref.dtype)
    elif mode == "shift":
        mu_ref, o_ref, carry_ref = refs[2], refs[3], refs[4]
        i = pl.program_id(1)
        tm = acc.shape[0]
        first = (i % tiles_per_seq) == 0
        last_prev = jnp.where(first, 0.0, carry_ref[SUBLANES - 1:SUBLANES, :])
        rolled = pltpu.roll(acc, 1, axis=0)
        row = lax.broadcasted_iota(jnp.int32, acc.shape, 0)
        prev = jnp.where(row == 0, last_prev, rolled)
        carry_ref[...] = acc[tm - SUBLANES:tm, :]
        o_ref[...] = (acc + mu_ref[...] * (prev - acc)).astype(o_ref.dtype)
    elif mode == "resid":
        r_ref, o_ref = refs[2], refs[3]
        o_ref[...] = (r_ref[...] + acc).astype(o_ref.dtype)
    elif mode == "ple":
        x_ref, e_ref, o_ref = refs[2], refs[3], refs[4]
        o_ref[...] = (x_ref[...] + jax.nn.sigmoid(acc) * e_ref[...]).astype(o_ref.dtype)
    else:
        raise ValueError(mode)


def matmul(a, b, *, mode="plain", extras=(), norm_gain=None, out_dtype=F32, tm=512, tn=1024, tiles_per_seq=1,
           name="mm"):
    m, k = a.shape
    n = b.shape[1]
    assert m % tm == 0 and n % tn == 0, (m, n, tm, tn)
    in_specs = [pl.BlockSpec((tm, k), lambda j, i: (i, 0)), pl.BlockSpec((k, tn), lambda j, i: (0, j))]
    if norm_gain is not None:
        in_specs.append(pl.BlockSpec((1, k), lambda j, i: (0, 0)))
        extras = (norm_gain.reshape(1, k),) + tuple(extras)
    scratch = []
    if mode == "shift":
        in_specs.append(pl.BlockSpec((1, tn), lambda j, i: (0, j)))
        scratch.append(pltpu.VMEM((SUBLANES, tn), F32))
    elif mode == "resid":
        in_specs.append(pl.BlockSpec((tm, tn), lambda j, i: (i, j)))
    elif mode == "ple":
        in_specs += [pl.BlockSpec((tm, tn), lambda j, i: (i, j)), pl.BlockSpec((tm, tn), lambda j, i: (i, j))]
    return pl.pallas_call(
        functools.partial(_mm_kernel, mode=mode, tiles_per_seq=tiles_per_seq, norm_a=norm_gain is not None),
        grid=(n // tn, m // tm),
        in_specs=in_specs,
        out_specs=pl.BlockSpec((tm, tn), lambda j, i: (i, j)),
        out_shape=jax.ShapeDtypeStruct((m, n), out_dtype),
        scratch_shapes=scratch,
        compiler_params=_params(ndim=2),
        name=name,
    )(a, b, *extras)


def _lora_kernel(x_ref, w_ref, b_ref, o_ref, *, tiles_per_half):
    x = x_ref[...]
    rank = w_ref.shape[0]
    decay_half = pl.program_id(1) < tiles_per_half
    xs = jnp.where(decay_half, jnp.tanh(x[:, :rank]), x[:, rank:])
    acc = jnp.dot(xs, w_ref[...], preferred_element_type=F32, precision=lax.Precision.HIGHEST)
    o_ref[...] = acc + b_ref[...]


def lora_project(proj_s, w_pair, bias, tm=1024, tn=1024):
    m = proj_s.shape[0]
    _, rank, n = w_pair.shape
    tm = min(tm, m)
    assert m % tm == 0 and n % tn == 0 and 2 * rank == LANES
    half = n // tn
    return pl.pallas_call(
        functools.partial(_lora_kernel, tiles_per_half=half),
        grid=(m // tm, 2 * half),
        in_specs=[
            pl.BlockSpec((tm, LANES), lambda i, j: (i, 0)),
            pl.BlockSpec((None, rank, tn), lambda i, j: (j // half, 0, j % half)),
            pl.BlockSpec((1, tn), lambda i, j: (0, j)),
        ],
        out_specs=pl.BlockSpec((tm, tn), lambda i, j: (i, j)),
        out_shape=jax.ShapeDtypeStruct((m, 2 * n), F32),
        compiler_params=_params(ndim=2),
        name="lora",
    )(proj_s, w_pair, bias)


SCAN_BATCH = 4
SCAN_HEADS = LANES // SCAN_BATCH
NGRP = LANES // SCAN_HEADS


def _segment_transpose(x):
    seg = lax.broadcasted_iota(jnp.int32, x[0].shape, 1) // SCAN_HEADS
    y = []
    for i in range(NGRP):
        out = None
        for j in range(SCAN_BATCH):
            shift = ((j - i) % NGRP) * SCAN_HEADS
            piece = x[j] if shift == 0 else pltpu.roll(x[j], shift, axis=1)
            out = piece if out is None else jnp.where(seg == j, piece, out)
        y.append(out)
    return y


def _to_lanes(x_ref, dst_ref, tc):
    for g in range(HEAD_A // NGRP):
        y = _segment_transpose([x_ref[b, :, g * LANES:(g + 1) * LANES] for b in range(SCAN_BATCH)])
        for n_lo in range(NGRP):
            n = g * NGRP + n_lo
            dst_ref[n * tc:(n + 1) * tc, :] = y[n_lo]


def _from_lanes_gated(src_ref, z_ref, o_ref, tc):
    for g in range(HEAD_A // NGRP):
        y = _segment_transpose([src_ref[(g * NGRP + n_lo) * tc:(g * NGRP + n_lo + 1) * tc, :] for n_lo in range(NGRP)])
        for b in range(SCAN_BATCH):
            z = z_ref[b, :, g * LANES:(g + 1) * LANES]
            o_ref[b, :, g * LANES:(g + 1) * LANES] = (y[b] * (z * jax.nn.sigmoid(z))).astype(o_ref.dtype)


def _scan_kernel(r_ref, k_ref, v_ref, wp_ref, ap_ref, z_ref, kkw_ref, kaw_ref, rkw_ref, lnw_ref, lnb_ref,
                 o_ref, state_ref, set_a, set_b, ans, os, bon, gend, *, tc):
    s = pl.program_id(1)
    srcs = (r_ref, k_ref, v_ref, wp_ref, ap_ref)

    @pl.when(s == 0)
    def _():
        state_ref[...] = jnp.zeros_like(state_ref)
        for src, dst in zip(srcs, set_a):
            _to_lanes(src, dst, tc)

    @pl.when(s % 2 == 1)
    def _():
        _scan_chunk(set_a, set_b, srcs, z_ref, kkw_ref, kaw_ref, rkw_ref, lnw_ref, lnb_ref, o_ref, state_ref,
                    ans, os, bon, gend, tc)

    @pl.when((s % 2 == 0) & (s > 0))
    def _():
        _scan_chunk(set_b, set_a, srcs, z_ref, kkw_ref, kaw_ref, rkw_ref, lnw_ref, lnb_ref, o_ref, state_ref,
                    ans, os, bon, gend, tc)


def _scan_chunk(cur, nxt, srcs, z_ref, kkw_ref, kaw_ref, rkw_ref, lnw_ref, lnb_ref, o_ref, state_ref,
                ans, os, bon, gend, tc):
    n_ch = HEAD_A
    rs, ks, vs, ws, bvs = cur
    n_groups = HEAD_A // NGRP
    groups_per_step = 2
    steps_per_rowblock = n_groups // groups_per_step
    assert tc * groups_per_step == n_groups * (tc // SUBLANES)

    def rows(n):
        return pl.ds(pl.multiple_of(n * tc, tc), tc)

    def norm_acc(n, acc):
        kkr = ks[rows(n), :] * kkw_ref[n]
        return acc + kkr * kkr

    nsq = lax.fori_loop(0, n_ch, norm_acc, jnp.zeros((tc, LANES), F32), unroll=8)
    inv = 1.0 / jnp.maximum(jnp.sqrt(nsq), 1e-12)

    tri = jnp.where(lax.broadcasted_iota(jnp.int32, (tc, tc), 0) >= lax.broadcasted_iota(jnp.int32, (tc, tc), 1),
                    1.0, 0.0)

    def prep(n, bacc):
        k = ks[rows(n), :]
        a = jax.nn.sigmoid(bvs[rows(n), :])
        kk = k * kkw_ref[n] * inv
        logw = -DECAY_SCALE * jax.nn.sigmoid(ws[rows(n), :])
        cum = jnp.dot(tri, logw, preferred_element_type=F32, precision=lax.Precision.HIGHEST)
        g = jnp.exp(cum)
        ig = jnp.exp(-cum)
        kmod = k * (1.0 + (a - 1.0) * kaw_ref[n])
        r = rs[rows(n), :]
        rs[rows(n), :] = r * g
        ks[rows(n), :] = kmod * ig
        bvs[rows(n), :] = kk * a * ig
        ans[rows(n), :] = -kk * jnp.exp(cum - logw)
        gend[pl.ds(n, 1), :] = g[tc - 1:tc, :]
        return bacc + r * kmod * rkw_ref[n]

    bon[...] = lax.fori_loop(0, n_ch, prep, jnp.zeros((tc, LANES), F32), unroll=16)

    def bcast(ref, row):
        return ref[pl.ds(row, 1), :][None]

    def sa_first(j, acc):
        return acc + state_ref[j] * bcast(ans, j * tc)

    slab = (n_ch // SUBLANES, SUBLANES, LANES)
    sa0 = lax.fori_loop(0, n_ch, sa_first, jnp.zeros(slab, F32), unroll=4)

    def step(t, sa):
        v = vs[pl.ds(t, n_ch, stride=tc), :].reshape(slab)
        t_next = jnp.minimum(t + 1, tc - 1)

        out = jnp.zeros(slab, F32)
        sa_next = jnp.zeros(slab, F32)
        for j in range(n_ch):
            row = j * tc + t
            s_new = state_ref[j] + sa * bcast(bvs, row) + v * bcast(ks, row)
            state_ref[j] = s_new
            out = out + s_new * bcast(rs, row)
            sa_next = sa_next + s_new * bcast(ans, j * tc + t_next)
        o = out.reshape(n_ch, LANES)
        mu = jnp.mean(o, axis=0, keepdims=True)
        d = o - mu
        var = jnp.mean(d * d, axis=0, keepdims=True)
        on = d * lax.rsqrt(var + GN_EPS) * lnw_ref[...] + lnb_ref[...]
        bonus = bon[pl.ds(t, 1), :] * v.reshape(n_ch, LANES)
        os[pl.ds(t, n_ch, stride=tc), :] = on + bonus

        row0 = pl.multiple_of((t // steps_per_rowblock) * SUBLANES, SUBLANES)
        for src, dst in zip(srcs, nxt):
            for gi in range(groups_per_step):
                g = (t % steps_per_rowblock) * groups_per_step + gi
                lane0 = pl.multiple_of(g * LANES, LANES)
                y = _segment_transpose([src[b, pl.ds(row0, SUBLANES), pl.ds(lane0, LANES)] for b in range(SCAN_BATCH)])
                for n_lo in range(NGRP):
                    dst[pl.ds(pl.multiple_of((g * NGRP + n_lo) * tc + row0, SUBLANES), SUBLANES), :] = y[n_lo]
        return sa_next

    lax.fori_loop(0, tc, step, sa0)

    def rescale(j, c):
        state_ref[j] = state_ref[j] * bcast(gend, j)
        return c

    lax.fori_loop(0, n_ch, rescale, 0, unroll=8)
    _from_lanes_gated(os, z_ref, o_ref, tc)


def rwkv_scan(proj_a, wa_pre, kkw, kaw, rkw, lnw, lnb, batch, seq, tc=32):
    width = HEAD_A * SCAN_HEADS
    n_chunks = seq // tc
    seq_in = lambda col: pl.BlockSpec((SCAN_BATCH, tc, width), lambda g, s: (g, jnp.minimum(s, n_chunks - 1), col))
    seq_out = lambda col: pl.BlockSpec((SCAN_BATCH, tc, width), lambda g, s: (g, jnp.maximum(s - 1, 0), col))
    par3 = pl.BlockSpec((HEAD_A, 1, LANES), lambda g, s: (0, 0, 0))
    par2 = pl.BlockSpec((HEAD_A, LANES), lambda g, s: (0, 0))
    buf = pltpu.VMEM((HEAD_A * tc, LANES), F32)
    return pl.pallas_call(
        functools.partial(_scan_kernel, tc=tc),
        grid=(batch // SCAN_BATCH, n_chunks + 1),
        in_specs=[seq_in(0), seq_in(1), seq_in(2), seq_in(0), seq_in(1), seq_out(3), par3, par3, par3, par2, par2],
        out_specs=seq_out(0),
        out_shape=jax.ShapeDtypeStruct((batch, seq, width), BF16),
        scratch_shapes=[pltpu.VMEM((HEAD_A, HEAD_A // SUBLANES, SUBLANES, LANES), F32), [buf] * 5, [buf] * 5,
                        buf, buf, pltpu.VMEM((tc, LANES), F32), pltpu.VMEM((HEAD_A, LANES), F32)],
        compiler_params=_params(ndim=2),
        name="rwkv_scan",
    )(proj_a, proj_a, proj_a, wa_pre, wa_pre, proj_a, kkw, kaw, rkw, lnw, lnb)


def _compress_kernel(x_ref, pe_ref, w1_ref, w2_ref, kg_ref, o_ref):
    which = pl.program_id(0)
    half = L_CMP // 2
    nblk = x_ref.shape[0] // CMP_STRIDE
    h1 = jnp.zeros((nblk, w1_ref.shape[-1]), F32)
    h2 = jnp.zeros((nblk, w1_ref.shape[-1]), F32)
    for l in range(half):
        x = x_ref[pl.ds(l, nblk, stride=CMP_STRIDE), :]
        a1 = (x + pe_ref[l:l + 1, :]).astype(BF16)
        a2 = (x + pe_ref[half + l:half + l + 1, :]).astype(BF16)
        h1 = h1 + jnp.dot(a1, w1_ref[l], preferred_element_type=F32)
        h2 = h2 + jnp.dot(a2, w1_ref[half + l], preferred_element_type=F32)
    hid = h1 + pltpu.roll(h2, nblk - 1, axis=0)
    hid = jax.nn.gelu(hid)
    out = jnp.dot(hid.astype(BF16), w2_ref[...], preferred_element_type=F32)
    ms = jnp.mean(out * out, axis=-1, keepdims=True)
    normed = out * lax.rsqrt(ms + NORM_EPS) * kg_ref[...]
    out = jnp.where(which == 0, normed, out)
    row = lax.broadcasted_iota(jnp.int32, out.shape, 0)
    o_ref[...] = jnp.where(row < nblk - 1, out, 0.0)


def compress(proj_b, kc_block, vc_block, pe, w1, w2, kgain, batch, seq):
    nblk = seq // CMP_STRIDE
    hidden = w1.shape[-1]

    def xmap(w, b, g):
        return (b, kc_block + w * (vc_block - kc_block) + g)

    return pl.pallas_call(
        _compress_kernel,
        grid=(2, batch, KV_GROUPS),
        in_specs=[
            pl.BlockSpec((seq, HEAD_B), xmap),
            pl.BlockSpec((None, L_CMP, HEAD_B), lambda w, b, g: (w, 0, 0)),
            pl.BlockSpec((None, L_CMP, HEAD_B, hidden), lambda w, b, g: (w, 0, 0, 0)),
            pl.BlockSpec((None, hidden, HEAD_B), lambda w, b, g: (w, 0, 0)),
            pl.BlockSpec((1, HEAD_B), lambda w, b, g: (0, 0)),
        ],
        out_specs=pl.BlockSpec((None, None, None, nblk, HEAD_B), lambda w, b, g: (w, b, g, 0, 0)),
        out_shape=jax.ShapeDtypeStruct((2, batch, KV_GROUPS, nblk, HEAD_B), F32),
        compiler_params=_params(ndim=3),
        name="nsa_compress",
    )(proj_b, pe, w1, w2, kgain)


def _dot_nt(a, b):
    return lax.dot_general(a, b, (((1,), (1,)), ((), ())), preferred_element_type=F32)


LOG2E = 1.4426950408889634
VAUG = 2 * HEAD_B


def _attn_kernel(q_ref, ks_ref, vs_ref, kw_ref, vw_ref, kc_ref, vc_ref, gt_ref, z_ref, qg_ref, kg_ref,
                 ovt_ref, ex_ref, wb_ref, o_ref, ksn_ref, vs1_ref, kwn_ref, vw1_ref, q_scr, m_scr, acc_scr,
                 oc_scr, ow_scr, *, seq):
    qt = pl.program_id(2)
    ncmp = kc_ref.shape[0]
    nsel = seq // L_SEL

    @pl.when(qt == 0)
    def _():
        ones_col = jnp.ones((seq, HEAD_B), BF16)
        ksn_ref[...] = _rms(ks_ref[...], kg_ref[1:2, :]).astype(BF16)
        vs1_ref[:, 0:HEAD_B] = vs_ref[...].astype(BF16)
        vs1_ref[:, HEAD_B:VAUG] = ones_col
        kwn_ref[0:WINDOW, :] = jnp.zeros((WINDOW, HEAD_B), BF16)
        vw1_ref[0:WINDOW, :] = jnp.zeros((WINDOW, VAUG), BF16)
        kwn_ref[WINDOW:WINDOW + seq, :] = _rms(kw_ref[...], kg_ref[2:3, :]).astype(BF16)
        vw1_ref[WINDOW:WINDOW + seq, 0:HEAD_B] = vw_ref[...].astype(BF16)
        vw1_ref[WINDOW:WINDOW + seq, HEAD_B:VAUG] = ones_col

    t0 = pl.multiple_of(qt * TQ, TQ)
    for r in range(REP):
        q_scr[r] = (_rms(q_ref[:, r * HEAD_B:(r + 1) * HEAD_B], qg_ref[...]) * (ATTN_SCALE * LOG2E)).astype(BF16)

    def t_of(shape):
        return t0 + lax.broadcasted_iota(jnp.int32, shape, 0)

    n_idx = lax.broadcasted_iota(jnp.int32, (TQ, ncmp), 1)
    bias_c = jnp.where(n_idx * CMP_STRIDE + (L_CMP - 1) <= t_of((TQ, ncmp)), 0.0, NEG_INF)
    bias_c = jnp.where(n_idx < ncmp - 1, bias_c, NEG_INF)
    row_ok = jnp.where(t_of((TQ, 1)) >= L_CMP - 1, 1.0, 0.0)
    q_all = q_scr[...].reshape(REP * TQ, HEAD_B)
    head = lambda x, r: x[r * TQ:(r + 1) * TQ]
    s_all = _dot_nt(q_all, kc_ref[...].astype(BF16))
    prs = []
    for r in range(REP):
        s = head(s_all, r) + bias_c
        p = jnp.exp2(s - jnp.max(s, axis=-1, keepdims=True))
        prs.append(p * (row_ok / jnp.sum(p, axis=-1, keepdims=True)))
    psum = prs[0] + prs[1] + prs[2] + prs[3]
    oc_scr[...] = jnp.dot(jnp.concatenate(prs, axis=0).astype(BF16), vc_ref[...].astype(BF16),
                          preferred_element_type=F32).reshape(REP, TQ, HEAD_B)
    imp_t = lax.dot_general(ovt_ref[...], psum, (((1,), (1,)), ((), ())), preferred_element_type=F32,
                            precision=lax.Precision.HIGHEST)

    j_idx = lax.broadcasted_iota(jnp.int32, (nsel, TQ), 0)
    t_sel = t0 + lax.broadcasted_iota(jnp.int32, (nsel, TQ), 1)
    cur = t_sel // L_SEL
    forced = jnp.where((j_idx == 0) | (j_idx == cur) | (j_idx == cur - 1), FORCE_BONUS, 0.0)
    score = jnp.where(j_idx * L_SEL <= t_sel, imp_t + forced, NEG_INF)
    rank = jnp.zeros((nsel, TQ), F32)
    for i in range(nsel):
        si = score[i:i + 1, :]
        ahead = (si > score) | ((si == score) & (j_idx > i))
        rank = rank + jnp.where(ahead, 1.0, 0.0)
    sel_t = jnp.where((rank < min(N_SEL, nsel)) & (score > 0.5 * NEG_INF), 1.0, 0.0)
    sel = jnp.concatenate([sel_t, jnp.zeros((LANES - nsel, TQ), F32)], axis=0).T
    not_sel = (1.0 - sel).astype(BF16)

    span = WINDOW + TQ
    kwin = kwn_ref[pl.ds(t0, span), :]
    vwin = vw1_ref[pl.ds(t0, span), :]
    lane_w = lax.broadcasted_iota(jnp.int32, (TQ, span), 1)
    bias_w = jnp.where(lane_w >= WINDOW - t0, wb_ref[...], NEG_INF)
    s_all = _dot_nt(q_all, kwin)
    ps = []
    for r in range(REP):
        s = head(s_all, r) + bias_w
        ps.append(jnp.exp2(s - jnp.max(s, axis=-1, keepdims=True)).astype(BF16))
    ow_scr[...] = jnp.dot(jnp.concatenate(ps, axis=0), vwin, preferred_element_type=F32)

    m_scr[...] = jnp.full(m_scr.shape, NEG_INF, F32)
    acc_scr[...] = jnp.zeros(acc_scr.shape, F32)

    def scores(c):
        k0 = pl.multiple_of(c * KC, KC)
        return _dot_nt(q_scr[...].reshape(REP * TQ, HEAD_B), ksn_ref[pl.ds(k0, KC), :])

    def sel_bias(c):
        return jnp.dot(not_sel, ex_ref[c], preferred_element_type=F32)

    def sel_chunk(c, s_all, bias, diagonal):
        k0 = pl.multiple_of(c * KC, KC)
        v1 = vs1_ref[pl.ds(k0, KC), :]
        if diagonal:
            lane = lax.broadcasted_iota(jnp.int32, (TQ, KC), 1)
            bias = jnp.where(k0 + lane <= t_of((TQ, KC)), bias, NEG_INF)
        ps, alphas = [], []
        for r in range(REP):
            s = head(s_all, r) + bias
            m_old = m_scr[r]
            m_new = jnp.maximum(m_old, jnp.max(s, axis=-1, keepdims=True))
            m_scr[r] = m_new
            ps.append(jnp.exp2(s - jnp.tile(m_new, (1, KC // LANES))).astype(BF16))
            alphas.append(jnp.exp2(m_old - m_new))
        pv = jnp.dot(jnp.concatenate(ps, axis=0), v1, preferred_element_type=F32)
        for r in range(REP):
            acc_scr[r] = jnp.tile(alphas[r], (1, VAUG // LANES)) * acc_scr[r] + head(pv, r)

    n_full = t0 // KC

    def full_chunk(c, carry):
        s_cur, bias_cur = carry
        nxt = (scores(c + 1), sel_bias(c + 1))
        sel_chunk(c, s_cur, bias_cur, False)
        return nxt

    s_last, bias_last = lax.fori_loop(0, n_full, full_chunk, (scores(0), sel_bias(0)))
    sel_chunk(n_full, s_last, bias_last, True)

    gts = jax.nn.sigmoid(gt_ref[...])
    for r in range(REP):
        ow = ow_scr[r * TQ:(r + 1) * TQ, :]
        o_w = ow[:, 0:HEAD_B] / ow[:, HEAD_B:VAUG]
        acc = acc_scr[r]
        o_s = acc[:, 0:HEAD_B] / acc[:, HEAD_B:VAUG]
        g0 = N_BRANCH * r
        o = gts[:, g0:g0 + 1] * oc_scr[r] + gts[:, g0 + 1:g0 + 2] * o_s + gts[:, g0 + 2:g0 + 3] * o_w
        z = z_ref[:, r * HEAD_B:(r + 1) * HEAD_B]
        o_ref[:, r * HEAD_B:(r + 1) * HEAD_B] = (o * (z * jax.nn.sigmoid(z))).astype(o_ref.dtype)


def nsa_attention(proj_b, proj_z, cmp_kv, gates, q_gain, k_gain, overlap_t, expand, win_bias, blocks, batch, seq):
    nq = seq // TQ
    ncmp = seq // CMP_STRIDE
    gw = REP * HEAD_B // LANES
    qspec = lambda off: pl.BlockSpec((TQ, REP * HEAD_B), lambda b, g, t: (b * nq + t, off // gw + g))
    kvspec = lambda off: pl.BlockSpec((seq, HEAD_B), lambda b, g, t: (b, off + g))
    cspec = lambda w: pl.BlockSpec((None, None, None, ncmp, HEAD_B), lambda b, g, t: (w, b, g, 0, 0))
    full = lambda a: pl.BlockSpec(a.shape, lambda b, g, t: (0,) * a.ndim)
    return pl.pallas_call(
        functools.partial(_attn_kernel, seq=seq),
        grid=(batch, KV_GROUPS, nq),
        in_specs=[
            qspec(blocks["q"]), kvspec(blocks["ks"]), kvspec(blocks["vs"]), kvspec(blocks["kw"]), kvspec(blocks["vw"]),
            cspec(0), cspec(1),
            pl.BlockSpec((None, None, TQ, GATE_PAD), lambda b, g, t: (b, g, t, 0)),
            qspec(blocks["z"]),
            full(q_gain), full(k_gain), full(overlap_t), full(expand), full(win_bias),
        ],
        out_specs=pl.BlockSpec((TQ, REP * HEAD_B), lambda b, g, t: (b * nq + t, g)),
        out_shape=jax.ShapeDtypeStruct((batch * seq, KV_GROUPS * REP * HEAD_B), BF16),
        scratch_shapes=[
            pltpu.VMEM((seq, HEAD_B), BF16), pltpu.VMEM((seq, VAUG), BF16),
            pltpu.VMEM((seq + WINDOW, HEAD_B), BF16), pltpu.VMEM((seq + WINDOW, VAUG), BF16),
            pltpu.VMEM((REP, TQ, HEAD_B), BF16), pltpu.VMEM((REP, TQ, LANES), F32),
            pltpu.VMEM((REP, TQ, VAUG), F32), pltpu.VMEM((REP, TQ, HEAD_B), F32),
            pltpu.VMEM((REP * TQ, VAUG), F32),
        ],
        compiler_params=_params(ndim=3),
        name="nsa_attention",
    )(proj_b, proj_b, proj_b, proj_b, proj_b, cmp_kv, cmp_kv, gates, proj_z, q_gain, k_gain, overlap_t, expand,
      win_bias)


def _merge_kernel(ya_ref, yb_ref, wa_ref, wb_ref, ga_ref, gb_ref, out_ref):
    ua = jnp.dot(ya_ref[...], wa_ref[...], preferred_element_type=F32)
    ub = jnp.dot(yb_ref[...], wb_ref[...], preferred_element_type=F32)
    out_ref[...] = (jax.nn.sigmoid(ga_ref[...]) * ua + jax.nn.sigmoid(gb_ref[...]) * ub).astype(out_ref.dtype)


def merge(ya, yb, w_a, w_b, proj_b, gate_col, tm=1024, tn=512):
    m, ka = ya.shape
    n = w_a.shape[1]
    tm = min(tm, m)
    assert gate_col % tn == 0 and n % tn == 0
    ga_block, gb_block = gate_col // tn, (gate_col + n) // tn
    return pl.pallas_call(
        _merge_kernel,
        grid=(m // tm, n // tn),
        in_specs=[
            pl.BlockSpec((tm, ka), lambda i, j: (i, 0)),
            pl.BlockSpec((tm, ka), lambda i, j: (i, 0)),
            pl.BlockSpec((ka, tn), lambda i, j: (0, j)),
            pl.BlockSpec((ka, tn), lambda i, j: (0, j)),
            pl.BlockSpec((tm, tn), lambda i, j: (i, ga_block + j)),
            pl.BlockSpec((tm, tn), lambda i, j: (i, gb_block + j)),
        ],
        out_specs=pl.BlockSpec((tm, tn), lambda i, j: (i, j)),
        out_shape=jax.ShapeDtypeStruct((m, n), BF16),
        compiler_params=_params(ndim=2),
        name="merge",
    )(ya, yb, w_a, w_b, proj_b, proj_b)


def _ple_embed_kernel(p_ref, w_ref, g_ref, o_ref):
    acc = jnp.dot(p_ref[...].astype(BF16), w_ref[...], preferred_element_type=F32)
    o_ref[...] = _rms(acc, g_ref[...]).astype(o_ref.dtype)


def ple_embed(p, w, g, tm=256):
    m, k = p.shape
    n = w.shape[1]
    return pl.pallas_call(
        _ple_embed_kernel,
        grid=(m // tm,),
        in_specs=[pl.BlockSpec((tm, k), lambda i: (i, 0)), pl.BlockSpec((k, n), lambda i: (0, 0)),
                  pl.BlockSpec((1, n), lambda i: (0, 0))],
        out_specs=pl.BlockSpec((tm, n), lambda i: (i, 0)),
        out_shape=jax.ShapeDtypeStruct((m, n), F32),
        compiler_params=_params(),
        name="ple_embed",
    )(p, w, g.reshape(1, n))


def head_minor(w, heads):
    lead = w.shape[:-1]
    return w.reshape(*lead, heads, HEAD_A).swapaxes(-1, -2).reshape(*lead, heads * HEAD_A)


def rwkv_branch(proj_a, proj_s, batch, seq, a_width, w_lora_up, w0, a_lora_up, a0, k_k, k_a, r_k, lnx_w, lnx_b):
    tokens = batch * seq
    heads_a = a_width // HEAD_A
    assert heads_a == SCAN_HEADS and batch % SCAN_BATCH == 0
    lora = w_lora_up.shape[0]
    hm = lambda w: head_minor(w, heads_a)
    w_lora = jnp.stack([hm(w_lora_up), hm(a_lora_up)])
    b_lora = jnp.concatenate([hm(w0), hm(a0)]).reshape(1, -1)
    wa_pre = lora_project(proj_s, w_lora, b_lora)

    def par_scan(t):
        return jnp.tile(t.reshape(heads_a, HEAD_A).T, (1, SCAN_BATCH))

    par3 = lambda t: par_scan(t).reshape(HEAD_A, 1, LANES)
    o = rwkv_scan(proj_a.reshape(batch, seq, -1), wa_pre.reshape(batch, seq, -1),
                  par3(k_k), par3(k_a), par3(r_k), par_scan(lnx_w), par_scan(lnx_b), batch, seq)
    return o.reshape(tokens, a_width)


def nsa_branch(proj_b, proj_z, graw, batch, seq, b_width, q_norm_g, k_norm_g, pe_cmp_k, pe_cmp_v,
               cmp_k_w1, cmp_k_w2, cmp_v_w1, cmp_v_w2):
    kv_width = KV_GROUPS * HEAD_B
    blk = lambda cols: cols // LANES
    blocks = {"q": 0, "kc": blk(b_width), "vc": blk(b_width + kv_width), "ks": blk(b_width + 2 * kv_width),
              "vs": blk(b_width + 3 * kv_width), "kw": blk(b_width + 4 * kv_width), "vw": blk(b_width + 5 * kv_width),
              "z": 0}
    pe = jnp.stack([pe_cmp_k, pe_cmp_v])
    hidden = cmp_k_w1.shape[1]
    w1 = jnp.stack([cmp_k_w1, cmp_v_w1]).reshape(2, L_CMP, HEAD_B, hidden).astype(BF16)
    w2 = jnp.stack([cmp_k_w2, cmp_v_w2]).astype(BF16)
    cmp_kv = compress(proj_b, blocks["kc"], blocks["vc"], pe, w1, w2, k_norm_g[0:1], batch, seq)

    ncmp = seq // CMP_STRIDE
    nsel = seq // L_SEL
    c_start = jnp.arange(ncmp) * CMP_STRIDE
    s_start = jnp.arange(nsel) * L_SEL
    overlap_t = ((c_start[None, :] < (s_start + L_SEL)[:, None]) & (s_start[:, None] < (c_start + L_CMP)[None, :])
                 & (jnp.arange(ncmp)[None, :] < ncmp - 1)).astype(F32)
    key_blk = jnp.arange(seq) // L_SEL
    expand = jnp.where(jnp.arange(LANES)[:, None] == key_blk[None, :], NEG_INF, 0.0).astype(BF16)
    expand = expand.reshape(LANES, seq // KC, KC).transpose(1, 0, 2)
    tl = jnp.arange(TQ)[:, None]
    u = jnp.arange(WINDOW + TQ)[None, :]
    win_bias = jnp.where((u > tl) & (u <= WINDOW + tl), 0.0, NEG_INF).astype(F32)
    gates = graw.reshape(batch, seq, KV_GROUPS, REP * N_BRANCH).transpose(0, 2, 1, 3)
    gates = jnp.pad(gates, ((0, 0), (0, 0), (0, 0), (0, GATE_PAD - REP * N_BRANCH)))
    return nsa_attention(proj_b, proj_z, cmp_kv, gates, q_norm_g.reshape(1, HEAD_B), k_norm_g, overlap_t, expand,
                         win_bias, blocks, batch, seq)


def _layer(x, p, norm_g, w_in, shift_mu, w_lora_up, w0, a_lora_up, a0, k_k, k_a, r_k, lnx_w, lnx_b,
           q_norm_g, k_norm_g, pe_cmp_k, pe_cmp_v, cmp_k_w1, cmp_k_w2, cmp_v_w1, cmp_v_w2,
           w_up_a, w_up_b, w_out, ple_pre_g, w_ple_gate, w_ple, ple_post_g):
    batch, seq, d = x.shape
    tokens = batch * seq
    a_width = w_up_a.shape[0]
    b_width = w_up_b.shape[0]
    heads_a = a_width // HEAD_A
    kv_width = KV_GROUPS * HEAD_B
    n_gate = KV_GROUPS * REP * N_BRANCH
    lora = w_lora_up.shape[0]
    assert lora == HEAD_A and a_lora_up.shape[0] == HEAD_A and 2 * lora == LANES
    a_cols = 4 * a_width + 2 * lora
    g_off = a_cols + b_width + 6 * kv_width
    zb_off = g_off + n_gate
    assert w_in.shape[1] == zb_off + b_width + 2 * d

    hm4 = lambda w: head_minor(w.reshape(*w.shape[:-1], 4, a_width), heads_a).reshape(*w.shape[:-1], 4 * a_width)
    w_a = hm4(w_in[:, :4 * a_width]).astype(BF16)
    mu_a = hm4(shift_mu[:4 * a_width]).reshape(1, -1)
    pad_s = LANES - n_gate
    w_s = jnp.concatenate([w_in[:, 4 * a_width:a_cols], w_in[:, g_off:zb_off], jnp.zeros((d, pad_s), F32)],
                          axis=1).astype(BF16)
    mu_s = jnp.concatenate([shift_mu[4 * a_width:], jnp.zeros((LANES,), F32)]).reshape(1, -1)
    w_b = w_in[:, a_cols:g_off].astype(BF16)
    w_z = w_in[:, zb_off:].astype(BF16)

    x2 = x.reshape(tokens, d)
    h = rmsnorm_rows(x2, norm_g, BF16)
    tm = 512
    shift = dict(mode="shift", tm=tm, tiles_per_seq=seq // tm)
    proj_a = matmul(h, w_a, extras=(mu_a,), tn=1024, name="proj_a", **shift)
    proj_s = matmul(h, w_s, extras=(mu_s,), tn=w_s.shape[1], name="proj_s", **shift)
    proj_b = matmul(h, w_b, tm=tm, tn=1024, name="proj_b")
    proj_z = matmul(h, w_z, tm=tm, tn=1024, name="proj_z")

    ya = rwkv_branch(proj_a, proj_s, batch, seq, a_width, w_lora_up, w0, a_lora_up, a0, k_k, k_a, r_k, lnx_w, lnx_b)
    graw = proj_s[:, 2 * lora:2 * lora + n_gate]
    yb = nsa_branch(proj_b, proj_z, graw, batch, seq, b_width, q_norm_g, k_norm_g, pe_cmp_k, pe_cmp_v,
                    cmp_k_w1, cmp_k_w2, cmp_v_w1, cmp_v_w2)

    w_up_a_nm = head_minor(w_up_a.T, heads_a).T
    merged = merge(ya, yb, w_up_a_nm.astype(BF16), w_up_b.astype(BF16), proj_z, b_width)
    x1 = matmul(merged, w_out.astype(BF16), mode="resid", extras=(x2,), tm=tm, tn=1024, name="out_proj")
    e = ple_embed(p.reshape(tokens, -1), w_ple.astype(BF16), ple_post_g)
    out = matmul(x1, w_ple_gate.astype(BF16), mode="ple", extras=(x1, e), norm_gain=ple_pre_g, tm=tm, tn=1024,
                 name="ple_gate")
    return out.reshape(batch, seq, d)


def kernel(x, p, norm_g, w_in, shift_mu, w_lora_up, w0, a_lora_up, a0, k_k, k_a, r_k, lnx_w, lnx_b, q_norm_g, k_norm_g, pe_cmp_k, pe_cmp_v, cmp_k_w1, cmp_k_w2, cmp_v_w1, cmp_v_w2, w_up_a, w_up_b, w_out, ple_pre_g, w_ple_gate, w_ple, ple_post_g):
    depth = w_in.shape[0]
    for i in range(depth):
        x = _layer(x, p[i], norm_g[i], w_in[i], shift_mu[i], w_lora_up[i], w0[i], a_lora_up[i], a0[i], k_k[i],
                   k_a[i], r_k[i], lnx_w[i], lnx_b[i], q_norm_g[i], k_norm_g[i], pe_cmp_k[i], pe_cmp_v[i],
                   cmp_k_w1[i], cmp_k_w2[i], cmp_v_w1[i], cmp_v_w2[i], w_up_a[i], w_up_b[i], w_out[i],
                   ple_pre_g[i], w_ple_gate[i], w_ple[i], ple_post_g[i])
    return x
```

```python
import functools

import jax
import jax.numpy as jnp
from jax import lax
from jax.experimental import pallas as pl
from jax.experimental.pallas import tpu as pltpu

F32 = jnp.float32
BF16 = jnp.bfloat16

LANES = 128
SUBLANES = 8
VMEM_LIMIT = 56 * 1024 * 1024

NORM_EPS = 1e-6
NEG_INF = -1e30
HEAD_A = 64
GN_EPS = 64e-5
HEAD_B = 128
KV_GROUPS = 4
REP = 4
N_BRANCH = 3
GATE_PAD = 16
L_CMP = 32
CMP_STRIDE = 16
L_SEL = 64
N_SEL = 16
WINDOW = 512
TQ = 256
KC = 256
FORCE_BONUS = 1e3
ATTN_SCALE = HEAD_B ** -0.5
DECAY_SCALE = 0.6065306597126334


def _params(vmem=VMEM_LIMIT, ndim=1):
    return pltpu.CompilerParams(dimension_semantics=("arbitrary",) * ndim, vmem_limit_bytes=vmem)


def _rms(x, gain):
    ms = jnp.mean(x * x, axis=-1, keepdims=True)
    return x * lax.rsqrt(ms + NORM_EPS) * gain


def _rmsnorm_kernel(x_ref, g_ref, o_ref):
    o_ref[...] = _rms(x_ref[...], g_ref[...]).astype(o_ref.dtype)


def rmsnorm_rows(x, g, out_dtype, tm=256):
    m, d = x.shape
    return pl.pallas_call(
        _rmsnorm_kernel,
        grid=(m // tm,),
        in_specs=[pl.BlockSpec((tm, d), lambda i: (i, 0)), pl.BlockSpec((1, d), lambda i: (0, 0))],
        out_specs=pl.BlockSpec((tm, d), lambda i: (i, 0)),
        out_shape=jax.ShapeDtypeStruct((m, d), out_dtype),
        compiler_params=_params(),
        name="rmsnorm",
    )(x, g.reshape(1, d))


def _mm_kernel(*refs, mode, tiles_per_seq, norm_a):
    a_ref, b_ref = refs[0], refs[1]
    if norm_a:
        x = a_ref[...]
        acc = jnp.dot((x * refs[2][...]).astype(BF16), b_ref[...], preferred_element_type=F32)
        acc = acc * lax.rsqrt(jnp.mean(x * x, axis=-1, keepdims=True) + NORM_EPS)
        refs = refs[:2] + refs[3:]
    else:
        acc = jnp.dot(a_ref[...], b_ref[...], preferred_element_type=F32)
    if mode == "plain":
        o_ref = refs[2]
        o_ref[...] = acc.astype(o_ref.dtype)
    elif mode == "shift":
        mu_ref, o_ref, carry_ref = refs[2], refs[3], refs[4]
        i = pl.program_id(1)
        tm = acc.shape[0]
        first = (i % tiles_per_seq) == 0
        last_prev = jnp.where(first, 0.0, carry_ref[SUBLANES - 1:SUBLANES, :])
        rolled = pltpu.roll(acc, 1, axis=0)
        row = lax.broadcasted_iota(jnp.int32, acc.shape, 0)
        prev = jnp.where(row == 0, last_prev, rolled)
        carry_ref[...] = acc[tm - SUBLANES:tm, :]
        o_ref[...] = (acc + mu_ref[...] * (prev - acc)).astype(o_ref.dtype)
    elif mode == "resid":
        r_ref, o_ref = refs[2], refs[3]
        o_ref[...] = (r_ref[...] + acc).astype(o_ref.dtype)
    elif mode == "ple":
        x_ref, e_ref, o_ref = refs[2], refs[3], refs[4]
        o_ref[...] = (x_ref[...] + jax.nn.sigmoid(acc) * e_ref[...]).astype(o_ref.dtype)
    else:
        raise ValueError(mode)


def matmul(a, b, *, mode="plain", extras=(), norm_gain=None, out_dtype=F32, tm=512, tn=1024, tiles_per_seq=1,
           name="mm"):
    m, k = a.shape
    n = b.shape[1]
    assert m % tm == 0 and n % tn == 0, (m, n, tm, tn)
    in_specs = [pl.BlockSpec((tm, k), lambda j, i: (i, 0)), pl.BlockSpec((k, tn), lambda j, i: (0, j))]
    if norm_gain is not None:
        in_specs.append(pl.BlockSpec((1, k), lambda j, i: (0, 0)))
        extras = (norm_gain.reshape(1, k),) + tuple(extras)
    scratch = []
    if mode == "shift":
        in_specs.append(pl.BlockSpec((1, tn), lambda j, i: (0, j)))
        scratch.append(pltpu.VMEM((SUBLANES, tn), F32))
    elif mode == "resid":
        in_specs.append(pl.BlockSpec((tm, tn), lambda j, i: (i, j)))
    elif mode == "ple":
        in_specs += [pl.BlockSpec((tm, tn), lambda j, i: (i, j)), pl.BlockSpec((tm, tn), lambda j, i: (i, j))]
    return pl.pallas_call(
        functools.partial(_mm_kernel, mode=mode, tiles_per_seq=tiles_per_seq, norm_a=norm_gain is not None),
        grid=(n // tn, m // tm),
        in_specs=in_specs,
        out_specs=pl.BlockSpec((tm, tn), lambda j, i: (i, j)),
        out_shape=jax.ShapeDtypeStruct((m, n), out_dtype),
        scratch_shapes=scratch,
        compiler_params=_params(ndim=2),
        name=name,
    )(a, b, *extras)


def _cast_cols_kernel(x_ref, nxt_ref, o_ref, *, shift):
    x = x_ref[...]
    if shift:
        x = jnp.concatenate([x[:, shift:], nxt_ref[:, :shift]], axis=1)
    o_ref[...] = x.astype(o_ref.dtype)


def cast_cols(w, start, width, bw=640, tk=512):
    k = w.shape[0]
    base, shift = start - start % LANES, start % LANES
    assert base % bw == 0 and width % bw == 0 and k % tk == 0 and bw % LANES == 0
    return pl.pallas_call(
        functools.partial(_cast_cols_kernel, shift=shift),
        grid=(k // tk, width // bw),
        in_specs=[pl.BlockSpec((tk, bw), lambda i, j: (i, base // bw + j)),
                  pl.BlockSpec((tk, LANES), lambda i, j: (i, (base + (j + 1) * bw) // LANES))],
        out_specs=pl.BlockSpec((tk, bw), lambda i, j: (i, j)),
        out_shape=jax.ShapeDtypeStruct((k, width), BF16),
        compiler_params=_params(ndim=2),
        name="cast_cols",
    )(w, w)


def _split_bf16(x):
    hi = x.astype(BF16)
    return hi, (x - hi.astype(F32)).astype(BF16)


def _lora_kernel(x_ref, whi_ref, wlo_ref, b_ref, o_ref, *, tiles_per_half):
    x = x_ref[...]
    rank = whi_ref.shape[0]
    decay_half = pl.program_id(1) < tiles_per_half
    xs = jnp.where(decay_half, jnp.tanh(x[:, :rank]), x[:, rank:])
    x_hi, x_lo = _split_bf16(xs)
    dot = lambda a, b: jnp.dot(a, b, preferred_element_type=F32)
    acc = dot(x_hi, whi_ref[...]) + (dot(x_hi, wlo_ref[...]) + dot(x_lo, whi_ref[...]))
    o_ref[...] = acc + b_ref[...]


def lora_project(proj_s, w_pair, bias, tm=1024, tn=1024):
    m = proj_s.shape[0]
    _, rank, n = w_pair.shape
    tm = min(tm, m)
    assert m % tm == 0 and n % tn == 0 and 2 * rank == LANES
    half = n // tn
    w_hi, w_lo = _split_bf16(w_pair)
    wspec = pl.BlockSpec((None, rank, tn), lambda i, j: (j // half, 0, j % half))
    return pl.pallas_call(
        functools.partial(_lora_kernel, tiles_per_half=half),
        grid=(m // tm, 2 * half),
        in_specs=[pl.BlockSpec((tm, LANES), lambda i, j: (i, 0)), wspec, wspec,
                  pl.BlockSpec((1, tn), lambda i, j: (0, j))],
        out_specs=pl.BlockSpec((tm, tn), lambda i, j: (i, j)),
        out_shape=jax.ShapeDtypeStruct((m, 2 * n), F32),
        compiler_params=_params(ndim=2),
        name="lora",
    )(proj_s, w_hi, w_lo, bias)


SCAN_BATCH = 4
SCAN_HEADS = LANES // SCAN_BATCH
NGRP = LANES // SCAN_HEADS


def _segment_transpose(x):
    seg = lax.broadcasted_iota(jnp.int32, x[0].shape, 1) // SCAN_HEADS
    y = []
    for i in range(NGRP):
        out = None
        for j in range(SCAN_BATCH):
            shift = ((j - i) % NGRP) * SCAN_HEADS
            piece = x[j] if shift == 0 else pltpu.roll(x[j], shift, axis=1)
            out = piece if out is None else jnp.where(seg == j, piece, out)
        y.append(out)
    return y


def _to_lanes(x_ref, dst_ref, tc):
    for g in range(HEAD_A // NGRP):
        y = _segment_transpose([x_ref[b, :, g * LANES:(g + 1) * LANES] for b in range(SCAN_BATCH)])
        for n_lo in range(NGRP):
            n = g * NGRP + n_lo
            dst_ref[n * tc:(n + 1) * tc, :] = y[n_lo]


def _from_lanes_gated(src_ref, z_ref, o_ref, tc):
    for g in range(HEAD_A // NGRP):
        y = _segment_transpose([src_ref[(g * NGRP + n_lo) * tc:(g * NGRP + n_lo + 1) * tc, :] for n_lo in range(NGRP)])
        for b in range(SCAN_BATCH):
            z = z_ref[b, :, g * LANES:(g + 1) * LANES]
            o_ref[b, :, g * LANES:(g + 1) * LANES] = (y[b] * (z * jax.nn.sigmoid(z))).astype(o_ref.dtype)


def _scan_kernel(r_ref, k_ref, v_ref, wp_ref, ap_ref, z_ref, kkw_ref, kaw_ref, rkw_ref, lnw_ref, lnb_ref,
                 o_ref, state_ref, set_a, set_b, ans, os, bon, gend, *, tc):
    s = pl.program_id(1)
    srcs = (r_ref, k_ref, v_ref, wp_ref, ap_ref)

    @pl.when(s == 0)
    def _():
        state_ref[...] = jnp.zeros_like(state_ref)
        for src, dst in zip(srcs, set_a):
            _to_lanes(src, dst, tc)

    @pl.when(s % 2 == 1)
    def _():
        _scan_chunk(set_a, set_b, srcs, z_ref, kkw_ref, kaw_ref, rkw_ref, lnw_ref, lnb_ref, o_ref, state_ref,
                    ans, os, bon, gend, tc)

    @pl.when((s % 2 == 0) & (s > 0))
    def _():
        _scan_chunk(set_b, set_a, srcs, z_ref, kkw_ref, kaw_ref, rkw_ref, lnw_ref, lnb_ref, o_ref, state_ref,
                    ans, os, bon, gend, tc)


def _scan_chunk(cur, nxt, srcs, z_ref, kkw_ref, kaw_ref, rkw_ref, lnw_ref, lnb_ref, o_ref, state_ref,
                ans, os, bon, gend, tc):
    n_ch = HEAD_A
    rs, ks, vs, ws, bvs = cur
    n_groups = HEAD_A // NGRP
    groups_per_step = 2
    steps_per_rowblock = n_groups // groups_per_step
    assert tc * groups_per_step == n_groups * (tc // SUBLANES)

    def rows(n):
        return pl.ds(pl.multiple_of(n * tc, tc), tc)

    def norm_acc(n, acc):
        kkr = ks[rows(n), :] * kkw_ref[n]
        return acc + kkr * kkr

    nsq = lax.fori_loop(0, n_ch, norm_acc, jnp.zeros((tc, LANES), F32), unroll=8)
    inv = 1.0 / jnp.maximum(jnp.sqrt(nsq), 1e-12)

    tri = jnp.where(lax.broadcasted_iota(jnp.int32, (tc, tc), 0) >= lax.broadcasted_iota(jnp.int32, (tc, tc), 1),
                    1.0, 0.0)

    def prep(n, bacc):
        k = ks[rows(n), :]
        a = jax.nn.sigmoid(bvs[rows(n), :])
        kk = k * kkw_ref[n] * inv
        logw = -DECAY_SCALE * jax.nn.sigmoid(ws[rows(n), :])
        cum = jnp.dot(tri, logw, preferred_element_type=F32, precision=lax.Precision.HIGHEST)
        g = jnp.exp(cum)
        ig = jnp.exp(-cum)
        kmod = k * (1.0 + (a - 1.0) * kaw_ref[n])
        r = rs[rows(n), :]
        rs[rows(n), :] = r * g
        ks[rows(n), :] = kmod * ig
        bvs[rows(n), :] = kk * a * ig
        ans[rows(n), :] = -kk * jnp.exp(cum - logw)
        gend[pl.ds(n, 1), :] = g[tc - 1:tc, :]
        return bacc + r * kmod * rkw_ref[n]

    bon[...] = lax.fori_loop(0, n_ch, prep, jnp.zeros((tc, LANES), F32), unroll=16)

    def bcast(ref, row):
        return ref[pl.ds(row, 1), :][None]

    def sa_first(j, acc):
        return acc + state_ref[j] * bcast(ans, j * tc)

    slab = (n_ch // SUBLANES, SUBLANES, LANES)
    sa0 = lax.fori_loop(0, n_ch, sa_first, jnp.zeros(slab, F32), unroll=4)

    def step(t, sa):
        v = vs[pl.ds(t, n_ch, stride=tc), :].reshape(slab)
        t_next = jnp.minimum(t + 1, tc - 1)

        out = jnp.zeros(slab, F32)
        sa_next = jnp.zeros(slab, F32)
        for j in range(n_ch):
            row = j * tc + t
            s_new = state_ref[j] + sa * bcast(bvs, row) + v * bcast(ks, row)
            state_ref[j] = s_new
            out = out + s_new * bcast(rs, row)
            sa_next = sa_next + s_new * bcast(ans, j * tc + t_next)
        o = out.reshape(n_ch, LANES)
        mu = jnp.mean(o, axis=0, keepdims=True)
        d = o - mu
        var = jnp.mean(d * d, axis=0, keepdims=True)
        on = d * lax.rsqrt(var + GN_EPS) * lnw_ref[...] + lnb_ref[...]
        bonus = bon[pl.ds(t, 1), :] * v.reshape(n_ch, LANES)
        os[pl.ds(t, n_ch, stride=tc), :] = on + bonus

        row0 = pl.multiple_of((t // steps_per_rowblock) * SUBLANES, SUBLANES)
        for src, dst in zip(srcs, nxt):
            for gi in range(groups_per_step):
                g = (t % steps_per_rowblock) * groups_per_step + gi
                lane0 = pl.multiple_of(g * LANES, LANES)
                y = _segment_transpose([src[b, pl.ds(row0, SUBLANES), pl.ds(lane0, LANES)] for b in range(SCAN_BATCH)])
                for n_lo in range(NGRP):
                    dst[pl.ds(pl.multiple_of((g * NGRP + n_lo) * tc + row0, SUBLANES), SUBLANES), :] = y[n_lo]
        return sa_next

    lax.fori_loop(0, tc, step, sa0)

    def rescale(j, c):
        state_ref[j] = state_ref[j] * bcast(gend, j)
        return c

    lax.fori_loop(0, n_ch, rescale, 0, unroll=8)
    _from_lanes_gated(os, z_ref, o_ref, tc)


def rwkv_scan(proj_a, wa_pre, kkw, kaw, rkw, lnw, lnb, batch, seq, tc=32):
    width = HEAD_A * SCAN_HEADS
    n_chunks = seq // tc
    seq_in = lambda col: pl.BlockSpec((SCAN_BATCH, tc, width), lambda g, s: (g, jnp.minimum(s, n_chunks - 1), col))
    seq_out = lambda col: pl.BlockSpec((SCAN_BATCH, tc, width), lambda g, s: (g, jnp.maximum(s - 1, 0), col))
    par3 = pl.BlockSpec((HEAD_A, 1, LANES), lambda g, s: (0, 0, 0))
    par2 = pl.BlockSpec((HEAD_A, LANES), lambda g, s: (0, 0))
    buf = pltpu.VMEM((HEAD_A * tc, LANES), F32)
    return pl.pallas_call(
        functools.partial(_scan_kernel, tc=tc),
        grid=(batch // SCAN_BATCH, n_chunks + 1),
        in_specs=[seq_in(0), seq_in(1), seq_in(2), seq_in(0), seq_in(1), seq_out(3), par3, par3, par3, par2, par2],
        out_specs=seq_out(0),
        out_shape=jax.ShapeDtypeStruct((batch, seq, width), BF16),
        scratch_shapes=[pltpu.VMEM((HEAD_A, HEAD_A // SUBLANES, SUBLANES, LANES), F32), [buf] * 5, [buf] * 5,
                        buf, buf, pltpu.VMEM((tc, LANES), F32), pltpu.VMEM((HEAD_A, LANES), F32)],
        compiler_params=_params(ndim=2),
        name="rwkv_scan",
    )(proj_a, proj_a, proj_a, wa_pre, wa_pre, proj_a, kkw, kaw, rkw, lnw, lnb)


def _compress_kernel(x_ref, pe_ref, w1_ref, w2_ref, kg_ref, o_ref):
    which = pl.program_id(0)
    half = L_CMP // 2
    nblk = x_ref.shape[0] // CMP_STRIDE
    h1 = jnp.zeros((nblk, w1_ref.shape[-1]), F32)
    h2 = jnp.zeros((nblk, w1_ref.shape[-1]), F32)
    for l in range(half):
        x = x_ref[pl.ds(l, nblk, stride=CMP_STRIDE), :]
        a1 = (x + pe_ref[l:l + 1, :]).astype(BF16)
        a2 = (x + pe_ref[half + l:half + l + 1, :]).astype(BF16)
        h1 = h1 + jnp.dot(a1, w1_ref[l], preferred_element_type=F32)
        h2 = h2 + jnp.dot(a2, w1_ref[half + l], preferred_element_type=F32)
    hid = h1 + pltpu.roll(h2, nblk - 1, axis=0)
    hid = jax.nn.gelu(hid)
    out = jnp.dot(hid.astype(BF16), w2_ref[...], preferred_element_type=F32)
    ms = jnp.mean(out * out, axis=-1, keepdims=True)
    normed = out * lax.rsqrt(ms + NORM_EPS) * kg_ref[...]
    out = jnp.where(which == 0, normed, out)
    row = lax.broadcasted_iota(jnp.int32, out.shape, 0)
    o_ref[...] = jnp.where(row < nblk - 1, out, 0.0)


def compress(proj_b, kc_block, vc_block, pe, w1, w2, kgain, batch, seq):
    nblk = seq // CMP_STRIDE
    hidden = w1.shape[-1]

    def xmap(w, b, g):
        return (b, kc_block + w * (vc_block - kc_block) + g)

    return pl.pallas_call(
        _compress_kernel,
        grid=(2, batch, KV_GROUPS),
        in_specs=[
            pl.BlockSpec((seq, HEAD_B), xmap),
            pl.BlockSpec((None, L_CMP, HEAD_B), lambda w, b, g: (w, 0, 0)),
            pl.BlockSpec((None, L_CMP, HEAD_B, hidden), lambda w, b, g: (w, 0, 0, 0)),
            pl.BlockSpec((None, hidden, HEAD_B), lambda w, b, g: (w, 0, 0)),
            pl.BlockSpec((1, HEAD_B), lambda w, b, g: (0, 0)),
        ],
        out_specs=pl.BlockSpec((None, None, None, nblk, HEAD_B), lambda w, b, g: (w, b, g, 0, 0)),
        out_shape=jax.ShapeDtypeStruct((2, batch, KV_GROUPS, nblk, HEAD_B), F32),
        compiler_params=_params(ndim=3),
        name="nsa_compress",
    )(proj_b, pe, w1, w2, kgain)


def _dot_nt(a, b):
    return lax.dot_general(a, b, (((1,), (1,)), ((), ())), preferred_element_type=F32)


LOG2E = 1.4426950408889634
VAUG = 2 * HEAD_B


def _attn_kernel(q_ref, ks_ref, vs_ref, kw_ref, vw_ref, kc_ref, vc_ref, gt_ref, z_ref, qg_ref, kg_ref,
                 ovt_ref, ex_ref, wb_ref, o_ref, ksn_ref, vs1_ref, kwn_ref, vw1_ref, q_scr, m_scr, acc_scr,
                 oc_scr, ow_scr, *, seq):
    qt = pl.program_id(2)
    ncmp = kc_ref.shape[0]
    nsel = seq // L_SEL

    @pl.when(qt == 0)
    def _():
        ones_col = jnp.ones((seq, HEAD_B), BF16)
        ksn_ref[...] = _rms(ks_ref[...], kg_ref[1:2, :]).astype(BF16)
        vs1_ref[:, 0:HEAD_B] = vs_ref[...].astype(BF16)
        vs1_ref[:, HEAD_B:VAUG] = ones_col
        kwn_ref[0:WINDOW, :] = jnp.zeros((WINDOW, HEAD_B), BF16)
        vw1_ref[0:WINDOW, :] = jnp.zeros((WINDOW, VAUG), BF16)
        kwn_ref[WINDOW:WINDOW + seq, :] = _rms(kw_ref[...], kg_ref[2:3, :]).astype(BF16)
        vw1_ref[WINDOW:WINDOW + seq, 0:HEAD_B] = vw_ref[...].astype(BF16)
        vw1_ref[WINDOW:WINDOW + seq, HEAD_B:VAUG] = ones_col

    t0 = pl.multiple_of(qt * TQ, TQ)
    for r in range(REP):
        q_scr[r] = (_rms(q_ref[:, r * HEAD_B:(r + 1) * HEAD_B], qg_ref[...]) * (ATTN_SCALE * LOG2E)).astype(BF16)

    def t_of(shape):
        return t0 + lax.broadcasted_iota(jnp.int32, shape, 0)

    n_idx = lax.broadcasted_iota(jnp.int32, (TQ, ncmp), 1)
    bias_c = jnp.where(n_idx * CMP_STRIDE + (L_CMP - 1) <= t_of((TQ, ncmp)), 0.0, NEG_INF)
    bias_c = jnp.where(n_idx < ncmp - 1, bias_c, NEG_INF)
    row_ok = jnp.where(t_of((TQ, 1)) >= L_CMP - 1, 1.0, 0.0)
    q_all = q_scr[...].reshape(REP * TQ, HEAD_B)
    head = lambda x, r: x[r * TQ:(r + 1) * TQ]
    s_all = _dot_nt(q_all, kc_ref[...].astype(BF16))
    prs = []
    for r in range(REP):
        s = head(s_all, r) + bias_c
        p = jnp.exp2(s - jnp.max(s, axis=-1, keepdims=True))
        prs.append(p * (row_ok / jnp.sum(p, axis=-1, keepdims=True)))
    psum = prs[0] + prs[1] + prs[2] + prs[3]
    oc_scr[...] = jnp.dot(jnp.concatenate(prs, axis=0).astype(BF16), vc_ref[...].astype(BF16),
                          preferred_element_type=F32).reshape(REP, TQ, HEAD_B)
    imp_t = lax.dot_general(ovt_ref[...], psum, (((1,), (1,)), ((), ())), preferred_element_type=F32,
                            precision=lax.Precision.HIGHEST)

    j_idx = lax.broadcasted_iota(jnp.int32, (nsel, TQ), 0)
    t_sel = t0 + lax.broadcasted_iota(jnp.int32, (nsel, TQ), 1)
    cur = t_sel // L_SEL
    forced = jnp.where((j_idx == 0) | (j_idx == cur) | (j_idx == cur - 1), FORCE_BONUS, 0.0)
    score = jnp.where(j_idx * L_SEL <= t_sel, imp_t + forced, NEG_INF)
    rank = jnp.zeros((nsel, TQ), F32)
    for i in range(nsel):
        si = score[i:i + 1, :]
        ahead = (si > score) | ((si == score) & (j_idx > i))
        rank = rank + jnp.where(ahead, 1.0, 0.0)
    sel_t = jnp.where((rank < min(N_SEL, nsel)) & (score > 0.5 * NEG_INF), 1.0, 0.0)
    sel = jnp.concatenate([sel_t, jnp.zeros((LANES - nsel, TQ), F32)], axis=0).T
    not_sel = (1.0 - sel).astype(BF16)

    span = WINDOW + TQ
    kwin = kwn_ref[pl.ds(t0, span), :]
    vwin = vw1_ref[pl.ds(t0, span), :]
    lane_w = lax.broadcasted_iota(jnp.int32, (TQ, span), 1)
    bias_w = jnp.where(lane_w >= WINDOW - t0, wb_ref[...], NEG_INF)
    s_all = _dot_nt(q_all, kwin)
    ps = []
    for r in range(REP):
        s = head(s_all, r) + bias_w
        ps.append(jnp.exp2(s - jnp.max(s, axis=-1, keepdims=True)).astype(BF16))
    ow_scr[...] = jnp.dot(jnp.concatenate(ps, axis=0), vwin, preferred_element_type=F32)

    m_scr[...] = jnp.full(m_scr.shape, NEG_INF, F32)
    acc_scr[...] = jnp.zeros(acc_scr.shape, F32)

    def scores(c):
        k0 = pl.multiple_of(c * KC, KC)
        return _dot_nt(q_scr[...].reshape(REP * TQ, HEAD_B), ksn_ref[pl.ds(k0, KC), :])

    def sel_bias(c):
        return jnp.dot(not_sel, ex_ref[c], preferred_element_type=F32)

    def sel_chunk(c, s_all, bias, diagonal):
        k0 = pl.multiple_of(c * KC, KC)
        v1 = vs1_ref[pl.ds(k0, KC), :]
        if diagonal:
            lane = lax.broadcasted_iota(jnp.int32, (TQ, KC), 1)
            bias = jnp.where(k0 + lane <= t_of((TQ, KC)), bias, NEG_INF)
        ps, alphas = [], []
        for r in range(REP):
            s = head(s_all, r) + bias
            m_old = m_scr[r]
            m_new = jnp.maximum(m_old, jnp.max(s, axis=-1, keepdims=True))
            m_scr[r] = m_new
            ps.append(jnp.exp2(s - jnp.tile(m_new, (1, KC // LANES))).astype(BF16))
            alphas.append(jnp.exp2(m_old - m_new))
        pv = jnp.dot(jnp.concatenate(ps, axis=0), v1, preferred_element_type=F32)
        for r in range(REP):
            acc_scr[r] = jnp.tile(alphas[r], (1, VAUG // LANES)) * acc_scr[r] + head(pv, r)

    n_full = t0 // KC

    def full_chunk(c, carry):
        s_cur, bias_cur = carry
        nxt = (scores(c + 1), sel_bias(c + 1))
        sel_chunk(c, s_cur, bias_cur, False)
        return nxt

    s_last, bias_last = lax.fori_loop(0, n_full, full_chunk, (scores(0), sel_bias(0)))
    sel_chunk(n_full, s_last, bias_last, True)

    gts = jax.nn.sigmoid(gt_ref[...])
    for r in range(REP):
        ow = ow_scr[r * TQ:(r + 1) * TQ, :]
        o_w = ow[:, 0:HEAD_B] / ow[:, HEAD_B:VAUG]
        acc = acc_scr[r]
        o_s = acc[:, 0:HEAD_B] / acc[:, HEAD_B:VAUG]
        g0 = N_BRANCH * r
        o = gts[:, g0:g0 + 1] * oc_scr[r] + gts[:, g0 + 1:g0 + 2] * o_s + gts[:, g0 + 2:g0 + 3] * o_w
        z = z_ref[:, r * HEAD_B:(r + 1) * HEAD_B]
        o_ref[:, r * HEAD_B:(r + 1) * HEAD_B] = (o * (z * jax.nn.sigmoid(z))).astype(o_ref.dtype)


def nsa_attention(proj_b, proj_z, cmp_kv, gates, q_gain, k_gain, overlap_t, expand, win_bias, blocks, batch, seq):
    nq = seq // TQ
    ncmp = seq // CMP_STRIDE
    gw = REP * HEAD_B // LANES
    qspec = lambda off: pl.BlockSpec((TQ, REP * HEAD_B), lambda b, g, t: (b * nq + t, off // gw + g))
    kvspec = lambda off: pl.BlockSpec((seq, HEAD_B), lambda b, g, t: (b, off + g))
    cspec = lambda w: pl.BlockSpec((None, None, None, ncmp, HEAD_B), lambda b, g, t: (w, b, g, 0, 0))
    full = lambda a: pl.BlockSpec(a.shape, lambda b, g, t: (0,) * a.ndim)
    return pl.pallas_call(
        functools.partial(_attn_kernel, seq=seq),
        grid=(batch, KV_GROUPS, nq),
        in_specs=[
            qspec(blocks["q"]), kvspec(blocks["ks"]), kvspec(blocks["vs"]), kvspec(blocks["kw"]), kvspec(blocks["vw"]),
            cspec(0), cspec(1),
            pl.BlockSpec((None, None, TQ, GATE_PAD), lambda b, g, t: (b, g, t, 0)),
            qspec(blocks["z"]),
            full(q_gain), full(k_gain), full(overlap_t), full(expand), full(win_bias),
        ],
        out_specs=pl.BlockSpec((TQ, REP * HEAD_B), lambda b, g, t: (b * nq + t, g)),
        out_shape=jax.ShapeDtypeStruct((batch * seq, KV_GROUPS * REP * HEAD_B), BF16),
        scratch_shapes=[
            pltpu.VMEM((seq, HEAD_B), BF16), pltpu.VMEM((seq, VAUG), BF16),
            pltpu.VMEM((seq + WINDOW, HEAD_B), BF16), pltpu.VMEM((seq + WINDOW, VAUG), BF16),
            pltpu.VMEM((REP, TQ, HEAD_B), BF16), pltpu.VMEM((REP, TQ, LANES), F32),
            pltpu.VMEM((REP, TQ, VAUG), F32), pltpu.VMEM((REP, TQ, HEAD_B), F32),
            pltpu.VMEM((REP * TQ, VAUG), F32),
        ],
        compiler_params=_params(ndim=3),
        name="nsa_attention",
    )(proj_b, proj_b, proj_b, proj_b, proj_b, cmp_kv, cmp_kv, gates, proj_z, q_gain, k_gain, overlap_t, expand,
      win_bias)


def _merge_kernel(ya_ref, yb_ref, wa_ref, wb_ref, ga_ref, gb_ref, out_ref):
    ua = jnp.dot(ya_ref[...], wa_ref[...], preferred_element_type=F32)
    ub = jnp.dot(yb_ref[...], wb_ref[...], preferred_element_type=F32)
    out_ref[...] = (jax.nn.sigmoid(ga_ref[...]) * ua + jax.nn.sigmoid(gb_ref[...]) * ub).astype(out_ref.dtype)


def merge(ya, yb, w_a, w_b, proj_b, gate_col, tm=1024, tn=512):
    m, ka = ya.shape
    n = w_a.shape[1]
    tm = min(tm, m)
    assert gate_col % tn == 0 and n % tn == 0
    ga_block, gb_block = gate_col // tn, (gate_col + n) // tn
    return pl.pallas_call(
        _merge_kernel,
        grid=(m // tm, n // tn),
        in_specs=[
            pl.BlockSpec((tm, ka), lambda i, j: (i, 0)),
            pl.BlockSpec((tm, ka), lambda i, j: (i, 0)),
            pl.BlockSpec((ka, tn), lambda i, j: (0, j)),
            pl.BlockSpec((ka, tn), lambda i, j: (0, j)),
            pl.BlockSpec((tm, tn), lambda i, j: (i, ga_block + j)),
            pl.BlockSpec((tm, tn), lambda i, j: (i, gb_block + j)),
        ],
        out_specs=pl.BlockSpec((tm, tn), lambda i, j: (i, j)),
        out_shape=jax.ShapeDtypeStruct((m, n), BF16),
        compiler_params=_params(ndim=2),
        name="merge",
    )(ya, yb, w_a, w_b, proj_b, proj_b)


def _ple_embed_kernel(p_ref, w_ref, g_ref, o_ref):
    acc = jnp.dot(p_ref[...].astype(BF16), w_ref[...], preferred_element_type=F32)
    o_ref[...] = _rms(acc, g_ref[...]).astype(o_ref.dtype)


def ple_embed(p, w, g, tm=256):
    m, k = p.shape
    n = w.shape[1]
    return pl.pallas_call(
        _ple_embed_kernel,
        grid=(m // tm,),
        in_specs=[pl.BlockSpec((tm, k), lambda i: (i, 0)), pl.BlockSpec((k, n), lambda i: (0, 0)),
                  pl.BlockSpec((1, n), lambda i: (0, 0))],
        out_specs=pl.BlockSpec((tm, n), lambda i: (i, 0)),
        out_shape=jax.ShapeDtypeStruct((m, n), F32),
        compiler_params=_params(),
        name="ple_embed",
    )(p, w, g.reshape(1, n))


def head_minor(w, heads):
    lead = w.shape[:-1]
    return w.reshape(*lead, heads, HEAD_A).swapaxes(-1, -2).reshape(*lead, heads * HEAD_A)


def rwkv_branch(proj_a, proj_s, batch, seq, a_width, w_lora_up, w0, a_lora_up, a0, k_k, k_a, r_k, lnx_w, lnx_b):
    tokens = batch * seq
    heads_a = a_width // HEAD_A
    assert heads_a == SCAN_HEADS and batch % SCAN_BATCH == 0
    lora = w_lora_up.shape[0]
    hm = lambda w: head_minor(w, heads_a)
    w_lora = jnp.stack([hm(w_lora_up), hm(a_lora_up)])
    b_lora = jnp.concatenate([hm(w0), hm(a0)]).reshape(1, -1)
    wa_pre = lora_project(proj_s, w_lora, b_lora)

    def par_scan(t):
        return jnp.tile(t.reshape(heads_a, HEAD_A).T, (1, SCAN_BATCH))

    par3 = lambda t: par_scan(t).reshape(HEAD_A, 1, LANES)
    o = rwkv_scan(proj_a.reshape(batch, seq, -1), wa_pre.reshape(batch, seq, -1),
                  par3(k_k), par3(k_a), par3(r_k), par_scan(lnx_w), par_scan(lnx_b), batch, seq)
    return o.reshape(tokens, a_width)


def nsa_branch(proj_b, proj_z, graw, batch, seq, b_width, q_norm_g, k_norm_g, pe_cmp_k, pe_cmp_v,
               cmp_k_w1, cmp_k_w2, cmp_v_w1, cmp_v_w2):
    kv_width = KV_GROUPS * HEAD_B
    blk = lambda cols: cols // LANES
    blocks = {"q": 0, "kc": blk(b_width), "vc": blk(b_width + kv_width), "ks": blk(b_width + 2 * kv_width),
              "vs": blk(b_width + 3 * kv_width), "kw": blk(b_width + 4 * kv_width), "vw": blk(b_width + 5 * kv_width),
              "z": 0}
    pe = jnp.stack([pe_cmp_k, pe_cmp_v])
    hidden = cmp_k_w1.shape[1]
    w1 = jnp.stack([cmp_k_w1, cmp_v_w1]).reshape(2, L_CMP, HEAD_B, hidden).astype(BF16)
    w2 = jnp.stack([cmp_k_w2, cmp_v_w2]).astype(BF16)
    cmp_kv = compress(proj_b, blocks["kc"], blocks["vc"], pe, w1, w2, k_norm_g[0:1], batch, seq)

    ncmp = seq // CMP_STRIDE
    nsel = seq // L_SEL
    c_start = jnp.arange(ncmp) * CMP_STRIDE
    s_start = jnp.arange(nsel) * L_SEL
    overlap_t = ((c_start[None, :] < (s_start + L_SEL)[:, None]) & (s_start[:, None] < (c_start + L_CMP)[None, :])
                 & (jnp.arange(ncmp)[None, :] < ncmp - 1)).astype(F32)
    key_blk = jnp.arange(seq) // L_SEL
    expand = jnp.where(jnp.arange(LANES)[:, None] == key_blk[None, :], NEG_INF, 0.0).astype(BF16)
    expand = expand.reshape(LANES, seq // KC, KC).transpose(1, 0, 2)
    tl = jnp.arange(TQ)[:, None]
    u = jnp.arange(WINDOW + TQ)[None, :]
    win_bias = jnp.where((u > tl) & (u <= WINDOW + tl), 0.0, NEG_INF).astype(F32)
    gates = graw.reshape(batch, seq, KV_GROUPS, REP * N_BRANCH).transpose(0, 2, 1, 3)
    gates = jnp.pad(gates, ((0, 0), (0, 0), (0, 0), (0, GATE_PAD - REP * N_BRANCH)))
    return nsa_attention(proj_b, proj_z, cmp_kv, gates, q_norm_g.reshape(1, HEAD_B), k_norm_g, overlap_t, expand,
                         win_bias, blocks, batch, seq)


def _layer(x, p, norm_g, w_in, shift_mu, w_lora_up, w0, a_lora_up, a0, k_k, k_a, r_k, lnx_w, lnx_b,
           q_norm_g, k_norm_g, pe_cmp_k, pe_cmp_v, cmp_k_w1, cmp_k_w2, cmp_v_w1, cmp_v_w2,
           w_up_a, w_up_b, w_out, ple_pre_g, w_ple_gate, w_ple, ple_post_g):
    batch, seq, d = x.shape
    tokens = batch * seq
    a_width = w_up_a.shape[0]
    b_width = w_up_b.shape[0]
    heads_a = a_width // HEAD_A
    kv_width = KV_GROUPS * HEAD_B
    n_gate = KV_GROUPS * REP * N_BRANCH
    lora = w_lora_up.shape[0]
    assert lora == HEAD_A and a_lora_up.shape[0] == HEAD_A and 2 * lora == LANES
    a_cols = 4 * a_width + 2 * lora
    g_off = a_cols + b_width + 6 * kv_width
    zb_off = g_off + n_gate
    assert w_in.shape[1] == zb_off + b_width + 2 * d

    hm4 = lambda w: head_minor(w.reshape(*w.shape[:-1], 4, a_width), heads_a).reshape(*w.shape[:-1], 4 * a_width)
    w_a = hm4(w_in[:, :4 * a_width]).astype(BF16)
    mu_a = hm4(shift_mu[:4 * a_width]).reshape(1, -1)
    pad_s = LANES - n_gate
    w_s = jnp.concatenate([w_in[:, 4 * a_width:a_cols], w_in[:, g_off:zb_off], jnp.zeros((d, pad_s), F32)],
                          axis=1).astype(BF16)
    mu_s = jnp.concatenate([shift_mu[4 * a_width:], jnp.zeros((LANES,), F32)]).reshape(1, -1)
    w_b = cast_cols(w_in, a_cols, g_off - a_cols)
    w_z = cast_cols(w_in, zb_off, w_in.shape[1] - zb_off)

    x2 = x.reshape(tokens, d)
    h = rmsnorm_rows(x2, norm_g, BF16)
    tm = 512
    shift = dict(mode="shift", tm=tm, tiles_per_seq=seq // tm)
    proj_a = matmul(h, w_a, extras=(mu_a,), tn=1024, name="proj_a", **shift)
    proj_s = matmul(h, w_s, extras=(mu_s,), tn=w_s.shape[1], name="proj_s", **shift)
    proj_b = matmul(h, w_b, tm=tm, tn=1024, name="proj_b")
    proj_z = matmul(h, w_z, tm=tm, tn=1024, name="proj_z")

    ya = rwkv_branch(proj_a, proj_s, batch, seq, a_width, w_lora_up, w0, a_lora_up, a0, k_k, k_a, r_k, lnx_w, lnx_b)
    graw = proj_s[:, 2 * lora:2 * lora + n_gate]
    yb = nsa_branch(proj_b, proj_z, graw, batch, seq, b_width, q_norm_g, k_norm_g, pe_cmp_k, pe_cmp_v,
                    cmp_k_w1, cmp_k_w2, cmp_v_w1, cmp_v_w2)

    w_up_a_nm = head_minor(w_up_a.T, heads_a).T
    merged = merge(ya, yb, w_up_a_nm.astype(BF16), w_up_b.astype(BF16), proj_z, b_width)
    x1 = matmul(merged, w_out.astype(BF16), mode="resid", extras=(x2,), tm=tm, tn=1024, name="out_proj")
    e = ple_embed(p.reshape(tokens, -1), w_ple.astype(BF16), ple_post_g)
    out = matmul(x1, w_ple_gate.astype(BF16), mode="ple", extras=(x1, e), norm_gain=ple_pre_g, tm=tm, tn=1024,
                 name="ple_gate")
    return out.reshape(batch, seq, d)


def kernel(x, p, norm_g, w_in, shift_mu, w_lora_up, w0, a_lora_up, a0, k_k, k_a, r_k, lnx_w, lnx_b, q_norm_g, k_norm_g, pe_cmp_k, pe_cmp_v, cmp_k_w1, cmp_k_w2, cmp_v_w1, cmp_v_w2, w_up_a, w_up_b, w_out, ple_pre_g, w_ple_gate, w_ple, ple_post_g):
    depth = w_in.shape[0]
    for i in range(depth):
        x = _layer(x, p[i], norm_g[i], w_in[i], shift_mu[i], w_lora_up[i], w0[i], a_lora_up[i], a0[i], k_k[i],
                   k_a[i], r_k[i], lnx_w[i], lnx_b[i], q_norm_g[i], k_norm_g[i], pe_cmp_k[i], pe_cmp_v[i],
                   cmp_k_w1[i], cmp_k_w2[i], cmp_v_w1[i], cmp_v_w2[i], w_up_a[i], w_up_b[i], w_out[i],
                   ple_pre_g[i], w_ple_gate[i], w_ple[i], ple_post_g[i])
    return x
```

```python
import functools

import jax
import jax.numpy as jnp
from jax import lax
from jax.experimental import pallas as pl
from jax.experimental.pallas import tpu as pltpu

F32 = jnp.float32
BF16 = jnp.bfloat16

LANES = 128
SUBLANES = 8
VMEM_LIMIT = 56 * 1024 * 1024

NORM_EPS = 1e-6
NEG_INF = -1e30
HEAD_A = 64
GN_EPS = 64e-5
HEAD_B = 128
KV_GROUPS = 4
REP = 4
N_BRANCH = 3
GATE_PAD = 16
L_CMP = 32
CMP_STRIDE = 16
L_SEL = 64
N_SEL = 16
WINDOW = 512
TQ = 256
KC = 256
FORCE_BONUS = 1e3
ATTN_SCALE = HEAD_B ** -0.5
DECAY_SCALE = 0.6065306597126334


def _params(vmem=VMEM_LIMIT, ndim=1):
    return pltpu.CompilerParams(dimension_semantics=("arbitrary",) * ndim, vmem_limit_bytes=vmem)


def _rms(x, gain):
    ms = jnp.mean(x * x, axis=-1, keepdims=True)
    return x * lax.rsqrt(ms + NORM_EPS) * gain


def _rmsnorm_kernel(x_ref, g_ref, o_ref):
    o_ref[...] = _rms(x_ref[...], g_ref[...]).astype(o_ref.dtype)


def rmsnorm_rows(x, g, out_dtype, tm=256):
    m, d = x.shape
    return pl.pallas_call(
        _rmsnorm_kernel,
        grid=(m // tm,),
        in_specs=[pl.BlockSpec((tm, d), lambda i: (i, 0)), pl.BlockSpec((1, d), lambda i: (0, 0))],
        out_specs=pl.BlockSpec((tm, d), lambda i: (i, 0)),
        out_shape=jax.ShapeDtypeStruct((m, d), out_dtype),
        compiler_params=_params(),
        name="rmsnorm",
    )(x, g.reshape(1, d))


def _mm_kernel(*refs, mode, tiles_per_seq, norm_a):
    a_ref, b_ref = refs[0], refs[1]
    if norm_a:
        x = a_ref[...]
        acc = jnp.dot((x * refs[2][...]).astype(BF16), b_ref[...], preferred_element_type=F32)
        acc = acc * lax.rsqrt(jnp.mean(x * x, axis=-1, keepdims=True) + NORM_EPS)
        refs = refs[:2] + refs[3:]
    else:
        acc = jnp.dot(a_ref[...], b_ref[...], preferred_element_type=F32)
    if mode == "plain":
        o_ref = refs[2]
        o_ref[...] = acc.astype(o_ref.dtype)
    elif mode == "shift":
        mu_ref, o_ref, carry_ref = refs[2], refs[3], refs[4]
        i = pl.program_id(1)
        tm = acc.shape[0]
        first = (i % tiles_per_seq) == 0
        last_prev = jnp.where(first, 0.0, carry_ref[SUBLANES - 1:SUBLANES, :])
        rolled = pltpu.roll(acc, 1, axis=0)
        row = lax.broadcasted_iota(jnp.int32, acc.shape, 0)
        prev = jnp.where(row == 0, last_prev, rolled)
        carry_ref[...] = acc[tm - SUBLANES:tm, :]
        o_ref[...] = (acc + mu_ref[...] * (prev - acc)).astype(o_ref.dtype)
    elif mode == "resid":
        r_ref, o_ref = refs[2], refs[3]
        o_ref[...] = (r_ref[...] + acc).astype(o_ref.dtype)
    elif mode == "ple":
        x_ref, e_ref, o_ref = refs[2], refs[3], refs[4]
        o_ref[...] = (x_ref[...] + jax.nn.sigmoid(acc) * e_ref[...]).astype(o_ref.dtype)
    else:
        raise ValueError(mode)


def matmul(a, b, *, mode="plain", extras=(), norm_gain=None, out_dtype=F32, tm=512, tn=1024, tiles_per_seq=1,
           name="mm"):
    m, k = a.shape
    n = b.shape[1]
    assert m % tm == 0 and n % tn == 0, (m, n, tm, tn)
    in_specs = [pl.BlockSpec((tm, k), lambda j, i: (i, 0)), pl.BlockSpec((k, tn), lambda j, i: (0, j))]
    if norm_gain is not None:
        in_specs.append(pl.BlockSpec((1, k), lambda j, i: (0, 0)))
        extras = (norm_gain.reshape(1, k),) + tuple(extras)
    scratch = []
    if mode == "shift":
        in_specs.append(pl.BlockSpec((1, tn), lambda j, i: (0, j)))
        scratch.append(pltpu.VMEM((SUBLANES, tn), F32))
    elif mode == "resid":
        in_specs.append(pl.BlockSpec((tm, tn), lambda j, i: (i, j)))
    elif mode == "ple":
        in_specs += [pl.BlockSpec((tm, tn), lambda j, i: (i, j)), pl.BlockSpec((tm, tn), lambda j, i: (i, j))]
    return pl.pallas_call(
        functools.partial(_mm_kernel, mode=mode, tiles_per_seq=tiles_per_seq, norm_a=norm_gain is not None),
        grid=(n // tn, m // tm),
        in_specs=in_specs,
        out_specs=pl.BlockSpec((tm, tn), lambda j, i: (i, j)),
        out_shape=jax.ShapeDtypeStruct((m, n), out_dtype),
        scratch_shapes=scratch,
        compiler_params=_params(ndim=2),
        name=name,
    )(a, b, *extras)


def _split_bf16(x):
    hi = x.astype(BF16)
    return hi, (x - hi.astype(F32)).astype(BF16)


def _lora_kernel(x_ref, whi_ref, wlo_ref, b_ref, o_ref, *, tiles_per_half):
    x = x_ref[...]
    rank = whi_ref.shape[0]
    decay_half = pl.program_id(1) < tiles_per_half
    xs = jnp.where(decay_half, jnp.tanh(x[:, :rank]), x[:, rank:])
    x_hi, x_lo = _split_bf16(xs)
    dot = lambda a, b: jnp.dot(a, b, preferred_element_type=F32)
    acc = dot(x_hi, whi_ref[...]) + (dot(x_hi, wlo_ref[...]) + dot(x_lo, whi_ref[...]))
    o_ref[...] = acc + b_ref[...]


def lora_project(proj_s, w_pair, bias, tm=1024, tn=1024):
    m = proj_s.shape[0]
    _, rank, n = w_pair.shape
    tm = min(tm, m)
    assert m % tm == 0 and n % tn == 0 and 2 * rank == LANES
    half = n // tn
    w_hi, w_lo = _split_bf16(w_pair)
    wspec = pl.BlockSpec((None, rank, tn), lambda i, j: (j // half, 0, j % half))
    return pl.pallas_call(
        functools.partial(_lora_kernel, tiles_per_half=half),
        grid=(m // tm, 2 * half),
        in_specs=[pl.BlockSpec((tm, LANES), lambda i, j: (i, 0)), wspec, wspec,
                  pl.BlockSpec((1, tn), lambda i, j: (0, j))],
        out_specs=pl.BlockSpec((tm, tn), lambda i, j: (i, j)),
        out_shape=jax.ShapeDtypeStruct((m, 2 * n), F32),
        compiler_params=_params(ndim=2),
        name="lora",
    )(proj_s, w_hi, w_lo, bias)


SCAN_BATCH = 4
SCAN_HEADS = LANES // SCAN_BATCH
NGRP = LANES // SCAN_HEADS


def _segment_transpose(x):
    seg = lax.broadcasted_iota(jnp.int32, x[0].shape, 1) // SCAN_HEADS
    y = []
    for i in range(NGRP):
        out = None
        for j in range(SCAN_BATCH):
            shift = ((j - i) % NGRP) * SCAN_HEADS
            piece = x[j] if shift == 0 else pltpu.roll(x[j], shift, axis=1)
            out = piece if out is None else jnp.where(seg == j, piece, out)
        y.append(out)
    return y


def _to_lanes(x_ref, dst_ref, tc):
    for g in range(HEAD_A // NGRP):
        y = _segment_transpose([x_ref[b, :, g * LANES:(g + 1) * LANES] for b in range(SCAN_BATCH)])
        for n_lo in range(NGRP):
            n = g * NGRP + n_lo
            dst_ref[n * tc:(n + 1) * tc, :] = y[n_lo]


def _from_lanes_gated(src_ref, z_ref, o_ref, tc):
    for g in range(HEAD_A // NGRP):
        y = _segment_transpose([src_ref[(g * NGRP + n_lo) * tc:(g * NGRP + n_lo + 1) * tc, :] for n_lo in range(NGRP)])
        for b in range(SCAN_BATCH):
            z = z_ref[b, :, g * LANES:(g + 1) * LANES]
            o_ref[b, :, g * LANES:(g + 1) * LANES] = (y[b] * (z * jax.nn.sigmoid(z))).astype(o_ref.dtype)


def _scan_kernel(r_ref, k_ref, v_ref, wp_ref, ap_ref, z_ref, kkw_ref, kaw_ref, rkw_ref, lnw_ref, lnb_ref,
                 o_ref, state_ref, set_a, set_b, ans, os, bon, gend, *, tc):
    s = pl.program_id(1)
    srcs = (r_ref, k_ref, v_ref, wp_ref, ap_ref)

    @pl.when(s == 0)
    def _():
        state_ref[...] = jnp.zeros_like(state_ref)
        for src, dst in zip(srcs, set_a):
            _to_lanes(src, dst, tc)

    @pl.when(s % 2 == 1)
    def _():
        _scan_chunk(set_a, set_b, srcs, z_ref, kkw_ref, kaw_ref, rkw_ref, lnw_ref, lnb_ref, o_ref, state_ref,
                    ans, os, bon, gend, tc)

    @pl.when((s % 2 == 0) & (s > 0))
    def _():
        _scan_chunk(set_b, set_a, srcs, z_ref, kkw_ref, kaw_ref, rkw_ref, lnw_ref, lnb_ref, o_ref, state_ref,
                    ans, os, bon, gend, tc)


def _scan_chunk(cur, nxt, srcs, z_ref, kkw_ref, kaw_ref, rkw_ref, lnw_ref, lnb_ref, o_ref, state_ref,
                ans, os, bon, gend, tc):
    n_ch = HEAD_A
    rs, ks, vs, ws, bvs = cur
    n_groups = HEAD_A // NGRP
    groups_per_step = 2
    steps_per_rowblock = n_groups // groups_per_step
    assert tc * groups_per_step == n_groups * (tc // SUBLANES)

    def rows(n):
        return pl.ds(pl.multiple_of(n * tc, tc), tc)

    def norm_acc(n, acc):
        kkr = ks[rows(n), :] * kkw_ref[n]
        return acc + kkr * kkr

    nsq = lax.fori_loop(0, n_ch, norm_acc, jnp.zeros((tc, LANES), F32), unroll=8)
    inv = 1.0 / jnp.maximum(jnp.sqrt(nsq), 1e-12)

    tri = jnp.where(lax.broadcasted_iota(jnp.int32, (tc, tc), 0) >= lax.broadcasted_iota(jnp.int32, (tc, tc), 1),
                    1.0, 0.0)

    def prep(n, bacc):
        k = ks[rows(n), :]
        a = jax.nn.sigmoid(bvs[rows(n), :])
        kk = k * kkw_ref[n] * inv
        logw = -DECAY_SCALE * jax.nn.sigmoid(ws[rows(n), :])
        cum = jnp.dot(tri, logw, preferred_element_type=F32, precision=lax.Precision.HIGHEST)
        g = jnp.exp(cum)
        ig = jnp.exp(-cum)
        kmod = k * (1.0 + (a - 1.0) * kaw_ref[n])
        r = rs[rows(n), :]
        rs[rows(n), :] = r * g
        ks[rows(n), :] = kmod * ig
        bvs[rows(n), :] = kk * a * ig
        ans[rows(n), :] = -kk * jnp.exp(cum - logw)
        gend[pl.ds(n, 1), :] = g[tc - 1:tc, :]
        return bacc + r * kmod * rkw_ref[n]

    bon[...] = lax.fori_loop(0, n_ch, prep, jnp.zeros((tc, LANES), F32), unroll=16)

    def bcast(ref, row):
        return ref[pl.ds(row, 1), :][None]

    def sa_first(j, acc):
        return acc + state_ref[j] * bcast(ans, j * tc)

    slab = (n_ch // SUBLANES, SUBLANES, LANES)
    sa0 = lax.fori_loop(0, n_ch, sa_first, jnp.zeros(slab, F32), unroll=4)

    def step(t, sa):
        v = vs[pl.ds(t, n_ch, stride=tc), :].reshape(slab)
        t_next = jnp.minimum(t + 1, tc - 1)

        out = jnp.zeros(slab, F32)
        sa_next = jnp.zeros(slab, F32)
        for j in range(n_ch):
            row = j * tc + t
            s_new = state_ref[j] + sa * bcast(bvs, row) + v * bcast(ks, row)
            state_ref[j] = s_new
            out = out + s_new * bcast(rs, row)
            sa_next = sa_next + s_new * bcast(ans, j * tc + t_next)
        o = out.reshape(n_ch, LANES)
        mu = jnp.mean(o, axis=0, keepdims=True)
        d = o - mu
        var = jnp.mean(d * d, axis=0, keepdims=True)
        on = d * lax.rsqrt(var + GN_EPS) * lnw_ref[...] + lnb_ref[...]
        bonus = bon[pl.ds(t, 1), :] * v.reshape(n_ch, LANES)
        os[pl.ds(t, n_ch, stride=tc), :] = on + bonus

        row0 = pl.multiple_of((t // steps_per_rowblock) * SUBLANES, SUBLANES)
        for src, dst in zip(srcs, nxt):
            for gi in range(groups_per_step):
                g = (t % steps_per_rowblock) * groups_per_step + gi
                lane0 = pl.multiple_of(g * LANES, LANES)
                y = _segment_transpose([src[b, pl.ds(row0, SUBLANES), pl.ds(lane0, LANES)] for b in range(SCAN_BATCH)])
                for n_lo in range(NGRP):
                    dst[pl.ds(pl.multiple_of((g * NGRP + n_lo) * tc + row0, SUBLANES), SUBLANES), :] = y[n_lo]
        return sa_next

    lax.fori_loop(0, tc, step, sa0)

    def rescale(j, c):
        state_ref[j] = state_ref[j] * bcast(gend, j)
        return c

    lax.fori_loop(0, n_ch, rescale, 0, unroll=8)
    _from_lanes_gated(os, z_ref, o_ref, tc)


def rwkv_scan(proj_a, wa_pre, kkw, kaw, rkw, lnw, lnb, batch, seq, tc=32):
    width = HEAD_A * SCAN_HEADS
    n_chunks = seq // tc
    seq_in = lambda col: pl.BlockSpec((SCAN_BATCH, tc, width), lambda g, s: (g, jnp.minimum(s, n_chunks - 1), col))
    seq_out = lambda col: pl.BlockSpec((SCAN_BATCH, tc, width), lambda g, s: (g, jnp.maximum(s - 1, 0), col))
    par3 = pl.BlockSpec((HEAD_A, 1, LANES), lambda g, s: (0, 0, 0))
    par2 = pl.BlockSpec((HEAD_A, LANES), lambda g, s: (0, 0))
    buf = pltpu.VMEM((HEAD_A * tc, LANES), F32)
    return pl.pallas_call(
        functools.partial(_scan_kernel, tc=tc),
        grid=(batch // SCAN_BATCH, n_chunks + 1),
        in_specs=[seq_in(0), seq_in(1), seq_in(2), seq_in(0), seq_in(1), seq_out(3), par3, par3, par3, par2, par2],
        out_specs=seq_out(0),
        out_shape=jax.ShapeDtypeStruct((batch, seq, width), BF16),
        scratch_shapes=[pltpu.VMEM((HEAD_A, HEAD_A // SUBLANES, SUBLANES, LANES), F32), [buf] * 5, [buf] * 5,
                        buf, buf, pltpu.VMEM((tc, LANES), F32), pltpu.VMEM((HEAD_A, LANES), F32)],
        compiler_params=_params(ndim=2),
        name="rwkv_scan",
    )(proj_a, proj_a, proj_a, wa_pre, wa_pre, proj_a, kkw, kaw, rkw, lnw, lnb)


def _compress_kernel(x_ref, pe_ref, w1_ref, w2_ref, kg_ref, o_ref):
    which = pl.program_id(0)
    half = L_CMP // 2
    nblk = x_ref.shape[0] // CMP_STRIDE
    h1 = jnp.zeros((nblk, w1_ref.shape[-1]), F32)
    h2 = jnp.zeros((nblk, w1_ref.shape[-1]), F32)
    for l in range(half):
        x = x_ref[pl.ds(l, nblk, stride=CMP_STRIDE), :]
        a1 = (x + pe_ref[l:l + 1, :]).astype(BF16)
        a2 = (x + pe_ref[half + l:half + l + 1, :]).astype(BF16)
        h1 = h1 + jnp.dot(a1, w1_ref[l], preferred_element_type=F32)
        h2 = h2 + jnp.dot(a2, w1_ref[half + l], preferred_element_type=F32)
    hid = h1 + pltpu.roll(h2, nblk - 1, axis=0)
    hid = jax.nn.gelu(hid)
    out = jnp.dot(hid.astype(BF16), w2_ref[...], preferred_element_type=F32)
    ms = jnp.mean(out * out, axis=-1, keepdims=True)
    normed = out * lax.rsqrt(ms + NORM_EPS) * kg_ref[...]
    out = jnp.where(which == 0, normed, out)
    row = lax.broadcasted_iota(jnp.int32, out.shape, 0)
    o_ref[...] = jnp.where(row < nblk - 1, out, 0.0)


def compress(proj_b, kc_block, vc_block, pe, w1, w2, kgain, batch, seq):
    nblk = seq // CMP_STRIDE
    hidden = w1.shape[-1]

    def xmap(w, b, g):
        return (b, kc_block + w * (vc_block - kc_block) + g)

    return pl.pallas_call(
        _compress_kernel,
        grid=(2, batch, KV_GROUPS),
        in_specs=[
            pl.BlockSpec((seq, HEAD_B), xmap),
            pl.BlockSpec((None, L_CMP, HEAD_B), lambda w, b, g: (w, 0, 0)),
            pl.BlockSpec((None, L_CMP, HEAD_B, hidden), lambda w, b, g: (w, 0, 0, 0)),
            pl.BlockSpec((None, hidden, HEAD_B), lambda w, b, g: (w, 0, 0)),
            pl.BlockSpec((1, HEAD_B), lambda w, b, g: (0, 0)),
        ],
        out_specs=pl.BlockSpec((None, None, None, nblk, HEAD_B), lambda w, b, g: (w, b, g, 0, 0)),
        out_shape=jax.ShapeDtypeStruct((2, batch, KV_GROUPS, nblk, HEAD_B), F32),
        compiler_params=_params(ndim=3),
        name="nsa_compress",
    )(proj_b, pe, w1, w2, kgain)


def _dot_nt(a, b):
    return lax.dot_general(a, b, (((1,), (1,)), ((), ())), preferred_element_type=F32)


LOG2E = 1.4426950408889634
VAUG = 2 * HEAD_B


def _attn_kernel(q_ref, ks_ref, vs_ref, kw_ref, vw_ref, kc_ref, vc_ref, gt_ref, z_ref, qg_ref, kg_ref,
                 ovt_ref, ex_ref, wb_ref, o_ref, ksn_ref, vs1_ref, kwn_ref, vw1_ref, q_scr, m_scr, acc_scr,
                 oc_scr, ow_scr, *, seq):
    qt = pl.program_id(2)
    ncmp = kc_ref.shape[0]
    nsel = seq // L_SEL

    @pl.when(qt == 0)
    def _():
        ones_col = jnp.ones((seq, HEAD_B), BF16)
        ksn_ref[...] = _rms(ks_ref[...], kg_ref[1:2, :]).astype(BF16)
        vs1_ref[:, 0:HEAD_B] = vs_ref[...].astype(BF16)
        vs1_ref[:, HEAD_B:VAUG] = ones_col
        kwn_ref[0:WINDOW, :] = jnp.zeros((WINDOW, HEAD_B), BF16)
        vw1_ref[0:WINDOW, :] = jnp.zeros((WINDOW, VAUG), BF16)
        kwn_ref[WINDOW:WINDOW + seq, :] = _rms(kw_ref[...], kg_ref[2:3, :]).astype(BF16)
        vw1_ref[WINDOW:WINDOW + seq, 0:HEAD_B] = vw_ref[...].astype(BF16)
        vw1_ref[WINDOW:WINDOW + seq, HEAD_B:VAUG] = ones_col

    t0 = pl.multiple_of(qt * TQ, TQ)
    for r in range(REP):
        q_scr[r] = (_rms(q_ref[:, r * HEAD_B:(r + 1) * HEAD_B], qg_ref[...]) * (ATTN_SCALE * LOG2E)).astype(BF16)

    def t_of(shape):
        return t0 + lax.broadcasted_iota(jnp.int32, shape, 0)

    n_idx = lax.broadcasted_iota(jnp.int32, (TQ, ncmp), 1)
    bias_c = jnp.where(n_idx * CMP_STRIDE + (L_CMP - 1) <= t_of((TQ, ncmp)), 0.0, NEG_INF)
    bias_c = jnp.where(n_idx < ncmp - 1, bias_c, NEG_INF)
    row_ok = jnp.where(t_of((TQ, 1)) >= L_CMP - 1, 1.0, 0.0)
    q_all = q_scr[...].reshape(REP * TQ, HEAD_B)
    head = lambda x, r: x[r * TQ:(r + 1) * TQ]
    s_all = _dot_nt(q_all, kc_ref[...].astype(BF16))
    prs = []
    for r in range(REP):
        s = head(s_all, r) + bias_c
        p = jnp.exp2(s - jnp.max(s, axis=-1, keepdims=True))
        prs.append(p * (row_ok / jnp.sum(p, axis=-1, keepdims=True)))
    psum = prs[0] + prs[1] + prs[2] + prs[3]
    oc_scr[...] = jnp.dot(jnp.concatenate(prs, axis=0).astype(BF16), vc_ref[...].astype(BF16),
                          preferred_element_type=F32).reshape(REP, TQ, HEAD_B)
    imp_t = lax.dot_general(ovt_ref[...], psum, (((1,), (1,)), ((), ())), preferred_element_type=F32,
                            precision=lax.Precision.HIGHEST)

    j_idx = lax.broadcasted_iota(jnp.int32, (nsel, TQ), 0)
    t_sel = t0 + lax.broadcasted_iota(jnp.int32, (nsel, TQ), 1)
    cur = t_sel // L_SEL
    forced = jnp.where((j_idx == 0) | (j_idx == cur) | (j_idx == cur - 1), FORCE_BONUS, 0.0)
    score = jnp.where(j_idx * L_SEL <= t_sel, imp_t + forced, NEG_INF)
    rank = jnp.zeros((nsel, TQ), F32)
    for i in range(nsel):
        si = score[i:i + 1, :]
        ahead = (si > score) | ((si == score) & (j_idx > i))
        rank = rank + jnp.where(ahead, 1.0, 0.0)
    sel_t = jnp.where((rank < min(N_SEL, nsel)) & (score > 0.5 * NEG_INF), 1.0, 0.0)
    sel = jnp.concatenate([sel_t, jnp.zeros((LANES - nsel, TQ), F32)], axis=0).T
    not_sel = (1.0 - sel).astype(BF16)

    span = WINDOW + TQ
    kwin = kwn_ref[pl.ds(t0, span), :]
    vwin = vw1_ref[pl.ds(t0, span), :]
    lane_w = lax.broadcasted_iota(jnp.int32, (TQ, span), 1)
    bias_w = jnp.where(lane_w >= WINDOW - t0, wb_ref[...], NEG_INF)
    s_all = _dot_nt(q_all, kwin)
    ps = []
    for r in range(REP):
        s = head(s_all, r) + bias_w
        ps.append(jnp.exp2(s - jnp.max(s, axis=-1, keepdims=True)).astype(BF16))
    ow_scr[...] = jnp.dot(jnp.concatenate(ps, axis=0), vwin, preferred_element_type=F32)

    m_scr[...] = jnp.full(m_scr.shape, NEG_INF, F32)
    acc_scr[...] = jnp.zeros(acc_scr.shape, F32)

    def scores(c):
        k0 = pl.multiple_of(c * KC, KC)
        return _dot_nt(q_scr[...].reshape(REP * TQ, HEAD_B), ksn_ref[pl.ds(k0, KC), :])

    def sel_bias(c):
        return jnp.dot(not_sel, ex_ref[c], preferred_element_type=F32)

    def sel_chunk(c, s_all, bias, diagonal):
        k0 = pl.multiple_of(c * KC, KC)
        v1 = vs1_ref[pl.ds(k0, KC), :]
        if diagonal:
            lane = lax.broadcasted_iota(jnp.int32, (TQ, KC), 1)
            bias = jnp.where(k0 + lane <= t_of((TQ, KC)), bias, NEG_INF)
        ps, alphas = [], []
        for r in range(REP):
            s = head(s_all, r) + bias
            m_old = m_scr[r]
            m_new = jnp.maximum(m_old, jnp.max(s, axis=-1, keepdims=True))
            m_scr[r] = m_new
            ps.append(jnp.exp2(s - jnp.tile(m_new, (1, KC // LANES))).astype(BF16))
            alphas.append(jnp.exp2(m_old - m_new))
        pv = jnp.dot(jnp.concatenate(ps, axis=0), v1, preferred_element_type=F32)
        for r in range(REP):
            acc_scr[r] = jnp.tile(alphas[r], (1, VAUG // LANES)) * acc_scr[r] + head(pv, r)

    n_full = t0 // KC

    def full_chunk(c, carry):
        s_cur, bias_cur = carry
        nxt = (scores(c + 1), sel_bias(c + 1))
        sel_chunk(c, s_cur, bias_cur, False)
        return nxt

    s_last, bias_last = lax.fori_loop(0, n_full, full_chunk, (scores(0), sel_bias(0)))
    sel_chunk(n_full, s_last, bias_last, True)

    gts = jax.nn.sigmoid(gt_ref[...])
    for r in range(REP):
        ow = ow_scr[r * TQ:(r + 1) * TQ, :]
        o_w = ow[:, 0:HEAD_B] / ow[:, HEAD_B:VAUG]
        acc = acc_scr[r]
        o_s = acc[:, 0:HEAD_B] / acc[:, HEAD_B:VAUG]
        g0 = N_BRANCH * r
        o = gts[:, g0:g0 + 1] * oc_scr[r] + gts[:, g0 + 1:g0 + 2] * o_s + gts[:, g0 + 2:g0 + 3] * o_w
        z = z_ref[:, r * HEAD_B:(r + 1) * HEAD_B]
        o_ref[:, r * HEAD_B:(r + 1) * HEAD_B] = (o * (z * jax.nn.sigmoid(z))).astype(o_ref.dtype)


def nsa_attention(proj_b, proj_z, cmp_kv, gates, q_gain, k_gain, overlap_t, expand, win_bias, blocks, batch, seq):
    nq = seq // TQ
    ncmp = seq // CMP_STRIDE
    gw = REP * HEAD_B // LANES
    qspec = lambda off: pl.BlockSpec((TQ, REP * HEAD_B), lambda b, g, t: (b * nq + t, off // gw + g))
    kvspec = lambda off: pl.BlockSpec((seq, HEAD_B), lambda b, g, t: (b, off + g))
    cspec = lambda w: pl.BlockSpec((None, None, None, ncmp, HEAD_B), lambda b, g, t: (w, b, g, 0, 0))
    full = lambda a: pl.BlockSpec(a.shape, lambda b, g, t: (0,) * a.ndim)
    return pl.pallas_call(
        functools.partial(_attn_kernel, seq=seq),
        grid=(batch, KV_GROUPS, nq),
        in_specs=[
            qspec(blocks["q"]), kvspec(blocks["ks"]), kvspec(blocks["vs"]), kvspec(blocks["kw"]), kvspec(blocks["vw"]),
            cspec(0), cspec(1),
            pl.BlockSpec((None, None, TQ, GATE_PAD), lambda b, g, t: (b, g, t, 0)),
            qspec(blocks["z"]),
            full(q_gain), full(k_gain), full(overlap_t), full(expand), full(win_bias),
        ],
        out_specs=pl.BlockSpec((TQ, REP * HEAD_B), lambda b, g, t: (b * nq + t, g)),
        out_shape=jax.ShapeDtypeStruct((batch * seq, KV_GROUPS * REP * HEAD_B), BF16),
        scratch_shapes=[
            pltpu.VMEM((seq, HEAD_B), BF16), pltpu.VMEM((seq, VAUG), BF16),
            pltpu.VMEM((seq + WINDOW, HEAD_B), BF16), pltpu.VMEM((seq + WINDOW, VAUG), BF16),
            pltpu.VMEM((REP, TQ, HEAD_B), BF16), pltpu.VMEM((REP, TQ, LANES), F32),
            pltpu.VMEM((REP, TQ, VAUG), F32), pltpu.VMEM((REP, TQ, HEAD_B), F32),
            pltpu.VMEM((REP * TQ, VAUG), F32),
        ],
        compiler_params=_params(ndim=3),
        name="nsa_attention",
    )(proj_b, proj_b, proj_b, proj_b, proj_b, cmp_kv, cmp_kv, gates, proj_z, q_gain, k_gain, overlap_t, expand,
      win_bias)


def _merge_kernel(ya_ref, yb_ref, wa_ref, wb_ref, ga_ref, gb_ref, out_ref):
    ua = jnp.dot(ya_ref[...], wa_ref[...], preferred_element_type=F32)
    ub = jnp.dot(yb_ref[...], wb_ref[...], preferred_element_type=F32)
    out_ref[...] = (jax.nn.sigmoid(ga_ref[...]) * ua + jax.nn.sigmoid(gb_ref[...]) * ub).astype(out_ref.dtype)


def merge(ya, yb, w_a, w_b, proj_b, gate_col, tm=1024, tn=512):
    m, ka = ya.shape
    n = w_a.shape[1]
    tm = min(tm, m)
    assert gate_col % tn == 0 and n % tn == 0
    ga_block, gb_block = gate_col // tn, (gate_col + n) // tn
    return pl.pallas_call(
        _merge_kernel,
        grid=(m // tm, n // tn),
        in_specs=[
            pl.BlockSpec((tm, ka), lambda i, j: (i, 0)),
            pl.BlockSpec((tm, ka), lambda i, j: (i, 0)),
            pl.BlockSpec((ka, tn), lambda i, j: (0, j)),
            pl.BlockSpec((ka, tn), lambda i, j: (0, j)),
            pl.BlockSpec((tm, tn), lambda i, j: (i, ga_block + j)),
            pl.BlockSpec((tm, tn), lambda i, j: (i, gb_block + j)),
        ],
        out_specs=pl.BlockSpec((tm, tn), lambda i, j: (i, j)),
        out_shape=jax.ShapeDtypeStruct((m, n), BF16),
        compiler_params=_params(ndim=2),
        name="merge",
    )(ya, yb, w_a, w_b, proj_b, proj_b)


def _ple_embed_kernel(p_ref, w_ref, g_ref, o_ref):
    acc = jnp.dot(p_ref[...].astype(BF16), w_ref[...], preferred_element_type=F32)
    o_ref[...] = _rms(acc, g_ref[...]).astype(o_ref.dtype)


def ple_embed(p, w, g, tm=256):
    m, k = p.shape
    n = w.shape[1]
    return pl.pallas_call(
        _ple_embed_kernel,
        grid=(m // tm,),
        in_specs=[pl.BlockSpec((tm, k), lambda i: (i, 0)), pl.BlockSpec((k, n), lambda i: (0, 0)),
                  pl.BlockSpec((1, n), lambda i: (0, 0))],
        out_specs=pl.BlockSpec((tm, n), lambda i: (i, 0)),
        out_shape=jax.ShapeDtypeStruct((m, n), F32),
        compiler_params=_params(),
        name="ple_embed",
    )(p, w, g.reshape(1, n))


def head_minor(w, heads):
    lead = w.shape[:-1]
    return w.reshape(*lead, heads, HEAD_A).swapaxes(-1, -2).reshape(*lead, heads * HEAD_A)


def rwkv_branch(proj_a, proj_s, batch, seq, a_width, w_lora_up, w0, a_lora_up, a0, k_k, k_a, r_k, lnx_w, lnx_b):
    tokens = batch * seq
    heads_a = a_width // HEAD_A
    assert heads_a == SCAN_HEADS and batch % SCAN_BATCH == 0
    lora = w_lora_up.shape[0]
    hm = lambda w: head_minor(w, heads_a)
    w_lora = jnp.stack([hm(w_lora_up), hm(a_lora_up)])
    b_lora = jnp.concatenate([hm(w0), hm(a0)]).reshape(1, -1)
    wa_pre = lora_project(proj_s, w_lora, b_lora)

    def par_scan(t):
        return jnp.tile(t.reshape(heads_a, HEAD_A).T, (1, SCAN_BATCH))

    par3 = lambda t: par_scan(t).reshape(HEAD_A, 1, LANES)
    o = rwkv_scan(proj_a.reshape(batch, seq, -1), wa_pre.reshape(batch, seq, -1),
                  par3(k_k), par3(k_a), par3(r_k), par_scan(lnx_w), par_scan(lnx_b), batch, seq)
    return o.reshape(tokens, a_width)


def nsa_branch(proj_b, proj_z, graw, batch, seq, b_width, q_norm_g, k_norm_g, pe_cmp_k, pe_cmp_v,
               cmp_k_w1, cmp_k_w2, cmp_v_w1, cmp_v_w2):
    kv_width = KV_GROUPS * HEAD_B
    blk = lambda cols: cols // LANES
    blocks = {"q": 0, "kc": blk(b_width), "vc": blk(b_width + kv_width), "ks": blk(b_width + 2 * kv_width),
              "vs": blk(b_width + 3 * kv_width), "kw": blk(b_width + 4 * kv_width), "vw": blk(b_width + 5 * kv_width),
              "z": 0}
    pe = jnp.stack([pe_cmp_k, pe_cmp_v])
    hidden = cmp_k_w1.shape[1]
    w1 = jnp.stack([cmp_k_w1, cmp_v_w1]).reshape(2, L_CMP, HEAD_B, hidden).astype(BF16)
    w2 = jnp.stack([cmp_k_w2, cmp_v_w2]).astype(BF16)
    cmp_kv = compress(proj_b, blocks["kc"], blocks["vc"], pe, w1, w2, k_norm_g[0:1], batch, seq)

    ncmp = seq // CMP_STRIDE
    nsel = seq // L_SEL
    c_start = jnp.arange(ncmp) * CMP_STRIDE
    s_start = jnp.arange(nsel) * L_SEL
    overlap_t = ((c_start[None, :] < (s_start + L_SEL)[:, None]) & (s_start[:, None] < (c_start + L_CMP)[None, :])
                 & (jnp.arange(ncmp)[None, :] < ncmp - 1)).astype(F32)
    key_blk = jnp.arange(seq) // L_SEL
    expand = jnp.where(jnp.arange(LANES)[:, None] == key_blk[None, :], NEG_INF, 0.0).astype(BF16)
    expand = expand.reshape(LANES, seq // KC, KC).transpose(1, 0, 2)
    tl = jnp.arange(TQ)[:, None]
    u = jnp.arange(WINDOW + TQ)[None, :]
    win_bias = jnp.where((u > tl) & (u <= WINDOW + tl), 0.0, NEG_INF).astype(F32)
    gates = graw.reshape(batch, seq, KV_GROUPS, REP * N_BRANCH).transpose(0, 2, 1, 3)
    gates = jnp.pad(gates, ((0, 0), (0, 0), (0, 0), (0, GATE_PAD - REP * N_BRANCH)))
    return nsa_attention(proj_b, proj_z, cmp_kv, gates, q_norm_g.reshape(1, HEAD_B), k_norm_g, overlap_t, expand,
                         win_bias, blocks, batch, seq)


def _layer(x, p, norm_g, w_in, shift_mu, w_lora_up, w0, a_lora_up, a0, k_k, k_a, r_k, lnx_w, lnx_b,
           q_norm_g, k_norm_g, pe_cmp_k, pe_cmp_v, cmp_k_w1, cmp_k_w2, cmp_v_w1, cmp_v_w2,
           w_up_a, w_up_b, w_out, ple_pre_g, w_ple_gate, w_ple, ple_post_g):
    batch, seq, d = x.shape
    tokens = batch * seq
    a_width = w_up_a.shape[0]
    b_width = w_up_b.shape[0]
    heads_a = a_width // HEAD_A
    kv_width = KV_GROUPS * HEAD_B
    n_gate = KV_GROUPS * REP * N_BRANCH
    lora = w_lora_up.shape[0]
    assert lora == HEAD_A and a_lora_up.shape[0] == HEAD_A and 2 * lora == LANES
    a_cols = 4 * a_width + 2 * lora
    g_off = a_cols + b_width + 6 * kv_width
    zb_off = g_off + n_gate
    assert w_in.shape[1] == zb_off + b_width + 2 * d

    hm4 = lambda w: head_minor(w.reshape(*w.shape[:-1], 4, a_width), heads_a).reshape(*w.shape[:-1], 4 * a_width)
    w_a = hm4(w_in[:, :4 * a_width]).astype(BF16)
    mu_a = hm4(shift_mu[:4 * a_width]).reshape(1, -1)
    pad_s = LANES - n_gate
    w_s = jnp.concatenate([w_in[:, 4 * a_width:a_cols], w_in[:, g_off:zb_off], jnp.zeros((d, pad_s), F32)],
                          axis=1).astype(BF16)
    mu_s = jnp.concatenate([shift_mu[4 * a_width:], jnp.zeros((LANES,), F32)]).reshape(1, -1)
    w_b = w_in[:, a_cols:g_off].astype(BF16)
    w_z = w_in[:, zb_off:].astype(BF16)

    x2 = x.reshape(tokens, d)
    h = rmsnorm_rows(x2, norm_g, BF16)
    tm = 512
    shift = dict(mode="shift", tm=tm, tiles_per_seq=seq // tm)
    proj_a = matmul(h, w_a, extras=(mu_a,), tn=1024, name="proj_a", **shift)
    proj_s = matmul(h, w_s, extras=(mu_s,), tn=w_s.shape[1], name="proj_s", **shift)
    proj_b = matmul(h, w_b, tm=tm, tn=1024, name="proj_b")
    proj_z = matmul(h, w_z, tm=tm, tn=1024, name="proj_z")

    ya = rwkv_branch(proj_a, proj_s, batch, seq, a_width, w_lora_up, w0, a_lora_up, a0, k_k, k_a, r_k, lnx_w, lnx_b)
    graw = proj_s[:, 2 * lora:2 * lora + n_gate]
    yb = nsa_branch(proj_b, proj_z, graw, batch, seq, b_width, q_norm_g, k_norm_g, pe_cmp_k, pe_cmp_v,
                    cmp_k_w1, cmp_k_w2, cmp_v_w1, cmp_v_w2)

    w_up_a_nm = head_minor(w_up_a.T, heads_a).T
    merged = merge(ya, yb, w_up_a_nm.astype(BF16), w_up_b.astype(BF16), proj_z, b_width)
    x1 = matmul(merged, w_out.astype(BF16), mode="resid", extras=(x2,), tm=tm, tn=1024, name="out_proj")
    e = ple_embed(p.reshape(tokens, -1), w_ple.astype(BF16), ple_post_g)
    out = matmul(x1, w_ple_gate.astype(BF16), mode="ple", extras=(x1, e), norm_gain=ple_pre_g, tm=tm, tn=1024,
                 name="ple_gate")
    return out.reshape(batch, seq, d)


def kernel(x, p, norm_g, w_in, shift_mu, w_lora_up, w0, a_lora_up, a0, k_k, k_a, r_k, lnx_w, lnx_b, q_norm_g, k_norm_g, pe_cmp_k, pe_cmp_v, cmp_k_w1, cmp_k_w2, cmp_v_w1, cmp_v_w2, w_up_a, w_up_b, w_out, ple_pre_g, w_ple_gate, w_ple, ple_post_g):
    depth = w_in.shape[0]
    for i in range(depth):
        x = _layer(x, p[i], norm_g[i], w_in[i], shift_mu[i], w_lora_up[i], w0[i], a_lora_up[i], a0[i], k_k[i],
                   k_a[i], r_k[i], lnx_w[i], lnx_b[i], q_norm_g[i], k_norm_g[i], pe_cmp_k[i], pe_cmp_v[i],
                   cmp_k_w1[i], cmp_k_w2[i], cmp_v_w1[i], cmp_v_w2[i], w_up_a[i], w_up_b[i], w_out[i],
                   ple_pre_g[i], w_ple_gate[i], w_ple[i], ple_post_g[i])
    return x
```

```python
import functools

import jax
import jax.numpy as jnp
from jax import lax
from jax.experimental import pallas as pl
from jax.experimental.pallas import tpu as pltpu

F32 = jnp.float32
BF16 = jnp.bfloat16

LANES = 128
SUBLANES = 8
VMEM_LIMIT = 56 * 1024 * 1024

NORM_EPS = 1e-6
NEG_INF = -1e30
HEAD_A = 64
GN_EPS = 64e-5
HEAD_B = 128
KV_GROUPS = 4
REP = 4
N_BRANCH = 3
GATE_PAD = 16
L_CMP = 32
CMP_STRIDE = 16
L_SEL = 64
N_SEL = 16
WINDOW = 512
TQ = 256
KC = 256
FORCE_BONUS = 1e3
ATTN_SCALE = HEAD_B ** -0.5
DECAY_SCALE = 0.6065306597126334


def _params(vmem=VMEM_LIMIT, ndim=1):
    return pltpu.CompilerParams(dimension_semantics=("arbitrary",) * ndim, vmem_limit_bytes=vmem)


def _rms(x, gain):
    ms = jnp.mean(x * x, axis=-1, keepdims=True)
    return x * lax.rsqrt(ms + NORM_EPS) * gain


def _token_shift(acc, mu_ref, carry_ref, i, tiles_per_seq):
    tm = acc.shape[0]
    first = (i % tiles_per_seq) == 0
    last_prev = jnp.where(first, 0.0, carry_ref[SUBLANES - 1:SUBLANES, :])
    rolled = pltpu.roll(acc, 1, axis=0)
    row = lax.broadcasted_iota(jnp.int32, acc.shape, 0)
    prev = jnp.where(row == 0, last_prev, rolled)
    carry_ref[...] = acc[tm - SUBLANES:tm, :]
    return acc + mu_ref[...] * (prev - acc)


def _norm_proj_kernel(x_ref, g_ref, w_ref, mu_ref, h_ref, o_ref, carry_ref, *, tiles_per_seq):
    h = _rms(x_ref[...], g_ref[...]).astype(h_ref.dtype)
    h_ref[...] = h
    acc = jnp.dot(h, w_ref[...], preferred_element_type=F32)
    o_ref[...] = _token_shift(acc, mu_ref, carry_ref, pl.program_id(0), tiles_per_seq)


def norm_and_skinny_proj(x, g, w, mu, tiles_per_seq, tm=512):
    m, k = x.shape
    n = w.shape[1]
    return pl.pallas_call(
        functools.partial(_norm_proj_kernel, tiles_per_seq=tiles_per_seq),
        grid=(m // tm,),
        in_specs=[pl.BlockSpec((tm, k), lambda i: (i, 0)), pl.BlockSpec((1, k), lambda i: (0, 0)),
                  pl.BlockSpec((k, n), lambda i: (0, 0)), pl.BlockSpec((1, n), lambda i: (0, 0))],
        out_specs=[pl.BlockSpec((tm, k), lambda i: (i, 0)), pl.BlockSpec((tm, n), lambda i: (i, 0))],
        out_shape=[jax.ShapeDtypeStruct((m, k), BF16), jax.ShapeDtypeStruct((m, n), F32)],
        scratch_shapes=[pltpu.VMEM((SUBLANES, n), F32)],
        compiler_params=_params(),
        name="norm_proj_s",
    )(x, g.reshape(1, k), w, mu)


def _mm_kernel(*refs, mode, tiles_per_seq, norm_a):
    a_ref, b_ref = refs[0], refs[1]
    if norm_a:
        x = a_ref[...]
        acc = jnp.dot((x * refs[2][...]).astype(BF16), b_ref[...], preferred_element_type=F32)
        acc = acc * lax.rsqrt(jnp.mean(x * x, axis=-1, keepdims=True) + NORM_EPS)
        refs = refs[:2] + refs[3:]
    else:
        acc = jnp.dot(a_ref[...], b_ref[...], preferred_element_type=F32)
    if mode == "plain":
        o_ref = refs[2]
        o_ref[...] = acc.astype(o_ref.dtype)
    elif mode == "shift":
        mu_ref, o_ref, carry_ref = refs[2], refs[3], refs[4]
        o_ref[...] = _token_shift(acc, mu_ref, carry_ref, pl.program_id(1), tiles_per_seq).astype(o_ref.dtype)
    elif mode == "resid":
        r_ref, o_ref = refs[2], refs[3]
        o_ref[...] = (r_ref[...] + acc).astype(o_ref.dtype)
    elif mode == "ple":
        x_ref, e_ref, o_ref = refs[2], refs[3], refs[4]
        o_ref[...] = (x_ref[...] + jax.nn.sigmoid(acc) * e_ref[...]).astype(o_ref.dtype)
    else:
        raise ValueError(mode)


def matmul(a, b, *, mode="plain", extras=(), norm_gain=None, out_dtype=F32, tm=512, tn=1024, tiles_per_seq=1,
           name="mm"):
    m, k = a.shape
    n = b.shape[1]
    assert m % tm == 0 and n % tn == 0, (m, n, tm, tn)
    in_specs = [pl.BlockSpec((tm, k), lambda j, i: (i, 0)), pl.BlockSpec((k, tn), lambda j, i: (0, j))]
    if norm_gain is not None:
        in_specs.append(pl.BlockSpec((1, k), lambda j, i: (0, 0)))
        extras = (norm_gain.reshape(1, k),) + tuple(extras)
    scratch = []
    if mode == "shift":
        in_specs.append(pl.BlockSpec((1, tn), lambda j, i: (0, j)))
        scratch.append(pltpu.VMEM((SUBLANES, tn), F32))
    elif mode == "resid":
        in_specs.append(pl.BlockSpec((tm, tn), lambda j, i: (i, j)))
    elif mode == "ple":
        in_specs += [pl.BlockSpec((tm, tn), lambda j, i: (i, j)), pl.BlockSpec((tm, tn), lambda j, i: (i, j))]
    return pl.pallas_call(
        functools.partial(_mm_kernel, mode=mode, tiles_per_seq=tiles_per_seq, norm_a=norm_gain is not None),
        grid=(n // tn, m // tm),
        in_specs=in_specs,
        out_specs=pl.BlockSpec((tm, tn), lambda j, i: (i, j)),
        out_shape=jax.ShapeDtypeStruct((m, n), out_dtype),
        scratch_shapes=scratch,
        compiler_params=_params(ndim=2),
        name=name,
    )(a, b, *extras)


def _split_bf16(x):
    hi = x.astype(BF16)
    return hi, (x - hi.astype(F32)).astype(BF16)


def _lora_kernel(x_ref, whi_ref, wlo_ref, b_ref, o_ref, *, tiles_per_half):
    x = x_ref[...]
    rank = whi_ref.shape[0]
    decay_half = pl.program_id(1) < tiles_per_half
    xs = jnp.where(decay_half, jnp.tanh(x[:, :rank]), x[:, rank:])
    x_hi, x_lo = _split_bf16(xs)
    dot = lambda a, b: jnp.dot(a, b, preferred_element_type=F32)
    acc = dot(x_hi, whi_ref[...]) + (dot(x_hi, wlo_ref[...]) + dot(x_lo, whi_ref[...]))
    o_ref[...] = acc + b_ref[...]


def lora_project(proj_s, w_pair, bias, tm=1024, tn=1024):
    m = proj_s.shape[0]
    _, rank, n = w_pair.shape
    tm = min(tm, m)
    assert m % tm == 0 and n % tn == 0 and 2 * rank == LANES
    half = n // tn
    w_hi, w_lo = _split_bf16(w_pair)
    wspec = pl.BlockSpec((None, rank, tn), lambda i, j: (j // half, 0, j % half))
    return pl.pallas_call(
        functools.partial(_lora_kernel, tiles_per_half=half),
        grid=(m // tm, 2 * half),
        in_specs=[pl.BlockSpec((tm, LANES), lambda i, j: (i, 0)), wspec, wspec,
                  pl.BlockSpec((1, tn), lambda i, j: (0, j))],
        out_specs=pl.BlockSpec((tm, tn), lambda i, j: (i, j)),
        out_shape=jax.ShapeDtypeStruct((m, 2 * n), F32),
        compiler_params=_params(ndim=2),
        name="lora",
    )(proj_s, w_hi, w_lo, bias)


SCAN_BATCH = 4
SCAN_HEADS = LANES // SCAN_BATCH
NGRP = LANES // SCAN_HEADS


def _segment_transpose(x):
    seg = lax.broadcasted_iota(jnp.int32, x[0].shape, 1) // SCAN_HEADS
    y = []
    for i in range(NGRP):
        out = None
        for j in range(SCAN_BATCH):
            shift = ((j - i) % NGRP) * SCAN_HEADS
            piece = x[j] if shift == 0 else pltpu.roll(x[j], shift, axis=1)
            out = piece if out is None else jnp.where(seg == j, piece, out)
        y.append(out)
    return y


def _to_lanes(x_ref, dst_ref, tc):
    for g in range(HEAD_A // NGRP):
        y = _segment_transpose([x_ref[b, :, g * LANES:(g + 1) * LANES] for b in range(SCAN_BATCH)])
        for n_lo in range(NGRP):
            n = g * NGRP + n_lo
            dst_ref[n * tc:(n + 1) * tc, :] = y[n_lo]


def _from_lanes_gated(src_ref, z_ref, o_ref, tc):
    for g in range(HEAD_A // NGRP):
        y = _segment_transpose([src_ref[(g * NGRP + n_lo) * tc:(g * NGRP + n_lo + 1) * tc, :] for n_lo in range(NGRP)])
        for b in range(SCAN_BATCH):
            z = z_ref[b, :, g * LANES:(g + 1) * LANES]
            o_ref[b, :, g * LANES:(g + 1) * LANES] = (y[b] * (z * jax.nn.sigmoid(z))).astype(o_ref.dtype)


def _scan_kernel(r_ref, k_ref, v_ref, wp_ref, ap_ref, z_ref, kkw_ref, kaw_ref, rkw_ref, lnw_ref, lnb_ref,
                 o_ref, state_ref, set_a, set_b, ans, os, bon, gend, *, tc):
    s = pl.program_id(1)
    srcs = (r_ref, k_ref, v_ref, wp_ref, ap_ref)

    @pl.when(s == 0)
    def _():
        state_ref[...] = jnp.zeros_like(state_ref)
        for src, dst in zip(srcs, set_a):
            _to_lanes(src, dst, tc)

    @pl.when(s % 2 == 1)
    def _():
        _scan_chunk(set_a, set_b, srcs, z_ref, kkw_ref, kaw_ref, rkw_ref, lnw_ref, lnb_ref, o_ref, state_ref,
                    ans, os, bon, gend, tc)

    @pl.when((s % 2 == 0) & (s > 0))
    def _():
        _scan_chunk(set_b, set_a, srcs, z_ref, kkw_ref, kaw_ref, rkw_ref, lnw_ref, lnb_ref, o_ref, state_ref,
                    ans, os, bon, gend, tc)


def _scan_chunk(cur, nxt, srcs, z_ref, kkw_ref, kaw_ref, rkw_ref, lnw_ref, lnb_ref, o_ref, state_ref,
                ans, os, bon, gend, tc):
    n_ch = HEAD_A
    rs, ks, vs, ws, bvs = cur
    n_groups = HEAD_A // NGRP
    groups_per_step = 2
    steps_per_rowblock = n_groups // groups_per_step
    assert tc * groups_per_step == n_groups * (tc // SUBLANES)

    def rows(n):
        return pl.ds(pl.multiple_of(n * tc, tc), tc)

    def norm_acc(n, acc):
        kkr = ks[rows(n), :] * kkw_ref[n]
        return acc + kkr * kkr

    nsq = lax.fori_loop(0, n_ch, norm_acc, jnp.zeros((tc, LANES), F32), unroll=8)
    inv = 1.0 / jnp.maximum(jnp.sqrt(nsq), 1e-12)

    tri = jnp.where(lax.broadcasted_iota(jnp.int32, (tc, tc), 0) >= lax.broadcasted_iota(jnp.int32, (tc, tc), 1),
                    1.0, 0.0)

    def prep(n, bacc):
        k = ks[rows(n), :]
        a = jax.nn.sigmoid(bvs[rows(n), :])
        kk = k * kkw_ref[n] * inv
        logw = -DECAY_SCALE * jax.nn.sigmoid(ws[rows(n), :])
        cum = jnp.dot(tri, logw, preferred_element_type=F32, precision=lax.Precision.HIGHEST)
        g = jnp.exp(cum)
        ig = jnp.exp(-cum)
        kmod = k * (1.0 + (a - 1.0) * kaw_ref[n])
        r = rs[rows(n), :]
        rs[rows(n), :] = r * g
        ks[rows(n), :] = kmod * ig
        bvs[rows(n), :] = kk * a * ig
        ans[rows(n), :] = -kk * jnp.exp(cum - logw)
        gend[pl.ds(n, 1), :] = g[tc - 1:tc, :]
        return bacc + r * kmod * rkw_ref[n]

    bon[...] = lax.fori_loop(0, n_ch, prep, jnp.zeros((tc, LANES), F32), unroll=16)

    def bcast(ref, row):
        return ref[pl.ds(row, 1), :][None]

    def sa_first(j, acc):
        return acc + state_ref[j] * bcast(ans, j * tc)

    slab = (n_ch // SUBLANES, SUBLANES, LANES)
    sa0 = lax.fori_loop(0, n_ch, sa_first, jnp.zeros(slab, F32), unroll=4)

    def step(t, sa):
        v = vs[pl.ds(t, n_ch, stride=tc), :].reshape(slab)
        t_next = jnp.minimum(t + 1, tc - 1)

        out = jnp.zeros(slab, F32)
        sa_next = jnp.zeros(slab, F32)
        for j in range(n_ch):
            row = j * tc + t
            s_new = state_ref[j] + sa * bcast(bvs, row) + v * bcast(ks, row)
            state_ref[j] = s_new
            out = out + s_new * bcast(rs, row)
            sa_next = sa_next + s_new * bcast(ans, j * tc + t_next)
        o = out.reshape(n_ch, LANES)
        mu = jnp.mean(o, axis=0, keepdims=True)
        d = o - mu
        var = jnp.mean(d * d, axis=0, keepdims=True)
        on = d * lax.rsqrt(var + GN_EPS) * lnw_ref[...] + lnb_ref[...]
        bonus = bon[pl.ds(t, 1), :] * v.reshape(n_ch, LANES)
        os[pl.ds(t, n_ch, stride=tc), :] = on + bonus

        row0 = pl.multiple_of((t // steps_per_rowblock) * SUBLANES, SUBLANES)
        for src, dst in zip(srcs, nxt):
            for gi in range(groups_per_step):
                g = (t % steps_per_rowblock) * groups_per_step + gi
                lane0 = pl.multiple_of(g * LANES, LANES)
                y = _segment_transpose([src[b, pl.ds(row0, SUBLANES), pl.ds(lane0, LANES)] for b in range(SCAN_BATCH)])
                for n_lo in range(NGRP):
                    dst[pl.ds(pl.multiple_of((g * NGRP + n_lo) * tc + row0, SUBLANES), SUBLANES), :] = y[n_lo]
        return sa_next

    lax.fori_loop(0, tc, step, sa0)

    def rescale(j, c):
        state_ref[j] = state_ref[j] * bcast(gend, j)
        return c

    lax.fori_loop(0, n_ch, rescale, 0, unroll=8)
    _from_lanes_gated(os, z_ref, o_ref, tc)


def rwkv_scan(proj_a, wa_pre, kkw, kaw, rkw, lnw, lnb, batch, seq, tc=32):
    width = HEAD_A * SCAN_HEADS
    n_chunks = seq // tc
    seq_in = lambda col: pl.BlockSpec((SCAN_BATCH, tc, width), lambda g, s: (g, jnp.minimum(s, n_chunks - 1), col))
    seq_out = lambda col: pl.BlockSpec((SCAN_BATCH, tc, width), lambda g, s: (g, jnp.maximum(s - 1, 0), col))
    par3 = pl.BlockSpec((HEAD_A, 1, LANES), lambda g, s: (0, 0, 0))
    par2 = pl.BlockSpec((HEAD_A, LANES), lambda g, s: (0, 0))
    buf = pltpu.VMEM((HEAD_A * tc, LANES), F32)
    return pl.pallas_call(
        functools.partial(_scan_kernel, tc=tc),
        grid=(batch // SCAN_BATCH, n_chunks + 1),
        in_specs=[seq_in(0), seq_in(1), seq_in(2), seq_in(0), seq_in(1), seq_out(3), par3, par3, par3, par2, par2],
        out_specs=seq_out(0),
        out_shape=jax.ShapeDtypeStruct((batch, seq, width), BF16),
        scratch_shapes=[pltpu.VMEM((HEAD_A, HEAD_A // SUBLANES, SUBLANES, LANES), F32), [buf] * 5, [buf] * 5,
                        buf, buf, pltpu.VMEM((tc, LANES), F32), pltpu.VMEM((HEAD_A, LANES), F32)],
        compiler_params=_params(ndim=2),
        name="rwkv_scan",
    )(proj_a, proj_a, proj_a, wa_pre, wa_pre, proj_a, kkw, kaw, rkw, lnw, lnb)


def _compress_kernel(x_ref, pe_ref, w1_ref, w2_ref, kg_ref, o_ref):
    which = pl.program_id(0)
    half = L_CMP // 2
    nblk = x_ref.shape[0] // CMP_STRIDE
    h1 = jnp.zeros((nblk, w1_ref.shape[-1]), F32)
    h2 = jnp.zeros((nblk, w1_ref.shape[-1]), F32)
    for l in range(half):
        x = x_ref[pl.ds(l, nblk, stride=CMP_STRIDE), :]
        a1 = (x + pe_ref[l:l + 1, :]).astype(BF16)
        a2 = (x + pe_ref[half + l:half + l + 1, :]).astype(BF16)
        h1 = h1 + jnp.dot(a1, w1_ref[l], preferred_element_type=F32)
        h2 = h2 + jnp.dot(a2, w1_ref[half + l], preferred_element_type=F32)
    hid = h1 + pltpu.roll(h2, nblk - 1, axis=0)
    hid = jax.nn.gelu(hid)
    out = jnp.dot(hid.astype(BF16), w2_ref[...], preferred_element_type=F32)
    ms = jnp.mean(out * out, axis=-1, keepdims=True)
    normed = out * lax.rsqrt(ms + NORM_EPS) * kg_ref[...]
    out = jnp.where(which == 0, normed, out)
    row = lax.broadcasted_iota(jnp.int32, out.shape, 0)
    o_ref[...] = jnp.where(row < nblk - 1, out, 0.0)


def compress(proj_b, kc_block, vc_block, pe, w1, w2, kgain, batch, seq):
    nblk = seq // CMP_STRIDE
    hidden = w1.shape[-1]

    def xmap(w, b, g):
        return (b, kc_block + w * (vc_block - kc_block) + g)

    return pl.pallas_call(
        _compress_kernel,
        grid=(2, batch, KV_GROUPS),
        in_specs=[
            pl.BlockSpec((seq, HEAD_B), xmap),
            pl.BlockSpec((None, L_CMP, HEAD_B), lambda w, b, g: (w, 0, 0)),
            pl.BlockSpec((None, L_CMP, HEAD_B, hidden), lambda w, b, g: (w, 0, 0, 0)),
            pl.BlockSpec((None, hidden, HEAD_B), lambda w, b, g: (w, 0, 0)),
            pl.BlockSpec((1, HEAD_B), lambda w, b, g: (0, 0)),
        ],
        out_specs=pl.BlockSpec((None, None, None, nblk, HEAD_B), lambda w, b, g: (w, b, g, 0, 0)),
        out_shape=jax.ShapeDtypeStruct((2, batch, KV_GROUPS, nblk, HEAD_B), F32),
        compiler_params=_params(ndim=3),
        name="nsa_compress",
    )(proj_b, pe, w1, w2, kgain)


def _dot_nt(a, b):
    return lax.dot_general(a, b, (((1,), (1,)), ((), ())), preferred_element_type=F32)


LOG2E = 1.4426950408889634
VAUG = 2 * HEAD_B


def _attn_kernel(q_ref, ks_ref, vs_ref, kw_ref, vw_ref, kc_ref, vc_ref, gt_ref, z_ref, qg_ref, kg_ref,
                 ovt_ref, ex_ref, wb_ref, o_ref, ksn_ref, vs1_ref, kwn_ref, vw1_ref, q_scr, m_scr, acc_scr,
                 oc_scr, ow_scr, *, seq):
    qt = pl.program_id(2)
    ncmp = kc_ref.shape[0]
    nsel = seq // L_SEL

    @pl.when(qt == 0)
    def _():
        ones_col = jnp.ones((seq, HEAD_B), BF16)
        ksn_ref[...] = _rms(ks_ref[...], kg_ref[1:2, :]).astype(BF16)
        vs1_ref[:, 0:HEAD_B] = vs_ref[...].astype(BF16)
        vs1_ref[:, HEAD_B:VAUG] = ones_col
        kwn_ref[0:WINDOW, :] = jnp.zeros((WINDOW, HEAD_B), BF16)
        vw1_ref[0:WINDOW, :] = jnp.zeros((WINDOW, VAUG), BF16)
        kwn_ref[WINDOW:WINDOW + seq, :] = _rms(kw_ref[...], kg_ref[2:3, :]).astype(BF16)
        vw1_ref[WINDOW:WINDOW + seq, 0:HEAD_B] = vw_ref[...].astype(BF16)
        vw1_ref[WINDOW:WINDOW + seq, HEAD_B:VAUG] = ones_col

    t0 = pl.multiple_of(qt * TQ, TQ)
    for r in range(REP):
        q_scr[r] = (_rms(q_ref[:, r * HEAD_B:(r + 1) * HEAD_B], qg_ref[...]) * (ATTN_SCALE * LOG2E)).astype(BF16)

    def t_of(shape):
        return t0 + lax.broadcasted_iota(jnp.int32, shape, 0)

    n_idx = lax.broadcasted_iota(jnp.int32, (TQ, ncmp), 1)
    bias_c = jnp.where(n_idx * CMP_STRIDE + (L_CMP - 1) <= t_of((TQ, ncmp)), 0.0, NEG_INF)
    bias_c = jnp.where(n_idx < ncmp - 1, bias_c, NEG_INF)
    row_ok = jnp.where(t_of((TQ, 1)) >= L_CMP - 1, 1.0, 0.0)
    q_all = q_scr[...].reshape(REP * TQ, HEAD_B)
    head = lambda x, r: x[r * TQ:(r + 1) * TQ]
    s_all = _dot_nt(q_all, kc_ref[...].astype(BF16))
    prs = []
    for r in range(REP):
        s = head(s_all, r) + bias_c
        p = jnp.exp2(s - jnp.max(s, axis=-1, keepdims=True))
        prs.append(p * (row_ok / jnp.sum(p, axis=-1, keepdims=True)))
    psum = prs[0] + prs[1] + prs[2] + prs[3]
    oc_scr[...] = jnp.dot(jnp.concatenate(prs, axis=0).astype(BF16), vc_ref[...].astype(BF16),
                          preferred_element_type=F32).reshape(REP, TQ, HEAD_B)
    imp_t = lax.dot_general(ovt_ref[...], psum, (((1,), (1,)), ((), ())), preferred_element_type=F32,
                            precision=lax.Precision.HIGHEST)

    j_idx = lax.broadcasted_iota(jnp.int32, (nsel, TQ), 0)
    t_sel = t0 + lax.broadcasted_iota(jnp.int32, (nsel, TQ), 1)
    cur = t_sel // L_SEL
    forced = jnp.where((j_idx == 0) | (j_idx == cur) | (j_idx == cur - 1), FORCE_BONUS, 0.0)
    score = jnp.where(j_idx * L_SEL <= t_sel, imp_t + forced, NEG_INF)
    rank = jnp.zeros((nsel, TQ), F32)
    for i in range(nsel):
        si = score[i:i + 1, :]
        ahead = (si > score) | ((si == score) & (j_idx > i))
        rank = rank + jnp.where(ahead, 1.0, 0.0)
    sel_t = jnp.where((rank < min(N_SEL, nsel)) & (score > 0.5 * NEG_INF), 1.0, 0.0)
    sel = jnp.concatenate([sel_t, jnp.zeros((LANES - nsel, TQ), F32)], axis=0).T
    not_sel = (1.0 - sel).astype(BF16)

    span = WINDOW + TQ
    kwin = kwn_ref[pl.ds(t0, span), :]
    vwin = vw1_ref[pl.ds(t0, span), :]
    lane_w = lax.broadcasted_iota(jnp.int32, (TQ, span), 1)
    bias_w = jnp.where(lane_w >= WINDOW - t0, wb_ref[...], NEG_INF)
    s_all = _dot_nt(q_all, kwin)
    ps = []
    for r in range(REP):
        s = head(s_all, r) + bias_w
        ps.append(jnp.exp2(s - jnp.max(s, axis=-1, keepdims=True)).astype(BF16))
    ow_scr[...] = jnp.dot(jnp.concatenate(ps, axis=0), vwin, preferred_element_type=F32)

    m_scr[...] = jnp.full(m_scr.shape, NEG_INF, F32)
    acc_scr[...] = jnp.zeros(acc_scr.shape, F32)

    def scores(c):
        k0 = pl.multiple_of(c * KC, KC)
        return _dot_nt(q_scr[...].reshape(REP * TQ, HEAD_B), ksn_ref[pl.ds(k0, KC), :])

    def sel_bias(c):
        return jnp.dot(not_sel, ex_ref[c], preferred_element_type=F32)

    def sel_chunk(c, s_all, bias, diagonal):
        k0 = pl.multiple_of(c * KC, KC)
        v1 = vs1_ref[pl.ds(k0, KC), :]
        if diagonal:
            lane = lax.broadcasted_iota(jnp.int32, (TQ, KC), 1)
            bias = jnp.where(k0 + lane <= t_of((TQ, KC)), bias, NEG_INF)
        ps, alphas = [], []
        for r in range(REP):
            s = head(s_all, r) + bias
            m_old = m_scr[r]
            m_new = jnp.maximum(m_old, jnp.max(s, axis=-1, keepdims=True))
            m_scr[r] = m_new
            ps.append(jnp.exp2(s - jnp.tile(m_new, (1, KC // LANES))).astype(BF16))
            alphas.append(jnp.exp2(m_old - m_new))
        pv = jnp.dot(jnp.concatenate(ps, axis=0), v1, preferred_element_type=F32)
        for r in range(REP):
            acc_scr[r] = jnp.tile(alphas[r], (1, VAUG // LANES)) * acc_scr[r] + head(pv, r)

    n_full = t0 // KC

    def full_chunk(c, carry):
        s_cur, bias_cur = carry
        nxt = (scores(c + 1), sel_bias(c + 1))
        sel_chunk(c, s_cur, bias_cur, False)
        return nxt

    s_last, bias_last = lax.fori_loop(0, n_full, full_chunk, (scores(0), sel_bias(0)))
    sel_chunk(n_full, s_last, bias_last, True)

    gts = jax.nn.sigmoid(gt_ref[...])
    for r in range(REP):
        ow = ow_scr[r * TQ:(r + 1) * TQ, :]
        o_w = ow[:, 0:HEAD_B] / ow[:, HEAD_B:VAUG]
        acc = acc_scr[r]
        o_s = acc[:, 0:HEAD_B] / acc[:, HEAD_B:VAUG]
        g0 = N_BRANCH * r
        o = gts[:, g0:g0 + 1] * oc_scr[r] + gts[:, g0 + 1:g0 + 2] * o_s + gts[:, g0 + 2:g0 + 3] * o_w
        z = z_ref[:, r * HEAD_B:(r + 1) * HEAD_B]
        o_ref[:, r * HEAD_B:(r + 1) * HEAD_B] = (o * (z * jax.nn.sigmoid(z))).astype(o_ref.dtype)


def nsa_attention(proj_b, proj_z, cmp_kv, gates, q_gain, k_gain, overlap_t, expand, win_bias, blocks, batch, seq):
    nq = seq // TQ
    ncmp = seq // CMP_STRIDE
    gw = REP * HEAD_B // LANES
    qspec = lambda off: pl.BlockSpec((TQ, REP * HEAD_B), lambda b, g, t: (b * nq + t, off // gw + g))
    kvspec = lambda off: pl.BlockSpec((seq, HEAD_B), lambda b, g, t: (b, off + g))
    cspec = lambda w: pl.BlockSpec((None, None, None, ncmp, HEAD_B), lambda b, g, t: (w, b, g, 0, 0))
    full = lambda a: pl.BlockSpec(a.shape, lambda b, g, t: (0,) * a.ndim)
    return pl.pallas_call(
        functools.partial(_attn_kernel, seq=seq),
        grid=(batch, KV_GROUPS, nq),
        in_specs=[
            qspec(blocks["q"]), kvspec(blocks["ks"]), kvspec(blocks["vs"]), kvspec(blocks["kw"]), kvspec(blocks["vw"]),
            cspec(0), cspec(1),
            pl.BlockSpec((None, None, TQ, GATE_PAD), lambda b, g, t: (b, g, t, 0)),
            qspec(blocks["z"]),
            full(q_gain), full(k_gain), full(overlap_t), full(expand), full(win_bias),
        ],
        out_specs=pl.BlockSpec((TQ, REP * HEAD_B), lambda b, g, t: (b * nq + t, g)),
        out_shape=jax.ShapeDtypeStruct((batch * seq, KV_GROUPS * REP * HEAD_B), BF16),
        scratch_shapes=[
            pltpu.VMEM((seq, HEAD_B), BF16), pltpu.VMEM((seq, VAUG), BF16),
            pltpu.VMEM((seq + WINDOW, HEAD_B), BF16), pltpu.VMEM((seq + WINDOW, VAUG), BF16),
            pltpu.VMEM((REP, TQ, HEAD_B), BF16), pltpu.VMEM((REP, TQ, LANES), F32),
            pltpu.VMEM((REP, TQ, VAUG), F32), pltpu.VMEM((REP, TQ, HEAD_B), F32),
            pltpu.VMEM((REP * TQ, VAUG), F32),
        ],
        compiler_params=_params(ndim=3),
        name="nsa_attention",
    )(proj_b, proj_b, proj_b, proj_b, proj_b, cmp_kv, cmp_kv, gates, proj_z, q_gain, k_gain, overlap_t, expand,
      win_bias)


def _merge_kernel(ya_ref, yb_ref, wa_ref, wb_ref, ga_ref, gb_ref, out_ref):
    ua = jnp.dot(ya_ref[...], wa_ref[...], preferred_element_type=F32)
    ub = jnp.dot(yb_ref[...], wb_ref[...], preferred_element_type=F32)
    out_ref[...] = (jax.nn.sigmoid(ga_ref[...]) * ua + jax.nn.sigmoid(gb_ref[...]) * ub).astype(out_ref.dtype)


def merge(ya, yb, w_a, w_b, proj_b, gate_col, tm=1024, tn=512):
    m, ka = ya.shape
    n = w_a.shape[1]
    tm = min(tm, m)
    assert gate_col % tn == 0 and n % tn == 0
    ga_block, gb_block = gate_col // tn, (gate_col + n) // tn
    return pl.pallas_call(
        _merge_kernel,
        grid=(m // tm, n // tn),
        in_specs=[
            pl.BlockSpec((tm, ka), lambda i, j: (i, 0)),
            pl.BlockSpec((tm, ka), lambda i, j: (i, 0)),
            pl.BlockSpec((ka, tn), lambda i, j: (0, j)),
            pl.BlockSpec((ka, tn), lambda i, j: (0, j)),
            pl.BlockSpec((tm, tn), lambda i, j: (i, ga_block + j)),
            pl.BlockSpec((tm, tn), lambda i, j: (i, gb_block + j)),
        ],
        out_specs=pl.BlockSpec((tm, tn), lambda i, j: (i, j)),
        out_shape=jax.ShapeDtypeStruct((m, n), BF16),
        compiler_params=_params(ndim=2),
        name="merge",
    )(ya, yb, w_a, w_b, proj_b, proj_b)


def _ple_embed_kernel(p_ref, w_ref, g_ref, o_ref):
    acc = jnp.dot(p_ref[...].astype(BF16), w_ref[...], preferred_element_type=F32)
    o_ref[...] = _rms(acc, g_ref[...]).astype(o_ref.dtype)


def ple_embed(p, w, g, tm=256):
    m, k = p.shape
    n = w.shape[1]
    return pl.pallas_call(
        _ple_embed_kernel,
        grid=(m // tm,),
        in_specs=[pl.BlockSpec((tm, k), lambda i: (i, 0)), pl.BlockSpec((k, n), lambda i: (0, 0)),
                  pl.BlockSpec((1, n), lambda i: (0, 0))],
        out_specs=pl.BlockSpec((tm, n), lambda i: (i, 0)),
        out_shape=jax.ShapeDtypeStruct((m, n), F32),
        compiler_params=_params(),
        name="ple_embed",
    )(p, w, g.reshape(1, n))


def head_minor(w, heads):
    lead = w.shape[:-1]
    return w.reshape(*lead, heads, HEAD_A).swapaxes(-1, -2).reshape(*lead, heads * HEAD_A)


def rwkv_branch(proj_a, proj_s, batch, seq, a_width, w_lora_up, w0, a_lora_up, a0, k_k, k_a, r_k, lnx_w, lnx_b):
    tokens = batch * seq
    heads_a = a_width // HEAD_A
    assert heads_a == SCAN_HEADS and batch % SCAN_BATCH == 0
    lora = w_lora_up.shape[0]
    hm = lambda w: head_minor(w, heads_a)
    w_lora = jnp.stack([hm(w_lora_up), hm(a_lora_up)])
    b_lora = jnp.concatenate([hm(w0), hm(a0)]).reshape(1, -1)
    wa_pre = lora_project(proj_s, w_lora, b_lora)

    def par_scan(t):
        return jnp.tile(t.reshape(heads_a, HEAD_A).T, (1, SCAN_BATCH))

    par3 = lambda t: par_scan(t).reshape(HEAD_A, 1, LANES)
    o = rwkv_scan(proj_a.reshape(batch, seq, -1), wa_pre.reshape(batch, seq, -1),
                  par3(k_k), par3(k_a), par3(r_k), par_scan(lnx_w), par_scan(lnx_b), batch, seq)
    return o.reshape(tokens, a_width)


def nsa_branch(proj_b, proj_z, graw, batch, seq, b_width, q_norm_g, k_norm_g, pe_cmp_k, pe_cmp_v,
               cmp_k_w1, cmp_k_w2, cmp_v_w1, cmp_v_w2):
    kv_width = KV_GROUPS * HEAD_B
    blk = lambda cols: cols // LANES
    blocks = {"q": 0, "kc": blk(b_width), "vc": blk(b_width + kv_width), "ks": blk(b_width + 2 * kv_width),
              "vs": blk(b_width + 3 * kv_width), "kw": blk(b_width + 4 * kv_width), "vw": blk(b_width + 5 * kv_width),
              "z": 0}
    pe = jnp.stack([pe_cmp_k, pe_cmp_v])
    hidden = cmp_k_w1.shape[1]
    w1 = jnp.stack([cmp_k_w1, cmp_v_w1]).reshape(2, L_CMP, HEAD_B, hidden).astype(BF16)
    w2 = jnp.stack([cmp_k_w2, cmp_v_w2]).astype(BF16)
    cmp_kv = compress(proj_b, blocks["kc"], blocks["vc"], pe, w1, w2, k_norm_g[0:1], batch, seq)

    ncmp = seq // CMP_STRIDE
    nsel = seq // L_SEL
    c_start = jnp.arange(ncmp) * CMP_STRIDE
    s_start = jnp.arange(nsel) * L_SEL
    overlap_t = ((c_start[None, :] < (s_start + L_SEL)[:, None]) & (s_start[:, None] < (c_start + L_CMP)[None, :])
                 & (jnp.arange(ncmp)[None, :] < ncmp - 1)).astype(F32)
    key_blk = jnp.arange(seq) // L_SEL
    expand = jnp.where(jnp.arange(LANES)[:, None] == key_blk[None, :], NEG_INF, 0.0).astype(BF16)
    expand = expand.reshape(LANES, seq // KC, KC).transpose(1, 0, 2)
    tl = jnp.arange(TQ)[:, None]
    u = jnp.arange(WINDOW + TQ)[None, :]
    win_bias = jnp.where((u > tl) & (u <= WINDOW + tl), 0.0, NEG_INF).astype(F32)
    gates = graw.reshape(batch, seq, KV_GROUPS, REP * N_BRANCH).transpose(0, 2, 1, 3)
    gates = jnp.pad(gates, ((0, 0), (0, 0), (0, 0), (0, GATE_PAD - REP * N_BRANCH)))
    return nsa_attention(proj_b, proj_z, cmp_kv, gates, q_norm_g.reshape(1, HEAD_B), k_norm_g, overlap_t, expand,
                         win_bias, blocks, batch, seq)


def _layer(x, p, norm_g, w_in, shift_mu, w_lora_up, w0, a_lora_up, a0, k_k, k_a, r_k, lnx_w, lnx_b,
           q_norm_g, k_norm_g, pe_cmp_k, pe_cmp_v, cmp_k_w1, cmp_k_w2, cmp_v_w1, cmp_v_w2,
           w_up_a, w_up_b, w_out, ple_pre_g, w_ple_gate, w_ple, ple_post_g):
    batch, seq, d = x.shape
    tokens = batch * seq
    a_width = w_up_a.shape[0]
    b_width = w_up_b.shape[0]
    heads_a = a_width // HEAD_A
    kv_width = KV_GROUPS * HEAD_B
    n_gate = KV_GROUPS * REP * N_BRANCH
    lora = w_lora_up.shape[0]
    assert lora == HEAD_A and a_lora_up.shape[0] == HEAD_A and 2 * lora == LANES
    a_cols = 4 * a_width + 2 * lora
    g_off = a_cols + b_width + 6 * kv_width
    zb_off = g_off + n_gate
    assert w_in.shape[1] == zb_off + b_width + 2 * d

    hm4 = lambda w: head_minor(w.reshape(*w.shape[:-1], 4, a_width), heads_a).reshape(*w.shape[:-1], 4 * a_width)
    w_a = hm4(w_in[:, :4 * a_width]).astype(BF16)
    mu_a = hm4(shift_mu[:4 * a_width]).reshape(1, -1)
    pad_s = LANES - n_gate
    w_s = jnp.concatenate([w_in[:, 4 * a_width:a_cols], w_in[:, g_off:zb_off], jnp.zeros((d, pad_s), F32)],
                          axis=1).astype(BF16)
    mu_s = jnp.concatenate([shift_mu[4 * a_width:], jnp.zeros((LANES,), F32)]).reshape(1, -1)
    w_b = w_in[:, a_cols:g_off].astype(BF16)
    w_z = w_in[:, zb_off:].astype(BF16)

    x2 = x.reshape(tokens, d)
    tm = 512
    h, proj_s = norm_and_skinny_proj(x2, norm_g, w_s, mu_s, seq // tm, tm=tm)
    proj_a = matmul(h, w_a, mode="shift", extras=(mu_a,), tm=tm, tn=1024, tiles_per_seq=seq // tm, name="proj_a")
    proj_b = matmul(h, w_b, tm=tm, tn=1024, name="proj_b")
    proj_z = matmul(h, w_z, tm=tm, tn=1024, name="proj_z")

    ya = rwkv_branch(proj_a, proj_s, batch, seq, a_width, w_lora_up, w0, a_lora_up, a0, k_k, k_a, r_k, lnx_w, lnx_b)
    graw = proj_s[:, 2 * lora:2 * lora + n_gate]
    yb = nsa_branch(proj_b, proj_z, graw, batch, seq, b_width, q_norm_g, k_norm_g, pe_cmp_k, pe_cmp_v,
                    cmp_k_w1, cmp_k_w2, cmp_v_w1, cmp_v_w2)

    w_up_a_nm = head_minor(w_up_a.T, heads_a).T
    merged = merge(ya, yb, w_up_a_nm.astype(BF16), w_up_b.astype(BF16), proj_z, b_width)
    x1 = matmul(merged, w_out.astype(BF16), mode="resid", extras=(x2,), tm=tm, tn=1024, name="out_proj")
    e = ple_embed(p.reshape(tokens, -1), w_ple.astype(BF16), ple_post_g)
    out = matmul(x1, w_ple_gate.astype(BF16), mode="ple", extras=(x1, e), norm_gain=ple_pre_g, tm=tm, tn=1024,
                 name="ple_gate")
    return out.reshape(batch, seq, d)


def kernel(x, p, norm_g, w_in, shift_mu, w_lora_up, w0, a_lora_up, a0, k_k, k_a, r_k, lnx_w, lnx_b, q_norm_g, k_norm_g, pe_cmp_k, pe_cmp_v, cmp_k_w1, cmp_k_w2, cmp_v_w1, cmp_v_w2, w_up_a, w_up_b, w_out, ple_pre_g, w_ple_gate, w_ple, ple_post_g):
    depth = w_in.shape[0]
    for i in range(depth):
        x = _layer(x, p[i], norm_g[i], w_in[i], shift_mu[i], w_lora_up[i], w0[i], a_lora_up[i], a0[i], k_k[i],
                   k_a[i], r_k[i], lnx_w[i], lnx_b[i], q_norm_g[i], k_norm_g[i], pe_cmp_k[i], pe_cmp_v[i],
                   cmp_k_w1[i], cmp_k_w2[i], cmp_v_w1[i], cmp_v_w2[i], w_up_a[i], w_up_b[i], w_out[i],
                   ple_pre_g[i], w_ple_gate[i], w_ple[i], ple_post_g[i])
    return x
```

```python
import functools

import jax
import jax.numpy as jnp
from jax import lax
from jax.experimental import pallas as pl
from jax.experimental.pallas import tpu as pltpu

F32 = jnp.float32
BF16 = jnp.bfloat16

LANES = 128
SUBLANES = 8
VMEM_LIMIT = 56 * 1024 * 1024

NORM_EPS = 1e-6
NEG_INF = -1e30
HEAD_A = 64
GN_EPS = 64e-5
HEAD_B = 128
KV_GROUPS = 4
REP = 4
N_BRANCH = 3
GATE_PAD = 16
L_CMP = 32
CMP_STRIDE = 16
L_SEL = 64
N_SEL = 16
WINDOW = 512
TQ = 256
KC = 256
FORCE_BONUS = 1e3
ATTN_SCALE = HEAD_B ** -0.5
DECAY_SCALE = 0.6065306597126334


def _params(vmem=VMEM_LIMIT, ndim=1):
    return pltpu.CompilerParams(dimension_semantics=("arbitrary",) * ndim, vmem_limit_bytes=vmem)


def _rms(x, gain):
    ms = jnp.mean(x * x, axis=-1, keepdims=True)
    return x * lax.rsqrt(ms + NORM_EPS) * gain


def _token_shift(acc, mu_ref, carry_ref, i, tiles_per_seq):
    tm = acc.shape[0]
    first = (i % tiles_per_seq) == 0
    last_prev = jnp.where(first, 0.0, carry_ref[SUBLANES - 1:SUBLANES, :])
    rolled = pltpu.roll(acc, 1, axis=0)
    row = lax.broadcasted_iota(jnp.int32, acc.shape, 0)
    prev = jnp.where(row == 0, last_prev, rolled)
    carry_ref[...] = acc[tm - SUBLANES:tm, :]
    return acc + mu_ref[...] * (prev - acc)


def _norm_proj_kernel(x_ref, g_ref, w_ref, mu_ref, h_ref, o_ref, carry_ref, *, tiles_per_seq):
    h = _rms(x_ref[...], g_ref[...]).astype(h_ref.dtype)
    h_ref[...] = h
    acc = jnp.dot(h, w_ref[...], preferred_element_type=F32)
    o_ref[...] = _token_shift(acc, mu_ref, carry_ref, pl.program_id(0), tiles_per_seq)


def norm_and_skinny_proj(x, g, w, mu, tiles_per_seq, tm=512):
    m, k = x.shape
    n = w.shape[1]
    return pl.pallas_call(
        functools.partial(_norm_proj_kernel, tiles_per_seq=tiles_per_seq),
        grid=(m // tm,),
        in_specs=[pl.BlockSpec((tm, k), lambda i: (i, 0)), pl.BlockSpec((1, k), lambda i: (0, 0)),
                  pl.BlockSpec((k, n), lambda i: (0, 0)), pl.BlockSpec((1, n), lambda i: (0, 0))],
        out_specs=[pl.BlockSpec((tm, k), lambda i: (i, 0)), pl.BlockSpec((tm, n), lambda i: (i, 0))],
        out_shape=[jax.ShapeDtypeStruct((m, k), BF16), jax.ShapeDtypeStruct((m, n), F32)],
        scratch_shapes=[pltpu.VMEM((SUBLANES, n), F32)],
        compiler_params=_params(),
        name="norm_proj_s",
    )(x, g.reshape(1, k), w, mu)


def _mm_kernel(*refs, mode, tiles_per_seq, norm_a):
    a_ref, b_ref = refs[0], refs[1]
    if norm_a:
        x = a_ref[...]
        acc = jnp.dot((x * refs[2][...]).astype(BF16), b_ref[...], preferred_element_type=F32)
        acc = acc * lax.rsqrt(jnp.mean(x * x, axis=-1, keepdims=True) + NORM_EPS)
        refs = refs[:2] + refs[3:]
    else:
        acc = jnp.dot(a_ref[...], b_ref[...], preferred_element_type=F32)
    if mode == "plain":
        o_ref = refs[2]
        o_ref[...] = acc.astype(o_ref.dtype)
    elif mode == "shift":
        mu_ref, o_ref, carry_ref = refs[2], refs[3], refs[4]
        o_ref[...] = _token_shift(acc, mu_ref, carry_ref, pl.program_id(1), tiles_per_seq).astype(o_ref.dtype)
    elif mode == "resid":
        r_ref, o_ref = refs[2], refs[3]
        o_ref[...] = (r_ref[...] + acc).astype(o_ref.dtype)
    elif mode == "ple":
        x_ref, e_ref, o_ref = refs[2], refs[3], refs[4]
        o_ref[...] = (x_ref[...] + jax.nn.sigmoid(acc) * e_ref[...]).astype(o_ref.dtype)
    else:
        raise ValueError(mode)


def matmul(a, b, *, mode="plain", extras=(), norm_gain=None, out_dtype=F32, tm=512, tn=1024, tiles_per_seq=1,
           name="mm"):
    m, k = a.shape
    n = b.shape[1]
    assert m % tm == 0 and n % tn == 0, (m, n, tm, tn)
    in_specs = [pl.BlockSpec((tm, k), lambda j, i: (i, 0)), pl.BlockSpec((k, tn), lambda j, i: (0, j))]
    if norm_gain is not None:
        in_specs.append(pl.BlockSpec((1, k), lambda j, i: (0, 0)))
        extras = (norm_gain.reshape(1, k),) + tuple(extras)
    scratch = []
    if mode == "shift":
        in_specs.append(pl.BlockSpec((1, tn), lambda j, i: (0, j)))
        scratch.append(pltpu.VMEM((SUBLANES, tn), F32))
    elif mode == "resid":
        in_specs.append(pl.BlockSpec((tm, tn), lambda j, i: (i, j)))
    elif mode == "ple":
        in_specs += [pl.BlockSpec((tm, tn), lambda j, i: (i, j)), pl.BlockSpec((tm, tn), lambda j, i: (i, j))]
    return pl.pallas_call(
        functools.partial(_mm_kernel, mode=mode, tiles_per_seq=tiles_per_seq, norm_a=norm_gain is not None),
        grid=(n // tn, m // tm),
        in_specs=in_specs,
        out_specs=pl.BlockSpec((tm, tn), lambda j, i: (i, j)),
        out_shape=jax.ShapeDtypeStruct((m, n), out_dtype),
        scratch_shapes=scratch,
        compiler_params=_params(ndim=2),
        name=name,
    )(a, b, *extras)


def _split_bf16(x):
    hi = x.astype(BF16)
    return hi, (x - hi.astype(F32)).astype(BF16)


def _lora_kernel(x_ref, whi_ref, wlo_ref, b_ref, o_ref, *, tiles_per_half):
    x = x_ref[...]
    rank = whi_ref.shape[0]
    decay_half = pl.program_id(1) < tiles_per_half
    xs = jnp.where(decay_half, jnp.tanh(x[:, :rank]), x[:, rank:])
    x_hi, x_lo = _split_bf16(xs)
    dot = lambda a, b: jnp.dot(a, b, preferred_element_type=F32)
    acc = dot(x_hi, whi_ref[...]) + (dot(x_hi, wlo_ref[...]) + dot(x_lo, whi_ref[...]))
    o_ref[...] = acc + b_ref[...]


def lora_project(proj_s, w_pair, bias, tm=1024, tn=1024):
    m = proj_s.shape[0]
    _, rank, n = w_pair.shape
    tm = min(tm, m)
    assert m % tm == 0 and n % tn == 0 and 2 * rank == LANES
    half = n // tn
    w_hi, w_lo = _split_bf16(w_pair)
    wspec = pl.BlockSpec((None, rank, tn), lambda i, j: (j // half, 0, j % half))
    return pl.pallas_call(
        functools.partial(_lora_kernel, tiles_per_half=half),
        grid=(m // tm, 2 * half),
        in_specs=[pl.BlockSpec((tm, LANES), lambda i, j: (i, 0)), wspec, wspec,
                  pl.BlockSpec((1, tn), lambda i, j: (0, j))],
        out_specs=pl.BlockSpec((tm, tn), lambda i, j: (i, j)),
        out_shape=jax.ShapeDtypeStruct((m, 2 * n), F32),
        compiler_params=_params(ndim=2),
        name="lora",
    )(proj_s, w_hi, w_lo, bias)


SCAN_BATCH = 4
SCAN_HEADS = LANES // SCAN_BATCH
NGRP = LANES // SCAN_HEADS


def _segment_transpose(x):
    seg = lax.broadcasted_iota(jnp.int32, x[0].shape, 1) // SCAN_HEADS
    y = []
    for i in range(NGRP):
        out = None
        for j in range(SCAN_BATCH):
            shift = ((j - i) % NGRP) * SCAN_HEADS
            piece = x[j] if shift == 0 else pltpu.roll(x[j], shift, axis=1)
            out = piece if out is None else jnp.where(seg == j, piece, out)
        y.append(out)
    return y


def _to_lanes(x_ref, dst_ref, tc):
    for g in range(HEAD_A // NGRP):
        y = _segment_transpose([x_ref[b, :, g * LANES:(g + 1) * LANES] for b in range(SCAN_BATCH)])
        for n_lo in range(NGRP):
            n = g * NGRP + n_lo
            dst_ref[n * tc:(n + 1) * tc, :] = y[n_lo]


def _from_lanes_gated(src_ref, z_ref, o_ref, tc):
    for g in range(HEAD_A // NGRP):
        y = _segment_transpose([src_ref[(g * NGRP + n_lo) * tc:(g * NGRP + n_lo + 1) * tc, :] for n_lo in range(NGRP)])
        for b in range(SCAN_BATCH):
            z = z_ref[b, :, g * LANES:(g + 1) * LANES]
            o_ref[b, :, g * LANES:(g + 1) * LANES] = (y[b] * (z * jax.nn.sigmoid(z))).astype(o_ref.dtype)


def _scan_kernel(r_ref, k_ref, v_ref, wp_ref, ap_ref, z_ref, kkw_ref, kaw_ref, rkw_ref, lnw_ref, lnb_ref,
                 o_ref, state_ref, set_a, set_b, ans, os, bon, gend, *, tc):
    s = pl.program_id(1)
    srcs = (r_ref, k_ref, v_ref, wp_ref, ap_ref)

    @pl.when(s == 0)
    def _():
        state_ref[...] = jnp.zeros_like(state_ref)
        for src, dst in zip(srcs, set_a):
            _to_lanes(src, dst, tc)

    @pl.when(s % 2 == 1)
    def _():
        _scan_chunk(set_a, set_b, srcs, z_ref, kkw_ref, kaw_ref, rkw_ref, lnw_ref, lnb_ref, o_ref, state_ref,
                    ans, os, bon, gend, tc)

    @pl.when((s % 2 == 0) & (s > 0))
    def _():
        _scan_chunk(set_b, set_a, srcs, z_ref, kkw_ref, kaw_ref, rkw_ref, lnw_ref, lnb_ref, o_ref, state_ref,
                    ans, os, bon, gend, tc)


def _scan_chunk(cur, nxt, srcs, z_ref, kkw_ref, kaw_ref, rkw_ref, lnw_ref, lnb_ref, o_ref, state_ref,
                ans, os, bon, gend, tc):
    n_ch = HEAD_A
    rs, ks, vs, ws, bvs = cur
    n_groups = HEAD_A // NGRP
    groups_per_step = 2
    steps_per_rowblock = n_groups // groups_per_step
    assert tc * groups_per_step == n_groups * (tc // SUBLANES)

    def rows(n):
        return pl.ds(pl.multiple_of(n * tc, tc), tc)

    def norm_acc(n, acc):
        kkr = ks[rows(n), :] * kkw_ref[n]
        return acc + kkr * kkr

    nsq = lax.fori_loop(0, n_ch, norm_acc, jnp.zeros((tc, LANES), F32), unroll=8)
    inv = 1.0 / jnp.maximum(jnp.sqrt(nsq), 1e-12)

    tri = jnp.where(lax.broadcasted_iota(jnp.int32, (tc, tc), 0) >= lax.broadcasted_iota(jnp.int32, (tc, tc), 1),
                    1.0, 0.0)

    def prep(n, bacc):
        k = ks[rows(n), :]
        a = jax.nn.sigmoid(bvs[rows(n), :])
        kk = k * kkw_ref[n] * inv
        logw = -DECAY_SCALE * jax.nn.sigmoid(ws[rows(n), :])
        cum = jnp.dot(tri, logw, preferred_element_type=F32, precision=lax.Precision.HIGHEST)
        g = jnp.exp(cum)
        ig = jnp.exp(-cum)
        kmod = k * (1.0 + (a - 1.0) * kaw_ref[n])
        r = rs[rows(n), :]
        rs[rows(n), :] = r * g
        ks[rows(n), :] = kmod * ig
        bvs[rows(n), :] = kk * a * ig
        ans[rows(n), :] = -kk * jnp.exp(cum - logw)
        gend[pl.ds(n, 1), :] = g[tc - 1:tc, :]
        return bacc + r * kmod * rkw_ref[n]

    bon[...] = lax.fori_loop(0, n_ch, prep, jnp.zeros((tc, LANES), F32), unroll=16)

    def bcast(ref, row):
        return ref[pl.ds(row, 1), :][None]

    def sa_first(j, acc):
        return acc + state_ref[j] * bcast(ans, j * tc)

    slab = (n_ch // SUBLANES, SUBLANES, LANES)
    sa0 = lax.fori_loop(0, n_ch, sa_first, jnp.zeros(slab, F32), unroll=4)

    def step(t, sa):
        v = vs[pl.ds(t, n_ch, stride=tc), :].reshape(slab)
        t_next = jnp.minimum(t + 1, tc - 1)

        out = jnp.zeros(slab, F32)
        sa_next = jnp.zeros(slab, F32)
        for j in range(n_ch):
            row = j * tc + t
            s_new = state_ref[j] + sa * bcast(bvs, row) + v * bcast(ks, row)
            state_ref[j] = s_new
            out = out + s_new * bcast(rs, row)
            sa_next = sa_next + s_new * bcast(ans, j * tc + t_next)
        o = out.reshape(n_ch, LANES)
        mu = jnp.mean(o, axis=0, keepdims=True)
        d = o - mu
        var = jnp.mean(d * d, axis=0, keepdims=True)
        on = d * lax.rsqrt(var + GN_EPS) * lnw_ref[...] + lnb_ref[...]
        bonus = bon[pl.ds(t, 1), :] * v.reshape(n_ch, LANES)
        os[pl.ds(t, n_ch, stride=tc), :] = on + bonus

        row0 = pl.multiple_of((t // steps_per_rowblock) * SUBLANES, SUBLANES)
        for src, dst in zip(srcs, nxt):
            for gi in range(groups_per_step):
                g = (t % steps_per_rowblock) * groups_per_step + gi
                lane0 = pl.multiple_of(g * LANES, LANES)
                y = _segment_transpose([src[b, pl.ds(row0, SUBLANES), pl.ds(lane0, LANES)] for b in range(SCAN_BATCH)])
                for n_lo in range(NGRP):
                    dst[pl.ds(pl.multiple_of((g * NGRP + n_lo) * tc + row0, SUBLANES), SUBLANES), :] = y[n_lo]
        return sa_next

    lax.fori_loop(0, tc, step, sa0)

    def rescale(j, c):
        state_ref[j] = state_ref[j] * bcast(gend, j)
        return c

    lax.fori_loop(0, n_ch, rescale, 0, unroll=8)
    _from_lanes_gated(os, z_ref, o_ref, tc)


def rwkv_scan(proj_a, wa_pre, kkw, kaw, rkw, lnw, lnb, batch, seq, tc=32):
    width = HEAD_A * SCAN_HEADS
    n_chunks = seq // tc
    seq_in = lambda col: pl.BlockSpec((SCAN_BATCH, tc, width), lambda g, s: (g, jnp.minimum(s, n_chunks - 1), col))
    seq_out = lambda col: pl.BlockSpec((SCAN_BATCH, tc, width), lambda g, s: (g, jnp.maximum(s - 1, 0), col))
    par3 = pl.BlockSpec((HEAD_A, 1, LANES), lambda g, s: (0, 0, 0))
    par2 = pl.BlockSpec((HEAD_A, LANES), lambda g, s: (0, 0))
    buf = pltpu.VMEM((HEAD_A * tc, LANES), F32)
    return pl.pallas_call(
        functools.partial(_scan_kernel, tc=tc),
        grid=(batch // SCAN_BATCH, n_chunks + 1),
        in_specs=[seq_in(0), seq_in(1), seq_in(2), seq_in(0), seq_in(1), seq_out(3), par3, par3, par3, par2, par2],
        out_specs=seq_out(0),
        out_shape=jax.ShapeDtypeStruct((batch, seq, width), BF16),
        scratch_shapes=[pltpu.VMEM((HEAD_A, HEAD_A // SUBLANES, SUBLANES, LANES), F32), [buf] * 5, [buf] * 5,
                        buf, buf, pltpu.VMEM((tc, LANES), F32), pltpu.VMEM((HEAD_A, LANES), F32)],
        compiler_params=_params(ndim=2),
        name="rwkv_scan",
    )(proj_a, proj_a, proj_a, wa_pre, wa_pre, proj_a, kkw, kaw, rkw, lnw, lnb)


def _compress_kernel(x_ref, pe_ref, w1_ref, w2_ref, kg_ref, o_ref):
    which = pl.program_id(0)
    half = L_CMP // 2
    nblk = x_ref.shape[0] // CMP_STRIDE
    h1 = jnp.zeros((nblk, w1_ref.shape[-1]), F32)
    h2 = jnp.zeros((nblk, w1_ref.shape[-1]), F32)
    for l in range(half):
        x = x_ref[pl.ds(l, nblk, stride=CMP_STRIDE), :]
        a1 = (x + pe_ref[l:l + 1, :]).astype(BF16)
        a2 = (x + pe_ref[half + l:half + l + 1, :]).astype(BF16)
        h1 = h1 + jnp.dot(a1, w1_ref[l], preferred_element_type=F32)
        h2 = h2 + jnp.dot(a2, w1_ref[half + l], preferred_element_type=F32)
    hid = h1 + pltpu.roll(h2, nblk - 1, axis=0)
    hid = jax.nn.gelu(hid)
    out = jnp.dot(hid.astype(BF16), w2_ref[...], preferred_element_type=F32)
    ms = jnp.mean(out * out, axis=-1, keepdims=True)
    normed = out * lax.rsqrt(ms + NORM_EPS) * kg_ref[...]
    out = jnp.where(which == 0, normed, out)
    row = lax.broadcasted_iota(jnp.int32, out.shape, 0)
    o_ref[...] = jnp.where(row < nblk - 1, out, 0.0)


def compress(proj_b, kc_block, vc_block, pe, w1, w2, kgain, batch, seq):
    nblk = seq // CMP_STRIDE
    hidden = w1.shape[-1]

    def xmap(w, b, g):
        return (b, kc_block + w * (vc_block - kc_block) + g)

    return pl.pallas_call(
        _compress_kernel,
        grid=(2, batch, KV_GROUPS),
        in_specs=[
            pl.BlockSpec((seq, HEAD_B), xmap),
            pl.BlockSpec((None, L_CMP, HEAD_B), lambda w, b, g: (w, 0, 0)),
            pl.BlockSpec((None, L_CMP, HEAD_B, hidden), lambda w, b, g: (w, 0, 0, 0)),
            pl.BlockSpec((None, hidden, HEAD_B), lambda w, b, g: (w, 0, 0)),
            pl.BlockSpec((1, HEAD_B), lambda w, b, g: (0, 0)),
        ],
        out_specs=pl.BlockSpec((None, None, None, nblk, HEAD_B), lambda w, b, g: (w, b, g, 0, 0)),
        out_shape=jax.ShapeDtypeStruct((2, batch, KV_GROUPS, nblk, HEAD_B), F32),
        compiler_params=_params(ndim=3),
        name="nsa_compress",
    )(proj_b, pe, w1, w2, kgain)


def _dot_nt(a, b):
    return lax.dot_general(a, b, (((1,), (1,)), ((), ())), preferred_element_type=F32)


LOG2E = 1.4426950408889634
VAUG = 2 * HEAD_B


def _attn_kernel(q_ref, ks_ref, vs_ref, kw_ref, vw_ref, kc_ref, vc_ref, gt_ref, z_ref, qg_ref, kg_ref,
                 ovt_ref, ex_ref, wb_ref, o_ref, ksn_ref, vs1_ref, kwn_ref, vw1_ref, q_scr, m_scr, acc_scr,
                 oc_scr, ow_scr, *, seq):
    qt = pl.program_id(2)
    ncmp = kc_ref.shape[0]
    nsel = seq // L_SEL

    @pl.when(qt == 0)
    def _():
        ones_col = jnp.ones((seq, HEAD_B), BF16)
        ksn_ref[...] = _rms(ks_ref[...], kg_ref[1:2, :]).astype(BF16)
        vs1_ref[:, 0:HEAD_B] = vs_ref[...].astype(BF16)
        vs1_ref[:, HEAD_B:VAUG] = ones_col
        kwn_ref[0:WINDOW, :] = jnp.zeros((WINDOW, HEAD_B), BF16)
        vw1_ref[0:WINDOW, :] = jnp.zeros((WINDOW, VAUG), BF16)
        kwn_ref[WINDOW:WINDOW + seq, :] = _rms(kw_ref[...], kg_ref[2:3, :]).astype(BF16)
        vw1_ref[WINDOW:WINDOW + seq, 0:HEAD_B] = vw_ref[...].astype(BF16)
        vw1_ref[WINDOW:WINDOW + seq, HEAD_B:VAUG] = ones_col

    t0 = pl.multiple_of(qt * TQ, TQ)
    for r in range(REP):
        q_scr[r] = (_rms(q_ref[:, r * HEAD_B:(r + 1) * HEAD_B], qg_ref[...]) * (ATTN_SCALE * LOG2E)).astype(BF16)

    def t_of(shape):
        return t0 + lax.broadcasted_iota(jnp.int32, shape, 0)

    n_idx = lax.broadcasted_iota(jnp.int32, (TQ, ncmp), 1)
    bias_c = jnp.where(n_idx * CMP_STRIDE + (L_CMP - 1) <= t_of((TQ, ncmp)), 0.0, NEG_INF)
    bias_c = jnp.where(n_idx < ncmp - 1, bias_c, NEG_INF)
    row_ok = jnp.where(t_of((TQ, 1)) >= L_CMP - 1, 1.0, 0.0)
    q_all = q_scr[...].reshape(REP * TQ, HEAD_B)
    head = lambda x, r: x[r * TQ:(r + 1) * TQ]
    s_all = _dot_nt(q_all, kc_ref[...].astype(BF16))
    prs = []
    for r in range(REP):
        s = head(s_all, r) + bias_c
        p = jnp.exp2(s - jnp.max(s, axis=-1, keepdims=True))
        prs.append(p * (row_ok / jnp.sum(p, axis=-1, keepdims=True)))
    psum = prs[0] + prs[1] + prs[2] + prs[3]
    oc_scr[...] = jnp.dot(jnp.concatenate(prs, axis=0).astype(BF16), vc_ref[...].astype(BF16),
                          preferred_element_type=F32).reshape(REP, TQ, HEAD_B)
    imp_t = lax.dot_general(ovt_ref[...], psum, (((1,), (1,)), ((), ())), preferred_element_type=F32,
                            precision=lax.Precision.HIGHEST)

    j_idx = lax.broadcasted_iota(jnp.int32, (nsel, TQ), 0)
    t_sel = t0 + lax.broadcasted_iota(jnp.int32, (nsel, TQ), 1)
    cur = t_sel // L_SEL
    forced = jnp.where((j_idx == 0) | (j_idx == cur) | (j_idx == cur - 1), FORCE_BONUS, 0.0)
    score = jnp.where(j_idx * L_SEL <= t_sel, imp_t + forced, NEG_INF)
    rank = jnp.zeros((nsel, TQ), F32)
    for i in range(nsel):
        si = score[i:i + 1, :]
        ahead = (si > score) | ((si == score) & (j_idx > i))
        rank = rank + jnp.where(ahead, 1.0, 0.0)
    sel_t = jnp.where((rank < min(N_SEL, nsel)) & (score > 0.5 * NEG_INF), 1.0, 0.0)
    sel = jnp.concatenate([sel_t, jnp.zeros((LANES - nsel, TQ), F32)], axis=0).T
    not_sel = (1.0 - sel).astype(BF16)

    span = WINDOW + TQ
    kwin = kwn_ref[pl.ds(t0, span), :]
    vwin = vw1_ref[pl.ds(t0, span), :]
    lane_w = lax.broadcasted_iota(jnp.int32, (TQ, span), 1)
    bias_w = jnp.where(lane_w >= WINDOW - t0, wb_ref[...], NEG_INF)
    s_all = _dot_nt(q_all, kwin)
    ps = []
    for r in range(REP):
        s = head(s_all, r) + bias_w
        ps.append(jnp.exp2(s - jnp.max(s, axis=-1, keepdims=True)).astype(BF16))
    ow_scr[...] = jnp.dot(jnp.concatenate(ps, axis=0), vwin, preferred_element_type=F32)

    m_scr[...] = jnp.full(m_scr.shape, NEG_INF, F32)
    acc_scr[...] = jnp.zeros(acc_scr.shape, F32)

    def scores(c):
        k0 = pl.multiple_of(c * KC, KC)
        return _dot_nt(q_scr[...].reshape(REP * TQ, HEAD_B), ksn_ref[pl.ds(k0, KC), :])

    def sel_bias(c):
        return jnp.dot(not_sel, ex_ref[c], preferred_element_type=F32)

    def sel_chunk(c, s_all, bias, diagonal):
        k0 = pl.multiple_of(c * KC, KC)
        v1 = vs1_ref[pl.ds(k0, KC), :]
        if diagonal:
            lane = lax.broadcasted_iota(jnp.int32, (TQ, KC), 1)
            bias = jnp.where(k0 + lane <= t_of((TQ, KC)), bias, NEG_INF)
        ps, alphas = [], []
        for r in range(REP):
            s = head(s_all, r) + bias
            m_old = m_scr[r]
            m_new = jnp.maximum(m_old, jnp.max(s, axis=-1, keepdims=True))
            m_scr[r] = m_new
            ps.append(jnp.exp2(s - jnp.tile(m_new, (1, KC // LANES))).astype(BF16))
            alphas.append(jnp.exp2(m_old - m_new))
        pv = jnp.dot(jnp.concatenate(ps, axis=0), v1, preferred_element_type=F32)
        for r in range(REP):
            acc_scr[r] = jnp.tile(alphas[r], (1, VAUG // LANES)) * acc_scr[r] + head(pv, r)

    n_full = t0 // KC

    def full_chunk(c, carry):
        s_cur, bias_cur = carry
        nxt = (scores(c + 1), sel_bias(c + 1))
        sel_chunk(c, s_cur, bias_cur, False)
        return nxt

    s_last, bias_last = lax.fori_loop(0, n_full, full_chunk, (scores(0), sel_bias(0)))
    sel_chunk(n_full, s_last, bias_last, True)

    gts = jax.nn.sigmoid(gt_ref[...])
    for r in range(REP):
        ow = ow_scr[r * TQ:(r + 1) * TQ, :]
        o_w = ow[:, 0:HEAD_B] / ow[:, HEAD_B:VAUG]
        acc = acc_scr[r]
        o_s = acc[:, 0:HEAD_B] / acc[:, HEAD_B:VAUG]
        g0 = N_BRANCH * r
        o = gts[:, g0:g0 + 1] * oc_scr[r] + gts[:, g0 + 1:g0 + 2] * o_s + gts[:, g0 + 2:g0 + 3] * o_w
        z = z_ref[:, r * HEAD_B:(r + 1) * HEAD_B]
        o_ref[:, r * HEAD_B:(r + 1) * HEAD_B] = (o * (z * jax.nn.sigmoid(z))).astype(o_ref.dtype)


def nsa_attention(proj_b, proj_z, cmp_kv, gates, q_gain, k_gain, overlap_t, expand, win_bias, blocks, batch, seq):
    nq = seq // TQ
    ncmp = seq // CMP_STRIDE
    gw = REP * HEAD_B // LANES
    qspec = lambda off: pl.BlockSpec((TQ, REP * HEAD_B), lambda b, g, t: (b * nq + t, off // gw + g))
    kvspec = lambda off: pl.BlockSpec((seq, HEAD_B), lambda b, g, t: (b, off + g))
    cspec = lambda w: pl.BlockSpec((None, None, None, ncmp, HEAD_B), lambda b, g, t: (w, b, g, 0, 0))
    full = lambda a: pl.BlockSpec(a.shape, lambda b, g, t: (0,) * a.ndim)
    return pl.pallas_call(
        functools.partial(_attn_kernel, seq=seq),
        grid=(batch, KV_GROUPS, nq),
        in_specs=[
            qspec(blocks["q"]), kvspec(blocks["ks"]), kvspec(blocks["vs"]), kvspec(blocks["kw"]), kvspec(blocks["vw"]),
            cspec(0), cspec(1),
            pl.BlockSpec((None, None, TQ, GATE_PAD), lambda b, g, t: (b, g, t, 0)),
            qspec(blocks["z"]),
            full(q_gain), full(k_gain), full(overlap_t), full(expand), full(win_bias),
        ],
        out_specs=pl.BlockSpec((TQ, REP * HEAD_B), lambda b, g, t: (b * nq + t, g)),
        out_shape=jax.ShapeDtypeStruct((batch * seq, KV_GROUPS * REP * HEAD_B), BF16),
        scratch_shapes=[
            pltpu.VMEM((seq, HEAD_B), BF16), pltpu.VMEM((seq, VAUG), BF16),
            pltpu.VMEM((seq + WINDOW, HEAD_B), BF16), pltpu.VMEM((seq + WINDOW, VAUG), BF16),
            pltpu.VMEM((REP, TQ, HEAD_B), BF16), pltpu.VMEM((REP, TQ, LANES), F32),
            pltpu.VMEM((REP, TQ, VAUG), F32), pltpu.VMEM((REP, TQ, HEAD_B), F32),
            pltpu.VMEM((REP * TQ, VAUG), F32),
        ],
        compiler_params=_params(ndim=3),
        name="nsa_attention",
    )(proj_b, proj_b, proj_b, proj_b, proj_b, cmp_kv, cmp_kv, gates, proj_z, q_gain, k_gain, overlap_t, expand,
      win_bias)


def _merge_kernel(ya_ref, yb_ref, wa_ref, wb_ref, ga_ref, gb_ref, out_ref):
    ua = jnp.dot(ya_ref[...], wa_ref[...], preferred_element_type=F32)
    ub = jnp.dot(yb_ref[...], wb_ref[...], preferred_element_type=F32)
    out_ref[...] = (jax.nn.sigmoid(ga_ref[...]) * ua + jax.nn.sigmoid(gb_ref[...]) * ub).astype(out_ref.dtype)


def merge(ya, yb, w_a, w_b, proj_b, gate_col, tm=1024, tn=512):
    m, ka = ya.shape
    n = w_a.shape[1]
    tm = min(tm, m)
    assert gate_col % tn == 0 and n % tn == 0
    ga_block, gb_block = gate_col // tn, (gate_col + n) // tn
    return pl.pallas_call(
        _merge_kernel,
        grid=(m // tm, n // tn),
        in_specs=[
            pl.BlockSpec((tm, ka), lambda i, j: (i, 0)),
            pl.BlockSpec((tm, ka), lambda i, j: (i, 0)),
            pl.BlockSpec((ka, tn), lambda i, j: (0, j)),
            pl.BlockSpec((ka, tn), lambda i, j: (0, j)),
            pl.BlockSpec((tm, tn), lambda i, j: (i, ga_block + j)),
            pl.BlockSpec((tm, tn), lambda i, j: (i, gb_block + j)),
        ],
        out_specs=pl.BlockSpec((tm, tn), lambda i, j: (i, j)),
        out_shape=jax.ShapeDtypeStruct((m, n), BF16),
        compiler_params=_params(ndim=2),
        name="merge",
    )(ya, yb, w_a, w_b, proj_b, proj_b)


def _ple_embed_kernel(p_ref, w_ref, g_ref, o_ref):
    acc = jnp.dot(p_ref[...].astype(BF16), w_ref[...], preferred_element_type=F32)
    o_ref[...] = _rms(acc, g_ref[...]).astype(o_ref.dtype)


def ple_embed(p, w, g, tm=256):
    m, k = p.shape
    n = w.shape[1]
    return pl.pallas_call(
        _ple_embed_kernel,
        grid=(m // tm,),
        in_specs=[pl.BlockSpec((tm, k), lambda i: (i, 0)), pl.BlockSpec((k, n), lambda i: (0, 0)),
                  pl.BlockSpec((1, n), lambda i: (0, 0))],
        out_specs=pl.BlockSpec((tm, n), lambda i: (i, 0)),
        out_shape=jax.ShapeDtypeStruct((m, n), F32),
        compiler_params=_params(),
        name="ple_embed",
    )(p, w, g.reshape(1, n))


def head_minor(w, heads):
    lead = w.shape[:-1]
    return w.reshape(*lead, heads, HEAD_A).swapaxes(-1, -2).reshape(*lead, heads * HEAD_A)


def rwkv_branch(proj_a, proj_s, batch, seq, a_width, w_lora_up, w0, a_lora_up, a0, k_k, k_a, r_k, lnx_w, lnx_b):
    tokens = batch * seq
    heads_a = a_width // HEAD_A
    assert heads_a == SCAN_HEADS and batch % SCAN_BATCH == 0
    lora = w_lora_up.shape[0]
    hm = lambda w: head_minor(w, heads_a)
    w_lora = jnp.stack([hm(w_lora_up), hm(a_lora_up)])
    b_lora = jnp.concatenate([hm(w0), hm(a0)]).reshape(1, -1)
    wa_pre = lora_project(proj_s, w_lora, b_lora)

    def par_scan(t):
        return jnp.tile(t.reshape(heads_a, HEAD_A).T, (1, SCAN_BATCH))

    par3 = lambda t: par_scan(t).reshape(HEAD_A, 1, LANES)
    o = rwkv_scan(proj_a.reshape(batch, seq, -1), wa_pre.reshape(batch, seq, -1),
                  par3(k_k), par3(k_a), par3(r_k), par_scan(lnx_w), par_scan(lnx_b), batch, seq)
    return o.reshape(tokens, a_width)


def nsa_branch(proj_b, proj_z, graw, batch, seq, b_width, q_norm_g, k_norm_g, pe_cmp_k, pe_cmp_v,
               cmp_k_w1, cmp_k_w2, cmp_v_w1, cmp_v_w2):
    kv_width = KV_GROUPS * HEAD_B
    blk = lambda cols: cols // LANES
    blocks = {"q": 0, "kc": blk(b_width), "vc": blk(b_width + kv_width), "ks": blk(b_width + 2 * kv_width),
              "vs": blk(b_width + 3 * kv_width), "kw": blk(b_width + 4 * kv_width), "vw": blk(b_width + 5 * kv_width),
              "z": 0}
    pe = jnp.stack([pe_cmp_k, pe_cmp_v])
    hidden = cmp_k_w1.shape[1]
    w1 = jnp.stack([cmp_k_w1, cmp_v_w1]).reshape(2, L_CMP, HEAD_B, hidden).astype(BF16)
    w2 = jnp.stack([cmp_k_w2, cmp_v_w2]).astype(BF16)
    cmp_kv = compress(proj_b, blocks["kc"], blocks["vc"], pe, w1, w2, k_norm_g[0:1], batch, seq)

    ncmp = seq // CMP_STRIDE
    nsel = seq // L_SEL
    c_start = jnp.arange(ncmp) * CMP_STRIDE
    s_start = jnp.arange(nsel) * L_SEL
    overlap_t = ((c_start[None, :] < (s_start + L_SEL)[:, None]) & (s_start[:, None] < (c_start + L_CMP)[None, :])
                 & (jnp.arange(ncmp)[None, :] < ncmp - 1)).astype(F32)
    key_blk = jnp.arange(seq) // L_SEL
    expand = jnp.where(jnp.arange(LANES)[:, None] == key_blk[None, :], NEG_INF, 0.0).astype(BF16)
    expand = expand.reshape(LANES, seq // KC, KC).transpose(1, 0, 2)
    tl = jnp.arange(TQ)[:, None]
    u = jnp.arange(WINDOW + TQ)[None, :]
    win_bias = jnp.where((u > tl) & (u <= WINDOW + tl), 0.0, NEG_INF).astype(F32)
    gates = graw.reshape(batch, seq, KV_GROUPS, REP * N_BRANCH).transpose(0, 2, 1, 3)
    gates = jnp.pad(gates, ((0, 0), (0, 0), (0, 0), (0, GATE_PAD - REP * N_BRANCH)))
    return nsa_attention(proj_b, proj_z, cmp_kv, gates, q_norm_g.reshape(1, HEAD_B), k_norm_g, overlap_t, expand,
                         win_bias, blocks, batch, seq)


def _layer(x, p, norm_g, w_in, shift_mu, w_lora_up, w0, a_lora_up, a0, k_k, k_a, r_k, lnx_w, lnx_b,
           q_norm_g, k_norm_g, pe_cmp_k, pe_cmp_v, cmp_k_w1, cmp_k_w2, cmp_v_w1, cmp_v_w2,
           w_up_a, w_up_b, w_out, ple_pre_g, w_ple_gate, w_ple, ple_post_g):
    batch, seq, d = x.shape
    tokens = batch * seq
    a_width = w_up_a.shape[0]
    b_width = w_up_b.shape[0]
    heads_a = a_width // HEAD_A
    kv_width = KV_GROUPS * HEAD_B
    n_gate = KV_GROUPS * REP * N_BRANCH
    lora = w_lora_up.shape[0]
    assert lora == HEAD_A and a_lora_up.shape[0] == HEAD_A and 2 * lora == LANES
    a_cols = 4 * a_width + 2 * lora
    g_off = a_cols + b_width + 6 * kv_width
    zb_off = g_off + n_gate
    assert w_in.shape[1] == zb_off + b_width + 2 * d

    hm4 = lambda w: head_minor(w.reshape(*w.shape[:-1], 4, a_width), heads_a).reshape(*w.shape[:-1], 4 * a_width)
    w_a = hm4(w_in[:, :4 * a_width]).astype(BF16)
    mu_a = hm4(shift_mu[:4 * a_width]).reshape(1, -1)
    pad_s = LANES - n_gate
    w_s = jnp.concatenate([w_in[:, 4 * a_width:a_cols], w_in[:, g_off:zb_off], jnp.zeros((d, pad_s), F32)],
                          axis=1).astype(BF16)
    mu_s = jnp.concatenate([shift_mu[4 * a_width:], jnp.zeros((LANES,), F32)]).reshape(1, -1)
    w_b = w_in[:, a_cols:g_off].astype(BF16)
    w_z = w_in[:, zb_off:].astype(BF16)

    x2 = x.reshape(tokens, d)
    tm = 512
    tm_wide = 2 * tm
    h, proj_s = norm_and_skinny_proj(x2, norm_g, w_s, mu_s, seq // tm, tm=tm)
    proj_a = matmul(h, w_a, mode="shift", extras=(mu_a,), tm=tm_wide, tn=1024, tiles_per_seq=seq // tm_wide,
                    name="proj_a")
    proj_b = matmul(h, w_b, tm=tm_wide, tn=1024, name="proj_b")
    proj_z = matmul(h, w_z, tm=tm_wide, tn=1024, name="proj_z")

    ya = rwkv_branch(proj_a, proj_s, batch, seq, a_width, w_lora_up, w0, a_lora_up, a0, k_k, k_a, r_k, lnx_w, lnx_b)
    graw = proj_s[:, 2 * lora:2 * lora + n_gate]
    yb = nsa_branch(proj_b, proj_z, graw, batch, seq, b_width, q_norm_g, k_norm_g, pe_cmp_k, pe_cmp_v,
                    cmp_k_w1, cmp_k_w2, cmp_v_w1, cmp_v_w2)

    w_up_a_nm = head_minor(w_up_a.T, heads_a).T
    merged = merge(ya, yb, w_up_a_nm.astype(BF16), w_up_b.astype(BF16), proj_z, b_width)
    x1 = matmul(merged, w_out.astype(BF16), mode="resid", extras=(x2,), tm=tm_wide, tn=1024, name="out_proj")
    e = ple_embed(p.reshape(tokens, -1), w_ple.astype(BF16), ple_post_g)
    out = matmul(x1, w_ple_gate.astype(BF16), mode="ple", extras=(x1, e), norm_gain=ple_pre_g, tm=tm, tn=1024,
                 name="ple_gate")
    return out.reshape(batch, seq, d)


def kernel(x, p, norm_g, w_in, shift_mu, w_lora_up, w0, a_lora_up, a0, k_k, k_a, r_k, lnx_w, lnx_b, q_norm_g, k_norm_g, pe_cmp_k, pe_cmp_v, cmp_k_w1, cmp_k_w2, cmp_v_w1, cmp_v_w2, w_up_a, w_up_b, w_out, ple_pre_g, w_ple_gate, w_ple, ple_post_g):
    depth = w_in.shape[0]
    for i in range(depth):
        x = _layer(x, p[i], norm_g[i], w_in[i], shift_mu[i], w_lora_up[i], w0[i], a_lora_up[i], a0[i], k_k[i],
                   k_a[i], r_k[i], lnx_w[i], lnx_b[i], q_norm_g[i], k_norm_g[i], pe_cmp_k[i], pe_cmp_v[i],
                   cmp_k_w1[i], cmp_k_w2[i], cmp_v_w1[i], cmp_v_w2[i], w_up_a[i], w_up_b[i], w_out[i],
                   ple_pre_g[i], w_ple_gate[i], w_ple[i], ple_post_g[i])
    return x
```

```python
import functools

import jax
import jax.numpy as jnp
from jax import lax
from jax.experimental import pallas as pl
from jax.experimental.pallas import tpu as pltpu

F32 = jnp.float32
BF16 = jnp.bfloat16

LANES = 128
SUBLANES = 8
VMEM_LIMIT = 56 * 1024 * 1024

NORM_EPS = 1e-6
NEG_INF = -1e30
HEAD_A = 64
GN_EPS = 64e-5
HEAD_B = 128
KV_GROUPS = 4
REP = 4
N_BRANCH = 3
GATE_PAD = 16
L_CMP = 32
CMP_STRIDE = 16
L_SEL = 64
N_SEL = 16
WINDOW = 512
TQ = 256
KC = 256
FORCE_BONUS = 1e3
ATTN_SCALE = HEAD_B ** -0.5
DECAY_SCALE = 0.6065306597126334


def _params(vmem=VMEM_LIMIT, ndim=1):
    return pltpu.CompilerParams(dimension_semantics=("arbitrary",) * ndim, vmem_limit_bytes=vmem)


def _rms(x, gain):
    ms = jnp.mean(x * x, axis=-1, keepdims=True)
    return x * lax.rsqrt(ms + NORM_EPS) * gain


def _token_shift(acc, mu_ref, carry_ref, i, tiles_per_seq):
    tm = acc.shape[0]
    first = (i % tiles_per_seq) == 0
    last_prev = jnp.where(first, 0.0, carry_ref[SUBLANES - 1:SUBLANES, :])
    rolled = pltpu.roll(acc, 1, axis=0)
    row = lax.broadcasted_iota(jnp.int32, acc.shape, 0)
    prev = jnp.where(row == 0, last_prev, rolled)
    carry_ref[...] = acc[tm - SUBLANES:tm, :]
    return acc + mu_ref[...] * (prev - acc)


def _norm_proj_kernel(x_ref, g_ref, w_ref, mu_ref, h_ref, o_ref, carry_ref, *, tiles_per_seq):
    h = _rms(x_ref[...], g_ref[...]).astype(h_ref.dtype)
    h_ref[...] = h
    acc = jnp.dot(h, w_ref[...], preferred_element_type=F32)
    o_ref[...] = _token_shift(acc, mu_ref, carry_ref, pl.program_id(0), tiles_per_seq)


def norm_and_skinny_proj(x, g, w, mu, tiles_per_seq, tm=512):
    m, k = x.shape
    n = w.shape[1]
    return pl.pallas_call(
        functools.partial(_norm_proj_kernel, tiles_per_seq=tiles_per_seq),
        grid=(m // tm,),
        in_specs=[pl.BlockSpec((tm, k), lambda i: (i, 0)), pl.BlockSpec((1, k), lambda i: (0, 0)),
                  pl.BlockSpec((k, n), lambda i: (0, 0)), pl.BlockSpec((1, n), lambda i: (0, 0))],
        out_specs=[pl.BlockSpec((tm, k), lambda i: (i, 0)), pl.BlockSpec((tm, n), lambda i: (i, 0))],
        out_shape=[jax.ShapeDtypeStruct((m, k), BF16), jax.ShapeDtypeStruct((m, n), F32)],
        scratch_shapes=[pltpu.VMEM((SUBLANES, n), F32)],
        compiler_params=_params(),
        name="norm_proj_s",
    )(x, g.reshape(1, k), w, mu)


def _mm_kernel(*refs, mode, tiles_per_seq, norm_a):
    a_ref, b_ref = refs[0], refs[1]
    if norm_a:
        x = a_ref[...]
        acc = jnp.dot((x * refs[2][...]).astype(BF16), b_ref[...], preferred_element_type=F32)
        acc = acc * lax.rsqrt(jnp.mean(x * x, axis=-1, keepdims=True) + NORM_EPS)
        refs = refs[:2] + refs[3:]
    else:
        acc = jnp.dot(a_ref[...], b_ref[...], preferred_element_type=F32)
    if mode == "plain":
        o_ref = refs[2]
        o_ref[...] = acc.astype(o_ref.dtype)
    elif mode == "shift":
        mu_ref, o_ref, carry_ref = refs[2], refs[3], refs[4]
        o_ref[...] = _token_shift(acc, mu_ref, carry_ref, pl.program_id(1), tiles_per_seq).astype(o_ref.dtype)
    elif mode == "resid":
        r_ref, o_ref = refs[2], refs[3]
        o_ref[...] = (r_ref[...] + acc).astype(o_ref.dtype)
    elif mode == "ple":
        x_ref, e_ref, o_ref = refs[2], refs[3], refs[4]
        o_ref[...] = (x_ref[...] + jax.nn.sigmoid(acc) * e_ref[...]).astype(o_ref.dtype)
    else:
        raise ValueError(mode)


def matmul(a, b, *, mode="plain", extras=(), norm_gain=None, out_dtype=F32, tm=512, tn=1024, tiles_per_seq=1,
           name="mm"):
    m, k = a.shape
    n = b.shape[1]
    assert m % tm == 0 and n % tn == 0, (m, n, tm, tn)
    in_specs = [pl.BlockSpec((tm, k), lambda j, i: (i, 0)), pl.BlockSpec((k, tn), lambda j, i: (0, j))]
    if norm_gain is not None:
        in_specs.append(pl.BlockSpec((1, k), lambda j, i: (0, 0)))
        extras = (norm_gain.reshape(1, k),) + tuple(extras)
    scratch = []
    if mode == "shift":
        in_specs.append(pl.BlockSpec((1, tn), lambda j, i: (0, j)))
        scratch.append(pltpu.VMEM((SUBLANES, tn), F32))
    elif mode == "resid":
        in_specs.append(pl.BlockSpec((tm, tn), lambda j, i: (i, j)))
    elif mode == "ple":
        in_specs += [pl.BlockSpec((tm, tn), lambda j, i: (i, j)), pl.BlockSpec((tm, tn), lambda j, i: (i, j))]
    return pl.pallas_call(
        functools.partial(_mm_kernel, mode=mode, tiles_per_seq=tiles_per_seq, norm_a=norm_gain is not None),
        grid=(n // tn, m // tm),
        in_specs=in_specs,
        out_specs=pl.BlockSpec((tm, tn), lambda j, i: (i, j)),
        out_shape=jax.ShapeDtypeStruct((m, n), out_dtype),
        scratch_shapes=scratch,
        compiler_params=_params(ndim=2),
        name=name,
    )(a, b, *extras)


def _split_bf16(x):
    hi = x.astype(BF16)
    return hi, (x - hi.astype(F32)).astype(BF16)


def _lora_kernel(x_ref, whi_ref, wlo_ref, b_ref, o_ref, *, tiles_per_half):
    x = x_ref[...]
    rank = whi_ref.shape[0]
    decay_half = pl.program_id(1) < tiles_per_half
    xs = jnp.where(decay_half, jnp.tanh(x[:, :rank]), x[:, rank:])
    x_hi, x_lo = _split_bf16(xs)
    dot = lambda a, b: jnp.dot(a, b, preferred_element_type=F32)
    acc = dot(x_hi, whi_ref[...]) + (dot(x_hi, wlo_ref[...]) + dot(x_lo, whi_ref[...]))
    o_ref[...] = acc + b_ref[...]


def lora_project(proj_s, w_pair, bias, tm=1024, tn=1024):
    m = proj_s.shape[0]
    _, rank, n = w_pair.shape
    tm = min(tm, m)
    assert m % tm == 0 and n % tn == 0 and 2 * rank == LANES
    half = n // tn
    w_hi, w_lo = _split_bf16(w_pair)
    wspec = pl.BlockSpec((None, rank, tn), lambda i, j: (j // half, 0, j % half))
    return pl.pallas_call(
        functools.partial(_lora_kernel, tiles_per_half=half),
        grid=(m // tm, 2 * half),
        in_specs=[pl.BlockSpec((tm, LANES), lambda i, j: (i, 0)), wspec, wspec,
                  pl.BlockSpec((1, tn), lambda i, j: (0, j))],
        out_specs=pl.BlockSpec((tm, tn), lambda i, j: (i, j)),
        out_shape=jax.ShapeDtypeStruct((m, 2 * n), F32),
        compiler_params=_params(ndim=2),
        name="lora",
    )(proj_s, w_hi, w_lo, bias)


SCAN_BATCH = 4
SCAN_HEADS = LANES // SCAN_BATCH
NGRP = LANES // SCAN_HEADS


def _segment_transpose(x):
    seg = lax.broadcasted_iota(jnp.int32, x[0].shape, 1) // SCAN_HEADS
    y = []
    for i in range(NGRP):
        out = None
        for j in range(SCAN_BATCH):
            shift = ((j - i) % NGRP) * SCAN_HEADS
            piece = x[j] if shift == 0 else pltpu.roll(x[j], shift, axis=1)
            out = piece if out is None else jnp.where(seg == j, piece, out)
        y.append(out)
    return y


def _to_lanes(x_ref, dst_ref, tc):
    for g in range(HEAD_A // NGRP):
        y = _segment_transpose([x_ref[b, :, g * LANES:(g + 1) * LANES] for b in range(SCAN_BATCH)])
        for n_lo in range(NGRP):
            n = g * NGRP + n_lo
            dst_ref[n * tc:(n + 1) * tc, :] = y[n_lo]


def _from_lanes_gated(src_ref, z_ref, o_ref, tc):
    for g in range(HEAD_A // NGRP):
        y = _segment_transpose([src_ref[(g * NGRP + n_lo) * tc:(g * NGRP + n_lo + 1) * tc, :] for n_lo in range(NGRP)])
        for b in range(SCAN_BATCH):
            z = z_ref[b, :, g * LANES:(g + 1) * LANES]
            o_ref[b, :, g * LANES:(g + 1) * LANES] = (y[b] * (z * jax.nn.sigmoid(z))).astype(o_ref.dtype)


def _scan_kernel(r_ref, k_ref, v_ref, wp_ref, ap_ref, z_ref, kkw_ref, kaw_ref, rkw_ref, lnw_ref, lnb_ref,
                 o_ref, state_ref, set_a, set_b, ans, os, bon, gend, *, tc):
    s = pl.program_id(1)
    srcs = (r_ref, k_ref, v_ref, wp_ref, ap_ref)

    @pl.when(s == 0)
    def _():
        state_ref[...] = jnp.zeros_like(state_ref)
        for src, dst in zip(srcs, set_a):
            _to_lanes(src, dst, tc)

    @pl.when(s % 2 == 1)
    def _():
        _scan_chunk(set_a, set_b, srcs, z_ref, kkw_ref, kaw_ref, rkw_ref, lnw_ref, lnb_ref, o_ref, state_ref,
                    ans, os, bon, gend, tc)

    @pl.when((s % 2 == 0) & (s > 0))
    def _():
        _scan_chunk(set_b, set_a, srcs, z_ref, kkw_ref, kaw_ref, rkw_ref, lnw_ref, lnb_ref, o_ref, state_ref,
                    ans, os, bon, gend, tc)


def _scan_chunk(cur, nxt, srcs, z_ref, kkw_ref, kaw_ref, rkw_ref, lnw_ref, lnb_ref, o_ref, state_ref,
                ans, os, bon, gend, tc):
    n_ch = HEAD_A
    rs, ks, vs, ws, bvs = cur
    n_groups = HEAD_A // NGRP
    groups_per_step = 2
    steps_per_rowblock = n_groups // groups_per_step
    assert tc * groups_per_step == n_groups * (tc // SUBLANES)

    def rows(n):
        return pl.ds(pl.multiple_of(n * tc, tc), tc)

    def norm_acc(n, acc):
        kkr = ks[rows(n), :] * kkw_ref[n]
        return acc + kkr * kkr

    nsq = lax.fori_loop(0, n_ch, norm_acc, jnp.zeros((tc, LANES), F32), unroll=8)
    inv = 1.0 / jnp.maximum(jnp.sqrt(nsq), 1e-12)

    tri = jnp.where(lax.broadcasted_iota(jnp.int32, (tc, tc), 0) >= lax.broadcasted_iota(jnp.int32, (tc, tc), 1),
                    1.0, 0.0)

    def prep(n, bacc):
        k = ks[rows(n), :]
        a = jax.nn.sigmoid(bvs[rows(n), :])
        kk = k * kkw_ref[n] * inv
        logw = -DECAY_SCALE * jax.nn.sigmoid(ws[rows(n), :])
        cum = jnp.dot(tri, logw, preferred_element_type=F32, precision=lax.Precision.HIGHEST)
        g = jnp.exp(cum)
        ig = jnp.exp(-cum)
        kmod = k * (1.0 + (a - 1.0) * kaw_ref[n])
        r = rs[rows(n), :]
        rs[rows(n), :] = r * g
        ks[rows(n), :] = kmod * ig
        bvs[rows(n), :] = kk * a * ig
        ans[rows(n), :] = -kk * jnp.exp(cum - logw)
        gend[pl.ds(n, 1), :] = g[tc - 1:tc, :]
        return bacc + r * kmod * rkw_ref[n]

    bon[...] = lax.fori_loop(0, n_ch, prep, jnp.zeros((tc, LANES), F32), unroll=16)

    def bcast(ref, row):
        return ref[pl.ds(row, 1), :][None]

    def sa_first(j, acc):
        return acc + state_ref[j] * bcast(ans, j * tc)

    slab = (n_ch // SUBLANES, SUBLANES, LANES)
    sa0 = lax.fori_loop(0, n_ch, sa_first, jnp.zeros(slab, F32), unroll=4)

    def step(t, sa):
        v = vs[pl.ds(t, n_ch, stride=tc), :].reshape(slab)
        t_next = jnp.minimum(t + 1, tc - 1)

        out = jnp.zeros(slab, F32)
        sa_next = jnp.zeros(slab, F32)
        for j in range(n_ch):
            row = j * tc + t
            s_new = state_ref[j] + sa * bcast(bvs, row) + v * bcast(ks, row)
            state_ref[j] = s_new
            out = out + s_new * bcast(rs, row)
            sa_next = sa_next + s_new * bcast(ans, j * tc + t_next)
        o = out.reshape(n_ch, LANES)
        mu = jnp.mean(o, axis=0, keepdims=True)
        d = o - mu
        var = jnp.mean(d * d, axis=0, keepdims=True)
        on = d * lax.rsqrt(var + GN_EPS) * lnw_ref[...] + lnb_ref[...]
        bonus = bon[pl.ds(t, 1), :] * v.reshape(n_ch, LANES)
        os[pl.ds(t, n_ch, stride=tc), :] = on + bonus

        row0 = pl.multiple_of((t // steps_per_rowblock) * SUBLANES, SUBLANES)
        for src, dst in zip(srcs, nxt):
            for gi in range(groups_per_step):
                g = (t % steps_per_rowblock) * groups_per_step + gi
                lane0 = pl.multiple_of(g * LANES, LANES)
                y = _segment_transpose([src[b, pl.ds(row0, SUBLANES), pl.ds(lane0, LANES)] for b in range(SCAN_BATCH)])
                for n_lo in range(NGRP):
                    dst[pl.ds(pl.multiple_of((g * NGRP + n_lo) * tc + row0, SUBLANES), SUBLANES), :] = y[n_lo]
        return sa_next

    lax.fori_loop(0, tc, step, sa0)

    def rescale(j, c):
        state_ref[j] = state_ref[j] * bcast(gend, j)
        return c

    lax.fori_loop(0, n_ch, rescale, 0, unroll=8)
    _from_lanes_gated(os, z_ref, o_ref, tc)


def rwkv_scan(proj_a, wa_pre, kkw, kaw, rkw, lnw, lnb, batch, seq, tc=32):
    width = HEAD_A * SCAN_HEADS
    n_chunks = seq // tc
    seq_in = lambda col: pl.BlockSpec((SCAN_BATCH, tc, width), lambda g, s: (g, jnp.minimum(s, n_chunks - 1), col))
    seq_out = lambda col: pl.BlockSpec((SCAN_BATCH, tc, width), lambda g, s: (g, jnp.maximum(s - 1, 0), col))
    par3 = pl.BlockSpec((HEAD_A, 1, LANES), lambda g, s: (0, 0, 0))
    par2 = pl.BlockSpec((HEAD_A, LANES), lambda g, s: (0, 0))
    buf = pltpu.VMEM((HEAD_A * tc, LANES), F32)
    return pl.pallas_call(
        functools.partial(_scan_kernel, tc=tc),
        grid=(batch // SCAN_BATCH, n_chunks + 1),
        in_specs=[seq_in(0), seq_in(1), seq_in(2), seq_in(0), seq_in(1), seq_out(3), par3, par3, par3, par2, par2],
        out_specs=seq_out(0),
        out_shape=jax.ShapeDtypeStruct((batch, seq, width), BF16),
        scratch_shapes=[pltpu.VMEM((HEAD_A, HEAD_A // SUBLANES, SUBLANES, LANES), F32), [buf] * 5, [buf] * 5,
                        buf, buf, pltpu.VMEM((tc, LANES), F32), pltpu.VMEM((HEAD_A, LANES), F32)],
        compiler_params=_params(ndim=2),
        name="rwkv_scan",
    )(proj_a, proj_a, proj_a, wa_pre, wa_pre, proj_a, kkw, kaw, rkw, lnw, lnb)


def _compress_kernel(x_ref, pe_ref, w1_ref, w2_ref, kg_ref, o_ref):
    which = pl.program_id(0)
    half = L_CMP // 2
    nblk = x_ref.shape[0] // CMP_STRIDE
    h1 = jnp.zeros((nblk, w1_ref.shape[-1]), F32)
    h2 = jnp.zeros((nblk, w1_ref.shape[-1]), F32)
    for l in range(half):
        x = x_ref[pl.ds(l, nblk, stride=CMP_STRIDE), :]
        a1 = (x + pe_ref[l:l + 1, :]).astype(BF16)
        a2 = (x + pe_ref[half + l:half + l + 1, :]).astype(BF16)
        h1 = h1 + jnp.dot(a1, w1_ref[l], preferred_element_type=F32)
        h2 = h2 + jnp.dot(a2, w1_ref[half + l], preferred_element_type=F32)
    hid = h1 + pltpu.roll(h2, nblk - 1, axis=0)
    hid = jax.nn.gelu(hid)
    out = jnp.dot(hid.astype(BF16), w2_ref[...], preferred_element_type=F32)
    ms = jnp.mean(out * out, axis=-1, keepdims=True)
    normed = out * lax.rsqrt(ms + NORM_EPS) * kg_ref[...]
    out = jnp.where(which == 0, normed, out)
    row = lax.broadcasted_iota(jnp.int32, out.shape, 0)
    o_ref[...] = jnp.where(row < nblk - 1, out, 0.0)


def compress(proj_b, kc_block, vc_block, pe, w1, w2, kgain, batch, seq):
    nblk = seq // CMP_STRIDE
    hidden = w1.shape[-1]

    def xmap(w, b, g):
        return (b, kc_block + w * (vc_block - kc_block) + g)

    return pl.pallas_call(
        _compress_kernel,
        grid=(2, batch, KV_GROUPS),
        in_specs=[
            pl.BlockSpec((seq, HEAD_B), xmap),
            pl.BlockSpec((None, L_CMP, HEAD_B), lambda w, b, g: (w, 0, 0)),
            pl.BlockSpec((None, L_CMP, HEAD_B, hidden), lambda w, b, g: (w, 0, 0, 0)),
            pl.BlockSpec((None, hidden, HEAD_B), lambda w, b, g: (w, 0, 0)),
            pl.BlockSpec((1, HEAD_B), lambda w, b, g: (0, 0)),
        ],
        out_specs=pl.BlockSpec((None, None, None, nblk, HEAD_B), lambda w, b, g: (w, b, g, 0, 0)),
        out_shape=jax.ShapeDtypeStruct((2, batch, KV_GROUPS, nblk, HEAD_B), F32),
        compiler_params=_params(ndim=3),
        name="nsa_compress",
    )(proj_b, pe, w1, w2, kgain)


def _dot_nt(a, b):
    return lax.dot_general(a, b, (((1,), (1,)), ((), ())), preferred_element_type=F32)


LOG2E = 1.4426950408889634
VAUG = 2 * HEAD_B


def _attn_kernel(q_ref, ks_ref, vs_ref, kw_ref, vw_ref, kc_ref, vc_ref, gt_ref, z_ref, qg_ref, kg_ref,
                 ovt_ref, ex_ref, wb_ref, o_ref, ksn_ref, vs1_ref, kwn_ref, vw1_ref, q_scr, m_scr, acc_scr,
                 oc_scr, ow_scr, *, seq):
    qt = pl.program_id(2)
    ncmp = kc_ref.shape[0]
    nsel = seq // L_SEL

    @pl.when(qt == 0)
    def _():
        ones_col = jnp.ones((seq, HEAD_B), BF16)
        ksn_ref[...] = _rms(ks_ref[...], kg_ref[1:2, :]).astype(BF16)
        vs1_ref[:, 0:HEAD_B] = vs_ref[...].astype(BF16)
        vs1_ref[:, HEAD_B:VAUG] = ones_col
        kwn_ref[0:WINDOW, :] = jnp.zeros((WINDOW, HEAD_B), BF16)
        vw1_ref[0:WINDOW, :] = jnp.zeros((WINDOW, VAUG), BF16)
        kwn_ref[WINDOW:WINDOW + seq, :] = _rms(kw_ref[...], kg_ref[2:3, :]).astype(BF16)
        vw1_ref[WINDOW:WINDOW + seq, 0:HEAD_B] = vw_ref[...].astype(BF16)
        vw1_ref[WINDOW:WINDOW + seq, HEAD_B:VAUG] = ones_col

    t0 = pl.multiple_of(qt * TQ, TQ)
    for r in range(REP):
        q_scr[r] = (_rms(q_ref[:, r * HEAD_B:(r + 1) * HEAD_B], qg_ref[...]) * (ATTN_SCALE * LOG2E)).astype(BF16)

    def t_of(shape):
        return t0 + lax.broadcasted_iota(jnp.int32, shape, 0)

    n_idx = lax.broadcasted_iota(jnp.int32, (TQ, ncmp), 1)
    bias_c = jnp.where(n_idx * CMP_STRIDE + (L_CMP - 1) <= t_of((TQ, ncmp)), 0.0, NEG_INF)
    bias_c = jnp.where(n_idx < ncmp - 1, bias_c, NEG_INF)
    row_ok = jnp.where(t_of((TQ, 1)) >= L_CMP - 1, 1.0, 0.0)
    q_all = q_scr[...].reshape(REP * TQ, HEAD_B)
    head = lambda x, r: x[r * TQ:(r + 1) * TQ]
    s_all = _dot_nt(q_all, kc_ref[...].astype(BF16))
    prs = []
    for r in range(REP):
        s = head(s_all, r) + bias_c
        p = jnp.exp2(s - jnp.max(s, axis=-1, keepdims=True))
        prs.append(p * (row_ok / jnp.sum(p, axis=-1, keepdims=True)))
    psum = prs[0] + prs[1] + prs[2] + prs[3]
    oc_scr[...] = jnp.dot(jnp.concatenate(prs, axis=0).astype(BF16), vc_ref[...].astype(BF16),
                          preferred_element_type=F32).reshape(REP, TQ, HEAD_B)
    imp_t = lax.dot_general(ovt_ref[...], psum, (((1,), (1,)), ((), ())), preferred_element_type=F32,
                            precision=lax.Precision.HIGHEST)

    j_idx = lax.broadcasted_iota(jnp.int32, (nsel, TQ), 0)
    t_sel = t0 + lax.broadcasted_iota(jnp.int32, (nsel, TQ), 1)
    cur = t_sel // L_SEL
    forced = jnp.where((j_idx == 0) | (j_idx == cur) | (j_idx == cur - 1), FORCE_BONUS, 0.0)
    score = jnp.where(j_idx * L_SEL <= t_sel, imp_t + forced, NEG_INF)
    rank = jnp.zeros((nsel, TQ), F32)
    for i in range(nsel):
        si = score[i:i + 1, :]
        ahead = (si > score) | ((si == score) & (j_idx > i))
        rank = rank + jnp.where(ahead, 1.0, 0.0)
    sel_t = jnp.where((rank < min(N_SEL, nsel)) & (score > 0.5 * NEG_INF), 1.0, 0.0)
    sel = jnp.concatenate([sel_t, jnp.zeros((LANES - nsel, TQ), F32)], axis=0).T
    not_sel = (1.0 - sel).astype(BF16)

    span = WINDOW + TQ
    kwin = kwn_ref[pl.ds(t0, span), :]
    vwin = vw1_ref[pl.ds(t0, span), :]
    lane_w = lax.broadcasted_iota(jnp.int32, (TQ, span), 1)
    bias_w = jnp.where(lane_w >= WINDOW - t0, wb_ref[...], NEG_INF)
    s_all = _dot_nt(q_all, kwin)
    ps = []
    for r in range(REP):
        s = head(s_all, r) + bias_w
        ps.append(jnp.exp2(s - jnp.max(s, axis=-1, keepdims=True)).astype(BF16))
    ow_scr[...] = jnp.dot(jnp.concatenate(ps, axis=0), vwin, preferred_element_type=F32)

    m_scr[...] = jnp.full(m_scr.shape, NEG_INF, F32)
    acc_scr[...] = jnp.zeros(acc_scr.shape, F32)

    def scores(c):
        k0 = pl.multiple_of(c * KC, KC)
        return _dot_nt(q_scr[...].reshape(REP * TQ, HEAD_B), ksn_ref[pl.ds(k0, KC), :])

    def sel_bias(c):
        return jnp.dot(not_sel, ex_ref[c], preferred_element_type=F32)

    def sel_chunk(c, s_all, bias, diagonal):
        k0 = pl.multiple_of(c * KC, KC)
        v1 = vs1_ref[pl.ds(k0, KC), :]
        if diagonal:
            lane = lax.broadcasted_iota(jnp.int32, (TQ, KC), 1)
            bias = jnp.where(k0 + lane <= t_of((TQ, KC)), bias, NEG_INF)
        ps, alphas = [], []
        for r in range(REP):
            s = head(s_all, r) + bias
            m_old = m_scr[r]
            m_new = jnp.maximum(m_old, jnp.max(s, axis=-1, keepdims=True))
            m_scr[r] = m_new
            ps.append(jnp.exp2(s - jnp.tile(m_new, (1, KC // LANES))).astype(BF16))
            alphas.append(jnp.exp2(m_old - m_new))
        pv = jnp.dot(jnp.concatenate(ps, axis=0), v1, preferred_element_type=F32)
        for r in range(REP):
            acc_scr[r] = jnp.tile(alphas[r], (1, VAUG // LANES)) * acc_scr[r] + head(pv, r)

    n_full = t0 // KC

    def full_chunk(c, carry):
        s_cur, bias_cur = carry
        nxt = (scores(c + 1), sel_bias(c + 1))
        sel_chunk(c, s_cur, bias_cur, False)
        return nxt

    s_last, bias_last = lax.fori_loop(0, n_full, full_chunk, (scores(0), sel_bias(0)))
    sel_chunk(n_full, s_last, bias_last, True)

    gts = jax.nn.sigmoid(gt_ref[...])
    for r in range(REP):
        ow = ow_scr[r * TQ:(r + 1) * TQ, :]
        o_w = ow[:, 0:HEAD_B] / ow[:, HEAD_B:VAUG]
        acc = acc_scr[r]
        o_s = acc[:, 0:HEAD_B] / acc[:, HEAD_B:VAUG]
        g0 = N_BRANCH * r
        o = gts[:, g0:g0 + 1] * oc_scr[r] + gts[:, g0 + 1:g0 + 2] * o_s + gts[:, g0 + 2:g0 + 3] * o_w
        z = z_ref[:, r * HEAD_B:(r + 1) * HEAD_B]
        o_ref[:, r * HEAD_B:(r + 1) * HEAD_B] = (o * (z * jax.nn.sigmoid(z))).astype(o_ref.dtype)


def nsa_attention(proj_b, proj_z, cmp_kv, gates, q_gain, k_gain, overlap_t, expand, win_bias, blocks, batch, seq):
    nq = seq // TQ
    ncmp = seq // CMP_STRIDE
    gw = REP * HEAD_B // LANES
    qspec = lambda off: pl.BlockSpec((TQ, REP * HEAD_B), lambda b, g, t: (b * nq + t, off // gw + g))
    kvspec = lambda off: pl.BlockSpec((seq, HEAD_B), lambda b, g, t: (b, off + g))
    cspec = lambda w: pl.BlockSpec((None, None, None, ncmp, HEAD_B), lambda b, g, t: (w, b, g, 0, 0))
    full = lambda a: pl.BlockSpec(a.shape, lambda b, g, t: (0,) * a.ndim)
    return pl.pallas_call(
        functools.partial(_attn_kernel, seq=seq),
        grid=(batch, KV_GROUPS, nq),
        in_specs=[
            qspec(blocks["q"]), kvspec(blocks["ks"]), kvspec(blocks["vs"]), kvspec(blocks["kw"]), kvspec(blocks["vw"]),
            cspec(0), cspec(1),
            pl.BlockSpec((None, None, TQ, GATE_PAD), lambda b, g, t: (b, g, t, 0)),
            qspec(blocks["z"]),
            full(q_gain), full(k_gain), full(overlap_t), full(expand), full(win_bias),
        ],
        out_specs=pl.BlockSpec((TQ, REP * HEAD_B), lambda b, g, t: (b * nq + t, g)),
        out_shape=jax.ShapeDtypeStruct((batch * seq, KV_GROUPS * REP * HEAD_B), BF16),
        scratch_shapes=[
            pltpu.VMEM((seq, HEAD_B), BF16), pltpu.VMEM((seq, VAUG), BF16),
            pltpu.VMEM((seq + WINDOW, HEAD_B), BF16), pltpu.VMEM((seq + WINDOW, VAUG), BF16),
            pltpu.VMEM((REP, TQ, HEAD_B), BF16), pltpu.VMEM((REP, TQ, LANES), F32),
            pltpu.VMEM((REP, TQ, VAUG), F32), pltpu.VMEM((REP, TQ, HEAD_B), F32),
            pltpu.VMEM((REP * TQ, VAUG), F32),
        ],
        compiler_params=_params(ndim=3),
        name="nsa_attention",
    )(proj_b, proj_b, proj_b, proj_b, proj_b, cmp_kv, cmp_kv, gates, proj_z, q_gain, k_gain, overlap_t, expand,
      win_bias)


def _merge_kernel(ya_ref, yb_ref, wa_ref, wb_ref, ga_ref, gb_ref, out_ref):
    ua = jnp.dot(ya_ref[...], wa_ref[...], preferred_element_type=F32)
    ub = jnp.dot(yb_ref[...], wb_ref[...], preferred_element_type=F32)
    out_ref[...] = (jax.nn.sigmoid(ga_ref[...]) * ua + jax.nn.sigmoid(gb_ref[...]) * ub).astype(out_ref.dtype)


def merge(ya, yb, w_a, w_b, proj_b, gate_col, tm=1024, tn=512):
    m, ka = ya.shape
    n = w_a.shape[1]
    tm = min(tm, m)
    assert gate_col % tn == 0 and n % tn == 0
    ga_block, gb_block = gate_col // tn, (gate_col + n) // tn
    deep = pl.Buffered(3)
    pipeline = pltpu.emit_pipeline(
        _merge_kernel,
        grid=(m // tm, n // tn),
        in_specs=[
            pl.BlockSpec((tm, ka), lambda i, j: (i, 0)),
            pl.BlockSpec((tm, ka), lambda i, j: (i, 0)),
            pl.BlockSpec((ka, tn), lambda i, j: (0, j), pipeline_mode=deep),
            pl.BlockSpec((ka, tn), lambda i, j: (0, j), pipeline_mode=deep),
            pl.BlockSpec((tm, tn), lambda i, j: (i, ga_block + j), pipeline_mode=deep),
            pl.BlockSpec((tm, tn), lambda i, j: (i, gb_block + j), pipeline_mode=deep),
        ],
        out_specs=[pl.BlockSpec((tm, tn), lambda i, j: (i, j))],
    )

    def outer(ya_hbm, yb_hbm, wa_hbm, wb_hbm, g_hbm, out_hbm):
        pipeline(ya_hbm, yb_hbm, wa_hbm, wb_hbm, g_hbm, g_hbm, out_hbm)

    return pl.pallas_call(
        outer,
        in_specs=[pl.BlockSpec(memory_space=pl.ANY)] * 5,
        out_specs=pl.BlockSpec(memory_space=pl.ANY),
        out_shape=jax.ShapeDtypeStruct((m, n), BF16),
        compiler_params=pltpu.CompilerParams(vmem_limit_bytes=VMEM_LIMIT),
        name="merge",
    )(ya, yb, w_a, w_b, proj_b)


def _ple_embed_kernel(p_ref, w_ref, g_ref, o_ref):
    acc = jnp.dot(p_ref[...].astype(BF16), w_ref[...], preferred_element_type=F32)
    o_ref[...] = _rms(acc, g_ref[...]).astype(o_ref.dtype)


def ple_embed(p, w, g, tm=256):
    m, k = p.shape
    n = w.shape[1]
    return pl.pallas_call(
        _ple_embed_kernel,
        grid=(m // tm,),
        in_specs=[pl.BlockSpec((tm, k), lambda i: (i, 0)), pl.BlockSpec((k, n), lambda i: (0, 0)),
                  pl.BlockSpec((1, n), lambda i: (0, 0))],
        out_specs=pl.BlockSpec((tm, n), lambda i: (i, 0)),
        out_shape=jax.ShapeDtypeStruct((m, n), F32),
        compiler_params=_params(),
        name="ple_embed",
    )(p, w, g.reshape(1, n))


def head_minor(w, heads):
    lead = w.shape[:-1]
    return w.reshape(*lead, heads, HEAD_A).swapaxes(-1, -2).reshape(*lead, heads * HEAD_A)


def rwkv_branch(proj_a, proj_s, batch, seq, a_width, w_lora_up, w0, a_lora_up, a0, k_k, k_a, r_k, lnx_w, lnx_b):
    tokens = batch * seq
    heads_a = a_width // HEAD_A
    assert heads_a == SCAN_HEADS and batch % SCAN_BATCH == 0
    lora = w_lora_up.shape[0]
    hm = lambda w: head_minor(w, heads_a)
    w_lora = jnp.stack([hm(w_lora_up), hm(a_lora_up)])
    b_lora = jnp.concatenate([hm(w0), hm(a0)]).reshape(1, -1)
    wa_pre = lora_project(proj_s, w_lora, b_lora)

    def par_scan(t):
        return jnp.tile(t.reshape(heads_a, HEAD_A).T, (1, SCAN_BATCH))

    par3 = lambda t: par_scan(t).reshape(HEAD_A, 1, LANES)
    o = rwkv_scan(proj_a.reshape(batch, seq, -1), wa_pre.reshape(batch, seq, -1),
                  par3(k_k), par3(k_a), par3(r_k), par_scan(lnx_w), par_scan(lnx_b), batch, seq)
    return o.reshape(tokens, a_width)


def nsa_branch(proj_b, proj_z, graw, batch, seq, b_width, q_norm_g, k_norm_g, pe_cmp_k, pe_cmp_v,
               cmp_k_w1, cmp_k_w2, cmp_v_w1, cmp_v_w2):
    kv_width = KV_GROUPS * HEAD_B
    blk = lambda cols: cols // LANES
    blocks = {"q": 0, "kc": blk(b_width), "vc": blk(b_width + kv_width), "ks": blk(b_width + 2 * kv_width),
              "vs": blk(b_width + 3 * kv_width), "kw": blk(b_width + 4 * kv_width), "vw": blk(b_width + 5 * kv_width),
              "z": 0}
    pe = jnp.stack([pe_cmp_k, pe_cmp_v])
    hidden = cmp_k_w1.shape[1]
    w1 = jnp.stack([cmp_k_w1, cmp_v_w1]).reshape(2, L_CMP, HEAD_B, hidden).astype(BF16)
    w2 = jnp.stack([cmp_k_w2, cmp_v_w2]).astype(BF16)
    cmp_kv = compress(proj_b, blocks["kc"], blocks["vc"], pe, w1, w2, k_norm_g[0:1], batch, seq)

    ncmp = seq // CMP_STRIDE
    nsel = seq // L_SEL
    c_start = jnp.arange(ncmp) * CMP_STRIDE
    s_start = jnp.arange(nsel) * L_SEL
    overlap_t = ((c_start[None, :] < (s_start + L_SEL)[:, None]) & (s_start[:, None] < (c_start + L_CMP)[None, :])
                 & (jnp.arange(ncmp)[None, :] < ncmp - 1)).astype(F32)
    key_blk = jnp.arange(seq) // L_SEL
    expand = jnp.where(jnp.arange(LANES)[:, None] == key_blk[None, :], NEG_INF, 0.0).astype(BF16)
    expand = expand.reshape(LANES, seq // KC, KC).transpose(1, 0, 2)
    tl = jnp.arange(TQ)[:, None]
    u = jnp.arange(WINDOW + TQ)[None, :]
    win_bias = jnp.where((u > tl) & (u <= WINDOW + tl), 0.0, NEG_INF).astype(F32)
    gates = graw.reshape(batch, seq, KV_GROUPS, REP * N_BRANCH).transpose(0, 2, 1, 3)
    gates = jnp.pad(gates, ((0, 0), (0, 0), (0, 0), (0, GATE_PAD - REP * N_BRANCH)))
    return nsa_attention(proj_b, proj_z, cmp_kv, gates, q_norm_g.reshape(1, HEAD_B), k_norm_g, overlap_t, expand,
                         win_bias, blocks, batch, seq)


def _layer(x, p, norm_g, w_in, shift_mu, w_lora_up, w0, a_lora_up, a0, k_k, k_a, r_k, lnx_w, lnx_b,
           q_norm_g, k_norm_g, pe_cmp_k, pe_cmp_v, cmp_k_w1, cmp_k_w2, cmp_v_w1, cmp_v_w2,
           w_up_a, w_up_b, w_out, ple_pre_g, w_ple_gate, w_ple, ple_post_g):
    batch, seq, d = x.shape
    tokens = batch * seq
    a_width = w_up_a.shape[0]
    b_width = w_up_b.shape[0]
    heads_a = a_width // HEAD_A
    kv_width = KV_GROUPS * HEAD_B
    n_gate = KV_GROUPS * REP * N_BRANCH
    lora = w_lora_up.shape[0]
    assert lora == HEAD_A and a_lora_up.shape[0] == HEAD_A and 2 * lora == LANES
    a_cols = 4 * a_width + 2 * lora
    g_off = a_cols + b_width + 6 * kv_width
    zb_off = g_off + n_gate
    assert w_in.shape[1] == zb_off + b_width + 2 * d

    hm4 = lambda w: head_minor(w.reshape(*w.shape[:-1], 4, a_width), heads_a).reshape(*w.shape[:-1], 4 * a_width)
    w_a = hm4(w_in[:, :4 * a_width]).astype(BF16)
    mu_a = hm4(shift_mu[:4 * a_width]).reshape(1, -1)
    pad_s = LANES - n_gate
    w_s = jnp.concatenate([w_in[:, 4 * a_width:a_cols], w_in[:, g_off:zb_off], jnp.zeros((d, pad_s), F32)],
                          axis=1).astype(BF16)
    mu_s = jnp.concatenate([shift_mu[4 * a_width:], jnp.zeros((LANES,), F32)]).reshape(1, -1)
    w_b = w_in[:, a_cols:g_off].astype(BF16)
    w_z = w_in[:, zb_off:].astype(BF16)

    x2 = x.reshape(tokens, d)
    tm = 512
    tm_wide = 2 * tm
    h, proj_s = norm_and_skinny_proj(x2, norm_g, w_s, mu_s, seq // tm, tm=tm)
    proj_a = matmul(h, w_a, mode="shift", extras=(mu_a,), tm=tm_wide, tn=1024, tiles_per_seq=seq // tm_wide,
                    name="proj_a")
    proj_b = matmul(h, w_b, tm=tm_wide, tn=1024, name="proj_b")
    proj_z = matmul(h, w_z, tm=tm_wide, tn=1024, name="proj_z")

    ya = rwkv_branch(proj_a, proj_s, batch, seq, a_width, w_lora_up, w0, a_lora_up, a0, k_k, k_a, r_k, lnx_w, lnx_b)
    graw = proj_s[:, 2 * lora:2 * lora + n_gate]
    yb = nsa_branch(proj_b, proj_z, graw, batch, seq, b_width, q_norm_g, k_norm_g, pe_cmp_k, pe_cmp_v,
                    cmp_k_w1, cmp_k_w2, cmp_v_w1, cmp_v_w2)

    w_up_a_nm = head_minor(w_up_a.T, heads_a).T
    merged = merge(ya, yb, w_up_a_nm.astype(BF16), w_up_b.astype(BF16), proj_z, b_width)
    x1 = matmul(merged, w_out.astype(BF16), mode="resid", extras=(x2,), tm=tm_wide, tn=1024, name="out_proj")
    e = ple_embed(p.reshape(tokens, -1), w_ple.astype(BF16), ple_post_g)
    out = matmul(x1, w_ple_gate.astype(BF16), mode="ple", extras=(x1, e), norm_gain=ple_pre_g, tm=tm, tn=1024,
                 name="ple_gate")
    return out.reshape(batch, seq, d)


def kernel(x, p, norm_g, w_in, shift_mu, w_lora_up, w0, a_lora_up, a0, k_k, k_a, r_k, lnx_w, lnx_b, q_norm_g, k_norm_g, pe_cmp_k, pe_cmp_v, cmp_k_w1, cmp_k_w2, cmp_v_w1, cmp_v_w2, w_up_a, w_up_b, w_out, ple_pre_g, w_ple_gate, w_ple, ple_post_g):
    depth = w_in.shape[0]
    for i in range(depth):
        x = _layer(x, p[i], norm_g[i], w_in[i], shift_mu[i], w_lora_up[i], w0[i], a_lora_up[i], a0[i], k_k[i],
                   k_a[i], r_k[i], lnx_w[i], lnx_b[i], q_norm_g[i], k_norm_g[i], pe_cmp_k[i], pe_cmp_v[i],
                   cmp_k_w1[i], cmp_k_w2[i], cmp_v_w1[i], cmp_v_w2[i], w_up_a[i], w_up_b[i], w_out[i],
                   ple_pre_g[i], w_ple_gate[i], w_ple[i], ple_post_g[i])
    return x
```
